```python
import math
import jax
import jax.numpy as jnp
from jax import lax
import numpy as np

D_MODEL = 2048
BATCH = 4
SEQ = 2048
DEPTH = 2
DEC_BATCH = 16
DEC_SEQ = 64
PAST_LEN = 4096

CHUNK = 64
D_MIX = D_MODEL
GM_WIDTH = D_MIX // 2
GM_HEADS = 8
GM_HEAD_DIM = GM_WIDTH // GM_HEADS
GM_CHUNK = 128
M_WIDTH = D_MIX - GM_WIDTH
M_HEAD_DIM = 64
M_HEADS = M_WIDTH // M_HEAD_DIM
M_GROUPS = 4
M_STATE = 128
M_CONV = 4
CONV_DIM = M_WIDTH + 2 * M_GROUPS * M_STATE
SSD_CHUNK = CHUNK
IN_DIM = 2 * GM_WIDTH + M_WIDTH + CONV_DIM + M_HEADS
SPLITS = (GM_WIDTH, 2 * GM_WIDTH, 2 * GM_WIDTH + M_WIDTH, 2 * GM_WIDTH + M_WIDTH + CONV_DIM)
N_EGROUPS = 4
EXPERTS_PER_GROUP = 4
N_EXPERTS = N_EGROUPS * EXPERTS_PER_GROUP
TOP_K_INNER = 2
D_EXPERT = 512
N_MOD = 6
EPS = 1e-6

kernel_name = "hymba_gmlp_ssd_hmoe_stream_step"


def rmsnorm(x, g):
    xf = x.astype(jnp.float32)
    y = xf * lax.rsqrt(jnp.mean(xf * xf, axis=-1, keepdims=True) + EPS)
    return (y * g.astype(jnp.float32)).astype(x.dtype)


def layernorm(x, g, b):
    xf = x.astype(jnp.float32)
    mu = jnp.mean(xf, axis=-1, keepdims=True)
    xc = xf - mu
    y = xc * lax.rsqrt(jnp.mean(xc * xc, axis=-1, keepdims=True) + EPS)
    return (y * g.astype(jnp.float32) + b.astype(jnp.float32)).astype(x.dtype)


def chunk_mlp(u, v, ln_g, ln_b, w_s, b_s):
    bsz, L, _ = v.shape
    vn = layernorm(v, ln_g, ln_b)
    pad = (-L) % GM_CHUNK
    vp = jnp.pad(vn, ((0, 0), (0, pad), (0, 0)))
    vp = vp.reshape(bsz, (L + pad) // GM_CHUNK, GM_CHUNK, GM_HEADS, GM_HEAD_DIM)
    causal = jnp.tril(jnp.ones((GM_CHUNK, GM_CHUNK), dtype=bool))
    w = jnp.where(causal, w_s, 0)
    mixed = jnp.einsum("hts,bcshd->bcthd", w, vp) + b_s.T[:, :, None]
    mixed = mixed.reshape(bsz, L + pad, GM_WIDTH)[:, :L]
    return u * mixed, vn


def causal_conv(xbc, conv_state, w, b):
    L = xbc.shape[1]
    xp = jnp.concatenate([conv_state.astype(xbc.dtype), xbc], axis=1)
    out = b
    for k in range(M_CONV):
        out = out + xp[:, k:k + L] * w[k]
    return jax.nn.silu(out), xp[:, -(M_CONV - 1):]


def segsum(x):
    T = x.shape[-1]
    xr = jnp.broadcast_to(x[..., None], x.shape + (T,))
    xr = jnp.where(jnp.tril(jnp.ones((T, T), dtype=bool), -1), xr, 0.0)
    cs = jnp.cumsum(xr, axis=-2)
    return jnp.where(jnp.tril(jnp.ones((T, T), dtype=bool)), cs, -jnp.inf)


def ssd_scan(x, dt, a, bm, cm, h0):
    bsz, L, H, P = x.shape
    N = bm.shape[-1]
    pad = (-L) % SSD_CHUNK
    if pad:
        x = jnp.pad(x, ((0, 0), (0, pad), (0, 0), (0, 0)))
        dt = jnp.pad(dt, ((0, 0), (0, pad), (0, 0)))
        bm = jnp.pad(bm, ((0, 0), (0, pad), (0, 0), (0, 0)))
        cm = jnp.pad(cm, ((0, 0), (0, pad), (0, 0), (0, 0)))
    nc = (L + pad) // SSD_CHUNK
    x = x.reshape(bsz, nc, SSD_CHUNK, H, P)
    dt = dt.reshape(bsz, nc, SSD_CHUNK, H)
    bm = bm.reshape(bsz, nc, SSD_CHUNK, H, N)
    cm = cm.reshape(bsz, nc, SSD_CHUNK, H, N)
    x_dt = x * dt[..., None]
    a_dt = jnp.transpose(dt * a, (0, 3, 1, 2))
    a_cs = jnp.cumsum(a_dt, axis=-1)
    scores = jnp.einsum("bclhn,bcshn->bhcls", cm, bm) * jnp.exp(segsum(a_dt))
    y_diag = jnp.einsum("bhcls,bcshp->bclhp", scores, x_dt)
    decay_states = jnp.exp(a_cs[..., -1:] - a_cs)
    states = jnp.einsum("bclhn,bhcl,bclhp->bchpn", bm, decay_states, x_dt)
    states = jnp.concatenate([h0[:, None], states], axis=1)
    a_chunk = jnp.pad(a_cs[..., -1], ((0, 0), (0, 0), (1, 0)))
    new_states = jnp.einsum("bhzc,bchpn->bzhpn", jnp.exp(segsum(a_chunk)), states)
    states, h_final = new_states[:, :-1], new_states[:, -1]
    y_off = jnp.einsum("bclhn,bchpn,bhcl->bclhp", cm, states, jnp.exp(a_cs))
    y = (y_diag + y_off).reshape(bsz, nc * SSD_CHUNK, H, P)[:, :L]
    return y, h_final


def mamba_ssd(xbc, z, dt_raw, h0, dt_bias, a_log, d_skip, norm_g):
    f32 = jnp.float32
    bsz, L, _ = xbc.shape
    xs, bm, cm = jnp.split(xbc, [M_WIDTH, M_WIDTH + M_GROUPS * M_STATE], axis=-1)
    rep = M_HEADS // M_GROUPS
    xs = xs.reshape(bsz, L, M_HEADS, M_HEAD_DIM).astype(f32)
    bm = jnp.repeat(bm.reshape(bsz, L, M_GROUPS, M_STATE), rep, axis=2).astype(f32)
    cm = jnp.repeat(cm.reshape(bsz, L, M_GROUPS, M_STATE), rep, axis=2).astype(f32)
    dt = jax.nn.softplus(dt_raw.astype(f32) + dt_bias.astype(f32))
    a = -jnp.exp(a_log.astype(f32))
    y, h_final = ssd_scan(xs, dt, a, bm, cm, h0.astype(f32))
    y = y + d_skip.astype(f32)[:, None] * xs
    gy = (y.reshape(bsz, L, M_WIDTH) * jax.nn.silu(z.astype(f32))).reshape(bsz, L, M_GROUPS, M_WIDTH // M_GROUPS)
    gy = gy * lax.rsqrt(jnp.mean(gy * gy, axis=-1, keepdims=True) + EPS)
    out = gy.reshape(bsz, L, M_WIDTH) * norm_g.astype(f32)
    return out.astype(xbc.dtype), h_final.astype(h0.dtype)


def hier_moe(h, w_rg, b_rg, w_re, b_re, w_gate, w_up, w_down):
    f32 = jnp.float32
    bsz, L, D = h.shape
    t = h.reshape(bsz * L, D)
    g_prob = jax.nn.softmax((t @ w_rg + b_rg).astype(f32), axis=-1)
    g_top, g_idx = lax.top_k(g_prob, 1)
    e_logits = (t @ w_re + b_re).astype(f32).reshape(-1, N_EGROUPS, EXPERTS_PER_GROUP)
    e_in = jnp.take_along_axis(e_logits, g_idx[:, :, None], axis=1)[:, 0]
    e_top, e_idx = lax.top_k(jax.nn.softmax(e_in, axis=-1), TOP_K_INNER)
    e_w = e_top / jnp.sum(e_top, axis=-1, keepdims=True) * g_top
    e_id = g_idx * EXPERTS_PER_GROUP + e_idx
    gates = jnp.sum(jax.nn.one_hot(e_id, N_EXPERTS, dtype=f32) * e_w[..., None], axis=1)
    hid = jax.nn.silu(jnp.einsum("td,edf->tef", t, w_gate)) * jnp.einsum("td,edf->tef", t, w_up)
    hid = hid * gates[..., None].astype(hid.dtype)
    out = jnp.einsum("tef,efd->td", hid, w_down)
    return out.reshape(bsz, L, D)


def setup_inputs(seed: int = 0) -> dict:
    key = jax.random.key(seed)
    ks = jax.random.split(key, 40)
    f32 = jnp.float32

    def nrm(k, shape, scale):
        return jax.random.normal(k, shape, f32) * scale

    dt0 = jnp.exp(jax.random.uniform(ks[16], (DEPTH, M_HEADS), f32, math.log(1e-3), math.log(1e-1)))
    return {
        "x_prompt": nrm(ks[0], (BATCH, SEQ, D_MODEL), 1.0),
        "x_sample": nrm(ks[1], (DEC_BATCH, DEC_SEQ, D_MODEL), 1.0),
        "state_ssm": nrm(ks[2], (DEPTH, DEC_BATCH, M_HEADS, M_HEAD_DIM, M_STATE), 0.5),
        "state_conv": nrm(ks[3], (DEPTH, DEC_BATCH, M_CONV - 1, CONV_DIM), 1.0),
        "c_prompt": nrm(ks[4], (BATCH, D_MODEL), 1.0),
        "c_sample": nrm(ks[5], (DEC_BATCH, D_MODEL), 1.0),
        "norm1_g": 1.0 + nrm(ks[6], (DEPTH, D_MODEL), 0.02),
        "norm2_g": 1.0 + nrm(ks[7], (DEPTH, D_MODEL), 0.02),
        "w_ada": nrm(ks[8], (DEPTH, D_MODEL, N_MOD * D_MODEL), 0.5 * D_MODEL ** -0.5),
        "b_ada": nrm(ks[9], (DEPTH, N_MOD * D_MODEL), 0.02),
        "w_in": nrm(ks[10], (DEPTH, D_MODEL, IN_DIM), D_MODEL ** -0.5),
        "gm_ln_g": 1.0 + nrm(ks[11], (DEPTH, GM_WIDTH), 0.02),
        "gm_ln_b": nrm(ks[12], (DEPTH, GM_WIDTH), 0.02),
        "gm_w_s": nrm(ks[13], (DEPTH, GM_HEADS, GM_CHUNK, GM_CHUNK), GM_CHUNK ** -0.5),
        "gm_b_s": 1.0 + nrm(ks[14], (DEPTH, GM_HEADS, GM_CHUNK), 0.02),
        "conv_w": nrm(ks[15], (DEPTH, M_CONV, CONV_DIM), M_CONV ** -0.5),
        "conv_b": nrm(ks[17], (DEPTH, CONV_DIM), 0.02),
        "dt_bias": dt0 + jnp.log(-jnp.expm1(-dt0)),
        "a_log": jnp.log(jax.random.uniform(ks[18], (DEPTH, M_HEADS), f32, 1.0, 16.0)),
        "d_skip": 1.0 + nrm(ks[19], (DEPTH, M_HEADS), 0.02),
        "mnorm_g": 1.0 + nrm(ks[20], (DEPTH, M_WIDTH), 0.02),
        "w_out": nrm(ks[21], (DEPTH, D_MIX, D_MODEL), D_MIX ** -0.5),
        "w_router_group": nrm(ks[22], (DEPTH, D_MODEL, N_EGROUPS), D_MODEL ** -0.5),
        "b_router_group": nrm(ks[23], (DEPTH, N_EGROUPS), 0.01),
        "w_router_expert": nrm(ks[24], (DEPTH, D_MODEL, N_EXPERTS), D_MODEL ** -0.5),
        "b_router_expert": nrm(ks[25], (DEPTH, N_EXPERTS), 0.01),
        "w_gate": nrm(ks[26], (DEPTH, N_EXPERTS, D_MODEL, D_EXPERT), D_MODEL ** -0.5),
        "w_up": nrm(ks[27], (DEPTH, N_EXPERTS, D_MODEL, D_EXPERT), D_MODEL ** -0.5),
        "w_down": nrm(ks[28], (DEPTH, N_EXPERTS, D_EXPERT, D_MODEL), D_EXPERT ** -0.5),
        "final_norm_g": 1.0 + nrm(ks[29], (D_MODEL,), 0.02),
    }


def reference(x_prompt, x_sample, state_ssm, state_conv, c_prompt, c_sample, norm1_g, norm2_g, w_ada, b_ada,
              w_in, gm_ln_g, gm_ln_b, gm_w_s, gm_b_s, conv_w, conv_b, dt_bias, a_log, d_skip, mnorm_g, w_out,
              w_router_group, b_router_group, w_router_expert, b_router_expert, w_gate, w_up, w_down,
              final_norm_g):

    def layer(x, c, l, conv0, ssm0):
        bsz = x.shape[0]
        mod = (jax.nn.silu(c) @ w_ada[l] + b_ada[l]).reshape(bsz, N_MOD, D_MODEL)
        shift1, scale1, gate1 = mod[:, 0, None, :], mod[:, 1, None, :], mod[:, 2, None, :]
        shift2, scale2, gate2 = mod[:, 3, None, :], mod[:, 4, None, :], mod[:, 5, None, :]
        h = rmsnorm(x, norm1_g[l]) * (1 + scale1) + shift1
        proj = h @ w_in[l]
        u, v, z, xbc, dt_raw = jnp.split(proj, SPLITS, axis=-1)
        gm_out, v_rows = chunk_mlp(jax.nn.gelu(u), jax.nn.gelu(v), gm_ln_g[l], gm_ln_b[l], gm_w_s[l], gm_b_s[l])
        xbc, conv_new = causal_conv(xbc, conv0, conv_w[l], conv_b[l])
        ssm_out, ssm_new = mamba_ssd(xbc, z, dt_raw, ssm0, dt_bias[l], a_log[l], d_skip[l], mnorm_g[l])
        x = x + gate1 * (jnp.concatenate([gm_out, ssm_out], axis=-1) @ w_out[l])
        h = rmsnorm(x, norm2_g[l]) * (1 + scale2) + shift2
        x = x + gate2 * hier_moe(h, w_router_group[l], b_router_group[l], w_router_expert[l], b_router_expert[l],
                                 w_gate[l], w_up[l], w_down[l])
        return x, conv_new, ssm_new, v_rows

    def run(x, c, conv_init, ssm_init):
        convs, ssms, vs = [], [], []
        for l in range(DEPTH):
            x, conv_new, ssm_new, v_rows = layer(x, c, l, conv_init[l], ssm_init[l])
            convs.append(conv_new)
            ssms.append(ssm_new)
            vs.append(v_rows)
        return rmsnorm(x, final_norm_g), jnp.stack(convs), jnp.stack(ssms), jnp.stack(vs)

    conv_zero = jnp.zeros((DEPTH, x_prompt.shape[0], M_CONV - 1, CONV_DIM), x_prompt.dtype)
    ssm_zero = jnp.zeros((DEPTH, x_prompt.shape[0], M_HEADS, M_HEAD_DIM, M_STATE), x_prompt.dtype)
    y_prompt, conv_prompt, ssm_prompt, _ = run(x_prompt, c_prompt, conv_zero, ssm_zero)
    y_sample, conv_sample, ssm_sample, gm_v_sample = run(x_sample, c_sample, state_conv, state_ssm)
    return (y_prompt, y_sample, ssm_prompt, conv_prompt, ssm_sample, conv_sample, gm_v_sample)
```

```python
import functools
import math

import jax
import jax.numpy as jnp
import numpy as np
from jax import lax
from jax.experimental import pallas as pl
from jax.experimental.pallas import tpu as pltpu

D_MODEL = 2048
DEPTH = 2
GM_WIDTH = 1024
GM_HEADS = 8
GM_HEAD_DIM = 128
GM_CHUNK = 128
M_WIDTH = 1024
M_HEAD_DIM = 64
M_HEADS = 16
M_GROUPS = 4
M_STATE = 128
M_CONV = 4
CONV_DIM = M_WIDTH + 2 * M_GROUPS * M_STATE
MAIN_DIM = 2 * GM_WIDTH + M_WIDTH + CONV_DIM
N_EGROUPS = 4
EXPERTS_PER_GROUP = 4
N_EXPERTS = 16
D_EXPERT = 512
N_MOD = 6
EPS = 1e-6

ROW_BLOCK = 64
LANES = 128
VMEM_LIMIT = 56 * 1024 * 1024

F32 = jnp.float32
BF16 = jnp.bfloat16


def _cparams(n_axes):
    return pltpu.CompilerParams(
        dimension_semantics=("arbitrary",) * n_axes, vmem_limit_bytes=VMEM_LIMIT)


def _silu(x):
    return x * jax.nn.sigmoid(x)


def _gelu_tanh(x):
    c = math.sqrt(2.0 / math.pi)
    return 0.5 * x * (1.0 + jnp.tanh(c * (x + 0.044715 * (x * x * x))))


def _dot(a, b):
    return jnp.dot(a, b, preferred_element_type=F32)


def _dot_nt(a, b, precision=None):
    return lax.dot_general(a, b, (((1,), (1,)), ((), ())), precision=precision,
                           preferred_element_type=F32)


def _dot_tn(a, b):
    return lax.dot_general(a, b, (((0,), (0,)), ((), ())), preferred_element_type=F32)


def _ada_kernel(c_ref, w_ref, b_ref, o_ref):
    c = _silu(c_ref[...]).astype(BF16)
    o_ref[...] = _dot(c, w_ref[...].astype(BF16)) + b_ref[...]


def _ada(c_pad, w_ada, b_ada):
    rows = c_pad.shape[0]
    tn = 1024
    n_out = N_MOD * D_MODEL
    return pl.pallas_call(
        _ada_kernel,
        grid=(DEPTH, n_out // tn),
        in_specs=[
            pl.BlockSpec((rows, D_MODEL), lambda l, j: (0, 0)),
            pl.BlockSpec((None, D_MODEL, tn), lambda l, j: (l, 0, j)),
            pl.BlockSpec((None, 1, tn), lambda l, j: (l, 0, j)),
        ],
        out_specs=pl.BlockSpec((None, rows, tn), lambda l, j: (l, 0, j)),
        out_shape=jax.ShapeDtypeStruct((DEPTH, rows, n_out), F32),
        compiler_params=_cparams(2),
        name="ada_ln",
    )(c_pad, w_ada, b_ada.reshape(DEPTH, 1, n_out))


def _inproj_kernel(x_ref, g_ref, shift_ref, scale_ref, w_ref, wdt_ref, proj_ref, dt_ref, h_ref):
    j = pl.program_id(1)
    n_blk = shift_ref.shape[0]

    @pl.when(j == 0)
    def _():
        g = g_ref[...]

        def body(r, carry):
            rows = pl.ds(pl.multiple_of(r * ROW_BLOCK, ROW_BLOCK), ROW_BLOCK)
            xf = x_ref[rows, :]
            y = xf * lax.rsqrt(jnp.mean(xf * xf, axis=-1, keepdims=True) + EPS) * g
            y = y * (1.0 + scale_ref[pl.ds(r, 1), :]) + shift_ref[pl.ds(r, 1), :]
            h_ref[rows, :] = y.astype(BF16)
            return carry

        lax.fori_loop(0, n_blk, body, 0)
        dt_ref[...] = _dot(h_ref[...], wdt_ref[...].astype(BF16))

    proj_ref[...] = _dot(h_ref[...], w_ref[...].astype(BF16)).astype(BF16)


def _inproj(x, norm_g, modb, w_in, w_dt, layer, tm=1024, tn=512):
    t_rows = x.shape[0]
    nb = tm // ROW_BLOCK
    return pl.pallas_call(
        _inproj_kernel,
        grid=(t_rows // tm, MAIN_DIM // tn),
        in_specs=[
            pl.BlockSpec((tm, D_MODEL), lambda i, j: (i, 0)),
            pl.BlockSpec((None, 1, D_MODEL), lambda i, j: (layer, 0, 0)),
            pl.BlockSpec((nb, D_MODEL), lambda i, j: (i, 0)),
            pl.BlockSpec((nb, D_MODEL), lambda i, j: (i, 1)),
            pl.BlockSpec((None, D_MODEL, tn), lambda i, j: (layer, 0, j)),
            pl.BlockSpec((D_MODEL, LANES), lambda i, j: (0, 0)),
        ],
        out_specs=[
            pl.BlockSpec((tm, tn), lambda i, j: (i, j)),
            pl.BlockSpec((tm, LANES), lambda i, j: (i, 0)),
        ],
        out_shape=[
            jax.ShapeDtypeStruct((t_rows, MAIN_DIM), BF16),
            jax.ShapeDtypeStruct((t_rows, LANES), F32),
        ],
        scratch_shapes=[pltpu.VMEM((tm, D_MODEL), BF16)],
        compiler_params=_cparams(2),
        name="in_proj",
    )(x, norm_g.reshape(DEPTH, 1, D_MODEL), modb, modb, w_in, w_dt)


def _gmlp_kernel(n_prompt_tiles, u_ref, v_ref, lng_ref, lnb_ref, w_ref, b_ref, out_ref, vn_ref):
    i = pl.program_id(0)
    u = _gelu_tanh(u_ref[...].astype(F32))
    v = _gelu_tanh(v_ref[...].astype(F32))
    mu = jnp.mean(v, axis=-1, keepdims=True)
    vc = v - mu
    vn = vc * lax.rsqrt(jnp.mean(vc * vc, axis=-1, keepdims=True) + EPS)
    vn = vn * lng_ref[...] + lnb_ref[...]
    vn_ref[...] = vn
    t_idx = lax.broadcasted_iota(jnp.int32, (GM_CHUNK, GM_CHUNK), 0)
    s_idx = lax.broadcasted_iota(jnp.int32, (GM_CHUNK, GM_CHUNK), 1)
    same_seq = (t_idx // ROW_BLOCK) == (s_idx // ROW_BLOCK)
    allowed = (s_idx <= t_idx) & (same_seq | (i < n_prompt_tiles))
    vnb = vn.astype(BF16)
    bias = b_ref[...]
    for h in range(GM_HEADS):
        cols = slice(h * GM_HEAD_DIM, (h + 1) * GM_HEAD_DIM)
        wm = jnp.where(allowed, w_ref[h], 0.0).astype(BF16)
        mixed = _dot(wm, vnb[:, cols]) + bias[:, h:h + 1]
        out_ref[:, cols] = (u[:, cols] * mixed).astype(BF16)


def _gmlp(proj, ln_g, ln_b, w_cat, b_cat, layer, n_prompt_tiles, n_sample_rows):
    t_rows = proj.shape[0]
    n_tiles = t_rows // GM_CHUNK
    kern = functools.partial(_gmlp_kernel, n_prompt_tiles)
    return pl.pallas_call(
        kern,
        grid=(n_tiles,),
        in_specs=[
            pl.BlockSpec((GM_CHUNK, GM_WIDTH), lambda i: (i, 0)),
            pl.BlockSpec((GM_CHUNK, GM_WIDTH), lambda i: (i, 1)),
            pl.BlockSpec((None, 1, GM_WIDTH), lambda i: (layer, 0, 0)),
            pl.BlockSpec((None, 1, GM_WIDTH), lambda i: (layer, 0, 0)),
            pl.BlockSpec((None, None, GM_HEADS, GM_CHUNK, GM_CHUNK),
                         lambda i: (layer, jnp.where(i < n_prompt_tiles, 0, 1), 0, 0, 0)),
            pl.BlockSpec((None, None, GM_CHUNK, GM_HEADS),
                         lambda i: (layer, jnp.where(i < n_prompt_tiles, 0, 1), 0, 0)),
        ],
        out_specs=[
            pl.BlockSpec((GM_CHUNK, GM_WIDTH), lambda i: (i, 0)),
            pl.BlockSpec((GM_CHUNK, GM_WIDTH), lambda i: (jnp.maximum(i - n_prompt_tiles, 0), 0)),
        ],
        out_shape=[
            jax.ShapeDtypeStruct((t_rows, GM_WIDTH), BF16),
            jax.ShapeDtypeStruct((n_sample_rows, GM_WIDTH), F32),
        ],
        compiler_params=_cparams(1),
        name="gmlp",
    )(proj, proj, ln_g.reshape(DEPTH, 1, GM_WIDTH), ln_b.reshape(DEPTH, 1, GM_WIDTH), w_cat, b_cat)


CONV_PAD = 8


def _ssd_kernel(seq_ref, first_ref, z_ref, xa_ref, xb_ref, dt_ref, h0_ref, c0_ref, cw_ref, cb_ref,
                dtb_ref, alog_ref, dskip_ref, ng_ref, out_ref, hfin_ref, cfin_ref,
                state_ref, win_ref, y_ref):
    i = pl.program_id(0)
    tail = M_CONV - 1
    lo = CONV_PAD - tail

    @pl.when(first_ref[i] == 1)
    def _():
        state_ref[...] = h0_ref[...]
        win_ref[lo:CONV_PAD, :] = c0_ref[...]

    half = CONV_DIM // 2
    win_ref[CONV_PAD:CONV_PAD + ROW_BLOCK, 0:half] = xa_ref[...].astype(F32)
    win_ref[CONV_PAD:CONV_PAD + ROW_BLOCK, half:CONV_DIM] = xb_ref[...].astype(F32)
    conv = cb_ref[...]
    for k in range(M_CONV):
        conv = conv + win_ref[lo + k:lo + k + ROW_BLOCK, :] * cw_ref[k:k + 1, :]
    new_tail = win_ref[CONV_PAD + ROW_BLOCK - tail:CONV_PAD + ROW_BLOCK, :]
    cfin_ref[...] = new_tail
    win_ref[lo:CONV_PAD, :] = new_tail
    xbc = _silu(conv)

    dt = jax.nn.softplus(dt_ref[:, 0:M_HEADS] + dtb_ref[...])
    a = -jnp.exp(alog_ref[...])
    a_dt = dt * a
    r_idx = lax.broadcasted_iota(jnp.int32, (ROW_BLOCK, ROW_BLOCK), 0)
    c_idx = lax.broadcasted_iota(jnp.int32, (ROW_BLOCK, ROW_BLOCK), 1)
    causal = c_idx <= r_idx
    tri = causal.astype(F32)
    a_cs = jnp.dot(tri, a_dt, precision=lax.Precision.HIGHEST, preferred_element_type=F32)
    eye = (lax.broadcasted_iota(jnp.int32, (M_HEADS, M_HEADS), 0)
           == lax.broadcasted_iota(jnp.int32, (M_HEADS, M_HEADS), 1)).astype(F32)
    a_cs_t = _dot_nt(eye, a_cs, precision=lax.Precision.HIGHEST)
    a_end = a_cs[ROW_BLOCK - 1:ROW_BLOCK, :]
    decay_in = jnp.exp(a_cs)
    decay_out = jnp.exp(a_end - a_cs)
    decay_blk = jnp.exp(a_end)
    dskip = dskip_ref[...]

    heads_per_group = M_HEADS // M_GROUPS
    for g in range(M_GROUPS):
        b_g = xbc[:, M_WIDTH + g * M_STATE:M_WIDTH + (g + 1) * M_STATE].astype(BF16)
        c_off = M_WIDTH + M_GROUPS * M_STATE
        c_g = xbc[:, c_off + g * M_STATE:c_off + (g + 1) * M_STATE].astype(BF16)
        cb = _dot_nt(c_g, b_g)
        for hh in range(heads_per_group):
            h = g * heads_per_group + hh
            cols = slice(h * M_HEAD_DIM, (h + 1) * M_HEAD_DIM)
            xs_h = xbc[:, cols]
            seg = a_cs[:, h:h + 1] - a_cs_t[h:h + 1, :]
            scores = cb * jnp.exp(jnp.where(causal, seg, -jnp.inf))
            x_dt = xs_h * dt[:, h:h + 1]
            y = _dot(scores.astype(BF16), x_dt.astype(BF16))
            st = state_ref[h]
            y = y + _dot_nt(c_g, st.astype(BF16)) * decay_in[:, h:h + 1]
            upd = _dot_tn((x_dt * decay_out[:, h:h + 1]).astype(BF16), b_g)
            state_ref[h] = st * decay_blk[:, h:h + 1] + upd
            y_ref[:, cols] = y + dskip[:, h:h + 1] * xs_h

    hfin_ref[...] = state_ref[...]
    gy = y_ref[...] * _silu(z_ref[...].astype(F32))
    gw = M_WIDTH // M_GROUPS
    ng = ng_ref[...]
    for g in range(M_GROUPS):
        cols = slice(g * gw, (g + 1) * gw)
        s = gy[:, cols]
        s = s * lax.rsqrt(jnp.mean(s * s, axis=-1, keepdims=True) + EPS)
        out_ref[:, cols] = (s * ng[:, cols]).astype(BF16)


def _ssd(proj, dt_raw, h0, c0, conv_w, conv_b, dt_bias, a_log, d_skip, mnorm_g, blk_seq, blk_first, layer):
    t_rows = proj.shape[0]
    n_blk = t_rows // ROW_BLOCK
    n_seq = h0.shape[0]
    half = CONV_DIM // 2
    z_blk = (2 * GM_WIDTH) // M_WIDTH
    xa_blk = (2 * GM_WIDTH + M_WIDTH) // half
    grid_spec = pltpu.PrefetchScalarGridSpec(
        num_scalar_prefetch=2,
        grid=(n_blk,),
        in_specs=[
            pl.BlockSpec((ROW_BLOCK, M_WIDTH), lambda i, s, f: (i, z_blk)),
            pl.BlockSpec((ROW_BLOCK, half), lambda i, s, f: (i, xa_blk)),
            pl.BlockSpec((ROW_BLOCK, half), lambda i, s, f: (i, xa_blk + 1)),
            pl.BlockSpec((ROW_BLOCK, LANES), lambda i, s, f: (i, 0)),
            pl.BlockSpec((None, M_HEADS, M_HEAD_DIM, M_STATE), lambda i, s, f: (s[i], 0, 0, 0)),
            pl.BlockSpec((None, M_CONV - 1, CONV_DIM), lambda i, s, f: (s[i], 0, 0)),
            pl.BlockSpec((None, M_CONV, CONV_DIM), lambda i, s, f: (layer, 0, 0)),
            pl.BlockSpec((None, 1, CONV_DIM), lambda i, s, f: (layer, 0, 0)),
            pl.BlockSpec((None, 1, M_HEADS), lambda i, s, f: (layer, 0, 0)),
            pl.BlockSpec((None, 1, M_HEADS), lambda i, s, f: (layer, 0, 0)),
            pl.BlockSpec((None, 1, M_HEADS), lambda i, s, f: (layer, 0, 0)),
            pl.BlockSpec((None, 1, M_WIDTH), lambda i, s, f: (layer, 0, 0)),
        ],
        out_specs=[
            pl.BlockSpec((ROW_BLOCK, M_WIDTH), lambda i, s, f: (i, 0)),
            pl.BlockSpec((None, M_HEADS, M_HEAD_DIM, M_STATE), lambda i, s, f: (s[i], 0, 0, 0)),
            pl.BlockSpec((None, M_CONV - 1, CONV_DIM), lambda i, s, f: (s[i], 0, 0)),
        ],
        scratch_shapes=[
            pltpu.VMEM((M_HEADS, M_HEAD_DIM, M_STATE), F32),
            pltpu.VMEM((CONV_PAD + ROW_BLOCK, CONV_DIM), F32),
            pltpu.VMEM((ROW_BLOCK, M_WIDTH), F32),
        ],
    )
    return pl.pallas_call(
        _ssd_kernel,
        grid_spec=grid_spec,
        out_shape=[
            jax.ShapeDtypeStruct((t_rows, M_WIDTH), BF16),
            jax.ShapeDtypeStruct((n_seq, M_HEADS, M_HEAD_DIM, M_STATE), F32),
            jax.ShapeDtypeStruct((n_seq, M_CONV - 1, CONV_DIM), F32),
        ],
        compiler_params=_cparams(1),
        name="conv_ssd",
    )(blk_seq, blk_first, proj, proj, proj, dt_raw, h0, c0, conv_w,
      conv_b.reshape(DEPTH, 1, CONV_DIM), dt_bias.reshape(DEPTH, 1, M_HEADS),
      a_log.reshape(DEPTH, 1, M_HEADS), d_skip.reshape(DEPTH, 1, M_HEADS),
      mnorm_g.reshape(DEPTH, 1, M_WIDTH))


def _outproj_kernel(gm_ref, ssm_ref, wa_ref, wb_ref, x_ref, gate_ref, o_ref):
    acc = _dot(gm_ref[...], wa_ref[...].astype(BF16)) + _dot(ssm_ref[...], wb_ref[...].astype(BF16))
    tm, tn = acc.shape
    nb = tm // ROW_BLOCK
    upd = acc.reshape(nb, ROW_BLOCK, tn) * gate_ref[...][:, None, :]
    o_ref[...] = x_ref[...] + upd.reshape(tm, tn)


def _outproj(gm, ssm, w_out, x, modb, layer, tm=1024, tn=512):
    t_rows = x.shape[0]
    nb = tm // ROW_BLOCK
    gate_blk = 2 * D_MODEL // tn
    return pl.pallas_call(
        _outproj_kernel,
        grid=(t_rows // tm, D_MODEL // tn),
        in_specs=[
            pl.BlockSpec((tm, GM_WIDTH), lambda i, j: (i, 0)),
            pl.BlockSpec((tm, M_WIDTH), lambda i, j: (i, 0)),
            pl.BlockSpec((None, GM_WIDTH, tn), lambda i, j: (layer, 0, j)),
            pl.BlockSpec((None, M_WIDTH, tn), lambda i, j: (layer, 1, j)),
            pl.BlockSpec((tm, tn), lambda i, j: (i, j)),
            pl.BlockSpec((nb, tn), lambda i, j: (i, gate_blk + j)),
        ],
        out_specs=pl.BlockSpec((tm, tn), lambda i, j: (i, j)),
        out_shape=jax.ShapeDtypeStruct((t_rows, D_MODEL), F32),
        compiler_params=_cparams(2),
        name="out_proj",
    )(gm, ssm, w_out, w_out, x, modb)


ROUTER_ROWS = 24


def _row_argmax_first(rows):
    m = rows[0]
    for r in rows[1:]:
        m = jnp.maximum(m, r)
    idx = jnp.full(m.shape, len(rows), jnp.int32)
    for k in reversed(range(len(rows))):
        idx = jnp.where(rows[k] == m, k, idx)
    return m, idx


def _router_kernel(x_ref, g_ref, shift_ref, scale_ref, wr_ref, br_ref, h_ref, gates_ref, hf_ref):
    n_blk = shift_ref.shape[0]
    g = g_ref[...]

    def body(r, carry):
        rows = pl.ds(pl.multiple_of(r * ROW_BLOCK, ROW_BLOCK), ROW_BLOCK)
        xf = x_ref[rows, :]
        y = xf * lax.rsqrt(jnp.mean(xf * xf, axis=-1, keepdims=True) + EPS) * g
        y = y * (1.0 + scale_ref[pl.ds(r, 1), :]) + shift_ref[pl.ds(r, 1), :]
        hf_ref[rows, :] = y
        h_ref[rows, :] = y.astype(BF16)
        return carry

    lax.fori_loop(0, n_blk, body, 0)
    logits = _dot_nt(wr_ref[...], hf_ref[...], precision=lax.Precision.HIGHEST) + br_ref[...]
    lg = [logits[k:k + 1, :] for k in range(N_EGROUPS)]
    g_max, g_idx = _row_argmax_first(lg)
    g_den = lg[0] * 0.0
    for r in lg:
        g_den = g_den + jnp.exp(r - g_max)
    g_top = 1.0 / g_den
    e_in = []
    for k in range(EXPERTS_PER_GROUP):
        sel = lg[0] * 0.0
        for gi in range(N_EGROUPS):
            row = N_EGROUPS + gi * EXPERTS_PER_GROUP + k
            sel = jnp.where(g_idx == gi, logits[row:row + 1, :], sel)
        e_in.append(sel)
    e_max, i1 = _row_argmax_first(e_in)
    e_exp = [jnp.exp(r - e_max) for r in e_in]
    e_den = e_exp[0]
    for r in e_exp[1:]:
        e_den = e_den + r
    p = [r / e_den for r in e_exp]
    p1 = p[0] * 0.0
    for k in range(EXPERTS_PER_GROUP):
        p1 = jnp.where(i1 == k, p[k], p1)
    rest = [jnp.where(i1 == k, -1.0, p[k]) for k in range(EXPERTS_PER_GROUP)]
    p2, i2 = _row_argmax_first(rest)
    norm = p1 + p2
    w1 = p1 / norm * g_top
    w2 = p2 / norm * g_top
    for gi in range(N_EGROUPS):
        for k in range(EXPERTS_PER_GROUP):
            val = jnp.where(i1 == k, w1, 0.0) + jnp.where(i2 == k, w2, 0.0)
            e = gi * EXPERTS_PER_GROUP + k
            gates_ref[e:e + 1, :] = jnp.where(g_idx == gi, val, 0.0)


def _router(x, norm_g, modb, w_r, b_r, layer, tm=512):
    t_rows = x.shape[0]
    nb = tm // ROW_BLOCK
    return pl.pallas_call(
        _router_kernel,
        grid=(t_rows // tm,),
        in_specs=[
            pl.BlockSpec((tm, D_MODEL), lambda i: (i, 0)),
            pl.BlockSpec((None, 1, D_MODEL), lambda i: (layer, 0, 0)),
            pl.BlockSpec((nb, D_MODEL), lambda i: (i, 3)),
            pl.BlockSpec((nb, D_MODEL), lambda i: (i, 4)),
            pl.BlockSpec((ROUTER_ROWS, D_MODEL), lambda i: (0, 0)),
            pl.BlockSpec((ROUTER_ROWS, 1), lambda i: (0, 0)),
        ],
        out_specs=[
            pl.BlockSpec((tm, D_MODEL), lambda i: (i, 0)),
            pl.BlockSpec((N_EXPERTS, tm), lambda i: (0, i)),
        ],
        out_shape=[
            jax.ShapeDtypeStruct((t_rows, D_MODEL), BF16),
            jax.ShapeDtypeStruct((N_EXPERTS, t_rows), F32),
        ],
        scratch_shapes=[pltpu.VMEM((tm, D_MODEL), F32)],
        compiler_params=_cparams(1),
        name="router",
    )(x, norm_g.reshape(DEPTH, 1, D_MODEL), modb, modb, w_r, b_r)


def _experts_kernel(f_split, h_ref, gates_ref, wg_ref, wu_ref, wd_ref, y_ref):
    s = pl.program_id(1)
    e = s // f_split
    x = h_ref[...]
    gate_act = _dot(x, wg_ref[...].astype(BF16))
    up = _dot(x, wu_ref[...].astype(BF16))
    lane = lax.broadcasted_iota(jnp.int32, gates_ref.shape, 1)
    gate_col = jnp.sum(jnp.where(lane == e, gates_ref[...], 0.0), axis=1, keepdims=True)
    hid = (_silu(gate_act) * up * gate_col).astype(BF16)
    contrib = _dot(hid, wd_ref[...].astype(BF16))

    @pl.when(s == 0)
    def _():
        y_ref[...] = contrib

    @pl.when(s > 0)
    def _():
        y_ref[...] += contrib


def _experts(h, gates, w_gate, w_up, w_down, layer, tm=1024, f_split=2):
    t_rows = h.shape[0]
    fb = D_EXPERT // f_split
    kern = functools.partial(_experts_kernel, f_split)
    return pl.pallas_call(
        kern,
        grid=(t_rows // tm, N_EXPERTS * f_split),
        in_specs=[
            pl.BlockSpec((tm, D_MODEL), lambda i, s: (i, 0)),
            pl.BlockSpec((tm, N_EXPERTS), lambda i, s: (i, 0)),
            pl.BlockSpec((None, None, D_MODEL, fb), lambda i, s: (layer, s // f_split, 0, s % f_split)),
            pl.BlockSpec((None, None, D_MODEL, fb), lambda i, s: (layer, s // f_split, 0, s % f_split)),
            pl.BlockSpec((None, None, fb, D_MODEL), lambda i, s: (layer, s // f_split, s % f_split, 0)),
        ],
        out_specs=pl.BlockSpec((tm, D_MODEL), lambda i, s: (i, 0)),
        out_shape=jax.ShapeDtypeStruct((t_rows, D_MODEL), F32),
        compiler_params=_cparams(2),
        name="experts",
    )(h, gates, w_gate, w_up, w_down)


def _residual_kernel(final_norm, x_ref, y_ref, gate_ref, fg_ref, o_ref):
    tm, d = x_ref.shape
    nb = tm // ROW_BLOCK
    upd = y_ref[...].reshape(nb, ROW_BLOCK, d) * gate_ref[...][:, None, :]
    x = x_ref[...] + upd.reshape(tm, d)
    if final_norm:
        x = x * lax.rsqrt(jnp.mean(x * x, axis=-1, keepdims=True) + EPS) * fg_ref[...]
    o_ref[...] = x


def _residual(x, y, modb, final_g, final_norm, tm=512):
    t_rows = x.shape[0]
    nb = tm // ROW_BLOCK
    kern = functools.partial(_residual_kernel, final_norm)
    return pl.pallas_call(
        kern,
        grid=(t_rows // tm,),
        in_specs=[
            pl.BlockSpec((tm, D_MODEL), lambda i: (i, 0)),
            pl.BlockSpec((tm, D_MODEL), lambda i: (i, 0)),
            pl.BlockSpec((nb, D_MODEL), lambda i: (i, 5)),
            pl.BlockSpec((1, D_MODEL), lambda i: (0, 0)),
        ],
        out_specs=pl.BlockSpec((tm, D_MODEL), lambda i: (i, 0)),
        out_shape=jax.ShapeDtypeStruct((t_rows, D_MODEL), F32),
        compiler_params=_cparams(1),
        name="residual",
    )(x, y, modb, final_g.reshape(1, D_MODEL))


def kernel(x_prompt, x_sample, state_ssm, state_conv, c_prompt, c_sample, norm1_g, norm2_g, w_ada, b_ada,
           w_in, gm_ln_g, gm_ln_b, gm_w_s, gm_b_s, conv_w, conv_b, dt_bias, a_log, d_skip, mnorm_g, w_out,
           w_router_group, b_router_group, w_router_expert, b_router_expert, w_gate, w_up, w_down,
           final_norm_g):
    n_p, len_p, _ = x_prompt.shape
    n_s, len_s, _ = x_sample.shape
    assert len_p % GM_CHUNK == 0 and len_s == ROW_BLOCK and (n_s * len_s) % GM_CHUNK == 0
    rows_p, rows_s = n_p * len_p, n_s * len_s
    t_rows = rows_p + rows_s
    n_seq = n_p + n_s
    blk_per_prompt = len_p // ROW_BLOCK

    x = jnp.concatenate([x_prompt.reshape(rows_p, D_MODEL), x_sample.reshape(rows_s, D_MODEL)], axis=0)
    blk_seq_np = np.concatenate([np.repeat(np.arange(n_p), blk_per_prompt), n_p + np.arange(n_s)])
    blk_first_np = np.concatenate([np.tile(np.arange(blk_per_prompt) == 0, n_p), np.ones(n_s, bool)])
    blk_seq = jnp.asarray(blk_seq_np, jnp.int32)
    blk_first = jnp.asarray(blk_first_np.astype(np.int32))

    c_all = jnp.concatenate([c_prompt, c_sample], axis=0)
    c_pad = jnp.pad(c_all, ((0, (-n_seq) % 8), (0, 0)))
    mod = _ada(c_pad, w_ada, b_ada)
    modb_all = jnp.concatenate(
        [jnp.repeat(mod[:, :n_p], blk_per_prompt, axis=1), mod[:, n_p:n_seq]], axis=1)

    h0 = jnp.concatenate([jnp.zeros((DEPTH, n_p) + state_ssm.shape[2:], F32), state_ssm], axis=1)
    c0 = jnp.concatenate([jnp.zeros((DEPTH, n_p) + state_conv.shape[2:], F32), state_conv], axis=1)

    half_w = gm_w_s[:, :, :ROW_BLOCK, :ROW_BLOCK]
    w_cat = jnp.stack([gm_w_s, jnp.tile(half_w, (1, 1, 2, 2))], axis=1)
    b_t = jnp.swapaxes(gm_b_s, 1, 2)
    b_cat = jnp.stack([b_t, jnp.tile(b_t[:, :ROW_BLOCK], (1, 2, 1))], axis=1)

    w_dt = jnp.pad(w_in[:, :, MAIN_DIM:], ((0, 0), (0, 0), (0, LANES - M_HEADS)))
    w_r = jnp.concatenate([jnp.swapaxes(w_router_group, 1, 2), jnp.swapaxes(w_router_expert, 1, 2)], axis=1)
    w_r = jnp.pad(w_r, ((0, 0), (0, ROUTER_ROWS - w_r.shape[1]), (0, 0)))
    b_r = jnp.concatenate([b_router_group, b_router_expert], axis=1)
    b_r = jnp.pad(b_r, ((0, 0), (0, ROUTER_ROWS - b_r.shape[1])))[:, :, None]

    ssm_fin, conv_fin, v_rows = [], [], []
    for l in range(DEPTH):
        modb = modb_all[l]
        proj, dt_raw = _inproj(x, norm1_g, modb, w_in, w_dt[l], l)
        gm_out, vn = _gmlp(proj, gm_ln_g, gm_ln_b, w_cat, b_cat, l, rows_p // GM_CHUNK, rows_s)
        ssm_out, hfin, cfin = _ssd(proj, dt_raw, h0[l], c0[l], conv_w, conv_b, dt_bias, a_log, d_skip,
                                   mnorm_g, blk_seq, blk_first, l)
        x = _outproj(gm_out, ssm_out, w_out, x, modb, l)
        h2, gates_t = _router(x, norm2_g, modb, w_r[l], b_r[l], l)
        y = _experts(h2, gates_t.T, w_gate, w_up, w_down, l)
        x = _residual(x, y, modb, final_norm_g, l == DEPTH - 1)
        ssm_fin.append(hfin)
        conv_fin.append(cfin)
        v_rows.append(vn.reshape(n_s, len_s, GM_WIDTH))

    ssm_fin = jnp.stack(ssm_fin)
    conv_fin = jnp.stack(conv_fin)
    y_prompt = x[:rows_p].reshape(n_p, len_p, D_MODEL)
    y_sample = x[rows_p:].reshape(n_s, len_s, D_MODEL)
    return (y_prompt, y_sample, ssm_fin[:, :n_p], conv_fin[:, :n_p], ssm_fin[:, n_p:], conv_fin[:, n_p:],
            jnp.stack(v_rows))
```

```python
import functools
import math

import jax
import jax.numpy as jnp
import numpy as np
from jax import lax
from jax.experimental import pallas as pl
from jax.experimental.pallas import tpu as pltpu

D_MODEL = 2048
DEPTH = 2
GM_WIDTH = 1024
GM_HEADS = 8
GM_HEAD_DIM = 128
GM_CHUNK = 128
M_WIDTH = 1024
M_HEAD_DIM = 64
M_HEADS = 16
M_GROUPS = 4
M_STATE = 128
M_CONV = 4
CONV_DIM = M_WIDTH + 2 * M_GROUPS * M_STATE
MAIN_DIM = 2 * GM_WIDTH + M_WIDTH + CONV_DIM
N_EGROUPS = 4
EXPERTS_PER_GROUP = 4
N_EXPERTS = 16
D_EXPERT = 512
N_MOD = 6
EPS = 1e-6

ROW_BLOCK = 64
LANES = 128
VMEM_LIMIT = 56 * 1024 * 1024

F32 = jnp.float32
BF16 = jnp.bfloat16


def _cparams(n_axes):
    return pltpu.CompilerParams(
        dimension_semantics=("arbitrary",) * n_axes, vmem_limit_bytes=VMEM_LIMIT)


def _silu(x):
    return x * jax.nn.sigmoid(x)


def _gelu_tanh(x):
    c = math.sqrt(2.0 / math.pi)
    return 0.5 * x * (1.0 + jnp.tanh(c * (x + 0.044715 * (x * x * x))))


def _dot(a, b):
    return jnp.dot(a, b, preferred_element_type=F32)


def _dot_nt(a, b, precision=None):
    return lax.dot_general(a, b, (((1,), (1,)), ((), ())), precision=precision,
                           preferred_element_type=F32)


def _dot_tn(a, b):
    return lax.dot_general(a, b, (((0,), (0,)), ((), ())), preferred_element_type=F32)


def _ada_kernel(c_ref, w_ref, b_ref, o_ref):
    c = _silu(c_ref[...]).astype(BF16)
    o_ref[...] = _dot(c, w_ref[...].astype(BF16)) + b_ref[...]


def _ada(c_pad, w_ada, b_ada):
    rows = c_pad.shape[0]
    tn = 1024
    n_out = N_MOD * D_MODEL
    return pl.pallas_call(
        _ada_kernel,
        grid=(DEPTH, n_out // tn),
        in_specs=[
            pl.BlockSpec((rows, D_MODEL), lambda l, j: (0, 0)),
            pl.BlockSpec((None, D_MODEL, tn), lambda l, j: (l, 0, j)),
            pl.BlockSpec((None, 1, tn), lambda l, j: (l, 0, j)),
        ],
        out_specs=pl.BlockSpec((None, rows, tn), lambda l, j: (l, 0, j)),
        out_shape=jax.ShapeDtypeStruct((DEPTH, rows, n_out), F32),
        compiler_params=_cparams(2),
        name="ada_ln",
    )(c_pad, w_ada, b_ada.reshape(DEPTH, 1, n_out))


def _inproj_kernel(x_ref, g_ref, shift_ref, scale_ref, w_ref, wdt_ref, proj_ref, dt_ref, h_ref):
    j = pl.program_id(1)
    n_blk = shift_ref.shape[0]

    @pl.when(j == 0)
    def _():
        g = g_ref[...]

        def body(r, carry):
            rows = pl.ds(pl.multiple_of(r * ROW_BLOCK, ROW_BLOCK), ROW_BLOCK)
            xf = x_ref[rows, :]
            y = xf * lax.rsqrt(jnp.mean(xf * xf, axis=-1, keepdims=True) + EPS) * g
            y = y * (1.0 + scale_ref[pl.ds(r, 1), :]) + shift_ref[pl.ds(r, 1), :]
            h_ref[rows, :] = y.astype(BF16)
            return carry

        lax.fori_loop(0, n_blk, body, 0)
        dt_ref[...] = _dot(h_ref[...], wdt_ref[...].astype(BF16))

    proj_ref[...] = _dot(h_ref[...], w_ref[...].astype(BF16)).astype(BF16)


def _inproj(x, norm_g, modb, w_in, w_dt, layer, tm=1024, tn=512):
    t_rows = x.shape[0]
    nb = tm // ROW_BLOCK
    return pl.pallas_call(
        _inproj_kernel,
        grid=(t_rows // tm, MAIN_DIM // tn),
        in_specs=[
            pl.BlockSpec((tm, D_MODEL), lambda i, j: (i, 0)),
            pl.BlockSpec((None, 1, D_MODEL), lambda i, j: (layer, 0, 0)),
            pl.BlockSpec((nb, D_MODEL), lambda i, j: (i, 0)),
            pl.BlockSpec((nb, D_MODEL), lambda i, j: (i, 1)),
            pl.BlockSpec((None, D_MODEL, tn), lambda i, j: (layer, 0, j)),
            pl.BlockSpec((D_MODEL, LANES), lambda i, j: (0, 0)),
        ],
        out_specs=[
            pl.BlockSpec((tm, tn), lambda i, j: (i, j)),
            pl.BlockSpec((tm, LANES), lambda i, j: (i, 0)),
        ],
        out_shape=[
            jax.ShapeDtypeStruct((t_rows, MAIN_DIM), BF16),
            jax.ShapeDtypeStruct((t_rows, LANES), F32),
        ],
        scratch_shapes=[pltpu.VMEM((tm, D_MODEL), BF16)],
        compiler_params=_cparams(2),
        name="in_proj",
    )(x, norm_g.reshape(DEPTH, 1, D_MODEL), modb, modb, w_in, w_dt)


def _gmlp_kernel(n_prompt_tiles, u_ref, v_ref, lng_ref, lnb_ref, w_ref, b_ref, out_ref, vn_ref):
    i = pl.program_id(0)
    u = _gelu_tanh(u_ref[...].astype(F32))
    v = _gelu_tanh(v_ref[...].astype(F32))
    mu = jnp.mean(v, axis=-1, keepdims=True)
    vc = v - mu
    vn = vc * lax.rsqrt(jnp.mean(vc * vc, axis=-1, keepdims=True) + EPS)
    vn = vn * lng_ref[...] + lnb_ref[...]
    vn_ref[...] = vn
    t_idx = lax.broadcasted_iota(jnp.int32, (GM_CHUNK, GM_CHUNK), 0)
    s_idx = lax.broadcasted_iota(jnp.int32, (GM_CHUNK, GM_CHUNK), 1)
    same_seq = (t_idx // ROW_BLOCK) == (s_idx // ROW_BLOCK)
    allowed = (s_idx <= t_idx) & (same_seq | (i < n_prompt_tiles))
    vnb = vn.astype(BF16)
    bias = b_ref[...]
    for h in range(GM_HEADS):
        cols = slice(h * GM_HEAD_DIM, (h + 1) * GM_HEAD_DIM)
        wm = jnp.where(allowed, w_ref[h], 0.0).astype(BF16)
        mixed = _dot(wm, vnb[:, cols]) + bias[:, h:h + 1]
        out_ref[:, cols] = (u[:, cols] * mixed).astype(BF16)


def _gmlp(proj, ln_g, ln_b, w_cat, b_cat, layer, n_prompt_tiles, n_sample_rows):
    t_rows = proj.shape[0]
    n_tiles = t_rows // GM_CHUNK
    kern = functools.partial(_gmlp_kernel, n_prompt_tiles)
    return pl.pallas_call(
        kern,
        grid=(n_tiles,),
        in_specs=[
            pl.BlockSpec((GM_CHUNK, GM_WIDTH), lambda i: (i, 0)),
            pl.BlockSpec((GM_CHUNK, GM_WIDTH), lambda i: (i, 1)),
            pl.BlockSpec((None, 1, GM_WIDTH), lambda i: (layer, 0, 0)),
            pl.BlockSpec((None, 1, GM_WIDTH), lambda i: (layer, 0, 0)),
            pl.BlockSpec((None, None, GM_HEADS, GM_CHUNK, GM_CHUNK),
                         lambda i: (layer, jnp.where(i < n_prompt_tiles, 0, 1), 0, 0, 0)),
            pl.BlockSpec((None, None, GM_CHUNK, GM_HEADS),
                         lambda i: (layer, jnp.where(i < n_prompt_tiles, 0, 1), 0, 0)),
        ],
        out_specs=[
            pl.BlockSpec((GM_CHUNK, GM_WIDTH), lambda i: (i, 0)),
            pl.BlockSpec((GM_CHUNK, GM_WIDTH), lambda i: (jnp.maximum(i - n_prompt_tiles, 0), 0)),
        ],
        out_shape=[
            jax.ShapeDtypeStruct((t_rows, GM_WIDTH), BF16),
            jax.ShapeDtypeStruct((n_sample_rows, GM_WIDTH), F32),
        ],
        compiler_params=_cparams(1),
        name="gmlp",
    )(proj, proj, ln_g.reshape(DEPTH, 1, GM_WIDTH), ln_b.reshape(DEPTH, 1, GM_WIDTH), w_cat, b_cat)


CONV_PAD = 8


def _ssd_kernel(seq_ref, first_ref, z_ref, xa_ref, xb_ref, dt_ref, h0_ref, c0_ref, cw_ref, cb_ref,
                dtb_ref, alog_ref, dskip_ref, ng_ref, out_ref, hfin_ref, cfin_ref,
                state_ref, win_ref, y_ref):
    i = pl.program_id(0)
    tail = M_CONV - 1
    lo = CONV_PAD - tail

    @pl.when(first_ref[i] == 1)
    def _():
        state_ref[...] = h0_ref[...]
        win_ref[lo:CONV_PAD, :] = c0_ref[...]

    half = CONV_DIM // 2
    win_ref[CONV_PAD:CONV_PAD + ROW_BLOCK, 0:half] = xa_ref[...].astype(F32)
    win_ref[CONV_PAD:CONV_PAD + ROW_BLOCK, half:CONV_DIM] = xb_ref[...].astype(F32)
    conv = cb_ref[...]
    for k in range(M_CONV):
        conv = conv + win_ref[lo + k:lo + k + ROW_BLOCK, :] * cw_ref[k:k + 1, :]
    new_tail = win_ref[CONV_PAD + ROW_BLOCK - tail:CONV_PAD + ROW_BLOCK, :]
    cfin_ref[...] = new_tail
    win_ref[lo:CONV_PAD, :] = new_tail
    xbc = _silu(conv)

    dt = jax.nn.softplus(dt_ref[:, 0:M_HEADS] + dtb_ref[...])
    a = -jnp.exp(alog_ref[...])
    a_dt = dt * a
    r_idx = lax.broadcasted_iota(jnp.int32, (ROW_BLOCK, ROW_BLOCK), 0)
    c_idx = lax.broadcasted_iota(jnp.int32, (ROW_BLOCK, ROW_BLOCK), 1)
    causal = c_idx <= r_idx
    tri = causal.astype(F32)
    a_cs = jnp.dot(tri, a_dt, precision=lax.Precision.HIGHEST, preferred_element_type=F32)
    eye = (lax.broadcasted_iota(jnp.int32, (M_HEADS, M_HEADS), 0)
           == lax.broadcasted_iota(jnp.int32, (M_HEADS, M_HEADS), 1)).astype(F32)
    a_cs_t = _dot_nt(eye, a_cs, precision=lax.Precision.HIGHEST)
    a_end = a_cs[ROW_BLOCK - 1:ROW_BLOCK, :]
    decay_in = jnp.exp(a_cs)
    decay_out = jnp.exp(a_end - a_cs)
    decay_blk = jnp.exp(a_end)
    dskip = dskip_ref[...]

    heads_per_group = M_HEADS // M_GROUPS
    for g in range(M_GROUPS):
        b_g = xbc[:, M_WIDTH + g * M_STATE:M_WIDTH + (g + 1) * M_STATE].astype(BF16)
        c_off = M_WIDTH + M_GROUPS * M_STATE
        c_g = xbc[:, c_off + g * M_STATE:c_off + (g + 1) * M_STATE].astype(BF16)
        cb = _dot_nt(c_g, b_g)
        for hh in range(heads_per_group):
            h = g * heads_per_group + hh
            cols = slice(h * M_HEAD_DIM, (h + 1) * M_HEAD_DIM)
            xs_h = xbc[:, cols]
            seg = a_cs[:, h:h + 1] - a_cs_t[h:h + 1, :]
            scores = cb * jnp.exp(jnp.where(causal, seg, -jnp.inf))
            x_dt = xs_h * dt[:, h:h + 1]
            y = _dot(scores.astype(BF16), x_dt.astype(BF16))
            st = state_ref[h]
            y = y + _dot_nt(c_g, st.astype(BF16)) * decay_in[:, h:h + 1]
            upd = _dot_tn((x_dt * decay_out[:, h:h + 1]).astype(BF16), b_g)
            state_ref[h] = st * decay_blk[:, h:h + 1] + upd
            y_ref[:, cols] = y + dskip[:, h:h + 1] * xs_h

    hfin_ref[...] = state_ref[...]
    gy = y_ref[...] * _silu(z_ref[...].astype(F32))
    gw = M_WIDTH // M_GROUPS
    ng = ng_ref[...]
    for g in range(M_GROUPS):
        cols = slice(g * gw, (g + 1) * gw)
        s = gy[:, cols]
        s = s * lax.rsqrt(jnp.mean(s * s, axis=-1, keepdims=True) + EPS)
        out_ref[:, cols] = (s * ng[:, cols]).astype(BF16)


def _ssd(proj, dt_raw, h0, c0, conv_w, conv_b, dt_bias, a_log, d_skip, mnorm_g, blk_seq, blk_first, layer):
    t_rows = proj.shape[0]
    n_blk = t_rows // ROW_BLOCK
    n_seq = h0.shape[0]
    half = CONV_DIM // 2
    z_blk = (2 * GM_WIDTH) // M_WIDTH
    xa_blk = (2 * GM_WIDTH + M_WIDTH) // half
    grid_spec = pltpu.PrefetchScalarGridSpec(
        num_scalar_prefetch=2,
        grid=(n_blk,),
        in_specs=[
            pl.BlockSpec((ROW_BLOCK, M_WIDTH), lambda i, s, f: (i, z_blk)),
            pl.BlockSpec((ROW_BLOCK, half), lambda i, s, f: (i, xa_blk)),
            pl.BlockSpec((ROW_BLOCK, half), lambda i, s, f: (i, xa_blk + 1)),
            pl.BlockSpec((ROW_BLOCK, LANES), lambda i, s, f: (i, 0)),
            pl.BlockSpec((None, M_HEADS, M_HEAD_DIM, M_STATE), lambda i, s, f: (s[i], 0, 0, 0)),
            pl.BlockSpec((None, M_CONV - 1, CONV_DIM), lambda i, s, f: (s[i], 0, 0)),
            pl.BlockSpec((None, M_CONV, CONV_DIM), lambda i, s, f: (layer, 0, 0)),
            pl.BlockSpec((None, 1, CONV_DIM), lambda i, s, f: (layer, 0, 0)),
            pl.BlockSpec((None, 1, M_HEADS), lambda i, s, f: (layer, 0, 0)),
            pl.BlockSpec((None, 1, M_HEADS), lambda i, s, f: (layer, 0, 0)),
            pl.BlockSpec((None, 1, M_HEADS), lambda i, s, f: (layer, 0, 0)),
            pl.BlockSpec((None, 1, M_WIDTH), lambda i, s, f: (layer, 0, 0)),
        ],
        out_specs=[
            pl.BlockSpec((ROW_BLOCK, M_WIDTH), lambda i, s, f: (i, 0)),
            pl.BlockSpec((None, M_HEADS, M_HEAD_DIM, M_STATE), lambda i, s, f: (s[i], 0, 0, 0)),
            pl.BlockSpec((None, M_CONV - 1, CONV_DIM), lambda i, s, f: (s[i], 0, 0)),
        ],
        scratch_shapes=[
            pltpu.VMEM((M_HEADS, M_HEAD_DIM, M_STATE), F32),
            pltpu.VMEM((CONV_PAD + ROW_BLOCK, CONV_DIM), F32),
            pltpu.VMEM((ROW_BLOCK, M_WIDTH), F32),
        ],
    )
    return pl.pallas_call(
        _ssd_kernel,
        grid_spec=grid_spec,
        out_shape=[
            jax.ShapeDtypeStruct((t_rows, M_WIDTH), BF16),
            jax.ShapeDtypeStruct((n_seq, M_HEADS, M_HEAD_DIM, M_STATE), F32),
            jax.ShapeDtypeStruct((n_seq, M_CONV - 1, CONV_DIM), F32),
        ],
        compiler_params=_cparams(1),
        name="conv_ssd",
    )(blk_seq, blk_first, proj, proj, proj, dt_raw, h0, c0, conv_w,
      conv_b.reshape(DEPTH, 1, CONV_DIM), dt_bias.reshape(DEPTH, 1, M_HEADS),
      a_log.reshape(DEPTH, 1, M_HEADS), d_skip.reshape(DEPTH, 1, M_HEADS),
      mnorm_g.reshape(DEPTH, 1, M_WIDTH))


def _outproj_kernel(gm_ref, ssm_ref, wa_ref, wb_ref, x_ref, gate_ref, o_ref):
    acc = _dot(gm_ref[...], wa_ref[...].astype(BF16)) + _dot(ssm_ref[...], wb_ref[...].astype(BF16))
    tm, tn = acc.shape
    nb = tm // ROW_BLOCK
    upd = acc.reshape(nb, ROW_BLOCK, tn) * gate_ref[...][:, None, :]
    o_ref[...] = x_ref[...] + upd.reshape(tm, tn)


def _outproj(gm, ssm, w_out, x, modb, layer, tm=1024, tn=512):
    t_rows = x.shape[0]
    nb = tm // ROW_BLOCK
    gate_blk = 2 * D_MODEL // tn
    return pl.pallas_call(
        _outproj_kernel,
        grid=(t_rows // tm, D_MODEL // tn),
        in_specs=[
            pl.BlockSpec((tm, GM_WIDTH), lambda i, j: (i, 0)),
            pl.BlockSpec((tm, M_WIDTH), lambda i, j: (i, 0)),
            pl.BlockSpec((None, GM_WIDTH, tn), lambda i, j: (layer, 0, j)),
            pl.BlockSpec((None, M_WIDTH, tn), lambda i, j: (layer, 1, j)),
            pl.BlockSpec((tm, tn), lambda i, j: (i, j)),
            pl.BlockSpec((nb, tn), lambda i, j: (i, gate_blk + j)),
        ],
        out_specs=pl.BlockSpec((tm, tn), lambda i, j: (i, j)),
        out_shape=jax.ShapeDtypeStruct((t_rows, D_MODEL), F32),
        compiler_params=_cparams(2),
        name="out_proj",
    )(gm, ssm, w_out, w_out, x, modb)


ROUTER_ROWS = 24
ROUTE_ROWS = 8
EXPERT_TILE = 512


def _row_argmax_first(rows):
    m = rows[0]
    for r in rows[1:]:
        m = jnp.maximum(m, r)
    idx = jnp.full(m.shape, len(rows), jnp.int32)
    for k in reversed(range(len(rows))):
        idx = jnp.where(rows[k] == m, k, idx)
    return m, idx


def _router_kernel(x_ref, g_ref, shift_ref, scale_ref, wr_ref, br_ref, h_ref, route_ref, count_ref, run_ref):
    n_blk = shift_ref.shape[0]
    g = g_ref[...]

    @pl.when(pl.program_id(0) == 0)
    def _():
        run_ref[...] = jnp.zeros_like(run_ref)

    def body(r, carry):
        rows = pl.ds(pl.multiple_of(r * ROW_BLOCK, ROW_BLOCK), ROW_BLOCK)
        xf = x_ref[rows, :]
        y = xf * lax.rsqrt(jnp.mean(xf * xf, axis=-1, keepdims=True) + EPS) * g
        h_ref[rows, :] = y * (1.0 + scale_ref[pl.ds(r, 1), :]) + shift_ref[pl.ds(r, 1), :]
        return carry

    lax.fori_loop(0, n_blk, body, 0)
    logits = _dot_nt(wr_ref[...], h_ref[...], precision=lax.Precision.HIGHEST) + br_ref[...]
    lg = [logits[k:k + 1, :] for k in range(N_EGROUPS)]
    g_max, g_idx = _row_argmax_first(lg)
    g_den = lg[0] * 0.0
    for r in lg:
        g_den = g_den + jnp.exp(r - g_max)
    g_top = 1.0 / g_den
    e_in = []
    for k in range(EXPERTS_PER_GROUP):
        sel = lg[0] * 0.0
        for gi in range(N_EGROUPS):
            row = N_EGROUPS + gi * EXPERTS_PER_GROUP + k
            sel = jnp.where(g_idx == gi, logits[row:row + 1, :], sel)
        e_in.append(sel)
    e_max, i1 = _row_argmax_first(e_in)
    e_exp = [jnp.exp(r - e_max) for r in e_in]
    e_den = e_exp[0]
    for r in e_exp[1:]:
        e_den = e_den + r
    p = [r / e_den for r in e_exp]
    p1 = p[0] * 0.0
    for k in range(EXPERTS_PER_GROUP):
        p1 = jnp.where(i1 == k, p[k], p1)
    rest = [jnp.where(i1 == k, -1.0, p[k]) for k in range(EXPERTS_PER_GROUP)]
    p2, i2 = _row_argmax_first(rest)
    norm = p1 + p2
    w1 = p1 / norm * g_top
    w2 = p2 / norm * g_top
    tm = logits.shape[1]
    row = lax.broadcasted_iota(jnp.int32, (ROUTE_ROWS, tm), 0)
    onehot = (row == g_idx).astype(F32)
    before = (lax.broadcasted_iota(jnp.int32, (tm, tm), 0)
              < lax.broadcasted_iota(jnp.int32, (tm, tm), 1)).astype(BF16)
    earlier = _dot(onehot.astype(BF16), before)
    rank = jnp.sum(onehot * (earlier + run_ref[:, 0:1]), axis=0, keepdims=True)
    run_ref[...] += jnp.sum(onehot, axis=1, keepdims=True)
    count_ref[...] = run_ref[...]

    route_ref[0:1, :] = g_idx.astype(F32)
    route_ref[1:2, :] = rank
    for k in range(EXPERTS_PER_GROUP):
        route_ref[2 + k:3 + k, :] = jnp.where(i1 == k, w1, 0.0) + jnp.where(i2 == k, w2, 0.0)
    route_ref[2 + EXPERTS_PER_GROUP:ROUTE_ROWS, :] = jnp.zeros((ROUTE_ROWS - 2 - EXPERTS_PER_GROUP, tm), F32)


def _router(x, norm_g, modb, w_r, b_r, layer, tm=512):
    t_rows = x.shape[0]
    nb = tm // ROW_BLOCK
    return pl.pallas_call(
        _router_kernel,
        grid=(t_rows // tm,),
        in_specs=[
            pl.BlockSpec((tm, D_MODEL), lambda i: (i, 0)),
            pl.BlockSpec((None, 1, D_MODEL), lambda i: (layer, 0, 0)),
            pl.BlockSpec((nb, D_MODEL), lambda i: (i, 3)),
            pl.BlockSpec((nb, D_MODEL), lambda i: (i, 4)),
            pl.BlockSpec((ROUTER_ROWS, D_MODEL), lambda i: (0, 0)),
            pl.BlockSpec((ROUTER_ROWS, 1), lambda i: (0, 0)),
        ],
        out_specs=[
            pl.BlockSpec((tm, D_MODEL), lambda i: (i, 0)),
            pl.BlockSpec((ROUTE_ROWS, tm), lambda i: (0, i)),
            pl.BlockSpec((ROUTE_ROWS, LANES), lambda i: (0, 0)),
        ],
        out_shape=[
            jax.ShapeDtypeStruct((t_rows, D_MODEL), F32),
            jax.ShapeDtypeStruct((ROUTE_ROWS, t_rows), F32),
            jax.ShapeDtypeStruct((ROUTE_ROWS, LANES), F32),
        ],
        scratch_shapes=[pltpu.VMEM((ROUTE_ROWS, LANES), F32)],
        compiler_params=_cparams(1),
        name="router",
    )(x, norm_g.reshape(DEPTH, 1, D_MODEL), modb, modb, w_r, b_r)


def _dispatch_tables(route, counts, t_rows, tm, n_tiles):
    gid = route[0].astype(jnp.int32)
    rank = route[1].astype(jnp.int32)
    cnt = counts[:N_EGROUPS, 0].astype(jnp.int32)
    padded = (cnt + tm - 1) // tm * tm
    ends = jnp.cumsum(padded)
    starts = ends - padded
    pos = starts[gid] + rank
    src = jnp.zeros((n_tiles * tm,), jnp.int32).at[pos].set(jnp.arange(t_rows, dtype=jnp.int32))
    tile_start = jnp.arange(n_tiles, dtype=jnp.int32) * tm
    tile_valid = (tile_start < ends[-1]).astype(jnp.int32)
    tile_group = jnp.sum((tile_start[:, None] >= ends[None, :]).astype(jnp.int32), axis=1)
    tile_group = jnp.minimum(tile_group, N_EGROUPS - 1)
    gates_sorted = jnp.take(route[2:2 + EXPERTS_PER_GROUP].T, src, axis=0)
    return pos, src, tile_group, tile_valid, gates_sorted


GATHER_CHUNK = 512


def _row_copy(src_ref, src_row, dst_ref, dst_row, sem):
    return pltpu.make_async_copy(src_ref.at[pl.ds(src_row, 1), :], dst_ref.at[pl.ds(dst_row, 1), :], sem)


def _gather_kernel(idx_ref, table_ref, out_ref, sem):
    base = pl.program_id(0) * GATHER_CHUNK

    def issue(r, carry):
        _row_copy(table_ref, idx_ref[base + r], out_ref, base + r, sem).start()
        return carry

    lax.fori_loop(0, GATHER_CHUNK, issue, 0, unroll=8)
    pltpu.make_async_copy(table_ref.at[pl.ds(0, GATHER_CHUNK), :],
                          out_ref.at[pl.ds(base, GATHER_CHUNK), :], sem).wait()


def _gather_rows(table, idx):
    n_out = idx.shape[0]
    grid_spec = pltpu.PrefetchScalarGridSpec(
        num_scalar_prefetch=1,
        grid=(n_out // GATHER_CHUNK,),
        in_specs=[pl.BlockSpec(memory_space=pl.ANY)],
        out_specs=pl.BlockSpec(memory_space=pl.ANY),
        scratch_shapes=[pltpu.SemaphoreType.DMA(())],
    )
    return pl.pallas_call(
        _gather_kernel,
        grid_spec=grid_spec,
        out_shape=jax.ShapeDtypeStruct((n_out, table.shape[1]), table.dtype),
        compiler_params=_cparams(1),
        name="gather_rows",
    )(idx, table)


def _experts_kernel(f_split, tg_ref, tv_ref, x_ref, gates_ref, wg_ref, wu_ref, wd_ref, y_ref, xb_ref):
    i = pl.program_id(0)
    s = pl.program_id(1)
    valid = tv_ref[i] == 1

    @pl.when(s == 0)
    def _():
        xb_ref[...] = x_ref[...].astype(BF16)
        y_ref[...] = jnp.zeros_like(y_ref)

    @pl.when(valid)
    def _():
        e = s // f_split
        x = xb_ref[...]
        gate_act = _dot(x, wg_ref[...].astype(BF16))
        up = _dot(x, wu_ref[...].astype(BF16))
        lane = lax.broadcasted_iota(jnp.int32, gates_ref.shape, 1)
        gate_col = jnp.sum(jnp.where(lane == e, gates_ref[...], 0.0), axis=1, keepdims=True)
        hid = (_silu(gate_act) * up * gate_col).astype(BF16)
        y_ref[...] += _dot(hid, wd_ref[...].astype(BF16))


def _experts(xs, gates_sorted, tile_group, tile_valid, w_gate, w_up, w_down, layer, tm, f_split=2):
    n_rows = xs.shape[0]
    fb = D_EXPERT // f_split
    last = EXPERTS_PER_GROUP * f_split - 1

    def expert_of(i, s, tg, tv):
        return tg[i] * EXPERTS_PER_GROUP + jnp.where(tv[i] == 1, s, last) // f_split

    def half_of(i, s, tg, tv):
        return jnp.where(tv[i] == 1, s, last) % f_split

    grid_spec = pltpu.PrefetchScalarGridSpec(
        num_scalar_prefetch=2,
        grid=(n_rows // tm, EXPERTS_PER_GROUP * f_split),
        in_specs=[
            pl.BlockSpec((tm, D_MODEL), lambda i, s, tg, tv: (i, 0)),
            pl.BlockSpec((tm, EXPERTS_PER_GROUP), lambda i, s, tg, tv: (i, 0)),
            pl.BlockSpec((None, None, D_MODEL, fb),
                         lambda i, s, tg, tv: (layer, expert_of(i, s, tg, tv), 0, half_of(i, s, tg, tv))),
            pl.BlockSpec((None, None, D_MODEL, fb),
                         lambda i, s, tg, tv: (layer, expert_of(i, s, tg, tv), 0, half_of(i, s, tg, tv))),
            pl.BlockSpec((None, None, fb, D_MODEL),
                         lambda i, s, tg, tv: (layer, expert_of(i, s, tg, tv), half_of(i, s, tg, tv), 0)),
        ],
        out_specs=pl.BlockSpec((tm, D_MODEL), lambda i, s, tg, tv: (i, 0)),
        scratch_shapes=[pltpu.VMEM((tm, D_MODEL), BF16)],
    )
    return pl.pallas_call(
        functools.partial(_experts_kernel, f_split),
        grid_spec=grid_spec,
        out_shape=jax.ShapeDtypeStruct((n_rows, D_MODEL), F32),
        compiler_params=_cparams(2),
        name="experts",
    )(tile_group, tile_valid, xs, gates_sorted, w_gate, w_up, w_down)


def _combine_kernel(final_norm, pos_ref, x_ref, gate_ref, fg_ref, ys_ref, o_ref, buf_ref, sem):
    tm, d = x_ref.shape
    base = pl.program_id(0) * tm

    def issue(r, carry):
        _row_copy(ys_ref, pos_ref[base + r], buf_ref, r, sem).start()
        return carry

    lax.fori_loop(0, tm, issue, 0, unroll=8)
    pltpu.make_async_copy(ys_ref.at[pl.ds(0, tm), :], buf_ref, sem).wait()
    nb = tm // ROW_BLOCK
    upd = buf_ref[...].reshape(nb, ROW_BLOCK, d) * gate_ref[...][:, None, :]
    x = x_ref[...] + upd.reshape(tm, d)
    if final_norm:
        x = x * lax.rsqrt(jnp.mean(x * x, axis=-1, keepdims=True) + EPS) * fg_ref[...]
    o_ref[...] = x


def _combine(x, ys, pos, modb, final_g, final_norm, tm=512):
    t_rows = x.shape[0]
    nb = tm // ROW_BLOCK
    grid_spec = pltpu.PrefetchScalarGridSpec(
        num_scalar_prefetch=1,
        grid=(t_rows // tm,),
        in_specs=[
            pl.BlockSpec((tm, D_MODEL), lambda i, p: (i, 0)),
            pl.BlockSpec((nb, D_MODEL), lambda i, p: (i, 5)),
            pl.BlockSpec((1, D_MODEL), lambda i, p: (0, 0)),
            pl.BlockSpec(memory_space=pl.ANY),
        ],
        out_specs=pl.BlockSpec((tm, D_MODEL), lambda i, p: (i, 0)),
        scratch_shapes=[pltpu.VMEM((tm, D_MODEL), F32), pltpu.SemaphoreType.DMA(())],
    )
    return pl.pallas_call(
        functools.partial(_combine_kernel, final_norm),
        grid_spec=grid_spec,
        out_shape=jax.ShapeDtypeStruct((t_rows, D_MODEL), F32),
        compiler_params=_cparams(1),
        name="combine",
    )(pos, x, modb, final_g.reshape(1, D_MODEL), ys)


def kernel(x_prompt, x_sample, state_ssm, state_conv, c_prompt, c_sample, norm1_g, norm2_g, w_ada, b_ada,
           w_in, gm_ln_g, gm_ln_b, gm_w_s, gm_b_s, conv_w, conv_b, dt_bias, a_log, d_skip, mnorm_g, w_out,
           w_router_group, b_router_group, w_router_expert, b_router_expert, w_gate, w_up, w_down,
           final_norm_g):
    n_p, len_p, _ = x_prompt.shape
    n_s, len_s, _ = x_sample.shape
    assert len_p % GM_CHUNK == 0 and len_s == ROW_BLOCK and (n_s * len_s) % GM_CHUNK == 0
    rows_p, rows_s = n_p * len_p, n_s * len_s
    t_rows = rows_p + rows_s
    n_seq = n_p + n_s
    blk_per_prompt = len_p // ROW_BLOCK
    assert t_rows % EXPERT_TILE == 0
    n_expert_tiles = t_rows // EXPERT_TILE + N_EGROUPS

    x = jnp.concatenate([x_prompt.reshape(rows_p, D_MODEL), x_sample.reshape(rows_s, D_MODEL)], axis=0)
    blk_seq_np = np.concatenate([np.repeat(np.arange(n_p), blk_per_prompt), n_p + np.arange(n_s)])
    blk_first_np = np.concatenate([np.tile(np.arange(blk_per_prompt) == 0, n_p), np.ones(n_s, bool)])
    blk_seq = jnp.asarray(blk_seq_np, jnp.int32)
    blk_first = jnp.asarray(blk_first_np.astype(np.int32))

    c_all = jnp.concatenate([c_prompt, c_sample], axis=0)
    c_pad = jnp.pad(c_all, ((0, (-n_seq) % 8), (0, 0)))
    mod = _ada(c_pad, w_ada, b_ada)
    modb_all = jnp.concatenate(
        [jnp.repeat(mod[:, :n_p], blk_per_prompt, axis=1), mod[:, n_p:n_seq]], axis=1)

    h0 = jnp.concatenate([jnp.zeros((DEPTH, n_p) + state_ssm.shape[2:], F32), state_ssm], axis=1)
    c0 = jnp.concatenate([jnp.zeros((DEPTH, n_p) + state_conv.shape[2:], F32), state_conv], axis=1)

    half_w = gm_w_s[:, :, :ROW_BLOCK, :ROW_BLOCK]
    w_cat = jnp.stack([gm_w_s, jnp.tile(half_w, (1, 1, 2, 2))], axis=1)
    b_t = jnp.swapaxes(gm_b_s, 1, 2)
    b_cat = jnp.stack([b_t, jnp.tile(b_t[:, :ROW_BLOCK], (1, 2, 1))], axis=1)

    w_dt = jnp.pad(w_in[:, :, MAIN_DIM:], ((0, 0), (0, 0), (0, LANES - M_HEADS)))
    w_r = jnp.concatenate([jnp.swapaxes(w_router_group, 1, 2), jnp.swapaxes(w_router_expert, 1, 2)], axis=1)
    w_r = jnp.pad(w_r, ((0, 0), (0, ROUTER_ROWS - w_r.shape[1]), (0, 0)))
    b_r = jnp.concatenate([b_router_group, b_router_expert], axis=1)
    b_r = jnp.pad(b_r, ((0, 0), (0, ROUTER_ROWS - b_r.shape[1])))[:, :, None]

    ssm_fin, conv_fin, v_rows = [], [], []
    for l in range(DEPTH):
        modb = modb_all[l]
        proj, dt_raw = _inproj(x, norm1_g, modb, w_in, w_dt[l], l)
        gm_out, vn = _gmlp(proj, gm_ln_g, gm_ln_b, w_cat, b_cat, l, rows_p // GM_CHUNK, rows_s)
        ssm_out, hfin, cfin = _ssd(proj, dt_raw, h0[l], c0[l], conv_w, conv_b, dt_bias, a_log, d_skip,
                                   mnorm_g, blk_seq, blk_first, l)
        x = _outproj(gm_out, ssm_out, w_out, x, modb, l)
        h2, route, counts = _router(x, norm2_g, modb, w_r[l], b_r[l], l)
        pos, src, tile_group, tile_valid, gates_sorted = _dispatch_tables(
            route, counts, t_rows, EXPERT_TILE, n_expert_tiles)
        xs = _gather_rows(h2, src)
        ys = _experts(xs, gates_sorted, tile_group, tile_valid, w_gate, w_up, w_down, l, EXPERT_TILE)
        x = _combine(x, ys, pos, modb, final_norm_g, l == DEPTH - 1)
        ssm_fin.append(hfin)
        conv_fin.append(cfin)
        v_rows.append(vn.reshape(n_s, len_s, GM_WIDTH))

    ssm_fin = jnp.stack(ssm_fin)
    conv_fin = jnp.stack(conv_fin)
    y_prompt = x[:rows_p].reshape(n_p, len_p, D_MODEL)
    y_sample = x[rows_p:].reshape(n_s, len_s, D_MODEL)
    return (y_prompt, y_sample, ssm_fin[:, :n_p], conv_fin[:, :n_p], ssm_fin[:, n_p:], conv_fin[:, n_p:],
            jnp.stack(v_rows))
```

```python
import functools
import math

import jax
import jax.numpy as jnp
import numpy as np
from jax import lax
from jax.experimental import pallas as pl
from jax.experimental.pallas import tpu as pltpu

D_MODEL = 2048
DEPTH = 2
GM_WIDTH = 1024
GM_HEADS = 8
GM_HEAD_DIM = 128
GM_CHUNK = 128
M_WIDTH = 1024
M_HEAD_DIM = 64
M_HEADS = 16
M_GROUPS = 4
M_STATE = 128
M_CONV = 4
CONV_DIM = M_WIDTH + 2 * M_GROUPS * M_STATE
MAIN_DIM = 2 * GM_WIDTH + M_WIDTH + CONV_DIM
N_EGROUPS = 4
EXPERTS_PER_GROUP = 4
N_EXPERTS = 16
D_EXPERT = 512
N_MOD = 6
EPS = 1e-6

ROW_BLOCK = 64
LANES = 128
VMEM_LIMIT = 56 * 1024 * 1024

F32 = jnp.float32
BF16 = jnp.bfloat16


def _cparams(n_axes):
    return pltpu.CompilerParams(
        dimension_semantics=("arbitrary",) * n_axes, vmem_limit_bytes=VMEM_LIMIT)


def _silu(x):
    return x * jax.nn.sigmoid(x)


def _gelu_tanh(x):
    c = math.sqrt(2.0 / math.pi)
    return 0.5 * x * (1.0 + jnp.tanh(c * (x + 0.044715 * (x * x * x))))


def _dot(a, b):
    return jnp.dot(a, b, preferred_element_type=F32)


def _dot_nt(a, b, precision=None):
    return lax.dot_general(a, b, (((1,), (1,)), ((), ())), precision=precision,
                           preferred_element_type=F32)


def _dot_tn(a, b):
    return lax.dot_general(a, b, (((0,), (0,)), ((), ())), preferred_element_type=F32)


def _ada_kernel(c_ref, w_ref, b_ref, o_ref):
    c = _silu(c_ref[...]).astype(BF16)
    o_ref[...] = _dot(c, w_ref[...].astype(BF16)) + b_ref[...]


def _ada(c_pad, w_ada, b_ada):
    rows = c_pad.shape[0]
    tn = 1024
    n_out = N_MOD * D_MODEL
    return pl.pallas_call(
        _ada_kernel,
        grid=(DEPTH, n_out // tn),
        in_specs=[
            pl.BlockSpec((rows, D_MODEL), lambda l, j: (0, 0)),
            pl.BlockSpec((None, D_MODEL, tn), lambda l, j: (l, 0, j)),
            pl.BlockSpec((None, 1, tn), lambda l, j: (l, 0, j)),
        ],
        out_specs=pl.BlockSpec((None, rows, tn), lambda l, j: (l, 0, j)),
        out_shape=jax.ShapeDtypeStruct((DEPTH, rows, n_out), F32),
        compiler_params=_cparams(2),
        name="ada_ln",
    )(c_pad, w_ada, b_ada.reshape(DEPTH, 1, n_out))


def _stream_specs(xs, block, col_of, n_j):
    if len(xs) == 1:
        return [pl.BlockSpec(block, lambda i, j: (i, col_of(j)))], None
    n_first = xs[0].shape[0] // block[0]

    def first(i, j):
        return jnp.minimum(i, n_first - 1), jnp.where(i < n_first, col_of(j), col_of(n_j - 1))

    def second(i, j):
        return jnp.maximum(i - n_first, 0), jnp.where(i < n_first, col_of(0), col_of(j))

    return [pl.BlockSpec(block, first), pl.BlockSpec(block, second)], n_first


def _stream_tile(x_refs, n_first, idx):
    if n_first is None:
        return x_refs[0][idx]
    return jnp.where(pl.program_id(0) < n_first, x_refs[0][idx], x_refs[1][idx])


def _inproj_kernel(n_first, *refs):
    n_x = 1 if n_first is None else 2
    x_refs = refs[:n_x]
    g_ref, shift_ref, scale_ref, w_ref, wdt_ref, proj_ref, dt_ref, h_ref = refs[n_x:]
    j = pl.program_id(1)
    n_blk = shift_ref.shape[0]

    @pl.when(j == 0)
    def _():
        g = g_ref[...]

        def body(r, carry):
            rows = pl.ds(pl.multiple_of(r * ROW_BLOCK, ROW_BLOCK), ROW_BLOCK)
            xf = _stream_tile(x_refs, n_first, (rows, slice(None)))
            y = xf * lax.rsqrt(jnp.mean(xf * xf, axis=-1, keepdims=True) + EPS) * g
            y = y * (1.0 + scale_ref[pl.ds(r, 1), :]) + shift_ref[pl.ds(r, 1), :]
            h_ref[rows, :] = y.astype(BF16)
            return carry

        lax.fori_loop(0, n_blk, body, 0)
        dt_ref[...] = _dot_nt(h_ref[...], wdt_ref[...].astype(BF16))

    proj_ref[...] = _dot_nt(h_ref[...], w_ref[...].astype(BF16)).astype(BF16)


def _inproj(xs, norm_g, modb, w_in_t, layer, tm=1024, tn=512):
    t_rows = sum(x.shape[0] for x in xs)
    nb = tm // ROW_BLOCK
    assert all(x.shape[0] % tm == 0 for x in xs)
    x_specs, n_first = _stream_specs(xs, (tm, D_MODEL), lambda j: 0, MAIN_DIM // tn)
    return pl.pallas_call(
        functools.partial(_inproj_kernel, n_first),
        grid=(t_rows // tm, MAIN_DIM // tn),
        in_specs=x_specs + [
            pl.BlockSpec((None, 1, D_MODEL), lambda i, j: (layer, 0, 0)),
            pl.BlockSpec((nb, D_MODEL), lambda i, j: (i, 0)),
            pl.BlockSpec((nb, D_MODEL), lambda i, j: (i, 1)),
            pl.BlockSpec((None, tn, D_MODEL), lambda i, j: (layer, j, 0)),
            pl.BlockSpec((None, M_HEADS, D_MODEL), lambda i, j: (layer, MAIN_DIM // M_HEADS, 0)),
        ],
        out_specs=[
            pl.BlockSpec((tm, tn), lambda i, j: (i, j)),
            pl.BlockSpec((tm, M_HEADS), lambda i, j: (i, 0)),
        ],
        out_shape=[
            jax.ShapeDtypeStruct((t_rows, MAIN_DIM), BF16),
            jax.ShapeDtypeStruct((t_rows, M_HEADS), F32),
        ],
        scratch_shapes=[pltpu.VMEM((tm, D_MODEL), BF16)],
        compiler_params=_cparams(2),
        name="in_proj",
    )(*xs, norm_g.reshape(DEPTH, 1, D_MODEL), modb, modb, w_in_t, w_in_t)


def _gmlp_kernel(n_prompt_tiles, u_ref, v_ref, lng_ref, lnb_ref, w_ref, b_ref, out_ref, vn_ref):
    i = pl.program_id(0)
    u = _gelu_tanh(u_ref[...].astype(F32))
    v = _gelu_tanh(v_ref[...].astype(F32))
    mu = jnp.mean(v, axis=-1, keepdims=True)
    vc = v - mu
    vn = vc * lax.rsqrt(jnp.mean(vc * vc, axis=-1, keepdims=True) + EPS)
    vn = vn * lng_ref[...] + lnb_ref[...]
    vn_ref[...] = vn
    t_idx = lax.broadcasted_iota(jnp.int32, (GM_CHUNK, GM_CHUNK), 0)
    s_idx = lax.broadcasted_iota(jnp.int32, (GM_CHUNK, GM_CHUNK), 1)
    same_seq = (t_idx // ROW_BLOCK) == (s_idx // ROW_BLOCK)
    allowed = (s_idx <= t_idx) & (same_seq | (i < n_prompt_tiles))
    vnb = vn.astype(BF16)
    bias = b_ref[...]
    for h in range(GM_HEADS):
        cols = slice(h * GM_HEAD_DIM, (h + 1) * GM_HEAD_DIM)
        wm = jnp.where(allowed, w_ref[h], 0.0).astype(BF16)
        mixed = _dot(wm, vnb[:, cols]) + bias[:, h:h + 1]
        out_ref[:, cols] = (u[:, cols] * mixed).astype(BF16)


def _gmlp(proj, ln_g, ln_b, w_cat, b_cat, layer, n_prompt_tiles, n_sample_rows):
    t_rows = proj.shape[0]
    n_tiles = t_rows // GM_CHUNK
    kern = functools.partial(_gmlp_kernel, n_prompt_tiles)
    return pl.pallas_call(
        kern,
        grid=(n_tiles,),
        in_specs=[
            pl.BlockSpec((GM_CHUNK, GM_WIDTH), lambda i: (i, 0)),
            pl.BlockSpec((GM_CHUNK, GM_WIDTH), lambda i: (i, 1)),
            pl.BlockSpec((None, 1, GM_WIDTH), lambda i: (layer, 0, 0)),
            pl.BlockSpec((None, 1, GM_WIDTH), lambda i: (layer, 0, 0)),
            pl.BlockSpec((None, None, GM_HEADS, GM_CHUNK, GM_CHUNK),
                         lambda i: (layer, jnp.where(i < n_prompt_tiles, 0, 1), 0, 0, 0)),
            pl.BlockSpec((None, None, GM_CHUNK, GM_HEADS),
                         lambda i: (layer, jnp.where(i < n_prompt_tiles, 0, 1), 0, 0)),
        ],
        out_specs=[
            pl.BlockSpec((GM_CHUNK, GM_WIDTH), lambda i: (i, 0)),
            pl.BlockSpec((GM_CHUNK, GM_WIDTH), lambda i: (jnp.maximum(i - n_prompt_tiles, 0), 0)),
        ],
        out_shape=[
            jax.ShapeDtypeStruct((t_rows, GM_WIDTH), BF16),
            jax.ShapeDtypeStruct((n_sample_rows, GM_WIDTH), F32),
        ],
        compiler_params=_cparams(1),
        name="gmlp",
    )(proj, proj, ln_g.reshape(DEPTH, 1, GM_WIDTH), ln_b.reshape(DEPTH, 1, GM_WIDTH), w_cat, b_cat)


CONV_PAD = 8
SSD_PROMPT_CHUNK = 256


def _ssd_kernel(q, zero_init, *refs):
    z_ref, xa_ref, xb_ref, dt_ref = refs[:4]
    refs = refs[4:]
    h0_ref = c0_ref = None
    if not zero_init:
        h0_ref, c0_ref = refs[:2]
        refs = refs[2:]
    cw_ref, cb_ref, dtb_ref, alog_ref, dskip_ref, ng_ref = refs[:6]
    refs = refs[6:]
    out_ref, hfin_ref, cfin_ref, state_ref, win_ref, xs_ref, bc_ref, xdt_ref, xdd_ref, y_ref = refs
    c = pl.program_id(1)
    tail = M_CONV - 1
    lo = CONV_PAD - tail
    hp = M_HEADS * M_HEAD_DIM

    @pl.when(c == 0)
    def _():
        if zero_init:
            state_ref[...] = jnp.zeros_like(state_ref)
            win_ref[lo:CONV_PAD, :] = jnp.zeros((tail, CONV_DIM), F32)
        else:
            state_ref[...] = h0_ref[...].reshape(hp, M_STATE).T
            win_ref[lo:CONV_PAD, :] = c0_ref[...]

    half = CONV_DIM // 2
    win_ref[CONV_PAD:CONV_PAD + q, 0:half] = xa_ref[...].astype(F32)
    win_ref[CONV_PAD:CONV_PAD + q, half:CONV_DIM] = xb_ref[...].astype(F32)
    for r0 in range(0, q, ROW_BLOCK):
        conv = cb_ref[...]
        for k in range(M_CONV):
            conv = conv + win_ref[lo + k + r0:lo + k + r0 + ROW_BLOCK, :] * cw_ref[k:k + 1, :]
        act = _silu(conv)
        xs_ref[r0:r0 + ROW_BLOCK, :] = act[:, 0:M_WIDTH]
        bc_ref[r0:r0 + ROW_BLOCK, :] = act[:, M_WIDTH:CONV_DIM].astype(BF16)
    new_tail = win_ref[CONV_PAD + q - tail:CONV_PAD + q, :]
    cfin_ref[...] = new_tail
    win_ref[lo:CONV_PAD, :] = new_tail

    hi = lax.Precision.HIGHEST
    dt = jax.nn.softplus(dt_ref[...] + dtb_ref[...])
    a = -jnp.exp(alog_ref[...])
    r_idx = lax.broadcasted_iota(jnp.int32, (q, q), 0)
    c_idx = lax.broadcasted_iota(jnp.int32, (q, q), 1)
    causal = c_idx <= r_idx
    a_cs = jnp.dot(causal.astype(F32), dt * a, precision=hi, preferred_element_type=F32)
    eye = (lax.broadcasted_iota(jnp.int32, (M_HEADS, M_HEADS), 0)
           == lax.broadcasted_iota(jnp.int32, (M_HEADS, M_HEADS), 1)).astype(F32)
    a_cs_t = _dot_nt(eye, a_cs, precision=hi)
    widen = (lax.broadcasted_iota(jnp.int32, (M_HEADS, hp), 1) // M_HEAD_DIM
             == lax.broadcasted_iota(jnp.int32, (M_HEADS, hp), 0)).astype(F32)
    dt_w = jnp.dot(dt, widen, precision=hi, preferred_element_type=F32)
    acs_w = jnp.dot(a_cs, widen, precision=hi, preferred_element_type=F32)
    dskip_w = jnp.dot(jnp.broadcast_to(dskip_ref[...], (8, M_HEADS)), widen, precision=hi,
                      preferred_element_type=F32)[0:1, :]
    a_end_w = acs_w[q - 1:q, :]
    x_dt = xs_ref[...] * dt_w
    xdt_ref[...] = x_dt.astype(BF16)
    xdd_ref[...] = (x_dt * jnp.exp(a_end_w - acs_w)).astype(BF16)
    decay_blk_w = jnp.exp(a_end_w)

    heads_per_group = M_HEADS // M_GROUPS
    gw = M_WIDTH // M_GROUPS
    pair_lane = lax.broadcasted_iota(jnp.int32, (q, 2 * M_HEAD_DIM), 1)
    for g in range(M_GROUPS):
        gcols = slice(g * gw, (g + 1) * gw)
        b_g = bc_ref[:, g * M_STATE:(g + 1) * M_STATE]
        c_g = bc_ref[:, (M_GROUPS + g) * M_STATE:(M_GROUPS + g + 1) * M_STATE]
        cb = _dot_nt(c_g, b_g)
        st = state_ref[:, gcols]
        y_off = _dot(c_g, st.astype(BF16))
        state_ref[:, gcols] = st * decay_blk_w[:, gcols] + _dot_tn(b_g, xdd_ref[:, gcols])
        for pr in range(heads_per_group // 2):
            h0 = g * heads_per_group + 2 * pr
            pcols = slice(h0 * M_HEAD_DIM, (h0 + 2) * M_HEAD_DIM)
            x_pair = xdt_ref[:, pcols]
            ys = []
            for h in (h0, h0 + 1):
                seg = a_cs[:, h:h + 1] - a_cs_t[h:h + 1, :]
                scores = cb * jnp.exp(jnp.where(causal, seg, -jnp.inf))
                ys.append(_dot(scores.astype(BF16), x_pair))
            y_ref[:, pcols] = jnp.where(pair_lane < M_HEAD_DIM, ys[0], ys[1])
        y_ref[:, gcols] = (y_ref[:, gcols] + y_off * jnp.exp(acs_w[:, gcols])
                           + dskip_w[:, gcols] * xs_ref[:, gcols])

    @pl.when(c == pl.num_programs(1) - 1)
    def _():
        hfin_ref[...] = state_ref[...].T.reshape(M_HEADS, M_HEAD_DIM, M_STATE)

    gy = y_ref[...] * _silu(z_ref[...].astype(F32))
    ng = ng_ref[...]
    for g in range(M_GROUPS):
        cols = slice(g * gw, (g + 1) * gw)
        s = gy[:, cols]
        s = s * lax.rsqrt(jnp.mean(s * s, axis=-1, keepdims=True) + EPS)
        out_ref[:, cols] = (s * ng[:, cols]).astype(BF16)


def _ssd(proj, dt_raw, h0, c0, conv_w, conv_b, dt_bias, a_log, d_skip, mnorm_g, layer,
         row_start, n_seq, seq_len, q):
    n_c = seq_len // q
    blk0 = row_start // q
    half = CONV_DIM // 2
    z_blk = (2 * GM_WIDTH) // M_WIDTH
    xa_blk = (2 * GM_WIDTH + M_WIDTH) // half
    zero_init = h0 is None

    def rows(col):
        return lambda s, c: (blk0 + s * n_c + c, col)

    def per_layer(*shape):
        return pl.BlockSpec((None,) + shape, lambda s, c: (layer,) + (0,) * len(shape))

    in_specs = [
        pl.BlockSpec((q, M_WIDTH), rows(z_blk)),
        pl.BlockSpec((q, half), rows(xa_blk)),
        pl.BlockSpec((q, half), rows(xa_blk + 1)),
        pl.BlockSpec((q, M_HEADS), rows(0)),
    ]
    args = [proj, proj, proj, dt_raw]
    if not zero_init:
        in_specs += [
            pl.BlockSpec((None, None, M_HEADS, M_HEAD_DIM, M_STATE), lambda s, c: (layer, s, 0, 0, 0)),
            pl.BlockSpec((None, None, M_CONV - 1, CONV_DIM), lambda s, c: (layer, s, 0, 0)),
        ]
        args += [h0, c0]
    in_specs += [per_layer(M_CONV, CONV_DIM), per_layer(1, CONV_DIM), per_layer(1, M_HEADS),
                 per_layer(1, M_HEADS), per_layer(1, M_HEADS), per_layer(1, M_WIDTH)]
    args += [conv_w, conv_b.reshape(DEPTH, 1, CONV_DIM), dt_bias.reshape(DEPTH, 1, M_HEADS),
             a_log.reshape(DEPTH, 1, M_HEADS), d_skip.reshape(DEPTH, 1, M_HEADS),
             mnorm_g.reshape(DEPTH, 1, M_WIDTH)]
    return pl.pallas_call(
        functools.partial(_ssd_kernel, q, zero_init),
        grid=(n_seq, n_c),
        in_specs=in_specs,
        out_specs=[
            pl.BlockSpec((q, M_WIDTH), lambda s, c: (s * n_c + c, 0)),
            pl.BlockSpec((None, M_HEADS, M_HEAD_DIM, M_STATE), lambda s, c: (s, 0, 0, 0)),
            pl.BlockSpec((None, M_CONV - 1, CONV_DIM), lambda s, c: (s, 0, 0)),
        ],
        out_shape=[
            jax.ShapeDtypeStruct((n_seq * seq_len, M_WIDTH), BF16),
            jax.ShapeDtypeStruct((n_seq, M_HEADS, M_HEAD_DIM, M_STATE), F32),
            jax.ShapeDtypeStruct((n_seq, M_CONV - 1, CONV_DIM), F32),
        ],
        scratch_shapes=[
            pltpu.VMEM((M_STATE, M_WIDTH), F32),
            pltpu.VMEM((CONV_PAD + q, CONV_DIM), F32),
            pltpu.VMEM((q, M_WIDTH), F32),
            pltpu.VMEM((q, 2 * M_GROUPS * M_STATE), BF16),
            pltpu.VMEM((q, M_WIDTH), BF16),
            pltpu.VMEM((q, M_WIDTH), BF16),
            pltpu.VMEM((q, M_WIDTH), F32),
        ],
        compiler_params=_cparams(2),
        name="conv_ssd",
    )(*args)


def _outproj_kernel(n_first_ssm, n_first, gm_ref, wa_ref, wb_ref, gate_ref, *refs):
    n_ssm = 1 if n_first_ssm is None else 2
    ssm_refs, x_refs, o_ref = refs[:n_ssm], refs[n_ssm:-1], refs[-1]
    ssm = _stream_tile(ssm_refs, n_first_ssm, Ellipsis)
    acc = _dot(gm_ref[...], wa_ref[...].astype(BF16)) + _dot(ssm, wb_ref[...].astype(BF16))
    tm, tn = acc.shape
    nb = tm // ROW_BLOCK
    upd = acc.reshape(nb, ROW_BLOCK, tn) * gate_ref[...][:, None, :]
    o_ref[...] = _stream_tile(x_refs, n_first, Ellipsis) + upd.reshape(tm, tn)


def _outproj(gm, ssms, w_out, xs, modb, layer, tm=1024, tn=512):
    t_rows = gm.shape[0]
    nb = tm // ROW_BLOCK
    gate_blk = 2 * D_MODEL // tn
    assert all(x.shape[0] % tm == 0 for x in xs)
    assert all(x.shape[0] % tm == 0 for x in ssms)
    x_specs, n_first = _stream_specs(xs, (tm, tn), lambda j: j, D_MODEL // tn)
    ssm_specs, n_first_ssm = _stream_specs(ssms, (tm, M_WIDTH), lambda j: 0, D_MODEL // tn)
    return pl.pallas_call(
        functools.partial(_outproj_kernel, n_first_ssm, n_first),
        grid=(t_rows // tm, D_MODEL // tn),
        in_specs=[
            pl.BlockSpec((tm, GM_WIDTH), lambda i, j: (i, 0)),
            pl.BlockSpec((None, GM_WIDTH, tn), lambda i, j: (layer, 0, j)),
            pl.BlockSpec((None, M_WIDTH, tn), lambda i, j: (layer, 1, j)),
            pl.BlockSpec((nb, tn), lambda i, j: (i, gate_blk + j)),
        ] + ssm_specs + x_specs,
        out_specs=pl.BlockSpec((tm, tn), lambda i, j: (i, j)),
        out_shape=jax.ShapeDtypeStruct((t_rows, D_MODEL), F32),
        compiler_params=_cparams(2),
        name="out_proj",
    )(gm, w_out, w_out, modb, *ssms, *xs)


ROUTER_ROWS = 24
ROUTE_ROWS = 8
EXPERT_TILE = 512


def _row_argmax_first(rows):
    m = rows[0]
    for r in rows[1:]:
        m = jnp.maximum(m, r)
    idx = jnp.full(m.shape, len(rows), jnp.int32)
    for k in reversed(range(len(rows))):
        idx = jnp.where(rows[k] == m, k, idx)
    return m, idx


def _router_kernel(x_ref, g_ref, shift_ref, scale_ref, wr_ref, br_ref, h_ref, route_ref, count_ref, run_ref):
    n_blk = shift_ref.shape[0]
    g = g_ref[...]

    @pl.when(pl.program_id(0) == 0)
    def _():
        run_ref[...] = jnp.zeros_like(run_ref)

    def body(r, carry):
        rows = pl.ds(pl.multiple_of(r * ROW_BLOCK, ROW_BLOCK), ROW_BLOCK)
        xf = x_ref[rows, :]
        y = xf * lax.rsqrt(jnp.mean(xf * xf, axis=-1, keepdims=True) + EPS) * g
        h_ref[rows, :] = y * (1.0 + scale_ref[pl.ds(r, 1), :]) + shift_ref[pl.ds(r, 1), :]
        return carry

    lax.fori_loop(0, n_blk, body, 0)
    logits = _dot_nt(wr_ref[...], h_ref[...], precision=lax.Precision.HIGHEST) + br_ref[...]
    lg = [logits[k:k + 1, :] for k in range(N_EGROUPS)]
    g_max, g_idx = _row_argmax_first(lg)
    g_den = lg[0] * 0.0
    for r in lg:
        g_den = g_den + jnp.exp(r - g_max)
    g_top = 1.0 / g_den
    e_in = []
    for k in range(EXPERTS_PER_GROUP):
        sel = lg[0] * 0.0
        for gi in range(N_EGROUPS):
            row = N_EGROUPS + gi * EXPERTS_PER_GROUP + k
            sel = jnp.where(g_idx == gi, logits[row:row + 1, :], sel)
        e_in.append(sel)
    e_max, i1 = _row_argmax_first(e_in)
    e_exp = [jnp.exp(r - e_max) for r in e_in]
    e_den = e_exp[0]
    for r in e_exp[1:]:
        e_den = e_den + r
    p = [r / e_den for r in e_exp]
    p1 = p[0] * 0.0
    for k in range(EXPERTS_PER_GROUP):
        p1 = jnp.where(i1 == k, p[k], p1)
    rest = [jnp.where(i1 == k, -1.0, p[k]) for k in range(EXPERTS_PER_GROUP)]
    p2, i2 = _row_argmax_first(rest)
    norm = p1 + p2
    w1 = p1 / norm * g_top
    w2 = p2 / norm * g_top
    tm = logits.shape[1]
    row = lax.broadcasted_iota(jnp.int32, (ROUTE_ROWS, tm), 0)
    onehot = (row == g_idx).astype(F32)
    before = (lax.broadcasted_iota(jnp.int32, (tm, tm), 0)
              < lax.broadcasted_iota(jnp.int32, (tm, tm), 1)).astype(BF16)
    earlier = _dot(onehot.astype(BF16), before)
    rank = jnp.sum(onehot * (earlier + run_ref[:, 0:1]), axis=0, keepdims=True)
    run_ref[...] += jnp.sum(onehot, axis=1, keepdims=True)
    count_ref[...] = run_ref[...]

    route_ref[0:1, :] = g_idx.astype(F32)
    route_ref[1:2, :] = rank
    for k in range(EXPERTS_PER_GROUP):
        route_ref[2 + k:3 + k, :] = jnp.where(i1 == k, w1, 0.0) + jnp.where(i2 == k, w2, 0.0)
    route_ref[2 + EXPERTS_PER_GROUP:ROUTE_ROWS, :] = jnp.zeros((ROUTE_ROWS - 2 - EXPERTS_PER_GROUP, tm), F32)


def _router(x, norm_g, modb, w_r, b_r, layer, tm=512):
    t_rows = x.shape[0]
    nb = tm // ROW_BLOCK
    return pl.pallas_call(
        _router_kernel,
        grid=(t_rows // tm,),
        in_specs=[
            pl.BlockSpec((tm, D_MODEL), lambda i: (i, 0)),
            pl.BlockSpec((None, 1, D_MODEL), lambda i: (layer, 0, 0)),
            pl.BlockSpec((nb, D_MODEL), lambda i: (i, 3)),
            pl.BlockSpec((nb, D_MODEL), lambda i: (i, 4)),
            pl.BlockSpec((ROUTER_ROWS, D_MODEL), lambda i: (0, 0)),
            pl.BlockSpec((ROUTER_ROWS, 1), lambda i: (0, 0)),
        ],
        out_specs=[
            pl.BlockSpec((tm, D_MODEL), lambda i: (i, 0)),
            pl.BlockSpec((ROUTE_ROWS, tm), lambda i: (0, i)),
            pl.BlockSpec((ROUTE_ROWS, LANES), lambda i: (0, 0)),
        ],
        out_shape=[
            jax.ShapeDtypeStruct((t_rows, D_MODEL), F32),
            jax.ShapeDtypeStruct((ROUTE_ROWS, t_rows), F32),
            jax.ShapeDtypeStruct((ROUTE_ROWS, LANES), F32),
        ],
        scratch_shapes=[pltpu.VMEM((ROUTE_ROWS, LANES), F32)],
        compiler_params=_cparams(1),
        name="router",
    )(x, norm_g.reshape(DEPTH, 1, D_MODEL), modb, modb, w_r, b_r)


def _dispatch_tables(route, counts, t_rows, tm, n_tiles):
    gid = route[0].astype(jnp.int32)
    rank = route[1].astype(jnp.int32)
    cnt = counts[:N_EGROUPS, 0].astype(jnp.int32)
    padded = (cnt + tm - 1) // tm * tm
    ends = jnp.cumsum(padded)
    starts = ends - padded
    pos = starts[gid] + rank
    src = jnp.zeros((n_tiles * tm,), jnp.int32).at[pos].set(jnp.arange(t_rows, dtype=jnp.int32))
    tile_start = jnp.arange(n_tiles, dtype=jnp.int32) * tm
    tile_valid = (tile_start < ends[-1]).astype(jnp.int32)
    tile_group = jnp.sum((tile_start[:, None] >= ends[None, :]).astype(jnp.int32), axis=1)
    tile_group = jnp.minimum(tile_group, N_EGROUPS - 1)
    gates_sorted = jnp.take(route[2:2 + EXPERTS_PER_GROUP].T, src, axis=0)
    return pos, src, tile_group, tile_valid, gates_sorted


def _start_row_gather(table_ref, idx_ref, idx_base, buf_ref, sem):
    def issue(r, carry):
        pltpu.make_async_copy(table_ref.at[pl.ds(idx_ref[idx_base + r], 1), :],
                              buf_ref.at[pl.ds(r, 1), :], sem).start()
        return carry

    lax.fori_loop(0, buf_ref.shape[0], issue, 0, unroll=8)


def _wait_row_gather(table_ref, buf_ref, sem):
    pltpu.make_async_copy(table_ref.at[pl.ds(0, buf_ref.shape[0]), :], buf_ref, sem).wait()


def _experts_kernel(f_split, tg_ref, tv_ref, src_ref, gates_ref, wg_ref, wu_ref, wd_ref, h_ref, y_ref,
                    xbuf_ref, xb_ref, sem):
    i = pl.program_id(0)
    s = pl.program_id(1)
    n_tiles = pl.num_programs(0)
    tm = xb_ref.shape[0]
    valid = tv_ref[i] == 1
    slot = i % 2

    @pl.when((s == 0) & (i == 0) & valid)
    def _():
        _start_row_gather(h_ref, src_ref, 0, xbuf_ref.at[0], sem.at[0])

    @pl.when((s == 0) & valid)
    def _():
        _wait_row_gather(h_ref, xbuf_ref.at[slot], sem.at[slot])

    @pl.when((s == 0) & (i + 1 < n_tiles))
    def _():
        @pl.when(tv_ref[jnp.minimum(i + 1, n_tiles - 1)] == 1)
        def _():
            _start_row_gather(h_ref, src_ref, (i + 1) * tm, xbuf_ref.at[1 - slot], sem.at[1 - slot])

    @pl.when(s == 0)
    def _():
        y_ref[...] = jnp.zeros_like(y_ref)

    @pl.when((s == 0) & valid)
    def _():
        xb_ref[...] = xbuf_ref[slot].astype(BF16)

    @pl.when(valid)
    def _():
        e = s // f_split
        x = xb_ref[...]
        gate_act = _dot(x, wg_ref[...].astype(BF16))
        up = _dot(x, wu_ref[...].astype(BF16))
        lane = lax.broadcasted_iota(jnp.int32, gates_ref.shape, 1)
        gate_col = jnp.sum(jnp.where(lane == e, gates_ref[...], 0.0), axis=1, keepdims=True)
        hid = (_silu(gate_act) * up * gate_col).astype(BF16)
        y_ref[...] += _dot(hid, wd_ref[...].astype(BF16))


def _experts(h, src, gates_sorted, tile_group, tile_valid, w_gate, w_up, w_down, layer, tm, f_split=2):
    n_rows = src.shape[0]
    fb = D_EXPERT // f_split
    last = EXPERTS_PER_GROUP * f_split - 1

    def expert_of(i, s, tg, tv, sr):
        return tg[i] * EXPERTS_PER_GROUP + jnp.where(tv[i] == 1, s, last) // f_split

    def half_of(i, s, tg, tv, sr):
        return jnp.where(tv[i] == 1, s, last) % f_split

    grid_spec = pltpu.PrefetchScalarGridSpec(
        num_scalar_prefetch=3,
        grid=(n_rows // tm, EXPERTS_PER_GROUP * f_split),
        in_specs=[
            pl.BlockSpec((tm, EXPERTS_PER_GROUP), lambda i, s, *_: (i, 0)),
            pl.BlockSpec((None, None, D_MODEL, fb), lambda i, s, *p: (layer, expert_of(i, s, *p), 0, half_of(i, s, *p))),
            pl.BlockSpec((None, None, D_MODEL, fb), lambda i, s, *p: (layer, expert_of(i, s, *p), 0, half_of(i, s, *p))),
            pl.BlockSpec((None, None, fb, D_MODEL), lambda i, s, *p: (layer, expert_of(i, s, *p), half_of(i, s, *p), 0)),
            pl.BlockSpec(memory_space=pl.ANY),
        ],
        out_specs=pl.BlockSpec((tm, D_MODEL), lambda i, s, *_: (i, 0)),
        scratch_shapes=[
            pltpu.VMEM((2, tm, D_MODEL), F32),
            pltpu.VMEM((tm, D_MODEL), BF16),
            pltpu.SemaphoreType.DMA((2,)),
        ],
    )
    return pl.pallas_call(
        functools.partial(_experts_kernel, f_split),
        grid_spec=grid_spec,
        out_shape=jax.ShapeDtypeStruct((n_rows, D_MODEL), F32),
        compiler_params=_cparams(2),
        name="experts",
    )(tile_group, tile_valid, src, gates_sorted, w_gate, w_up, w_down, h)


def _combine_kernel(final_norm, n_first, pos_ref, x_ref, gate_ref, fg_ref, ys_ref, *refs):
    o_refs, buf_ref, sem = refs[:-2], refs[-2], refs[-1]
    tm, d = x_ref.shape
    _start_row_gather(ys_ref, pos_ref, pl.program_id(0) * tm, buf_ref, sem)
    _wait_row_gather(ys_ref, buf_ref, sem)
    nb = tm // ROW_BLOCK
    upd = buf_ref[...].reshape(nb, ROW_BLOCK, d) * gate_ref[...][:, None, :]
    x = x_ref[...] + upd.reshape(tm, d)
    if final_norm:
        x = x * lax.rsqrt(jnp.mean(x * x, axis=-1, keepdims=True) + EPS) * fg_ref[...]
    if n_first is None:
        o_refs[0][...] = x
    else:
        @pl.when(pl.program_id(0) < n_first)
        def _():
            o_refs[0][...] = x

        @pl.when(pl.program_id(0) >= n_first)
        def _():
            o_refs[1][...] = x


def _combine(x, ys, pos, modb, final_g, final_norm, split_rows=None, tm=512):
    t_rows = x.shape[0]
    nb = tm // ROW_BLOCK
    if split_rows is None:
        n_first = None
        out_specs = pl.BlockSpec((tm, D_MODEL), lambda i, p: (i, 0))
        out_shape = jax.ShapeDtypeStruct((t_rows, D_MODEL), F32)
    else:
        assert split_rows % tm == 0 and (t_rows - split_rows) % tm == 0
        n_first = split_rows // tm
        out_specs = [pl.BlockSpec((tm, D_MODEL), lambda i, p: (jnp.minimum(i, n_first - 1), 0)),
                     pl.BlockSpec((tm, D_MODEL), lambda i, p: (jnp.maximum(i - n_first, 0), 0))]
        out_shape = [jax.ShapeDtypeStruct((split_rows, D_MODEL), F32),
                     jax.ShapeDtypeStruct((t_rows - split_rows, D_MODEL), F32)]
    grid_spec = pltpu.PrefetchScalarGridSpec(
        num_scalar_prefetch=1,
        grid=(t_rows // tm,),
        in_specs=[
            pl.BlockSpec((tm, D_MODEL), lambda i, p: (i, 0)),
            pl.BlockSpec((nb, D_MODEL), lambda i, p: (i, 5)),
            pl.BlockSpec((1, D_MODEL), lambda i, p: (0, 0)),
            pl.BlockSpec(memory_space=pl.ANY),
        ],
        out_specs=out_specs,
        scratch_shapes=[pltpu.VMEM((tm, D_MODEL), F32), pltpu.SemaphoreType.DMA(())],
    )
    return pl.pallas_call(
        functools.partial(_combine_kernel, final_norm, n_first),
        grid_spec=grid_spec,
        out_shape=out_shape,
        compiler_params=_cparams(1),
        name="combine",
    )(pos, x, modb, final_g.reshape(1, D_MODEL), ys)


def kernel(x_prompt, x_sample, state_ssm, state_conv, c_prompt, c_sample, norm1_g, norm2_g, w_ada, b_ada,
           w_in, gm_ln_g, gm_ln_b, gm_w_s, gm_b_s, conv_w, conv_b, dt_bias, a_log, d_skip, mnorm_g, w_out,
           w_router_group, b_router_group, w_router_expert, b_router_expert, w_gate, w_up, w_down,
           final_norm_g):
    n_p, len_p, _ = x_prompt.shape
    n_s, len_s, _ = x_sample.shape
    assert len_p % GM_CHUNK == 0 and len_s == ROW_BLOCK and (n_s * len_s) % GM_CHUNK == 0
    rows_p, rows_s = n_p * len_p, n_s * len_s
    t_rows = rows_p + rows_s
    n_seq = n_p + n_s
    blk_per_prompt = len_p // ROW_BLOCK
    assert t_rows % EXPERT_TILE == 0
    n_expert_tiles = t_rows // EXPERT_TILE + N_EGROUPS

    xs = (x_prompt.reshape(rows_p, D_MODEL), x_sample.reshape(rows_s, D_MODEL))

    c_all = jnp.concatenate([c_prompt, c_sample], axis=0)
    c_pad = jnp.pad(c_all, ((0, (-n_seq) % 8), (0, 0)))
    mod = _ada(c_pad, w_ada, b_ada)
    modb_all = jnp.concatenate(
        [jnp.repeat(mod[:, :n_p], blk_per_prompt, axis=1), mod[:, n_p:n_seq]], axis=1)

    half_w = gm_w_s[:, :, :ROW_BLOCK, :ROW_BLOCK]
    w_cat = jnp.stack([gm_w_s, jnp.tile(half_w, (1, 1, 2, 2))], axis=1)
    b_t = jnp.swapaxes(gm_b_s, 1, 2)
    b_cat = jnp.stack([b_t, jnp.tile(b_t[:, :ROW_BLOCK], (1, 2, 1))], axis=1)

    w_in_t = jnp.swapaxes(w_in, 1, 2)
    w_r =jnp.concatenate([jnp.swapaxes(w_router_group, 1, 2), jnp.swapaxes(w_router_expert, 1, 2)], axis=1)
    w_r = jnp.pad(w_r, ((0, 0), (0, ROUTER_ROWS - w_r.shape[1]), (0, 0)))
    b_r = jnp.concatenate([b_router_group, b_router_expert], axis=1)
    b_r = jnp.pad(b_r, ((0, 0), (0, ROUTER_ROWS - b_r.shape[1])))[:, :, None]

    ssm_p, conv_p, ssm_s, conv_s, v_rows = [], [], [], [], []
    for l in range(DEPTH):
        last = l == DEPTH - 1
        modb = modb_all[l]
        proj, dt_raw = _inproj(xs, norm1_g, modb, w_in_t, l)
        gm_out, vn = _gmlp(proj, gm_ln_g, gm_ln_b, w_cat, b_cat, l, rows_p // GM_CHUNK, rows_s)
        ssm_args = (conv_w, conv_b, dt_bias, a_log, d_skip, mnorm_g, l)
        ssm_out_p, hfin_p, cfin_p = _ssd(proj, dt_raw, None, None, *ssm_args, 0, n_p, len_p, SSD_PROMPT_CHUNK)
        ssm_out_s, hfin_s, cfin_s = _ssd(proj, dt_raw, state_ssm, state_conv, *ssm_args, rows_p, n_s, len_s, len_s)
        x = _outproj(gm_out, (ssm_out_p, ssm_out_s), w_out, xs, modb, l)
        h2, route, counts = _router(x, norm2_g, modb, w_r[l], b_r[l], l)
        pos, src, tile_group, tile_valid, gates_sorted = _dispatch_tables(
            route, counts, t_rows, EXPERT_TILE, n_expert_tiles)
        ys = _experts(h2, src, gates_sorted, tile_group, tile_valid, w_gate, w_up, w_down, l, EXPERT_TILE)
        out = _combine(x, ys, pos, modb, final_norm_g, last, rows_p if last else None)
        xs = tuple(out) if last else (out,)
        ssm_p.append(hfin_p)
        conv_p.append(cfin_p)
        ssm_s.append(hfin_s)
        conv_s.append(cfin_s)
        v_rows.append(vn.reshape(n_s, len_s, GM_WIDTH))

    y_prompt = xs[0].reshape(n_p, len_p, D_MODEL)
    y_sample = xs[1].reshape(n_s, len_s, D_MODEL)
    return (y_prompt, y_sample, jnp.stack(ssm_p), jnp.stack(conv_p), jnp.stack(ssm_s), jnp.stack(conv_s),
            jnp.stack(v_rows))
```

```python
import functools
import math

import jax
import jax.numpy as jnp
import numpy as np
from jax import lax
from jax.experimental import pallas as pl
from jax.experimental.pallas import tpu as pltpu

D_MODEL = 2048
DEPTH = 2
GM_WIDTH = 1024
GM_HEADS = 8
GM_HEAD_DIM = 128
GM_CHUNK = 128
M_WIDTH = 1024
M_HEAD_DIM = 64
M_HEADS = 16
M_GROUPS = 4
M_STATE = 128
M_CONV = 4
CONV_DIM = M_WIDTH + 2 * M_GROUPS * M_STATE
MAIN_DIM = 2 * GM_WIDTH + M_WIDTH + CONV_DIM
N_EGROUPS = 4
EXPERTS_PER_GROUP = 4
N_EXPERTS = 16
D_EXPERT = 512
N_MOD = 6
EPS = 1e-6

ROW_BLOCK = 64
LANES = 128
VMEM_LIMIT = 56 * 1024 * 1024

F32 = jnp.float32
BF16 = jnp.bfloat16


def _cparams(n_axes):
    return pltpu.CompilerParams(
        dimension_semantics=("arbitrary",) * n_axes, vmem_limit_bytes=VMEM_LIMIT)


def _silu(x):
    return x * jax.nn.sigmoid(x)


def _gelu_tanh(x):
    c = math.sqrt(2.0 / math.pi)
    return 0.5 * x * (1.0 + jnp.tanh(c * (x + 0.044715 * (x * x * x))))


def _dot(a, b):
    return jnp.dot(a, b, preferred_element_type=F32)


def _dot_nt(a, b, precision=None):
    return lax.dot_general(a, b, (((1,), (1,)), ((), ())), precision=precision,
                           preferred_element_type=F32)


def _dot_tn(a, b):
    return lax.dot_general(a, b, (((0,), (0,)), ((), ())), preferred_element_type=F32)


def _ada_kernel(c_ref, w_ref, b_ref, o_ref):
    c = _silu(c_ref[...]).astype(BF16)
    o_ref[...] = _dot(c, w_ref[...].astype(BF16)) + b_ref[...]


def _ada(c_pad, w_ada, b_ada):
    rows = c_pad.shape[0]
    tn = 1024
    n_out = N_MOD * D_MODEL
    return pl.pallas_call(
        _ada_kernel,
        grid=(DEPTH, n_out // tn),
        in_specs=[
            pl.BlockSpec((rows, D_MODEL), lambda l, j: (0, 0)),
            pl.BlockSpec((None, D_MODEL, tn), lambda l, j: (l, 0, j)),
            pl.BlockSpec((None, 1, tn), lambda l, j: (l, 0, j)),
        ],
        out_specs=pl.BlockSpec((None, rows, tn), lambda l, j: (l, 0, j)),
        out_shape=jax.ShapeDtypeStruct((DEPTH, rows, n_out), F32),
        compiler_params=_cparams(2),
        name="ada_ln",
    )(c_pad, w_ada, b_ada.reshape(DEPTH, 1, n_out))


def _stream_specs(xs, block, col_of, n_j):
    if len(xs) == 1:
        return [pl.BlockSpec(block, lambda i, j: (i, col_of(j)))], None
    n_first = xs[0].shape[0] // block[0]

    def first(i, j):
        return jnp.minimum(i, n_first - 1), jnp.where(i < n_first, col_of(j), col_of(n_j - 1))

    def second(i, j):
        return jnp.maximum(i - n_first, 0), jnp.where(i < n_first, col_of(0), col_of(j))

    return [pl.BlockSpec(block, first), pl.BlockSpec(block, second)], n_first


def _stream_tile(x_refs, n_first, idx):
    if n_first is None:
        return x_refs[0][idx]
    return jnp.where(pl.program_id(0) < n_first, x_refs[0][idx], x_refs[1][idx])


def _inproj_kernel(n_first, *refs):
    n_x = 1 if n_first is None else 2
    x_refs = refs[:n_x]
    g_ref, shift_ref, scale_ref, w_ref, wdt_ref, proj_ref, dt_ref, h_ref = refs[n_x:]
    j = pl.program_id(1)
    n_blk = shift_ref.shape[0]

    @pl.when(j == 0)
    def _():
        g = g_ref[...]

        def body(r, carry):
            rows = pl.ds(pl.multiple_of(r * ROW_BLOCK, ROW_BLOCK), ROW_BLOCK)
            xf = _stream_tile(x_refs, n_first, (rows, slice(None)))
            y = xf * lax.rsqrt(jnp.mean(xf * xf, axis=-1, keepdims=True) + EPS) * g
            y = y * (1.0 + scale_ref[pl.ds(r, 1), :]) + shift_ref[pl.ds(r, 1), :]
            h_ref[rows, :] = y.astype(BF16)
            return carry

        lax.fori_loop(0, n_blk, body, 0)
        dt_ref[...] = _dot_nt(h_ref[...], wdt_ref[...].astype(BF16))

    proj_ref[...] = _dot_nt(h_ref[...], w_ref[...].astype(BF16)).astype(BF16)


def _inproj(xs, norm_g, modb, w_in_t, layer, tm=1024, tn=512):
    t_rows = sum(x.shape[0] for x in xs)
    nb = tm // ROW_BLOCK
    assert all(x.shape[0] % tm == 0 for x in xs)
    x_specs, n_first = _stream_specs(xs, (tm, D_MODEL), lambda j: 0, MAIN_DIM // tn)
    return pl.pallas_call(
        functools.partial(_inproj_kernel, n_first),
        grid=(t_rows // tm, MAIN_DIM // tn),
        in_specs=x_specs + [
            pl.BlockSpec((None, 1, D_MODEL), lambda i, j: (layer, 0, 0)),
            pl.BlockSpec((nb, D_MODEL), lambda i, j: (i, 0)),
            pl.BlockSpec((nb, D_MODEL), lambda i, j: (i, 1)),
            pl.BlockSpec((None, tn, D_MODEL), lambda i, j: (layer, j, 0)),
            pl.BlockSpec((None, M_HEADS, D_MODEL), lambda i, j: (layer, MAIN_DIM // M_HEADS, 0)),
        ],
        out_specs=[
            pl.BlockSpec((tm, tn), lambda i, j: (i, j)),
            pl.BlockSpec((tm, M_HEADS), lambda i, j: (i, 0)),
        ],
        out_shape=[
            jax.ShapeDtypeStruct((t_rows, MAIN_DIM), BF16),
            jax.ShapeDtypeStruct((t_rows, M_HEADS), F32),
        ],
        scratch_shapes=[pltpu.VMEM((tm, D_MODEL), BF16)],
        compiler_params=_cparams(2),
        name="in_proj",
    )(*xs, norm_g.reshape(DEPTH, 1, D_MODEL), modb, modb, w_in_t, w_in_t)


def _gmlp_kernel(n_prompt_tiles, u_ref, v_ref, lng_ref, lnb_ref, w_ref, b_ref, out_ref, vn_ref):
    i = pl.program_id(0)
    u = _gelu_tanh(u_ref[...].astype(F32))
    v = _gelu_tanh(v_ref[...].astype(F32))
    mu = jnp.mean(v, axis=-1, keepdims=True)
    vc = v - mu
    vn = vc * lax.rsqrt(jnp.mean(vc * vc, axis=-1, keepdims=True) + EPS)
    vn = vn * lng_ref[...] + lnb_ref[...]
    vn_ref[...] = vn
    t_idx = lax.broadcasted_iota(jnp.int32, (GM_CHUNK, GM_CHUNK), 0)
    s_idx = lax.broadcasted_iota(jnp.int32, (GM_CHUNK, GM_CHUNK), 1)
    same_seq = (t_idx // ROW_BLOCK) == (s_idx // ROW_BLOCK)
    allowed = (s_idx <= t_idx) & (same_seq | (i < n_prompt_tiles))
    vnb = vn.astype(BF16)
    bias = b_ref[...]
    for h in range(GM_HEADS):
        cols = slice(h * GM_HEAD_DIM, (h + 1) * GM_HEAD_DIM)
        wm = jnp.where(allowed, w_ref[h], 0.0).astype(BF16)
        mixed = _dot(wm, vnb[:, cols]) + bias[:, h:h + 1]
        out_ref[:, cols] = (u[:, cols] * mixed).astype(BF16)


def _gmlp(proj, ln_g, ln_b, w_cat, b_cat, layer, n_prompt_tiles, n_sample_rows):
    t_rows = proj.shape[0]
    n_tiles = t_rows // GM_CHUNK
    kern = functools.partial(_gmlp_kernel, n_prompt_tiles)
    return pl.pallas_call(
        kern,
        grid=(n_tiles,),
        in_specs=[
            pl.BlockSpec((GM_CHUNK, GM_WIDTH), lambda i: (i, 0)),
            pl.BlockSpec((GM_CHUNK, GM_WIDTH), lambda i: (i, 1)),
            pl.BlockSpec((None, 1, GM_WIDTH), lambda i: (layer, 0, 0)),
            pl.BlockSpec((None, 1, GM_WIDTH), lambda i: (layer, 0, 0)),
            pl.BlockSpec((None, None, GM_HEADS, GM_CHUNK, GM_CHUNK),
                         lambda i: (layer, jnp.where(i < n_prompt_tiles, 0, 1), 0, 0, 0)),
            pl.BlockSpec((None, None, GM_CHUNK, GM_HEADS),
                         lambda i: (layer, jnp.where(i < n_prompt_tiles, 0, 1), 0, 0)),
        ],
        out_specs=[
            pl.BlockSpec((GM_CHUNK, GM_WIDTH), lambda i: (i, 0)),
            pl.BlockSpec((GM_CHUNK, GM_WIDTH), lambda i: (jnp.maximum(i - n_prompt_tiles, 0), 0)),
        ],
        out_shape=[
            jax.ShapeDtypeStruct((t_rows, GM_WIDTH), BF16),
            jax.ShapeDtypeStruct((n_sample_rows, GM_WIDTH), F32),
        ],
        compiler_params=_cparams(1),
        name="gmlp",
    )(proj, proj, ln_g.reshape(DEPTH, 1, GM_WIDTH), ln_b.reshape(DEPTH, 1, GM_WIDTH), w_cat, b_cat)


CONV_PAD = 8
SSD_PROMPT_CHUNK = 256


def _ssd_kernel(q, zero_init, *refs):
    z_ref, xa_ref, xb_ref, dt_ref = refs[:4]
    refs = refs[4:]
    h0_ref = c0_ref = None
    if not zero_init:
        h0_ref, c0_ref = refs[:2]
        refs = refs[2:]
    cw_ref, cb_ref, dtb_ref, alog_ref, dskip_ref, ng_ref = refs[:6]
    refs = refs[6:]
    out_ref, hfin_ref, cfin_ref, state_ref, win_ref, xs_ref, bc_ref, xdt_ref, xdd_ref, y_ref = refs
    c = pl.program_id(1)
    tail = M_CONV - 1
    lo = CONV_PAD - tail
    hp = M_HEADS * M_HEAD_DIM

    @pl.when(c == 0)
    def _():
        if zero_init:
            state_ref[...] = jnp.zeros_like(state_ref)
            win_ref[lo:CONV_PAD, :] = jnp.zeros((tail, CONV_DIM), F32)
        else:
            state_ref[...] = h0_ref[...].reshape(hp, M_STATE).T
            win_ref[lo:CONV_PAD, :] = c0_ref[...]

    half = CONV_DIM // 2
    win_ref[CONV_PAD:CONV_PAD + q, 0:half] = xa_ref[...].astype(F32)
    win_ref[CONV_PAD:CONV_PAD + q, half:CONV_DIM] = xb_ref[...].astype(F32)
    for r0 in range(0, q, ROW_BLOCK):
        conv = cb_ref[...]
        for k in range(M_CONV):
            conv = conv + win_ref[lo + k + r0:lo + k + r0 + ROW_BLOCK, :] * cw_ref[k:k + 1, :]
        act = _silu(conv)
        xs_ref[r0:r0 + ROW_BLOCK, :] = act[:, 0:M_WIDTH]
        bc_ref[r0:r0 + ROW_BLOCK, :] = act[:, M_WIDTH:CONV_DIM].astype(BF16)
    new_tail = win_ref[CONV_PAD + q - tail:CONV_PAD + q, :]
    cfin_ref[...] = new_tail
    win_ref[lo:CONV_PAD, :] = new_tail

    hi = lax.Precision.HIGHEST
    dt = jax.nn.softplus(dt_ref[...] + dtb_ref[...])
    a = -jnp.exp(alog_ref[...])
    r_idx = lax.broadcasted_iota(jnp.int32, (q, q), 0)
    c_idx = lax.broadcasted_iota(jnp.int32, (q, q), 1)
    causal = c_idx <= r_idx
    a_cs = jnp.dot(causal.astype(F32), dt * a, precision=hi, preferred_element_type=F32)
    eye = (lax.broadcasted_iota(jnp.int32, (M_HEADS, M_HEADS), 0)
           == lax.broadcasted_iota(jnp.int32, (M_HEADS, M_HEADS), 1)).astype(F32)
    a_cs_t = _dot_nt(eye, a_cs, precision=hi)
    widen = (lax.broadcasted_iota(jnp.int32, (M_HEADS, hp), 1) // M_HEAD_DIM
             == lax.broadcasted_iota(jnp.int32, (M_HEADS, hp), 0)).astype(F32)
    dt_w = jnp.dot(dt, widen, precision=hi, preferred_element_type=F32)
    acs_w = jnp.dot(a_cs, widen, precision=hi, preferred_element_type=F32)
    dskip_w = jnp.dot(jnp.broadcast_to(dskip_ref[...], (8, M_HEADS)), widen, precision=hi,
                      preferred_element_type=F32)[0:1, :]
    a_end_w = acs_w[q - 1:q, :]
    x_dt = xs_ref[...] * dt_w
    xdt_ref[...] = x_dt.astype(BF16)
    xdd_ref[...] = (x_dt * jnp.exp(a_end_w - acs_w)).astype(BF16)
    decay_blk_w = jnp.exp(a_end_w)

    heads_per_group = M_HEADS // M_GROUPS
    gw = M_WIDTH // M_GROUPS
    pair_lane = lax.broadcasted_iota(jnp.int32, (q, 2 * M_HEAD_DIM), 1)
    for g in range(M_GROUPS):
        gcols = slice(g * gw, (g + 1) * gw)
        b_g = bc_ref[:, g * M_STATE:(g + 1) * M_STATE]
        c_g = bc_ref[:, (M_GROUPS + g) * M_STATE:(M_GROUPS + g + 1) * M_STATE]
        cb = _dot_nt(c_g, b_g)
        st = state_ref[:, gcols]
        y_off = _dot(c_g, st.astype(BF16))
        state_ref[:, gcols] = st * decay_blk_w[:, gcols] + _dot_tn(b_g, xdd_ref[:, gcols])
        for pr in range(heads_per_group // 2):
            h0 = g * heads_per_group + 2 * pr
            pcols = slice(h0 * M_HEAD_DIM, (h0 + 2) * M_HEAD_DIM)
            x_pair = xdt_ref[:, pcols]
            ys = []
            for h in (h0, h0 + 1):
                seg = a_cs[:, h:h + 1] - a_cs_t[h:h + 1, :]
                scores = cb * jnp.exp(jnp.where(causal, seg, -jnp.inf))
                ys.append(_dot(scores.astype(BF16), x_pair))
            y_ref[:, pcols] = jnp.where(pair_lane < M_HEAD_DIM, ys[0], ys[1])
        y_ref[:, gcols] = (y_ref[:, gcols] + y_off * jnp.exp(acs_w[:, gcols])
                           + dskip_w[:, gcols] * xs_ref[:, gcols])

    @pl.when(c == pl.num_programs(1) - 1)
    def _():
        hfin_ref[...] = state_ref[...].T.reshape(M_HEADS, M_HEAD_DIM, M_STATE)

    gy = y_ref[...] * _silu(z_ref[...].astype(F32))
    ng = ng_ref[...]
    for g in range(M_GROUPS):
        cols = slice(g * gw, (g + 1) * gw)
        s = gy[:, cols]
        s = s * lax.rsqrt(jnp.mean(s * s, axis=-1, keepdims=True) + EPS)
        out_ref[:, cols] = (s * ng[:, cols]).astype(BF16)


def _ssd(proj, dt_raw, h0, c0, conv_w, conv_b, dt_bias, a_log, d_skip, mnorm_g, layer,
         row_start, n_seq, seq_len, q):
    n_c = seq_len // q
    blk0 = row_start // q
    half = CONV_DIM // 2
    z_blk = (2 * GM_WIDTH) // M_WIDTH
    xa_blk = (2 * GM_WIDTH + M_WIDTH) // half
    zero_init = h0 is None

    def rows(col):
        return lambda s, c: (blk0 + s * n_c + c, col)

    def per_layer(*shape):
        return pl.BlockSpec((None,) + shape, lambda s, c: (layer,) + (0,) * len(shape))

    in_specs = [
        pl.BlockSpec((q, M_WIDTH), rows(z_blk)),
        pl.BlockSpec((q, half), rows(xa_blk)),
        pl.BlockSpec((q, half), rows(xa_blk + 1)),
        pl.BlockSpec((q, M_HEADS), rows(0)),
    ]
    args = [proj, proj, proj, dt_raw]
    if not zero_init:
        in_specs += [
            pl.BlockSpec((None, None, M_HEADS, M_HEAD_DIM, M_STATE), lambda s, c: (layer, s, 0, 0, 0)),
            pl.BlockSpec((None, None, M_CONV - 1, CONV_DIM), lambda s, c: (layer, s, 0, 0)),
        ]
        args += [h0, c0]
    in_specs += [per_layer(M_CONV, CONV_DIM), per_layer(1, CONV_DIM), per_layer(1, M_HEADS),
                 per_layer(1, M_HEADS), per_layer(1, M_HEADS), per_layer(1, M_WIDTH)]
    args += [conv_w, conv_b.reshape(DEPTH, 1, CONV_DIM), dt_bias.reshape(DEPTH, 1, M_HEADS),
             a_log.reshape(DEPTH, 1, M_HEADS), d_skip.reshape(DEPTH, 1, M_HEADS),
             mnorm_g.reshape(DEPTH, 1, M_WIDTH)]
    return pl.pallas_call(
        functools.partial(_ssd_kernel, q, zero_init),
        grid=(n_seq, n_c),
        in_specs=in_specs,
        out_specs=[
            pl.BlockSpec((q, M_WIDTH), lambda s, c: (s * n_c + c, 0)),
            pl.BlockSpec((None, M_HEADS, M_HEAD_DIM, M_STATE), lambda s, c: (s, 0, 0, 0)),
            pl.BlockSpec((None, M_CONV - 1, CONV_DIM), lambda s, c: (s, 0, 0)),
        ],
        out_shape=[
            jax.ShapeDtypeStruct((n_seq * seq_len, M_WIDTH), BF16),
            jax.ShapeDtypeStruct((n_seq, M_HEADS, M_HEAD_DIM, M_STATE), F32),
            jax.ShapeDtypeStruct((n_seq, M_CONV - 1, CONV_DIM), F32),
        ],
        scratch_shapes=[
            pltpu.VMEM((M_STATE, M_WIDTH), F32),
            pltpu.VMEM((CONV_PAD + q, CONV_DIM), F32),
            pltpu.VMEM((q, M_WIDTH), F32),
            pltpu.VMEM((q, 2 * M_GROUPS * M_STATE), BF16),
            pltpu.VMEM((q, M_WIDTH), BF16),
            pltpu.VMEM((q, M_WIDTH), BF16),
            pltpu.VMEM((q, M_WIDTH), F32),
        ],
        compiler_params=_cparams(2),
        name="conv_ssd",
    )(*args)


def _outproj_kernel(n_first_ssm, n_first, gm_ref, wa_ref, wb_ref, gate_ref, *refs):
    n_ssm = 1 if n_first_ssm is None else 2
    ssm_refs, x_refs, o_ref = refs[:n_ssm], refs[n_ssm:-1], refs[-1]
    ssm = _stream_tile(ssm_refs, n_first_ssm, Ellipsis)
    acc = _dot(gm_ref[...], wa_ref[...].astype(BF16)) + _dot(ssm, wb_ref[...].astype(BF16))
    tm, tn = acc.shape
    nb = tm // ROW_BLOCK
    upd = acc.reshape(nb, ROW_BLOCK, tn) * gate_ref[...][:, None, :]
    o_ref[...] = _stream_tile(x_refs, n_first, Ellipsis) + upd.reshape(tm, tn)


def _outproj(gm, ssms, w_out, xs, modb, layer, tm=1024, tn=512):
    t_rows = gm.shape[0]
    nb = tm // ROW_BLOCK
    gate_blk = 2 * D_MODEL // tn
    assert all(x.shape[0] % tm == 0 for x in xs)
    assert all(x.shape[0] % tm == 0 for x in ssms)
    x_specs, n_first = _stream_specs(xs, (tm, tn), lambda j: j, D_MODEL // tn)
    ssm_specs, n_first_ssm = _stream_specs(ssms, (tm, M_WIDTH), lambda j: 0, D_MODEL // tn)
    return pl.pallas_call(
        functools.partial(_outproj_kernel, n_first_ssm, n_first),
        grid=(t_rows // tm, D_MODEL // tn),
        in_specs=[
            pl.BlockSpec((tm, GM_WIDTH), lambda i, j: (i, 0)),
            pl.BlockSpec((None, GM_WIDTH, tn), lambda i, j: (layer, 0, j)),
            pl.BlockSpec((None, M_WIDTH, tn), lambda i, j: (layer, 1, j)),
            pl.BlockSpec((nb, tn), lambda i, j: (i, gate_blk + j)),
        ] + ssm_specs + x_specs,
        out_specs=pl.BlockSpec((tm, tn), lambda i, j: (i, j)),
        out_shape=jax.ShapeDtypeStruct((t_rows, D_MODEL), F32),
        compiler_params=_cparams(2),
        name="out_proj",
    )(gm, w_out, w_out, modb, *ssms, *xs)


ROUTER_ROWS = 24
ROUTE_ROWS = 8
EXPERT_TILE = 768


def _row_argmax_first(rows):
    m = rows[0]
    for r in rows[1:]:
        m = jnp.maximum(m, r)
    idx = jnp.full(m.shape, len(rows), jnp.int32)
    for k in reversed(range(len(rows))):
        idx = jnp.where(rows[k] == m, k, idx)
    return m, idx


def _router_kernel(x_ref, g_ref, shift_ref, scale_ref, wr_ref, br_ref, h_ref, route_ref, count_ref, run_ref):
    n_blk = shift_ref.shape[0]
    g = g_ref[...]

    @pl.when(pl.program_id(0) == 0)
    def _():
        run_ref[...] = jnp.zeros_like(run_ref)

    def body(r, carry):
        rows = pl.ds(pl.multiple_of(r * ROW_BLOCK, ROW_BLOCK), ROW_BLOCK)
        xf = x_ref[rows, :]
        y = xf * lax.rsqrt(jnp.mean(xf * xf, axis=-1, keepdims=True) + EPS) * g
        h_ref[rows, 0:D_MODEL] = y * (1.0 + scale_ref[pl.ds(r, 1), :]) + shift_ref[pl.ds(r, 1), :]
        return carry

    lax.fori_loop(0, n_blk, body, 0)
    logits = _dot_nt(wr_ref[...], h_ref[:, 0:D_MODEL], precision=lax.Precision.HIGHEST) + br_ref[...]
    lg = [logits[k:k + 1, :] for k in range(N_EGROUPS)]
    g_max, g_idx = _row_argmax_first(lg)
    g_den = lg[0] * 0.0
    for r in lg:
        g_den = g_den + jnp.exp(r - g_max)
    g_top = 1.0 / g_den
    e_in = []
    for k in range(EXPERTS_PER_GROUP):
        sel = lg[0] * 0.0
        for gi in range(N_EGROUPS):
            row = N_EGROUPS + gi * EXPERTS_PER_GROUP + k
            sel = jnp.where(g_idx == gi, logits[row:row + 1, :], sel)
        e_in.append(sel)
    e_max, i1 = _row_argmax_first(e_in)
    e_exp = [jnp.exp(r - e_max) for r in e_in]
    e_den = e_exp[0]
    for r in e_exp[1:]:
        e_den = e_den + r
    p = [r / e_den for r in e_exp]
    p1 = p[0] * 0.0
    for k in range(EXPERTS_PER_GROUP):
        p1 = jnp.where(i1 == k, p[k], p1)
    rest = [jnp.where(i1 == k, -1.0, p[k]) for k in range(EXPERTS_PER_GROUP)]
    p2, i2 = _row_argmax_first(rest)
    norm = p1 + p2
    w1 = p1 / norm * g_top
    w2 = p2 / norm * g_top
    tm = logits.shape[1]
    row = lax.broadcasted_iota(jnp.int32, (ROUTE_ROWS, tm), 0)
    onehot = (row == g_idx).astype(F32)
    before = (lax.broadcasted_iota(jnp.int32, (tm, tm), 0)
              < lax.broadcasted_iota(jnp.int32, (tm, tm), 1)).astype(BF16)
    earlier = _dot(onehot.astype(BF16), before)
    rank = jnp.sum(onehot * (earlier + run_ref[:, 0:1]), axis=0, keepdims=True)
    run_ref[...] += jnp.sum(onehot, axis=1, keepdims=True)
    count_ref[...] = run_ref[...]

    route_ref[0:1, :] = g_idx.astype(F32)
    route_ref[1:2, :] = rank
    for k in range(EXPERTS_PER_GROUP):
        route_ref[2 + k:3 + k, :] = jnp.where(i1 == k, w1, 0.0) + jnp.where(i2 == k, w2, 0.0)
    route_ref[2 + EXPERTS_PER_GROUP:ROUTE_ROWS, :] = jnp.zeros((ROUTE_ROWS - 2 - EXPERTS_PER_GROUP, tm), F32)
    place = (lax.broadcasted_iota(jnp.int32, (ROUTE_ROWS, LANES), 0)
             == lax.broadcasted_iota(jnp.int32, (ROUTE_ROWS, LANES), 1)).astype(F32)
    h_ref[:, D_MODEL:D_MODEL + LANES] = lax.dot_general(
        route_ref[...], place, (((0,), (0,)), ((), ())), precision=lax.Precision.HIGHEST,
        preferred_element_type=F32)


def _router(x, norm_g, modb, w_r, b_r, layer, tm=512):
    t_rows = x.shape[0]
    nb = tm // ROW_BLOCK
    return pl.pallas_call(
        _router_kernel,
        grid=(t_rows // tm,),
        in_specs=[
            pl.BlockSpec((tm, D_MODEL), lambda i: (i, 0)),
            pl.BlockSpec((None, 1, D_MODEL), lambda i: (layer, 0, 0)),
            pl.BlockSpec((nb, D_MODEL), lambda i: (i, 3)),
            pl.BlockSpec((nb, D_MODEL), lambda i: (i, 4)),
            pl.BlockSpec((ROUTER_ROWS, D_MODEL), lambda i: (0, 0)),
            pl.BlockSpec((ROUTER_ROWS, 1), lambda i: (0, 0)),
        ],
        out_specs=[
            pl.BlockSpec((tm, D_MODEL + LANES), lambda i: (i, 0)),
            pl.BlockSpec((ROUTE_ROWS, tm), lambda i: (0, i)),
            pl.BlockSpec((ROUTE_ROWS, LANES), lambda i: (0, 0)),
        ],
        out_shape=[
            jax.ShapeDtypeStruct((t_rows, D_MODEL + LANES), F32),
            jax.ShapeDtypeStruct((ROUTE_ROWS, t_rows), F32),
            jax.ShapeDtypeStruct((ROUTE_ROWS, LANES), F32),
        ],
        scratch_shapes=[pltpu.VMEM((ROUTE_ROWS, LANES), F32)],
        compiler_params=_cparams(1),
        name="router",
    )(x, norm_g.reshape(DEPTH, 1, D_MODEL), modb, modb, w_r, b_r)


def _dispatch_tables(route, counts, t_rows, tm, n_tiles):
    gid = route[0].astype(jnp.int32)
    rank = route[1].astype(jnp.int32)
    cnt = counts[:N_EGROUPS, 0].astype(jnp.int32)
    padded = (cnt + tm - 1) // tm * tm
    ends = jnp.cumsum(padded)
    starts = ends - padded
    pos = starts[gid] + rank
    src = jnp.zeros((n_tiles * tm,), jnp.int32).at[pos].set(jnp.arange(t_rows, dtype=jnp.int32))
    tile_start = jnp.arange(n_tiles, dtype=jnp.int32) * tm
    tile_valid = (tile_start < ends[-1]).astype(jnp.int32)
    tile_group = jnp.sum((tile_start[:, None] >= ends[None, :]).astype(jnp.int32), axis=1)
    tile_group = jnp.minimum(tile_group, N_EGROUPS - 1)
    return pos, src, tile_group, tile_valid


def _start_row_gather(table_ref, idx_ref, idx_base, buf_ref, sem):
    def issue(r, carry):
        pltpu.make_async_copy(table_ref.at[pl.ds(idx_ref[idx_base + r], 1), :],
                              buf_ref.at[pl.ds(r, 1), :], sem).start()
        return carry

    lax.fori_loop(0, buf_ref.shape[0], issue, 0, unroll=8)


def _wait_row_gather(table_ref, buf_ref, sem):
    pltpu.make_async_copy(table_ref.at[pl.ds(0, buf_ref.shape[0]), :], buf_ref, sem).wait()


def _experts_kernel(f_split, tg_ref, tv_ref, src_ref, wg_ref, wu_ref, wd_ref, h_ref, y_ref,
                    xbuf_ref, xb_ref, sem):
    i = pl.program_id(0)
    s = pl.program_id(1)
    n_tiles = pl.num_programs(0)
    tm = xb_ref.shape[0]
    valid = tv_ref[i] == 1
    slot = i % 2

    @pl.when((s == 0) & (i == 0) & valid)
    def _():
        _start_row_gather(h_ref, src_ref, 0, xbuf_ref.at[0], sem.at[0])

    @pl.when((s == 0) & valid)
    def _():
        _wait_row_gather(h_ref, xbuf_ref.at[slot], sem.at[slot])

    @pl.when((s == 0) & (i + 1 < n_tiles))
    def _():
        @pl.when(tv_ref[jnp.minimum(i + 1, n_tiles - 1)] == 1)
        def _():
            _start_row_gather(h_ref, src_ref, (i + 1) * tm, xbuf_ref.at[1 - slot], sem.at[1 - slot])

    @pl.when(s == 0)
    def _():
        y_ref[...] = jnp.zeros_like(y_ref)

    @pl.when((s == 0) & valid)
    def _():
        xb_ref[...] = xbuf_ref[slot, :, 0:D_MODEL].astype(BF16)

    @pl.when(valid)
    def _():
        e = s // f_split
        x = xb_ref[...]
        gate_act = _dot(x, wg_ref[...].astype(BF16))
        up = _dot(x, wu_ref[...].astype(BF16))
        record = xbuf_ref[slot, :, D_MODEL:D_MODEL + LANES]
        lane = lax.broadcasted_iota(jnp.int32, record.shape, 1)
        gate_col = jnp.sum(jnp.where(lane == 2 + e, record, 0.0), axis=1, keepdims=True)
        hid = (_silu(gate_act) * up * gate_col).astype(BF16)
        y_ref[...] += _dot(hid, wd_ref[...].astype(BF16))


def _experts(h, src, tile_group, tile_valid, w_gate, w_up, w_down, layer, tm, f_split=2):
    n_rows = src.shape[0]
    fb = D_EXPERT // f_split
    last = EXPERTS_PER_GROUP * f_split - 1

    def expert_of(i, s, tg, tv, sr):
        return tg[i] * EXPERTS_PER_GROUP + jnp.where(tv[i] == 1, s, last) // f_split

    def half_of(i, s, tg, tv, sr):
        return jnp.where(tv[i] == 1, s, last) % f_split

    grid_spec = pltpu.PrefetchScalarGridSpec(
        num_scalar_prefetch=3,
        grid=(n_rows // tm, EXPERTS_PER_GROUP * f_split),
        in_specs=[
            pl.BlockSpec((None, None, D_MODEL, fb), lambda i, s, *p: (layer, expert_of(i, s, *p), 0, half_of(i, s, *p))),
            pl.BlockSpec((None, None, D_MODEL, fb), lambda i, s, *p: (layer, expert_of(i, s, *p), 0, half_of(i, s, *p))),
            pl.BlockSpec((None, None, fb, D_MODEL), lambda i, s, *p: (layer, expert_of(i, s, *p), half_of(i, s, *p), 0)),
            pl.BlockSpec(memory_space=pl.ANY),
        ],
        out_specs=pl.BlockSpec((tm, D_MODEL), lambda i, s, *_: (i, 0)),
        scratch_shapes=[
            pltpu.VMEM((2, tm, D_MODEL + LANES), F32),
            pltpu.VMEM((tm, D_MODEL), BF16),
            pltpu.SemaphoreType.DMA((2,)),
        ],
    )
    return pl.pallas_call(
        functools.partial(_experts_kernel, f_split),
        grid_spec=grid_spec,
        out_shape=jax.ShapeDtypeStruct((n_rows, D_MODEL), F32),
        compiler_params=_cparams(2),
        name="experts",
    )(tile_group, tile_valid, src, w_gate, w_up, w_down, h)


def _combine_kernel(final_norm, n_first, pos_ref, x_ref, gate_ref, fg_ref, ys_ref, *refs):
    o_refs, buf_ref, sem = refs[:-2], refs[-2], refs[-1]
    tm, d = x_ref.shape
    i = pl.program_id(0)
    slot = i % 2

    @pl.when(i == 0)
    def _():
        _start_row_gather(ys_ref, pos_ref, 0, buf_ref.at[0], sem.at[0])

    _wait_row_gather(ys_ref, buf_ref.at[slot], sem.at[slot])

    @pl.when(i + 1 < pl.num_programs(0))
    def _():
        _start_row_gather(ys_ref, pos_ref, (i + 1) * tm, buf_ref.at[1 - slot], sem.at[1 - slot])

    nb = tm // ROW_BLOCK
    upd = buf_ref[slot].reshape(nb, ROW_BLOCK, d) * gate_ref[...][:, None, :]
    x = x_ref[...] + upd.reshape(tm, d)
    if final_norm:
        x = x * lax.rsqrt(jnp.mean(x * x, axis=-1, keepdims=True) + EPS) * fg_ref[...]
    if n_first is None:
        o_refs[0][...] = x
    else:
        @pl.when(pl.program_id(0) < n_first)
        def _():
            o_refs[0][...] = x

        @pl.when(pl.program_id(0) >= n_first)
        def _():
            o_refs[1][...] = x


def _combine(x, ys, pos, modb, final_g, final_norm, split_rows=None, tm=512):
    t_rows = x.shape[0]
    nb = tm // ROW_BLOCK
    if split_rows is None:
        n_first = None
        out_specs = pl.BlockSpec((tm, D_MODEL), lambda i, p: (i, 0))
        out_shape = jax.ShapeDtypeStruct((t_rows, D_MODEL), F32)
    else:
        assert split_rows % tm == 0 and (t_rows - split_rows) % tm == 0
        n_first = split_rows // tm
        out_specs = [pl.BlockSpec((tm, D_MODEL), lambda i, p: (jnp.minimum(i, n_first - 1), 0)),
                     pl.BlockSpec((tm, D_MODEL), lambda i, p: (jnp.maximum(i - n_first, 0), 0))]
        out_shape = [jax.ShapeDtypeStruct((split_rows, D_MODEL), F32),
                     jax.ShapeDtypeStruct((t_rows - split_rows, D_MODEL), F32)]
    grid_spec = pltpu.PrefetchScalarGridSpec(
        num_scalar_prefetch=1,
        grid=(t_rows // tm,),
        in_specs=[
            pl.BlockSpec((tm, D_MODEL), lambda i, p: (i, 0)),
            pl.BlockSpec((nb, D_MODEL), lambda i, p: (i, 5)),
            pl.BlockSpec((1, D_MODEL), lambda i, p: (0, 0)),
            pl.BlockSpec(memory_space=pl.ANY),
        ],
        out_specs=out_specs,
        scratch_shapes=[pltpu.VMEM((2, tm, D_MODEL), F32), pltpu.SemaphoreType.DMA((2,))],
    )
    return pl.pallas_call(
        functools.partial(_combine_kernel, final_norm, n_first),
        grid_spec=grid_spec,
        out_shape=out_shape,
        compiler_params=_cparams(1),
        name="combine",
    )(pos, x, modb, final_g.reshape(1, D_MODEL), ys)


def kernel(x_prompt, x_sample, state_ssm, state_conv, c_prompt, c_sample, norm1_g, norm2_g, w_ada, b_ada,
           w_in, gm_ln_g, gm_ln_b, gm_w_s, gm_b_s, conv_w, conv_b, dt_bias, a_log, d_skip, mnorm_g, w_out,
           w_router_group, b_router_group, w_router_expert, b_router_expert, w_gate, w_up, w_down,
           final_norm_g):
    n_p, len_p, _ = x_prompt.shape
    n_s, len_s, _ = x_sample.shape
    assert len_p % GM_CHUNK == 0 and len_s == ROW_BLOCK and (n_s * len_s) % GM_CHUNK == 0
    rows_p, rows_s = n_p * len_p, n_s * len_s
    t_rows = rows_p + rows_s
    n_seq = n_p + n_s
    blk_per_prompt = len_p // ROW_BLOCK
    assert t_rows % EXPERT_TILE == 0
    n_expert_tiles = t_rows // EXPERT_TILE + N_EGROUPS

    xs = (x_prompt.reshape(rows_p, D_MODEL), x_sample.reshape(rows_s, D_MODEL))

    c_all = jnp.concatenate([c_prompt, c_sample], axis=0)
    c_pad = jnp.pad(c_all, ((0, (-n_seq) % 8), (0, 0)))
    mod = _ada(c_pad, w_ada, b_ada)
    modb_all = jnp.concatenate(
        [jnp.repeat(mod[:, :n_p], blk_per_prompt, axis=1), mod[:, n_p:n_seq]], axis=1)

    half_w = gm_w_s[:, :, :ROW_BLOCK, :ROW_BLOCK]
    w_cat = jnp.stack([gm_w_s, jnp.tile(half_w, (1, 1, 2, 2))], axis=1)
    b_t = jnp.swapaxes(gm_b_s, 1, 2)
    b_cat = jnp.stack([b_t, jnp.tile(b_t[:, :ROW_BLOCK], (1, 2, 1))], axis=1)

    w_in_t = jnp.swapaxes(w_in, 1, 2)
    w_r =jnp.concatenate([jnp.swapaxes(w_router_group, 1, 2), jnp.swapaxes(w_router_expert, 1, 2)], axis=1)
    w_r = jnp.pad(w_r, ((0, 0), (0, ROUTER_ROWS - w_r.shape[1]), (0, 0)))
    b_r = jnp.concatenate([b_router_group, b_router_expert], axis=1)
    b_r = jnp.pad(b_r, ((0, 0), (0, ROUTER_ROWS - b_r.shape[1])))[:, :, None]

    ssm_p, conv_p, ssm_s, conv_s, v_rows = [], [], [], [], []
    for l in range(DEPTH):
        last = l == DEPTH - 1
        modb = modb_all[l]
        proj, dt_raw = _inproj(xs, norm1_g, modb, w_in_t, l)
        gm_out, vn = _gmlp(proj, gm_ln_g, gm_ln_b, w_cat, b_cat, l, rows_p // GM_CHUNK, rows_s)
        ssm_args = (conv_w, conv_b, dt_bias, a_log, d_skip, mnorm_g, l)
        ssm_out_p, hfin_p, cfin_p = _ssd(proj, dt_raw, None, None, *ssm_args, 0, n_p, len_p, SSD_PROMPT_CHUNK)
        ssm_out_s, hfin_s, cfin_s = _ssd(proj, dt_raw, state_ssm, state_conv, *ssm_args, rows_p, n_s, len_s, len_s)
        x = _outproj(gm_out, (ssm_out_p, ssm_out_s), w_out, xs, modb, l)
        h2, route, counts = _router(x, norm2_g, modb, w_r[l], b_r[l], l)
        pos, src, tile_group, tile_valid = _dispatch_tables(route, counts, t_rows, EXPERT_TILE, n_expert_tiles)
        ys = _experts(h2, src, tile_group, tile_valid, w_gate, w_up, w_down, l, EXPERT_TILE)
        out = _combine(x, ys, pos, modb, final_norm_g, last, rows_p if last else None)
        xs = tuple(out) if last else (out,)
        ssm_p.append(hfin_p)
        conv_p.append(cfin_p)
        ssm_s.append(hfin_s)
        conv_s.append(cfin_s)
        v_rows.append(vn.reshape(n_s, len_s, GM_WIDTH))

    y_prompt = xs[0].reshape(n_p, len_p, D_MODEL)
    y_sample = xs[1].reshape(n_s, len_s, D_MODEL)
    return (y_prompt, y_sample, jnp.stack(ssm_p), jnp.stack(conv_p), jnp.stack(ssm_s), jnp.stack(conv_s),
            jnp.stack(v_rows))
```

```python
import functools
import math

import jax
import jax.numpy as jnp
import numpy as np
from jax import lax
from jax.experimental import pallas as pl
from jax.experimental.pallas import tpu as pltpu

D_MODEL = 2048
DEPTH = 2
GM_WIDTH = 1024
GM_HEADS = 8
GM_HEAD_DIM = 128
GM_CHUNK = 128
M_WIDTH = 1024
M_HEAD_DIM = 64
M_HEADS = 16
M_GROUPS = 4
M_STATE = 128
M_CONV = 4
CONV_DIM = M_WIDTH + 2 * M_GROUPS * M_STATE
MAIN_DIM = 2 * GM_WIDTH + M_WIDTH + CONV_DIM
N_EGROUPS = 4
EXPERTS_PER_GROUP = 4
N_EXPERTS = 16
D_EXPERT = 512
N_MOD = 6
EPS = 1e-6

ROW_BLOCK = 64
LANES = 128
VMEM_LIMIT = 56 * 1024 * 1024

F32 = jnp.float32
BF16 = jnp.bfloat16


def _cparams(n_axes):
    return pltpu.CompilerParams(
        dimension_semantics=("arbitrary",) * n_axes, vmem_limit_bytes=VMEM_LIMIT)


def _silu(x):
    return x * jax.nn.sigmoid(x)


def _gelu_tanh(x):
    c = math.sqrt(2.0 / math.pi)
    return 0.5 * x * (1.0 + jnp.tanh(c * (x + 0.044715 * (x * x * x))))


def _dot(a, b):
    return jnp.dot(a, b, preferred_element_type=F32)


def _dot_nt(a, b, precision=None):
    return lax.dot_general(a, b, (((1,), (1,)), ((), ())), precision=precision,
                           preferred_element_type=F32)


def _dot_tn(a, b):
    return lax.dot_general(a, b, (((0,), (0,)), ((), ())), preferred_element_type=F32)


def _ada_kernel(c_ref, w_ref, b_ref, o_ref):
    c = _silu(c_ref[...]).astype(BF16)
    o_ref[...] = _dot(c, w_ref[...].astype(BF16)) + b_ref[...]


def _ada(c_pad, w_ada, b_ada):
    rows = c_pad.shape[0]
    tn = 1024
    n_out = N_MOD * D_MODEL
    return pl.pallas_call(
        _ada_kernel,
        grid=(DEPTH, n_out // tn),
        in_specs=[
            pl.BlockSpec((rows, D_MODEL), lambda l, j: (0, 0)),
            pl.BlockSpec((None, D_MODEL, tn), lambda l, j: (l, 0, j)),
            pl.BlockSpec((None, 1, tn), lambda l, j: (l, 0, j)),
        ],
        out_specs=pl.BlockSpec((None, rows, tn), lambda l, j: (l, 0, j)),
        out_shape=jax.ShapeDtypeStruct((DEPTH, rows, n_out), F32),
        compiler_params=_cparams(2),
        name="ada_ln",
    )(c_pad, w_ada, b_ada.reshape(DEPTH, 1, n_out))


def _stream_specs(xs, block, col_of, n_j):
    if len(xs) == 1:
        return [pl.BlockSpec(block, lambda i, j: (i, col_of(j)))], None
    n_first = xs[0].shape[0] // block[0]

    def first(i, j):
        return jnp.minimum(i, n_first - 1), jnp.where(i < n_first, col_of(j), col_of(n_j - 1))

    def second(i, j):
        return jnp.maximum(i - n_first, 0), jnp.where(i < n_first, col_of(0), col_of(j))

    return [pl.BlockSpec(block, first), pl.BlockSpec(block, second)], n_first


def _stream_tile(x_refs, n_first, idx):
    if n_first is None:
        return x_refs[0][idx]
    return jnp.where(pl.program_id(0) < n_first, x_refs[0][idx], x_refs[1][idx])


def _inproj_kernel(n_first, *refs):
    n_x = 1 if n_first is None else 2
    x_refs = refs[:n_x]
    g_ref, shift_ref, scale_ref, w_ref, wdt_ref, proj_ref, dt_ref, h_ref = refs[n_x:]
    j = pl.program_id(1)
    n_blk = shift_ref.shape[0]

    @pl.when(j == 0)
    def _():
        g = g_ref[...]

        def body(r, carry):
            rows = pl.ds(pl.multiple_of(r * ROW_BLOCK, ROW_BLOCK), ROW_BLOCK)
            xf = _stream_tile(x_refs, n_first, (rows, slice(None)))
            y = xf * lax.rsqrt(jnp.mean(xf * xf, axis=-1, keepdims=True) + EPS) * g
            y = y * (1.0 + scale_ref[pl.ds(r, 1), :]) + shift_ref[pl.ds(r, 1), :]
            h_ref[rows, :] = y.astype(BF16)
            return carry

        lax.fori_loop(0, n_blk, body, 0)
        dt_ref[...] = _dot_nt(h_ref[...], wdt_ref[...].astype(BF16))

    proj_ref[...] = _dot_nt(h_ref[...], w_ref[...].astype(BF16)).astype(BF16)


def _inproj(xs, norm_g, modb, w_in_t, layer, tm=1024, tn=512):
    t_rows = sum(x.shape[0] for x in xs)
    nb = tm // ROW_BLOCK
    assert all(x.shape[0] % tm == 0 for x in xs)
    x_specs, n_first = _stream_specs(xs, (tm, D_MODEL), lambda j: 0, MAIN_DIM // tn)
    return pl.pallas_call(
        functools.partial(_inproj_kernel, n_first),
        grid=(t_rows // tm, MAIN_DIM // tn),
        in_specs=x_specs + [
            pl.BlockSpec((None, 1, D_MODEL), lambda i, j: (layer, 0, 0)),
            pl.BlockSpec((nb, D_MODEL), lambda i, j: (i, 0)),
            pl.BlockSpec((nb, D_MODEL), lambda i, j: (i, 1)),
            pl.BlockSpec((None, tn, D_MODEL), lambda i, j: (layer, j, 0)),
            pl.BlockSpec((None, M_HEADS, D_MODEL), lambda i, j: (layer, MAIN_DIM // M_HEADS, 0)),
        ],
        out_specs=[
            pl.BlockSpec((tm, tn), lambda i, j: (i, j)),
            pl.BlockSpec((tm, M_HEADS), lambda i, j: (i, 0)),
        ],
        out_shape=[
            jax.ShapeDtypeStruct((t_rows, MAIN_DIM), BF16),
            jax.ShapeDtypeStruct((t_rows, M_HEADS), F32),
        ],
        scratch_shapes=[pltpu.VMEM((tm, D_MODEL), BF16)],
        compiler_params=_cparams(2),
        name="in_proj",
    )(*xs, norm_g.reshape(DEPTH, 1, D_MODEL), modb, modb, w_in_t, w_in_t)


def _gmlp_kernel(n_prompt_tiles, u_ref, v_ref, lng_ref, lnb_ref, w_ref, b_ref, out_ref, vn_ref):
    i = pl.program_id(0)
    u = _gelu_tanh(u_ref[...].astype(F32))
    v = _gelu_tanh(v_ref[...].astype(F32))
    mu = jnp.mean(v, axis=-1, keepdims=True)
    vc = v - mu
    vn = vc * lax.rsqrt(jnp.mean(vc * vc, axis=-1, keepdims=True) + EPS)
    vn = vn * lng_ref[...] + lnb_ref[...]
    vn_ref[...] = vn
    t_idx = lax.broadcasted_iota(jnp.int32, (GM_CHUNK, GM_CHUNK), 0)
    s_idx = lax.broadcasted_iota(jnp.int32, (GM_CHUNK, GM_CHUNK), 1)
    same_seq = (t_idx // ROW_BLOCK) == (s_idx // ROW_BLOCK)
    allowed = (s_idx <= t_idx) & (same_seq | (i < n_prompt_tiles))
    vnb = vn.astype(BF16)
    bias = b_ref[...]
    for h in range(GM_HEADS):
        cols = slice(h * GM_HEAD_DIM, (h + 1) * GM_HEAD_DIM)
        wm = jnp.where(allowed, w_ref[h], 0.0).astype(BF16)
        mixed = _dot(wm, vnb[:, cols]) + bias[:, h:h + 1]
        out_ref[:, cols] = (u[:, cols] * mixed).astype(BF16)


def _gmlp(proj, ln_g, ln_b, w_cat, b_cat, layer, n_prompt_tiles, n_sample_rows):
    t_rows = proj.shape[0]
    n_tiles = t_rows // GM_CHUNK
    kern = functools.partial(_gmlp_kernel, n_prompt_tiles)
    return pl.pallas_call(
        kern,
        grid=(n_tiles,),
        in_specs=[
            pl.BlockSpec((GM_CHUNK, GM_WIDTH), lambda i: (i, 0)),
            pl.BlockSpec((GM_CHUNK, GM_WIDTH), lambda i: (i, 1)),
            pl.BlockSpec((None, 1, GM_WIDTH), lambda i: (layer, 0, 0)),
            pl.BlockSpec((None, 1, GM_WIDTH), lambda i: (layer, 0, 0)),
            pl.BlockSpec((None, None, GM_HEADS, GM_CHUNK, GM_CHUNK),
                         lambda i: (layer, jnp.where(i < n_prompt_tiles, 0, 1), 0, 0, 0)),
            pl.BlockSpec((None, None, GM_CHUNK, GM_HEADS),
                         lambda i: (layer, jnp.where(i < n_prompt_tiles, 0, 1), 0, 0)),
        ],
        out_specs=[
            pl.BlockSpec((GM_CHUNK, GM_WIDTH), lambda i: (i, 0)),
            pl.BlockSpec((GM_CHUNK, GM_WIDTH), lambda i: (jnp.maximum(i - n_prompt_tiles, 0), 0)),
        ],
        out_shape=[
            jax.ShapeDtypeStruct((t_rows, GM_WIDTH), BF16),
            jax.ShapeDtypeStruct((n_sample_rows, GM_WIDTH), F32),
        ],
        compiler_params=_cparams(1),
        name="gmlp",
    )(proj, proj, ln_g.reshape(DEPTH, 1, GM_WIDTH), ln_b.reshape(DEPTH, 1, GM_WIDTH), w_cat, b_cat)


CONV_PAD = 8
SSD_PROMPT_CHUNK = 256


def _widen(v, onehot):
    p0 = v.astype(BF16)
    r1 = v - p0.astype(F32)
    p1 = r1.astype(BF16)
    p2 = (r1 - p1.astype(F32)).astype(BF16)
    return _dot(p0, onehot) + _dot(p1, onehot) + _dot(p2, onehot)


def _ssd_kernel(q, zero_init, *refs):
    z_ref, xa_ref, xb_ref, dt_ref = refs[:4]
    refs = refs[4:]
    h0_ref = c0_ref = None
    if not zero_init:
        h0_ref, c0_ref = refs[:2]
        refs = refs[2:]
    cw_ref, cb_ref, dtb_ref, alog_ref, dskip_ref, ng_ref = refs[:6]
    refs = refs[6:]
    out_ref, hfin_ref, cfin_ref, state_ref, win_ref, xs_ref, bc_ref, xdt_ref, xdd_ref, y_ref = refs
    c = pl.program_id(1)
    tail = M_CONV - 1
    lo = CONV_PAD - tail
    hp = M_HEADS * M_HEAD_DIM

    @pl.when(c == 0)
    def _():
        if zero_init:
            state_ref[...] = jnp.zeros_like(state_ref)
            win_ref[lo:CONV_PAD, :] = jnp.zeros((tail, CONV_DIM), F32)
        else:
            state_ref[...] = h0_ref[...].reshape(hp, M_STATE).T
            win_ref[lo:CONV_PAD, :] = c0_ref[...]

    half = CONV_DIM // 2
    win_ref[CONV_PAD:CONV_PAD + q, 0:half] = xa_ref[...].astype(F32)
    win_ref[CONV_PAD:CONV_PAD + q, half:CONV_DIM] = xb_ref[...].astype(F32)
    for r0 in range(0, q, ROW_BLOCK):
        conv = cb_ref[...]
        for k in range(M_CONV):
            conv = conv + win_ref[lo + k + r0:lo + k + r0 + ROW_BLOCK, :] * cw_ref[k:k + 1, :]
        act = _silu(conv)
        xs_ref[r0:r0 + ROW_BLOCK, :] = act[:, 0:M_WIDTH]
        bc_ref[r0:r0 + ROW_BLOCK, :] = act[:, M_WIDTH:CONV_DIM].astype(BF16)
    new_tail = win_ref[CONV_PAD + q - tail:CONV_PAD + q, :]
    cfin_ref[...] = new_tail
    win_ref[lo:CONV_PAD, :] = new_tail

    hi = lax.Precision.HIGHEST
    dt = jax.nn.softplus(dt_ref[...] + dtb_ref[...])
    a = -jnp.exp(alog_ref[...])
    r_idx = lax.broadcasted_iota(jnp.int32, (q, q), 0)
    c_idx = lax.broadcasted_iota(jnp.int32, (q, q), 1)
    causal = c_idx <= r_idx
    a_cs = jnp.dot(causal.astype(F32), dt * a, precision=hi, preferred_element_type=F32)
    eye = (lax.broadcasted_iota(jnp.int32, (M_HEADS, M_HEADS), 0)
           == lax.broadcasted_iota(jnp.int32, (M_HEADS, M_HEADS), 1)).astype(F32)
    a_cs_t = _dot_nt(eye, a_cs, precision=hi)
    widen = (lax.broadcasted_iota(jnp.int32, (M_HEADS, hp), 1) // M_HEAD_DIM
             == lax.broadcasted_iota(jnp.int32, (M_HEADS, hp), 0)).astype(BF16)
    dt_w = _widen(dt, widen)
    acs_w = _widen(a_cs, widen)
    dskip_w = _widen(jnp.broadcast_to(dskip_ref[...], (8, M_HEADS)), widen)[0:1, :]
    a_end_w = acs_w[q - 1:q, :]
    x_dt = xs_ref[...] * dt_w
    xdt_ref[...] = x_dt.astype(BF16)
    xdd_ref[...] = (x_dt * jnp.exp(a_end_w - acs_w)).astype(BF16)
    decay_blk_w = jnp.exp(a_end_w)

    heads_per_group = M_HEADS // M_GROUPS
    gw = M_WIDTH // M_GROUPS
    pair_lane = lax.broadcasted_iota(jnp.int32, (q, 2 * M_HEAD_DIM), 1)
    for g in range(M_GROUPS):
        gcols = slice(g * gw, (g + 1) * gw)
        b_g = bc_ref[:, g * M_STATE:(g + 1) * M_STATE]
        c_g = bc_ref[:, (M_GROUPS + g) * M_STATE:(M_GROUPS + g + 1) * M_STATE]
        cb = _dot_nt(c_g, b_g)
        st = state_ref[:, gcols]
        y_off = _dot(c_g, st.astype(BF16))
        state_ref[:, gcols] = st * decay_blk_w[:, gcols] + _dot_tn(b_g, xdd_ref[:, gcols])
        for pr in range(heads_per_group // 2):
            h0 = g * heads_per_group + 2 * pr
            pcols = slice(h0 * M_HEAD_DIM, (h0 + 2) * M_HEAD_DIM)
            x_pair = xdt_ref[:, pcols]
            ys = []
            for h in (h0, h0 + 1):
                seg = a_cs[:, h:h + 1] - a_cs_t[h:h + 1, :]
                scores = cb * jnp.exp(jnp.where(causal, seg, -jnp.inf))
                ys.append(_dot(scores.astype(BF16), x_pair))
            y_ref[:, pcols] = jnp.where(pair_lane < M_HEAD_DIM, ys[0], ys[1])
        y_ref[:, gcols] = (y_ref[:, gcols] + y_off * jnp.exp(acs_w[:, gcols])
                           + dskip_w[:, gcols] * xs_ref[:, gcols])

    @pl.when(c == pl.num_programs(1) - 1)
    def _():
        hfin_ref[...] = state_ref[...].T.reshape(M_HEADS, M_HEAD_DIM, M_STATE)

    gy = y_ref[...] * _silu(z_ref[...].astype(F32))
    ng = ng_ref[...]
    for g in range(M_GROUPS):
        cols = slice(g * gw, (g + 1) * gw)
        s = gy[:, cols]
        s = s * lax.rsqrt(jnp.mean(s * s, axis=-1, keepdims=True) + EPS)
        out_ref[:, cols] = (s * ng[:, cols]).astype(BF16)


def _ssd(proj, dt_raw, h0, c0, conv_w, conv_b, dt_bias, a_log, d_skip, mnorm_g, layer,
         row_start, n_seq, seq_len, q):
    n_c = seq_len // q
    blk0 = row_start // q
    half = CONV_DIM // 2
    z_blk = (2 * GM_WIDTH) // M_WIDTH
    xa_blk = (2 * GM_WIDTH + M_WIDTH) // half
    zero_init = h0 is None

    def rows(col):
        return lambda s, c: (blk0 + s * n_c + c, col)

    def per_layer(*shape):
        return pl.BlockSpec((None,) + shape, lambda s, c: (layer,) + (0,) * len(shape))

    in_specs = [
        pl.BlockSpec((q, M_WIDTH), rows(z_blk)),
        pl.BlockSpec((q, half), rows(xa_blk)),
        pl.BlockSpec((q, half), rows(xa_blk + 1)),
        pl.BlockSpec((q, M_HEADS), rows(0)),
    ]
    args = [proj, proj, proj, dt_raw]
    if not zero_init:
        in_specs += [
            pl.BlockSpec((None, None, M_HEADS, M_HEAD_DIM, M_STATE), lambda s, c: (layer, s, 0, 0, 0)),
            pl.BlockSpec((None, None, M_CONV - 1, CONV_DIM), lambda s, c: (layer, s, 0, 0)),
        ]
        args += [h0, c0]
    in_specs += [per_layer(M_CONV, CONV_DIM), per_layer(1, CONV_DIM), per_layer(1, M_HEADS),
                 per_layer(1, M_HEADS), per_layer(1, M_HEADS), per_layer(1, M_WIDTH)]
    args += [conv_w, conv_b.reshape(DEPTH, 1, CONV_DIM), dt_bias.reshape(DEPTH, 1, M_HEADS),
             a_log.reshape(DEPTH, 1, M_HEADS), d_skip.reshape(DEPTH, 1, M_HEADS),
             mnorm_g.reshape(DEPTH, 1, M_WIDTH)]
    return pl.pallas_call(
        functools.partial(_ssd_kernel, q, zero_init),
        grid=(n_seq, n_c),
        in_specs=in_specs,
        out_specs=[
            pl.BlockSpec((q, M_WIDTH), lambda s, c: (s * n_c + c, 0)),
            pl.BlockSpec((None, M_HEADS, M_HEAD_DIM, M_STATE), lambda s, c: (s, 0, 0, 0)),
            pl.BlockSpec((None, M_CONV - 1, CONV_DIM), lambda s, c: (s, 0, 0)),
        ],
        out_shape=[
            jax.ShapeDtypeStruct((n_seq * seq_len, M_WIDTH), BF16),
            jax.ShapeDtypeStruct((n_seq, M_HEADS, M_HEAD_DIM, M_STATE), F32),
            jax.ShapeDtypeStruct((n_seq, M_CONV - 1, CONV_DIM), F32),
        ],
        scratch_shapes=[
            pltpu.VMEM((M_STATE, M_WIDTH), F32),
            pltpu.VMEM((CONV_PAD + q, CONV_DIM), F32),
            pltpu.VMEM((q, M_WIDTH), F32),
            pltpu.VMEM((q, 2 * M_GROUPS * M_STATE), BF16),
            pltpu.VMEM((q, M_WIDTH), BF16),
            pltpu.VMEM((q, M_WIDTH), BF16),
            pltpu.VMEM((q, M_WIDTH), F32),
        ],
        compiler_params=_cparams(2),
        name="conv_ssd",
    )(*args)


def _outproj_kernel(n_first_ssm, n_first, gm_ref, wa_ref, wb_ref, gate_ref, *refs):
    n_ssm = 1 if n_first_ssm is None else 2
    ssm_refs, x_refs, o_ref = refs[:n_ssm], refs[n_ssm:-1], refs[-1]
    ssm = _stream_tile(ssm_refs, n_first_ssm, Ellipsis)
    acc = _dot(gm_ref[...], wa_ref[...].astype(BF16)) + _dot(ssm, wb_ref[...].astype(BF16))
    tm, tn = acc.shape
    nb = tm // ROW_BLOCK
    upd = acc.reshape(nb, ROW_BLOCK, tn) * gate_ref[...][:, None, :]
    o_ref[...] = _stream_tile(x_refs, n_first, Ellipsis) + upd.reshape(tm, tn)


def _outproj(gm, ssms, w_out, xs, modb, layer, tm=1024, tn=512):
    t_rows = gm.shape[0]
    nb = tm // ROW_BLOCK
    gate_blk = 2 * D_MODEL // tn
    assert all(x.shape[0] % tm == 0 for x in xs)
    assert all(x.shape[0] % tm == 0 for x in ssms)
    x_specs, n_first = _stream_specs(xs, (tm, tn), lambda j: j, D_MODEL // tn)
    ssm_specs, n_first_ssm = _stream_specs(ssms, (tm, M_WIDTH), lambda j: 0, D_MODEL // tn)
    return pl.pallas_call(
        functools.partial(_outproj_kernel, n_first_ssm, n_first),
        grid=(t_rows // tm, D_MODEL // tn),
        in_specs=[
            pl.BlockSpec((tm, GM_WIDTH), lambda i, j: (i, 0)),
            pl.BlockSpec((None, GM_WIDTH, tn), lambda i, j: (layer, 0, j)),
            pl.BlockSpec((None, M_WIDTH, tn), lambda i, j: (layer, 1, j)),
            pl.BlockSpec((nb, tn), lambda i, j: (i, gate_blk + j)),
        ] + ssm_specs + x_specs,
        out_specs=pl.BlockSpec((tm, tn), lambda i, j: (i, j)),
        out_shape=jax.ShapeDtypeStruct((t_rows, D_MODEL), F32),
        compiler_params=_cparams(2),
        name="out_proj",
    )(gm, w_out, w_out, modb, *ssms, *xs)


ROUTER_ROWS = 24
ROUTE_ROWS = 8
EXPERT_TILE = 1024
EXPERT_SUB = 256


def _row_argmax_first(rows):
    m = rows[0]
    for r in rows[1:]:
        m = jnp.maximum(m, r)
    idx = jnp.full(m.shape, len(rows), jnp.int32)
    for k in reversed(range(len(rows))):
        idx = jnp.where(rows[k] == m, k, idx)
    return m, idx


def _router_kernel(x_ref, g_ref, shift_ref, scale_ref, wr_ref, br_ref, h_ref, route_ref, count_ref, run_ref):
    n_blk = shift_ref.shape[0]
    g = g_ref[...]

    @pl.when(pl.program_id(0) == 0)
    def _():
        run_ref[...] = jnp.zeros_like(run_ref)

    def body(r, carry):
        rows = pl.ds(pl.multiple_of(r * ROW_BLOCK, ROW_BLOCK), ROW_BLOCK)
        xf = x_ref[rows, :]
        y = xf * lax.rsqrt(jnp.mean(xf * xf, axis=-1, keepdims=True) + EPS) * g
        h_ref[rows, 0:D_MODEL] = y * (1.0 + scale_ref[pl.ds(r, 1), :]) + shift_ref[pl.ds(r, 1), :]
        return carry

    lax.fori_loop(0, n_blk, body, 0)
    logits = _dot_nt(wr_ref[...], h_ref[:, 0:D_MODEL], precision=lax.Precision.HIGHEST) + br_ref[...]
    lg = [logits[k:k + 1, :] for k in range(N_EGROUPS)]
    g_max, g_idx = _row_argmax_first(lg)
    g_den = lg[0] * 0.0
    for r in lg:
        g_den = g_den + jnp.exp(r - g_max)
    g_top = 1.0 / g_den
    e_in = []
    for k in range(EXPERTS_PER_GROUP):
        sel = lg[0] * 0.0
        for gi in range(N_EGROUPS):
            row = N_EGROUPS + gi * EXPERTS_PER_GROUP + k
            sel = jnp.where(g_idx == gi, logits[row:row + 1, :], sel)
        e_in.append(sel)
    e_max, i1 = _row_argmax_first(e_in)
    e_exp = [jnp.exp(r - e_max) for r in e_in]
    e_den = e_exp[0]
    for r in e_exp[1:]:
        e_den = e_den + r
    p = [r / e_den for r in e_exp]
    p1 = p[0] * 0.0
    for k in range(EXPERTS_PER_GROUP):
        p1 = jnp.where(i1 == k, p[k], p1)
    rest = [jnp.where(i1 == k, -1.0, p[k]) for k in range(EXPERTS_PER_GROUP)]
    p2, i2 = _row_argmax_first(rest)
    norm = p1 + p2
    w1 = p1 / norm * g_top
    w2 = p2 / norm * g_top
    tm = logits.shape[1]
    row = lax.broadcasted_iota(jnp.int32, (ROUTE_ROWS, tm), 0)
    onehot = (row == g_idx).astype(F32)
    before = (lax.broadcasted_iota(jnp.int32, (tm, tm), 0)
              < lax.broadcasted_iota(jnp.int32, (tm, tm), 1)).astype(BF16)
    earlier = _dot(onehot.astype(BF16), before)
    rank = jnp.sum(onehot * (earlier + run_ref[:, 0:1]), axis=0, keepdims=True)
    run_ref[...] += jnp.sum(onehot, axis=1, keepdims=True)
    count_ref[...] = run_ref[...]

    route_ref[0:1, :] = g_idx.astype(F32)
    route_ref[1:2, :] = rank
    for k in range(EXPERTS_PER_GROUP):
        route_ref[2 + k:3 + k, :] = jnp.where(i1 == k, w1, 0.0) + jnp.where(i2 == k, w2, 0.0)
    route_ref[2 + EXPERTS_PER_GROUP:ROUTE_ROWS, :] = jnp.zeros((ROUTE_ROWS - 2 - EXPERTS_PER_GROUP, tm), F32)
    place = (lax.broadcasted_iota(jnp.int32, (ROUTE_ROWS, LANES), 0)
             == lax.broadcasted_iota(jnp.int32, (ROUTE_ROWS, LANES), 1)).astype(F32)
    h_ref[:, D_MODEL:D_MODEL + LANES] = lax.dot_general(
        route_ref[...], place, (((0,), (0,)), ((), ())), precision=lax.Precision.HIGHEST,
        preferred_element_type=F32)


def _router(x, norm_g, modb, w_r, b_r, layer, tm=512):
    t_rows = x.shape[0]
    nb = tm // ROW_BLOCK
    return pl.pallas_call(
        _router_kernel,
        grid=(t_rows // tm,),
        in_specs=[
            pl.BlockSpec((tm, D_MODEL), lambda i: (i, 0)),
            pl.BlockSpec((None, 1, D_MODEL), lambda i: (layer, 0, 0)),
            pl.BlockSpec((nb, D_MODEL), lambda i: (i, 3)),
            pl.BlockSpec((nb, D_MODEL), lambda i: (i, 4)),
            pl.BlockSpec((ROUTER_ROWS, D_MODEL), lambda i: (0, 0)),
            pl.BlockSpec((ROUTER_ROWS, 1), lambda i: (0, 0)),
        ],
        out_specs=[
            pl.BlockSpec((tm, D_MODEL + LANES), lambda i: (i, 0)),
            pl.BlockSpec((ROUTE_ROWS, tm), lambda i: (0, i)),
            pl.BlockSpec((ROUTE_ROWS, LANES), lambda i: (0, 0)),
        ],
        out_shape=[
            jax.ShapeDtypeStruct((t_rows, D_MODEL + LANES), F32),
            jax.ShapeDtypeStruct((ROUTE_ROWS, t_rows), F32),
            jax.ShapeDtypeStruct((ROUTE_ROWS, LANES), F32),
        ],
        scratch_shapes=[pltpu.VMEM((ROUTE_ROWS, LANES), F32)],
        compiler_params=_cparams(1),
        name="router",
    )(x, norm_g.reshape(DEPTH, 1, D_MODEL), modb, modb, w_r, b_r)


def _dispatch_tables(route, counts, t_rows, tm, n_tiles):
    gid = route[0].astype(jnp.int32)
    rank = route[1].astype(jnp.int32)
    cnt = counts[:N_EGROUPS, 0].astype(jnp.int32)
    padded = (cnt + tm - 1) // tm * tm
    ends = jnp.cumsum(padded)
    starts = ends - padded
    pos = starts[gid] + rank
    src = jnp.zeros((n_tiles * tm,), jnp.int32).at[pos].set(jnp.arange(t_rows, dtype=jnp.int32))
    tile_start = jnp.arange(n_tiles, dtype=jnp.int32) * tm
    tile_group = jnp.sum((tile_start[:, None] >= ends[None, :]).astype(jnp.int32), axis=1)
    tile_group = jnp.minimum(tile_group, N_EGROUPS - 1)
    tile_rows = jnp.clip(cnt[tile_group] - (tile_start - starts[tile_group]), 0, tm)
    tile_nsub = (tile_rows + EXPERT_SUB - 1) // EXPERT_SUB
    return pos, src, tile_group, tile_nsub


def _start_row_gather(table_ref, idx_ref, idx_base, buf_ref, sem):
    def issue(r, carry):
        pltpu.make_async_copy(table_ref.at[pl.ds(idx_ref[idx_base + r], 1), :],
                              buf_ref.at[pl.ds(r, 1), :], sem).start()
        return carry

    lax.fori_loop(0, buf_ref.shape[0], issue, 0, unroll=8)


def _wait_row_gather(table_ref, buf_ref, sem):
    pltpu.make_async_copy(table_ref.at[pl.ds(0, buf_ref.shape[0]), :], buf_ref, sem).wait()


def _start_sub_block_gathers(table_ref, idx_ref, idx_base, n_sub, buf_ref, sem):
    def sub(b, carry):
        r0 = pl.multiple_of(b * EXPERT_SUB, EXPERT_SUB)
        _start_row_gather(table_ref, idx_ref, idx_base + r0, buf_ref.at[pl.ds(r0, EXPERT_SUB)], sem)
        return carry

    lax.fori_loop(0, n_sub, sub, 0)


def _wait_sub_block_gathers(table_ref, n_sub, buf_ref, sem):
    def sub(b, carry):
        r0 = pl.multiple_of(b * EXPERT_SUB, EXPERT_SUB)
        _wait_row_gather(table_ref, buf_ref.at[pl.ds(r0, EXPERT_SUB)], sem)
        return carry

    lax.fori_loop(0, n_sub, sub, 0)


def _experts_kernel(f_split, tg_ref, ns_ref, src_ref, wg_ref, wu_ref, wd_ref, h_ref, y_ref,
                    xbuf_ref, xb_ref, rec_ref, wgb_ref, wub_ref, wdb_ref, sem):
    i = pl.program_id(0)
    s = pl.program_id(1)
    n_tiles = pl.num_programs(0)
    tm = xb_ref.shape[0]
    n_sub = ns_ref[i]

    @pl.when((s == 0) & (i == 0))
    def _():
        _start_sub_block_gathers(h_ref, src_ref, 0, n_sub, xbuf_ref, sem)

    @pl.when(s == 0)
    def _():
        y_ref[...] = jnp.zeros_like(y_ref)
        _wait_sub_block_gathers(h_ref, n_sub, xbuf_ref, sem)

        def keep(b, carry):
            rows = pl.ds(pl.multiple_of(b * EXPERT_SUB, EXPERT_SUB), EXPERT_SUB)
            xb_ref[rows, :] = xbuf_ref[rows, 0:D_MODEL].astype(BF16)
            rec_ref[rows, :] = xbuf_ref[rows, D_MODEL:D_MODEL + LANES]
            return carry

        lax.fori_loop(0, n_sub, keep, 0)

        @pl.when(i + 1 < n_tiles)
        def _():
            nxt = jnp.minimum(i + 1, n_tiles - 1)
            _start_sub_block_gathers(h_ref, src_ref, nxt * tm, ns_ref[nxt], xbuf_ref, sem)

    @pl.when(n_sub > 0)
    def _():
        e = s // f_split
        wgb_ref[...] = wg_ref[...].astype(BF16)
        wub_ref[...] = wu_ref[...].astype(BF16)
        wdb_ref[...] = wd_ref[...].astype(BF16)

        def body(b, carry):
            rows = pl.ds(pl.multiple_of(b * EXPERT_SUB, EXPERT_SUB), EXPERT_SUB)
            x = xb_ref[rows, :]
            gate_act = _dot(x, wgb_ref[...])
            up = _dot(x, wub_ref[...])
            record = rec_ref[rows, :]
            lane = lax.broadcasted_iota(jnp.int32, record.shape, 1)
            gate_col = jnp.sum(jnp.where(lane == 2 + e, record, 0.0), axis=1, keepdims=True)
            hid = (_silu(gate_act) * up * gate_col).astype(BF16)
            y_ref[rows, :] += _dot(hid, wdb_ref[...])
            return carry

        lax.fori_loop(0, n_sub, body, 0)


def _experts(h, src, tile_group, tile_nsub, w_gate, w_up, w_down, layer, tm, f_split=2):
    n_rows = src.shape[0]
    fb = D_EXPERT // f_split
    last = EXPERTS_PER_GROUP * f_split - 1

    def expert_of(i, s, tg, ns, sr):
        return tg[i] * EXPERTS_PER_GROUP + jnp.where(ns[i] > 0, s, last) // f_split

    def half_of(i, s, tg, ns, sr):
        return jnp.where(ns[i] > 0, s, last) % f_split

    grid_spec = pltpu.PrefetchScalarGridSpec(
        num_scalar_prefetch=3,
        grid=(n_rows // tm, EXPERTS_PER_GROUP * f_split),
        in_specs=[
            pl.BlockSpec((None, None, D_MODEL, fb), lambda i, s, *p: (layer, expert_of(i, s, *p), 0, half_of(i, s, *p))),
            pl.BlockSpec((None, None, D_MODEL, fb), lambda i, s, *p: (layer, expert_of(i, s, *p), 0, half_of(i, s, *p))),
            pl.BlockSpec((None, None, fb, D_MODEL), lambda i, s, *p: (layer, expert_of(i, s, *p), half_of(i, s, *p), 0)),
            pl.BlockSpec(memory_space=pl.ANY),
        ],
        out_specs=pl.BlockSpec((tm, D_MODEL), lambda i, s, *_: (i, 0)),
        scratch_shapes=[
            pltpu.VMEM((tm, D_MODEL + LANES), F32),
            pltpu.VMEM((tm, D_MODEL), BF16),
            pltpu.VMEM((tm, LANES), F32),
            pltpu.VMEM((D_MODEL, fb), BF16),
            pltpu.VMEM((D_MODEL, fb), BF16),
            pltpu.VMEM((fb, D_MODEL), BF16),
            pltpu.SemaphoreType.DMA(()),
        ],
    )
    return pl.pallas_call(
        functools.partial(_experts_kernel, f_split),
        grid_spec=grid_spec,
        out_shape=jax.ShapeDtypeStruct((n_rows, D_MODEL), F32),
        compiler_params=_cparams(2),
        name="experts",
    )(tile_group, tile_nsub, src, w_gate, w_up, w_down, h)


def _combine_kernel(final_norm, n_first, pos_ref, x_ref, gate_ref, fg_ref, ys_ref, *refs):
    o_refs, buf_ref, sem = refs[:-2], refs[-2], refs[-1]
    tm, d = x_ref.shape
    i = pl.program_id(0)
    slot = i % 2

    @pl.when(i == 0)
    def _():
        _start_row_gather(ys_ref, pos_ref, 0, buf_ref.at[0], sem.at[0])

    _wait_row_gather(ys_ref, buf_ref.at[slot], sem.at[slot])

    @pl.when(i + 1 < pl.num_programs(0))
    def _():
        _start_row_gather(ys_ref, pos_ref, (i + 1) * tm, buf_ref.at[1 - slot], sem.at[1 - slot])

    nb = tm // ROW_BLOCK
    upd = buf_ref[slot].reshape(nb, ROW_BLOCK, d) * gate_ref[...][:, None, :]
    x = x_ref[...] + upd.reshape(tm, d)
    if final_norm:
        x = x * lax.rsqrt(jnp.mean(x * x, axis=-1, keepdims=True) + EPS) * fg_ref[...]
    if n_first is None:
        o_refs[0][...] = x
    else:
        @pl.when(pl.program_id(0) < n_first)
        def _():
            o_refs[0][...] = x

        @pl.when(pl.program_id(0) >= n_first)
        def _():
            o_refs[1][...] = x


def _combine(x, ys, pos, modb, final_g, final_norm, split_rows=None, tm=512):
    t_rows = x.shape[0]
    nb = tm // ROW_BLOCK
    if split_rows is None:
        n_first = None
        out_specs = pl.BlockSpec((tm, D_MODEL), lambda i, p: (i, 0))
        out_shape = jax.ShapeDtypeStruct((t_rows, D_MODEL), F32)
    else:
        assert split_rows % tm == 0 and (t_rows - split_rows) % tm == 0
        n_first = split_rows // tm
        out_specs = [pl.BlockSpec((tm, D_MODEL), lambda i, p: (jnp.minimum(i, n_first - 1), 0)),
                     pl.BlockSpec((tm, D_MODEL), lambda i, p: (jnp.maximum(i - n_first, 0), 0))]
        out_shape = [jax.ShapeDtypeStruct((split_rows, D_MODEL), F32),
                     jax.ShapeDtypeStruct((t_rows - split_rows, D_MODEL), F32)]
    grid_spec = pltpu.PrefetchScalarGridSpec(
        num_scalar_prefetch=1,
        grid=(t_rows // tm,),
        in_specs=[
            pl.BlockSpec((tm, D_MODEL), lambda i, p: (i, 0)),
            pl.BlockSpec((nb, D_MODEL), lambda i, p: (i, 5)),
            pl.BlockSpec((1, D_MODEL), lambda i, p: (0, 0)),
            pl.BlockSpec(memory_space=pl.ANY),
        ],
        out_specs=out_specs,
        scratch_shapes=[pltpu.VMEM((2, tm, D_MODEL), F32), pltpu.SemaphoreType.DMA((2,))],
    )
    return pl.pallas_call(
        functools.partial(_combine_kernel, final_norm, n_first),
        grid_spec=grid_spec,
        out_shape=out_shape,
        compiler_params=_cparams(1),
        name="combine",
    )(pos, x, modb, final_g.reshape(1, D_MODEL), ys)


def kernel(x_prompt, x_sample, state_ssm, state_conv, c_prompt, c_sample, norm1_g, norm2_g, w_ada, b_ada,
           w_in, gm_ln_g, gm_ln_b, gm_w_s, gm_b_s, conv_w, conv_b, dt_bias, a_log, d_skip, mnorm_g, w_out,
           w_router_group, b_router_group, w_router_expert, b_router_expert, w_gate, w_up, w_down,
           final_norm_g):
    n_p, len_p, _ = x_prompt.shape
    n_s, len_s, _ = x_sample.shape
    assert len_p % GM_CHUNK == 0 and len_s == ROW_BLOCK and (n_s * len_s) % GM_CHUNK == 0
    rows_p, rows_s = n_p * len_p, n_s * len_s
    t_rows = rows_p + rows_s
    n_seq = n_p + n_s
    blk_per_prompt = len_p // ROW_BLOCK
    assert t_rows % EXPERT_TILE == 0
    n_expert_tiles = t_rows // EXPERT_TILE + N_EGROUPS

    xs = (x_prompt.reshape(rows_p, D_MODEL), x_sample.reshape(rows_s, D_MODEL))

    c_all = jnp.concatenate([c_prompt, c_sample], axis=0)
    c_pad = jnp.pad(c_all, ((0, (-n_seq) % 8), (0, 0)))
    mod = _ada(c_pad, w_ada, b_ada)
    modb_all = jnp.concatenate(
        [jnp.repeat(mod[:, :n_p], blk_per_prompt, axis=1), mod[:, n_p:n_seq]], axis=1)

    half_w = gm_w_s[:, :, :ROW_BLOCK, :ROW_BLOCK]
    w_cat = jnp.stack([gm_w_s, jnp.tile(half_w, (1, 1, 2, 2))], axis=1)
    b_t = jnp.swapaxes(gm_b_s, 1, 2)
    b_cat = jnp.stack([b_t, jnp.tile(b_t[:, :ROW_BLOCK], (1, 2, 1))], axis=1)

    w_in_t = jnp.swapaxes(w_in, 1, 2)
    w_r =jnp.concatenate([jnp.swapaxes(w_router_group, 1, 2), jnp.swapaxes(w_router_expert, 1, 2)], axis=1)
    w_r = jnp.pad(w_r, ((0, 0), (0, ROUTER_ROWS - w_r.shape[1]), (0, 0)))
    b_r = jnp.concatenate([b_router_group, b_router_expert], axis=1)
    b_r = jnp.pad(b_r, ((0, 0), (0, ROUTER_ROWS - b_r.shape[1])))[:, :, None]

    ssm_p, conv_p, ssm_s, conv_s, v_rows = [], [], [], [], []
    for l in range(DEPTH):
        last = l == DEPTH - 1
        modb = modb_all[l]
        proj, dt_raw = _inproj(xs, norm1_g, modb, w_in_t, l, tn=512 if len(xs) == 2 else 1024)
        gm_out, vn = _gmlp(proj, gm_ln_g, gm_ln_b, w_cat, b_cat, l, rows_p // GM_CHUNK, rows_s)
        ssm_args = (conv_w, conv_b, dt_bias, a_log, d_skip, mnorm_g, l)
        ssm_out_p, hfin_p, cfin_p = _ssd(proj, dt_raw, None, None, *ssm_args, 0, n_p, len_p, SSD_PROMPT_CHUNK)
        ssm_out_s, hfin_s, cfin_s = _ssd(proj, dt_raw, state_ssm, state_conv, *ssm_args, rows_p, n_s, len_s, len_s)
        x = _outproj(gm_out, (ssm_out_p, ssm_out_s), w_out, xs, modb, l)
        h2, route, counts = _router(x, norm2_g, modb, w_r[l], b_r[l], l)
        pos, src, tile_group, tile_nsub = _dispatch_tables(route, counts, t_rows, EXPERT_TILE, n_expert_tiles)
        ys = _experts(h2, src, tile_group, tile_nsub, w_gate, w_up, w_down, l, EXPERT_TILE)
        out = _combine(x, ys, pos, modb, final_norm_g, last, rows_p if last else None)
        xs = tuple(out) if last else (out,)
        ssm_p.append(hfin_p)
        conv_p.append(cfin_p)
        ssm_s.append(hfin_s)
        conv_s.append(cfin_s)
        v_rows.append(vn.reshape(n_s, len_s, GM_WIDTH))

    y_prompt = xs[0].reshape(n_p, len_p, D_MODEL)
    y_sample = xs[1].reshape(n_s, len_s, D_MODEL)
    return (y_prompt, y_sample, jnp.stack(ssm_p), jnp.stack(conv_p), jnp.stack(ssm_s), jnp.stack(conv_s),
            jnp.stack(v_rows))
```

```python
import functools
import math

import jax
import jax.numpy as jnp
import numpy as np
from jax import lax
from jax.experimental import pallas as pl
from jax.experimental.pallas import tpu as pltpu

D_MODEL = 2048
DEPTH = 2
GM_WIDTH = 1024
GM_HEADS = 8
GM_HEAD_DIM = 128
GM_CHUNK = 128
M_WIDTH = 1024
M_HEAD_DIM = 64
M_HEADS = 16
M_GROUPS = 4
M_STATE = 128
M_CONV = 4
CONV_DIM = M_WIDTH + 2 * M_GROUPS * M_STATE
MAIN_DIM = 2 * GM_WIDTH + M_WIDTH + CONV_DIM
N_EGROUPS = 4
EXPERTS_PER_GROUP = 4
N_EXPERTS = 16
D_EXPERT = 512
N_MOD = 6
EPS = 1e-6

ROW_BLOCK = 64
LANES = 128
VMEM_LIMIT = 56 * 1024 * 1024

F32 = jnp.float32
BF16 = jnp.bfloat16


def _cparams(n_axes):
    return pltpu.CompilerParams(
        dimension_semantics=("arbitrary",) * n_axes, vmem_limit_bytes=VMEM_LIMIT)


def _silu(x):
    return x * jax.nn.sigmoid(x)


def _gelu_tanh(x):
    c = math.sqrt(2.0 / math.pi)
    return 0.5 * x * (1.0 + jnp.tanh(c * (x + 0.044715 * (x * x * x))))


def _dot(a, b):
    return jnp.dot(a, b, preferred_element_type=F32)


def _dot_nt(a, b, precision=None):
    return lax.dot_general(a, b, (((1,), (1,)), ((), ())), precision=precision,
                           preferred_element_type=F32)


def _dot_tn(a, b):
    return lax.dot_general(a, b, (((0,), (0,)), ((), ())), preferred_element_type=F32)


def _ada_kernel(c_ref, w_ref, b_ref, o_ref):
    c = _silu(c_ref[...]).astype(BF16)
    o_ref[...] = _dot(c, w_ref[...].astype(BF16)) + b_ref[...]


def _ada(c_pad, w_ada, b_ada):
    rows = c_pad.shape[0]
    tn = 1024
    n_out = N_MOD * D_MODEL
    return pl.pallas_call(
        _ada_kernel,
        grid=(DEPTH, n_out // tn),
        in_specs=[
            pl.BlockSpec((rows, D_MODEL), lambda l, j: (0, 0)),
            pl.BlockSpec((None, D_MODEL, tn), lambda l, j: (l, 0, j)),
            pl.BlockSpec((None, 1, tn), lambda l, j: (l, 0, j)),
        ],
        out_specs=pl.BlockSpec((None, rows, tn), lambda l, j: (l, 0, j)),
        out_shape=jax.ShapeDtypeStruct((DEPTH, rows, n_out), F32),
        compiler_params=_cparams(2),
        name="ada_ln",
    )(c_pad, w_ada, b_ada.reshape(DEPTH, 1, n_out))


def _stream_specs(xs, block, col_of, n_j):
    if len(xs) == 1:
        return [pl.BlockSpec(block, lambda i, j: (i, col_of(j)))], None
    n_first = xs[0].shape[0] // block[0]

    def first(i, j):
        return jnp.minimum(i, n_first - 1), jnp.where(i < n_first, col_of(j), col_of(n_j - 1))

    def second(i, j):
        return jnp.maximum(i - n_first, 0), jnp.where(i < n_first, col_of(0), col_of(j))

    return [pl.BlockSpec(block, first), pl.BlockSpec(block, second)], n_first


def _stream_tile(x_refs, n_first, idx):
    if n_first is None:
        return x_refs[0][idx]
    return jnp.where(pl.program_id(0) < n_first, x_refs[0][idx], x_refs[1][idx])


def _inproj_kernel(n_first, *refs):
    n_x = 1 if n_first is None else 2
    x_refs = refs[:n_x]
    g_ref, shift_ref, scale_ref, w_ref, wdt_ref, proj_ref, dt_ref, h_ref = refs[n_x:]
    j = pl.program_id(1)
    n_blk = shift_ref.shape[0]

    @pl.when(j == 0)
    def _():
        g = g_ref[...]

        def body(r, carry):
            rows = pl.ds(pl.multiple_of(r * ROW_BLOCK, ROW_BLOCK), ROW_BLOCK)
            xf = _stream_tile(x_refs, n_first, (rows, slice(None)))
            y = xf * lax.rsqrt(jnp.mean(xf * xf, axis=-1, keepdims=True) + EPS) * g
            y = y * (1.0 + scale_ref[pl.ds(r, 1), :]) + shift_ref[pl.ds(r, 1), :]
            h_ref[rows, :] = y.astype(BF16)
            return carry

        lax.fori_loop(0, n_blk, body, 0)
        dt_ref[...] = _dot_nt(h_ref[...], wdt_ref[...].astype(BF16))

    proj_ref[...] = _dot_nt(h_ref[...], w_ref[...].astype(BF16)).astype(BF16)


def _inproj(xs, norm_g, modb, w_in_t, layer, tm=1024, tn=512):
    t_rows = sum(x.shape[0] for x in xs)
    nb = tm // ROW_BLOCK
    assert all(x.shape[0] % tm == 0 for x in xs)
    x_specs, n_first = _stream_specs(xs, (tm, D_MODEL), lambda j: 0, MAIN_DIM // tn)
    return pl.pallas_call(
        functools.partial(_inproj_kernel, n_first),
        grid=(t_rows // tm, MAIN_DIM // tn),
        in_specs=x_specs + [
            pl.BlockSpec((None, 1, D_MODEL), lambda i, j: (layer, 0, 0)),
            pl.BlockSpec((nb, D_MODEL), lambda i, j: (i, 0)),
            pl.BlockSpec((nb, D_MODEL), lambda i, j: (i, 1)),
            pl.BlockSpec((None, tn, D_MODEL), lambda i, j: (layer, j, 0)),
            pl.BlockSpec((None, M_HEADS, D_MODEL), lambda i, j: (layer, MAIN_DIM // M_HEADS, 0)),
        ],
        out_specs=[
            pl.BlockSpec((tm, tn), lambda i, j: (i, j)),
            pl.BlockSpec((tm, M_HEADS), lambda i, j: (i, 0)),
        ],
        out_shape=[
            jax.ShapeDtypeStruct((t_rows, MAIN_DIM), BF16),
            jax.ShapeDtypeStruct((t_rows, M_HEADS), F32),
        ],
        scratch_shapes=[pltpu.VMEM((tm, D_MODEL), BF16)],
        compiler_params=_cparams(2),
        name="in_proj",
    )(*xs, norm_g.reshape(DEPTH, 1, D_MODEL), modb, modb, w_in_t, w_in_t)


def _gmlp_kernel(n_prompt_tiles, u_ref, v_ref, lng_ref, lnb_ref, w_ref, b_ref, out_ref, vn_ref):
    i = pl.program_id(0)
    t_idx = lax.broadcasted_iota(jnp.int32, (GM_CHUNK, GM_CHUNK), 0)
    s_idx = lax.broadcasted_iota(jnp.int32, (GM_CHUNK, GM_CHUNK), 1)
    same_seq = (t_idx // ROW_BLOCK) == (s_idx // ROW_BLOCK)
    allowed = (s_idx <= t_idx) & (same_seq | (i < n_prompt_tiles))
    w_masked = [jnp.where(allowed, w_ref[h], 0.0).astype(BF16) for h in range(GM_HEADS)]
    bias = b_ref[...]
    for r0 in range(0, u_ref.shape[0], GM_CHUNK):
        rows = slice(r0, r0 + GM_CHUNK)
        u = _gelu_tanh(u_ref[rows, :].astype(F32))
        v = _gelu_tanh(v_ref[rows, :].astype(F32))
        mu = jnp.mean(v, axis=-1, keepdims=True)
        vc = v - mu
        vn = vc * lax.rsqrt(jnp.mean(vc * vc, axis=-1, keepdims=True) + EPS)
        vn = vn * lng_ref[...] + lnb_ref[...]
        vn_ref[rows, :] = vn
        vnb = vn.astype(BF16)
        for h in range(GM_HEADS):
            cols = slice(h * GM_HEAD_DIM, (h + 1) * GM_HEAD_DIM)
            mixed = _dot(w_masked[h], vnb[:, cols]) + bias[:, h:h + 1]
            out_ref[rows, cols] = (u[:, cols] * mixed).astype(BF16)


def _gmlp(proj, ln_g, ln_b, w_cat, b_cat, layer, n_prompt_tiles, n_sample_rows):
    t_rows = proj.shape[0]
    n_tiles = t_rows // GM_TILE
    kern = functools.partial(_gmlp_kernel, n_prompt_tiles)
    return pl.pallas_call(
        kern,
        grid=(n_tiles,),
        in_specs=[
            pl.BlockSpec((GM_TILE, GM_WIDTH), lambda i: (i, 0)),
            pl.BlockSpec((GM_TILE, GM_WIDTH), lambda i: (i, 1)),
            pl.BlockSpec((None, 1, GM_WIDTH), lambda i: (layer, 0, 0)),
            pl.BlockSpec((None, 1, GM_WIDTH), lambda i: (layer, 0, 0)),
            pl.BlockSpec((None, None, GM_HEADS, GM_CHUNK, GM_CHUNK),
                         lambda i: (layer, jnp.where(i < n_prompt_tiles, 0, 1), 0, 0, 0)),
            pl.BlockSpec((None, None, GM_CHUNK, GM_HEADS),
                         lambda i: (layer, jnp.where(i < n_prompt_tiles, 0, 1), 0, 0)),
        ],
        out_specs=[
            pl.BlockSpec((GM_TILE, GM_WIDTH), lambda i: (i, 0)),
            pl.BlockSpec((GM_TILE, GM_WIDTH), lambda i: (jnp.maximum(i - n_prompt_tiles, 0), 0)),
        ],
        out_shape=[
            jax.ShapeDtypeStruct((t_rows, GM_WIDTH), BF16),
            jax.ShapeDtypeStruct((n_sample_rows, GM_WIDTH), F32),
        ],
        compiler_params=_cparams(1),
        name="gmlp",
    )(proj, proj, ln_g.reshape(DEPTH, 1, GM_WIDTH), ln_b.reshape(DEPTH, 1, GM_WIDTH), w_cat, b_cat)


GM_TILE = 4 * GM_CHUNK
CONV_PAD = 8
SSD_PROMPT_CHUNK = 256


def _widen(v, onehot):
    p0 = v.astype(BF16)
    r1 = v - p0.astype(F32)
    p1 = r1.astype(BF16)
    p2 = (r1 - p1.astype(F32)).astype(BF16)
    return _dot(p0, onehot) + _dot(p1, onehot) + _dot(p2, onehot)


def _ssd_kernel(q, zero_init, *refs):
    z_ref, xa_ref, xb_ref, dt_ref = refs[:4]
    refs = refs[4:]
    h0_ref = c0_ref = None
    if not zero_init:
        h0_ref, c0_ref = refs[:2]
        refs = refs[2:]
    cw_ref, cb_ref, dtb_ref, alog_ref, dskip_ref, ng_ref = refs[:6]
    refs = refs[6:]
    out_ref, hfin_ref, cfin_ref, state_ref, win_ref, xs_ref, bc_ref, xdt_ref, xdd_ref, y_ref = refs
    c = pl.program_id(1)
    tail = M_CONV - 1
    lo = CONV_PAD - tail
    hp = M_HEADS * M_HEAD_DIM

    @pl.when(c == 0)
    def _():
        if zero_init:
            state_ref[...] = jnp.zeros_like(state_ref)
            win_ref[lo:CONV_PAD, :] = jnp.zeros((tail, CONV_DIM), F32)
        else:
            state_ref[...] = h0_ref[...].reshape(hp, M_STATE).T
            win_ref[lo:CONV_PAD, :] = c0_ref[...]

    half = CONV_DIM // 2
    win_ref[CONV_PAD:CONV_PAD + q, 0:half] = xa_ref[...].astype(F32)
    win_ref[CONV_PAD:CONV_PAD + q, half:CONV_DIM] = xb_ref[...].astype(F32)
    for r0 in range(0, q, ROW_BLOCK):
        conv = cb_ref[...]
        for k in range(M_CONV):
            conv = conv + win_ref[lo + k + r0:lo + k + r0 + ROW_BLOCK, :] * cw_ref[k:k + 1, :]
        act = _silu(conv)
        xs_ref[r0:r0 + ROW_BLOCK, :] = act[:, 0:M_WIDTH]
        bc_ref[r0:r0 + ROW_BLOCK, :] = act[:, M_WIDTH:CONV_DIM].astype(BF16)
    new_tail = win_ref[CONV_PAD + q - tail:CONV_PAD + q, :]
    cfin_ref[...] = new_tail
    win_ref[lo:CONV_PAD, :] = new_tail

    hi = lax.Precision.HIGHEST
    dt = jax.nn.softplus(dt_ref[...] + dtb_ref[...])
    a = -jnp.exp(alog_ref[...])
    r_idx = lax.broadcasted_iota(jnp.int32, (q, q), 0)
    c_idx = lax.broadcasted_iota(jnp.int32, (q, q), 1)
    causal = c_idx <= r_idx
    a_cs = jnp.dot(causal.astype(F32), dt * a, precision=hi, preferred_element_type=F32)
    eye = (lax.broadcasted_iota(jnp.int32, (M_HEADS, M_HEADS), 0)
           == lax.broadcasted_iota(jnp.int32, (M_HEADS, M_HEADS), 1)).astype(F32)
    a_cs_t = _dot_nt(eye, a_cs, precision=hi)
    widen = (lax.broadcasted_iota(jnp.int32, (M_HEADS, hp), 1) // M_HEAD_DIM
             == lax.broadcasted_iota(jnp.int32, (M_HEADS, hp), 0)).astype(BF16)
    dt_w = _widen(dt, widen)
    acs_w = _widen(a_cs, widen)
    dskip_w = _widen(jnp.broadcast_to(dskip_ref[...], (8, M_HEADS)), widen)[0:1, :]
    a_end_w = acs_w[q - 1:q, :]
    x_dt = xs_ref[...] * dt_w
    xdt_ref[...] = x_dt.astype(BF16)
    xdd_ref[...] = (x_dt * jnp.exp(a_end_w - acs_w)).astype(BF16)
    decay_blk_w = jnp.exp(a_end_w)

    heads_per_group = M_HEADS // M_GROUPS
    gw = M_WIDTH // M_GROUPS
    pair_lane = lax.broadcasted_iota(jnp.int32, (q, 2 * M_HEAD_DIM), 1)
    for g in range(M_GROUPS):
        gcols = slice(g * gw, (g + 1) * gw)
        b_g = bc_ref[:, g * M_STATE:(g + 1) * M_STATE]
        c_g = bc_ref[:, (M_GROUPS + g) * M_STATE:(M_GROUPS + g + 1) * M_STATE]
        cb = _dot_nt(c_g, b_g)
        st = state_ref[:, gcols]
        y_off = _dot(c_g, st.astype(BF16))
        state_ref[:, gcols] = st * decay_blk_w[:, gcols] + _dot_tn(b_g, xdd_ref[:, gcols])
        for pr in range(heads_per_group // 2):
            h0 = g * heads_per_group + 2 * pr
            pcols = slice(h0 * M_HEAD_DIM, (h0 + 2) * M_HEAD_DIM)
            x_pair = xdt_ref[:, pcols]
            ys = []
            for h in (h0, h0 + 1):
                seg = a_cs[:, h:h + 1] - a_cs_t[h:h + 1, :]
                scores = cb * jnp.exp(jnp.where(causal, seg, -jnp.inf))
                ys.append(_dot(scores.astype(BF16), x_pair))
            y_ref[:, pcols] = jnp.where(pair_lane < M_HEAD_DIM, ys[0], ys[1])
        y_ref[:, gcols] = (y_ref[:, gcols] + y_off * jnp.exp(acs_w[:, gcols])
                           + dskip_w[:, gcols] * xs_ref[:, gcols])

    @pl.when(c == pl.num_programs(1) - 1)
    def _():
        hfin_ref[...] = state_ref[...].T.reshape(M_HEADS, M_HEAD_DIM, M_STATE)

    gy = y_ref[...] * _silu(z_ref[...].astype(F32))
    ng = ng_ref[...]
    for g in range(M_GROUPS):
        cols = slice(g * gw, (g + 1) * gw)
        s = gy[:, cols]
        s = s * lax.rsqrt(jnp.mean(s * s, axis=-1, keepdims=True) + EPS)
        out_ref[:, cols] = (s * ng[:, cols]).astype(BF16)


def _ssd(proj, dt_raw, h0, c0, conv_w, conv_b, dt_bias, a_log, d_skip, mnorm_g, layer,
         row_start, n_seq, seq_len, q):
    n_c = seq_len // q
    blk0 = row_start // q
    half = CONV_DIM // 2
    z_blk = (2 * GM_WIDTH) // M_WIDTH
    xa_blk = (2 * GM_WIDTH + M_WIDTH) // half
    zero_init = h0 is None

    def rows(col):
        return lambda s, c: (blk0 + s * n_c + c, col)

    def per_layer(*shape):
        return pl.BlockSpec((None,) + shape, lambda s, c: (layer,) + (0,) * len(shape))

    in_specs = [
        pl.BlockSpec((q, M_WIDTH), rows(z_blk)),
        pl.BlockSpec((q, half), rows(xa_blk)),
        pl.BlockSpec((q, half), rows(xa_blk + 1)),
        pl.BlockSpec((q, M_HEADS), rows(0)),
    ]
    args = [proj, proj, proj, dt_raw]
    if not zero_init:
        in_specs += [
            pl.BlockSpec((None, None, M_HEADS, M_HEAD_DIM, M_STATE), lambda s, c: (layer, s, 0, 0, 0)),
            pl.BlockSpec((None, None, M_CONV - 1, CONV_DIM), lambda s, c: (layer, s, 0, 0)),
        ]
        args += [h0, c0]
    in_specs += [per_layer(M_CONV, CONV_DIM), per_layer(1, CONV_DIM), per_layer(1, M_HEADS),
                 per_layer(1, M_HEADS), per_layer(1, M_HEADS), per_layer(1, M_WIDTH)]
    args += [conv_w, conv_b.reshape(DEPTH, 1, CONV_DIM), dt_bias.reshape(DEPTH, 1, M_HEADS),
             a_log.reshape(DEPTH, 1, M_HEADS), d_skip.reshape(DEPTH, 1, M_HEADS),
             mnorm_g.reshape(DEPTH, 1, M_WIDTH)]
    return pl.pallas_call(
        functools.partial(_ssd_kernel, q, zero_init),
        grid=(n_seq, n_c),
        in_specs=in_specs,
        out_specs=[
            pl.BlockSpec((q, M_WIDTH), lambda s, c: (s * n_c + c, 0)),
            pl.BlockSpec((None, M_HEADS, M_HEAD_DIM, M_STATE), lambda s, c: (s, 0, 0, 0)),
            pl.BlockSpec((None, M_CONV - 1, CONV_DIM), lambda s, c: (s, 0, 0)),
        ],
        out_shape=[
            jax.ShapeDtypeStruct((n_seq * seq_len, M_WIDTH), BF16),
            jax.ShapeDtypeStruct((n_seq, M_HEADS, M_HEAD_DIM, M_STATE), F32),
            jax.ShapeDtypeStruct((n_seq, M_CONV - 1, CONV_DIM), F32),
        ],
        scratch_shapes=[
            pltpu.VMEM((M_STATE, M_WIDTH), F32),
            pltpu.VMEM((CONV_PAD + q, CONV_DIM), F32),
            pltpu.VMEM((q, M_WIDTH), F32),
            pltpu.VMEM((q, 2 * M_GROUPS * M_STATE), BF16),
            pltpu.VMEM((q, M_WIDTH), BF16),
            pltpu.VMEM((q, M_WIDTH), BF16),
            pltpu.VMEM((q, M_WIDTH), F32),
        ],
        compiler_params=_cparams(2),
        name="conv_ssd",
    )(*args)


def _outproj_kernel(n_first_ssm, n_first, gm_ref, wa_ref, wb_ref, gate_ref, *refs):
    n_ssm = 1 if n_first_ssm is None else 2
    ssm_refs, x_refs, o_ref = refs[:n_ssm], refs[n_ssm:-1], refs[-1]
    ssm = _stream_tile(ssm_refs, n_first_ssm, Ellipsis)
    acc = _dot(gm_ref[...], wa_ref[...].astype(BF16)) + _dot(ssm, wb_ref[...].astype(BF16))
    tm, tn = acc.shape
    nb = tm // ROW_BLOCK
    upd = acc.reshape(nb, ROW_BLOCK, tn) * gate_ref[...][:, None, :]
    o_ref[...] = _stream_tile(x_refs, n_first, Ellipsis) + upd.reshape(tm, tn)


def _outproj(gm, ssms, w_out, xs, modb, layer, tm=1024, tn=512):
    t_rows = gm.shape[0]
    nb = tm // ROW_BLOCK
    gate_blk = 2 * D_MODEL // tn
    assert all(x.shape[0] % tm == 0 for x in xs)
    assert all(x.shape[0] % tm == 0 for x in ssms)
    x_specs, n_first = _stream_specs(xs, (tm, tn), lambda j: j, D_MODEL // tn)
    ssm_specs, n_first_ssm = _stream_specs(ssms, (tm, M_WIDTH), lambda j: 0, D_MODEL // tn)
    return pl.pallas_call(
        functools.partial(_outproj_kernel, n_first_ssm, n_first),
        grid=(t_rows // tm, D_MODEL // tn),
        in_specs=[
            pl.BlockSpec((tm, GM_WIDTH), lambda i, j: (i, 0)),
            pl.BlockSpec((None, GM_WIDTH, tn), lambda i, j: (layer, 0, j)),
            pl.BlockSpec((None, M_WIDTH, tn), lambda i, j: (layer, 1, j)),
            pl.BlockSpec((nb, tn), lambda i, j: (i, gate_blk + j)),
        ] + ssm_specs + x_specs,
        out_specs=pl.BlockSpec((tm, tn), lambda i, j: (i, j)),
        out_shape=jax.ShapeDtypeStruct((t_rows, D_MODEL), F32),
        compiler_params=_cparams(2),
        name="out_proj",
    )(gm, w_out, w_out, modb, *ssms, *xs)


ROUTE_ROWS = 8
EXPERT_TILE = 1024
EXPERT_SUB = 256


def _row_argmax_first(rows):
    m = rows[0]
    for r in rows[1:]:
        m = jnp.maximum(m, r)
    idx = jnp.full(m.shape, len(rows), jnp.int32)
    for k in reversed(range(len(rows))):
        idx = jnp.where(rows[k] == m, k, idx)
    return m, idx


def _router_kernel(x_ref, g_ref, shift_ref, scale_ref, wr_ref, br_ref, h_ref, route_ref, count_ref, run_ref):
    n_blk = shift_ref.shape[0]
    g = g_ref[...]

    @pl.when(pl.program_id(0) == 0)
    def _():
        run_ref[...] = jnp.zeros_like(run_ref)

    def body(r, carry):
        rows = pl.ds(pl.multiple_of(r * ROW_BLOCK, ROW_BLOCK), ROW_BLOCK)
        xf = x_ref[rows, :]
        y = xf * lax.rsqrt(jnp.mean(xf * xf, axis=-1, keepdims=True) + EPS) * g
        h_ref[rows, 0:D_MODEL] = y * (1.0 + scale_ref[pl.ds(r, 1), :]) + shift_ref[pl.ds(r, 1), :]
        return carry

    lax.fori_loop(0, n_blk, body, 0)
    logits = (jnp.dot(h_ref[:, 0:D_MODEL], wr_ref[...], precision=lax.Precision.HIGHEST,
                      preferred_element_type=F32) + br_ref[...]).T
    lg = [logits[k:k + 1, :] for k in range(N_EGROUPS)]
    g_max, g_idx = _row_argmax_first(lg)
    g_den = lg[0] * 0.0
    for r in lg:
        g_den = g_den + jnp.exp(r - g_max)
    g_top = 1.0 / g_den
    e_in = []
    for k in range(EXPERTS_PER_GROUP):
        sel = lg[0] * 0.0
        for gi in range(N_EGROUPS):
            row = N_EGROUPS + gi * EXPERTS_PER_GROUP + k
            sel = jnp.where(g_idx == gi, logits[row:row + 1, :], sel)
        e_in.append(sel)
    e_max, i1 = _row_argmax_first(e_in)
    e_exp = [jnp.exp(r - e_max) for r in e_in]
    e_den = e_exp[0]
    for r in e_exp[1:]:
        e_den = e_den + r
    p = [r / e_den for r in e_exp]
    p1 = p[0] * 0.0
    for k in range(EXPERTS_PER_GROUP):
        p1 = jnp.where(i1 == k, p[k], p1)
    rest = [jnp.where(i1 == k, -1.0, p[k]) for k in range(EXPERTS_PER_GROUP)]
    p2, i2 = _row_argmax_first(rest)
    norm = p1 + p2
    w1 = p1 / norm * g_top
    w2 = p2 / norm * g_top
    tm = logits.shape[1]
    row = lax.broadcasted_iota(jnp.int32, (ROUTE_ROWS, tm), 0)
    onehot = (row == g_idx).astype(F32)
    before = (lax.broadcasted_iota(jnp.int32, (tm, tm), 0)
              < lax.broadcasted_iota(jnp.int32, (tm, tm), 1)).astype(BF16)
    earlier = _dot(onehot.astype(BF16), before)
    rank = jnp.sum(onehot * (earlier + run_ref[:, 0:1]), axis=0, keepdims=True)
    run_ref[...] += jnp.sum(onehot, axis=1, keepdims=True)
    count_ref[...] = run_ref[...]

    route_ref[0:1, :] = g_idx.astype(F32)
    route_ref[1:2, :] = rank
    for k in range(EXPERTS_PER_GROUP):
        route_ref[2 + k:3 + k, :] = jnp.where(i1 == k, w1, 0.0) + jnp.where(i2 == k, w2, 0.0)
    route_ref[2 + EXPERTS_PER_GROUP:ROUTE_ROWS, :] = jnp.zeros((ROUTE_ROWS - 2 - EXPERTS_PER_GROUP, tm), F32)
    place = (lax.broadcasted_iota(jnp.int32, (ROUTE_ROWS, LANES), 0)
             == lax.broadcasted_iota(jnp.int32, (ROUTE_ROWS, LANES), 1)).astype(F32)
    h_ref[:, D_MODEL:D_MODEL + LANES] = lax.dot_general(
        route_ref[...], place, (((0,), (0,)), ((), ())), precision=lax.Precision.HIGHEST,
        preferred_element_type=F32)


def _router(x, norm_g, modb, w_r, b_r, layer, tm=512):
    t_rows = x.shape[0]
    nb = tm // ROW_BLOCK
    return pl.pallas_call(
        _router_kernel,
        grid=(t_rows // tm,),
        in_specs=[
            pl.BlockSpec((tm, D_MODEL), lambda i: (i, 0)),
            pl.BlockSpec((None, 1, D_MODEL), lambda i: (layer, 0, 0)),
            pl.BlockSpec((nb, D_MODEL), lambda i: (i, 3)),
            pl.BlockSpec((nb, D_MODEL), lambda i: (i, 4)),
            pl.BlockSpec((D_MODEL, LANES), lambda i: (0, 0)),
            pl.BlockSpec((1, LANES), lambda i: (0, 0)),
        ],
        out_specs=[
            pl.BlockSpec((tm, D_MODEL + LANES), lambda i: (i, 0)),
            pl.BlockSpec((ROUTE_ROWS, tm), lambda i: (0, i)),
            pl.BlockSpec((ROUTE_ROWS, LANES), lambda i: (0, 0)),
        ],
        out_shape=[
            jax.ShapeDtypeStruct((t_rows, D_MODEL + LANES), F32),
            jax.ShapeDtypeStruct((ROUTE_ROWS, t_rows), F32),
            jax.ShapeDtypeStruct((ROUTE_ROWS, LANES), F32),
        ],
        scratch_shapes=[pltpu.VMEM((ROUTE_ROWS, LANES), F32)],
        compiler_params=_cparams(1),
        name="router",
    )(x, norm_g.reshape(DEPTH, 1, D_MODEL), modb, modb, w_r, b_r)


def _dispatch_tables(route, counts, t_rows, tm, n_tiles):
    gid = route[0].astype(jnp.int32)
    rank = route[1].astype(jnp.int32)
    cnt = counts[:N_EGROUPS, 0].astype(jnp.int32)
    padded = (cnt + tm - 1) // tm * tm
    ends = jnp.cumsum(padded)
    starts = ends - padded
    pos = starts[gid] + rank
    src = jnp.zeros((n_tiles * tm,), jnp.int32).at[pos].set(jnp.arange(t_rows, dtype=jnp.int32))
    tile_start = jnp.arange(n_tiles, dtype=jnp.int32) * tm
    tile_group = jnp.sum((tile_start[:, None] >= ends[None, :]).astype(jnp.int32), axis=1)
    tile_group = jnp.minimum(tile_group, N_EGROUPS - 1)
    tile_rows = jnp.clip(cnt[tile_group] - (tile_start - starts[tile_group]), 0, tm)
    tile_nsub = (tile_rows + EXPERT_SUB - 1) // EXPERT_SUB
    return pos, src, tile_group, tile_nsub


def _start_row_gather(table_ref, idx_ref, idx_base, buf_ref, sem):
    def issue(r, carry):
        pltpu.make_async_copy(table_ref.at[pl.ds(idx_ref[idx_base + r], 1), :],
                              buf_ref.at[pl.ds(r, 1), :], sem).start()
        return carry

    lax.fori_loop(0, buf_ref.shape[0], issue, 0, unroll=8)


def _wait_row_gather(table_ref, buf_ref, sem):
    pltpu.make_async_copy(table_ref.at[pl.ds(0, buf_ref.shape[0]), :], buf_ref, sem).wait()


def _start_sub_block_gathers(table_ref, idx_ref, idx_base, n_sub, buf_ref, sem):
    def sub(b, carry):
        r0 = pl.multiple_of(b * EXPERT_SUB, EXPERT_SUB)
        _start_row_gather(table_ref, idx_ref, idx_base + r0, buf_ref.at[pl.ds(r0, EXPERT_SUB)], sem)
        return carry

    lax.fori_loop(0, n_sub, sub, 0)


def _wait_sub_block_gathers(table_ref, n_sub, buf_ref, sem):
    def sub(b, carry):
        r0 = pl.multiple_of(b * EXPERT_SUB, EXPERT_SUB)
        _wait_row_gather(table_ref, buf_ref.at[pl.ds(r0, EXPERT_SUB)], sem)
        return carry

    lax.fori_loop(0, n_sub, sub, 0)


def _experts_kernel(f_split, tg_ref, ns_ref, src_ref, wg_ref, wu_ref, wd_ref, h_ref, y_ref,
                    xbuf_ref, xb_ref, rec_ref, wgb_ref, wub_ref, wdb_ref, sem):
    i = pl.program_id(0)
    s = pl.program_id(1)
    n_tiles = pl.num_programs(0)
    tm = xb_ref.shape[0]
    n_sub = ns_ref[i]

    @pl.when((s == 0) & (i == 0))
    def _():
        _start_sub_block_gathers(h_ref, src_ref, 0, n_sub, xbuf_ref, sem)

    @pl.when(s == 0)
    def _():
        y_ref[...] = jnp.zeros_like(y_ref)
        _wait_sub_block_gathers(h_ref, n_sub, xbuf_ref, sem)

        def keep(b, carry):
            rows = pl.ds(pl.multiple_of(b * EXPERT_SUB, EXPERT_SUB), EXPERT_SUB)
            xb_ref[rows, :] = xbuf_ref[rows, 0:D_MODEL].astype(BF16)
            rec_ref[rows, :] = xbuf_ref[rows, D_MODEL:D_MODEL + LANES]
            return carry

        lax.fori_loop(0, n_sub, keep, 0)

        @pl.when(i + 1 < n_tiles)
        def _():
            nxt = jnp.minimum(i + 1, n_tiles - 1)
            _start_sub_block_gathers(h_ref, src_ref, nxt * tm, ns_ref[nxt], xbuf_ref, sem)

    @pl.when(n_sub > 0)
    def _():
        wgb_ref[...] = wg_ref[...].astype(BF16)
        wub_ref[...] = wu_ref[...].astype(BF16)
        wdb_ref[...] = wd_ref[...].astype(BF16)

    for k in range(1, tm // EXPERT_SUB + 1):
        @pl.when(n_sub == k)
        def _(k=k):
            e = s // f_split
            rows = slice(0, k * EXPERT_SUB)
            x = xb_ref[rows, :]
            gate_act = _dot(x, wgb_ref[...])
            up = _dot(x, wub_ref[...])
            record = rec_ref[rows, :]
            lane = lax.broadcasted_iota(jnp.int32, record.shape, 1)
            gate_col = jnp.sum(jnp.where(lane == 2 + e, record, 0.0), axis=1, keepdims=True)
            hid = (_silu(gate_act) * up * gate_col).astype(BF16)
            y_ref[rows, :] += _dot(hid, wdb_ref[...])


def _experts(h, src, tile_group, tile_nsub, w_gate, w_up, w_down, layer, tm, f_split=2):
    n_rows = src.shape[0]
    fb = D_EXPERT // f_split
    last = EXPERTS_PER_GROUP * f_split - 1

    def expert_of(i, s, tg, ns, sr):
        return tg[i] * EXPERTS_PER_GROUP + jnp.where(ns[i] > 0, s, last) // f_split

    def half_of(i, s, tg, ns, sr):
        return jnp.where(ns[i] > 0, s, last) % f_split

    grid_spec = pltpu.PrefetchScalarGridSpec(
        num_scalar_prefetch=3,
        grid=(n_rows // tm, EXPERTS_PER_GROUP * f_split),
        in_specs=[
            pl.BlockSpec((None, None, D_MODEL, fb), lambda i, s, *p: (layer, expert_of(i, s, *p), 0, half_of(i, s, *p))),
            pl.BlockSpec((None, None, D_MODEL, fb), lambda i, s, *p: (layer, expert_of(i, s, *p), 0, half_of(i, s, *p))),
            pl.BlockSpec((None, None, fb, D_MODEL), lambda i, s, *p: (layer, expert_of(i, s, *p), half_of(i, s, *p), 0)),
            pl.BlockSpec(memory_space=pl.ANY),
        ],
        out_specs=pl.BlockSpec((tm, D_MODEL), lambda i, s, *_: (i, 0)),
        scratch_shapes=[
            pltpu.VMEM((tm, D_MODEL + LANES), F32),
            pltpu.VMEM((tm, D_MODEL), BF16),
            pltpu.VMEM((tm, LANES), F32),
            pltpu.VMEM((D_MODEL, fb), BF16),
            pltpu.VMEM((D_MODEL, fb), BF16),
            pltpu.VMEM((fb, D_MODEL), BF16),
            pltpu.SemaphoreType.DMA(()),
        ],
    )
    return pl.pallas_call(
        functools.partial(_experts_kernel, f_split),
        grid_spec=grid_spec,
        out_shape=jax.ShapeDtypeStruct((n_rows, D_MODEL), F32),
        compiler_params=_cparams(2),
        name="experts",
    )(tile_group, tile_nsub, src, w_gate, w_up, w_down, h)


def _combine_kernel(final_norm, n_first, pos_ref, x_ref, gate_ref, fg_ref, ys_ref, *refs):
    o_refs, buf_ref, sem = refs[:-2], refs[-2], refs[-1]
    tm, d = x_ref.shape
    i = pl.program_id(0)
    slot = i % 2

    @pl.when(i == 0)
    def _():
        _start_row_gather(ys_ref, pos_ref, 0, buf_ref.at[0], sem.at[0])

    _wait_row_gather(ys_ref, buf_ref.at[slot], sem.at[slot])

    @pl.when(i + 1 < pl.num_programs(0))
    def _():
        _start_row_gather(ys_ref, pos_ref, (i + 1) * tm, buf_ref.at[1 - slot], sem.at[1 - slot])

    nb = tm // ROW_BLOCK
    upd = buf_ref[slot].reshape(nb, ROW_BLOCK, d) * gate_ref[...][:, None, :]
    x = x_ref[...] + upd.reshape(tm, d)
    if final_norm:
        x = x * lax.rsqrt(jnp.mean(x * x, axis=-1, keepdims=True) + EPS) * fg_ref[...]
    if n_first is None:
        o_refs[0][...] = x
    else:
        @pl.when(pl.program_id(0) < n_first)
        def _():
            o_refs[0][...] = x

        @pl.when(pl.program_id(0) >= n_first)
        def _():
            o_refs[1][...] = x


def _combine(x, ys, pos, modb, final_g, final_norm, split_rows=None, tm=512):
    t_rows = x.shape[0]
    nb = tm // ROW_BLOCK
    if split_rows is None:
        n_first = None
        out_specs = pl.BlockSpec((tm, D_MODEL), lambda i, p: (i, 0))
        out_shape = jax.ShapeDtypeStruct((t_rows, D_MODEL), F32)
    else:
        assert split_rows % tm == 0 and (t_rows - split_rows) % tm == 0
        n_first = split_rows // tm
        out_specs = [pl.BlockSpec((tm, D_MODEL), lambda i, p: (jnp.minimum(i, n_first - 1), 0)),
                     pl.BlockSpec((tm, D_MODEL), lambda i, p: (jnp.maximum(i - n_first, 0), 0))]
        out_shape = [jax.ShapeDtypeStruct((split_rows, D_MODEL), F32),
                     jax.ShapeDtypeStruct((t_rows - split_rows, D_MODEL), F32)]
    grid_spec = pltpu.PrefetchScalarGridSpec(
        num_scalar_prefetch=1,
        grid=(t_rows // tm,),
        in_specs=[
            pl.BlockSpec((tm, D_MODEL), lambda i, p: (i, 0)),
            pl.BlockSpec((nb, D_MODEL), lambda i, p: (i, 5)),
            pl.BlockSpec((1, D_MODEL), lambda i, p: (0, 0)),
            pl.BlockSpec(memory_space=pl.ANY),
        ],
        out_specs=out_specs,
        scratch_shapes=[pltpu.VMEM((2, tm, D_MODEL), F32), pltpu.SemaphoreType.DMA((2,))],
    )
    return pl.pallas_call(
        functools.partial(_combine_kernel, final_norm, n_first),
        grid_spec=grid_spec,
        out_shape=out_shape,
        compiler_params=_cparams(1),
        name="combine",
    )(pos, x, modb, final_g.reshape(1, D_MODEL), ys)


def kernel(x_prompt, x_sample, state_ssm, state_conv, c_prompt, c_sample, norm1_g, norm2_g, w_ada, b_ada,
           w_in, gm_ln_g, gm_ln_b, gm_w_s, gm_b_s, conv_w, conv_b, dt_bias, a_log, d_skip, mnorm_g, w_out,
           w_router_group, b_router_group, w_router_expert, b_router_expert, w_gate, w_up, w_down,
           final_norm_g):
    n_p, len_p, _ = x_prompt.shape
    n_s, len_s, _ = x_sample.shape
    assert len_p % GM_TILE == 0 and len_s == ROW_BLOCK and (n_s * len_s) % GM_TILE == 0
    rows_p, rows_s = n_p * len_p, n_s * len_s
    t_rows = rows_p + rows_s
    n_seq = n_p + n_s
    blk_per_prompt = len_p // ROW_BLOCK
    assert t_rows % EXPERT_TILE == 0
    n_expert_tiles = t_rows // EXPERT_TILE + N_EGROUPS

    xs = (x_prompt.reshape(rows_p, D_MODEL), x_sample.reshape(rows_s, D_MODEL))

    c_all = jnp.concatenate([c_prompt, c_sample], axis=0)
    c_pad = jnp.pad(c_all, ((0, (-n_seq) % 8), (0, 0)))
    mod = _ada(c_pad, w_ada, b_ada)
    modb_all = jnp.concatenate(
        [jnp.repeat(mod[:, :n_p], blk_per_prompt, axis=1), mod[:, n_p:n_seq]], axis=1)

    half_w = gm_w_s[:, :, :ROW_BLOCK, :ROW_BLOCK]
    w_cat = jnp.stack([gm_w_s, jnp.tile(half_w, (1, 1, 2, 2))], axis=1)
    b_t = jnp.swapaxes(gm_b_s, 1, 2)
    b_cat = jnp.stack([b_t, jnp.tile(b_t[:, :ROW_BLOCK], (1, 2, 1))], axis=1)

    w_in_t = jnp.swapaxes(w_in, 1, 2)
    w_r = jnp.concatenate([w_router_group, w_router_expert], axis=2)
    w_r = jnp.pad(w_r, ((0, 0), (0, 0), (0, LANES - w_r.shape[2])))
    b_r = jnp.concatenate([b_router_group, b_router_expert], axis=1)
    b_r = jnp.pad(b_r, ((0, 0), (0, LANES - b_r.shape[1])))[:, None, :]

    ssm_p, conv_p, ssm_s, conv_s, v_rows = [], [], [], [], []
    for l in range(DEPTH):
        last = l == DEPTH - 1
        modb = modb_all[l]
        proj, dt_raw = _inproj(xs, norm1_g, modb, w_in_t, l, tn=512 if len(xs) == 2 else 1024)
        gm_out, vn = _gmlp(proj, gm_ln_g, gm_ln_b, w_cat, b_cat, l, rows_p // GM_TILE, rows_s)
        ssm_args = (conv_w, conv_b, dt_bias, a_log, d_skip, mnorm_g, l)
        ssm_out_p, hfin_p, cfin_p = _ssd(proj, dt_raw, None, None, *ssm_args, 0, n_p, len_p, SSD_PROMPT_CHUNK)
        ssm_out_s, hfin_s, cfin_s = _ssd(proj, dt_raw, state_ssm, state_conv, *ssm_args, rows_p, n_s, len_s, len_s)
        x = _outproj(gm_out, (ssm_out_p, ssm_out_s), w_out, xs, modb, l)
        h2, route, counts = _router(x, norm2_g, modb, w_r[l], b_r[l], l)
        pos, src, tile_group, tile_nsub = _dispatch_tables(route, counts, t_rows, EXPERT_TILE, n_expert_tiles)
        ys = _experts(h2, src, tile_group, tile_nsub, w_gate, w_up, w_down, l, EXPERT_TILE)
        out = _combine(x, ys, pos, modb, final_norm_g, last, rows_p if last else None)
        xs = tuple(out) if last else (out,)
        ssm_p.append(hfin_p)
        conv_p.append(cfin_p)
        ssm_s.append(hfin_s)
        conv_s.append(cfin_s)
        v_rows.append(vn.reshape(n_s, len_s, GM_WIDTH))

    y_prompt = xs[0].reshape(n_p, len_p, D_MODEL)
    y_sample = xs[1].reshape(n_s, len_s, D_MODEL)
    return (y_prompt, y_sample, jnp.stack(ssm_p), jnp.stack(conv_p), jnp.stack(ssm_s), jnp.stack(conv_s),
            jnp.stack(v_rows))
```

```python
import functools
import math

import jax
import jax.numpy as jnp
import numpy as np
from jax import lax
from jax.experimental import pallas as pl
from jax.experimental.pallas import tpu as pltpu

D_MODEL = 2048
DEPTH = 2
GM_WIDTH = 1024
GM_HEADS = 8
GM_HEAD_DIM = 128
GM_CHUNK = 128
M_WIDTH = 1024
M_HEAD_DIM = 64
M_HEADS = 16
M_GROUPS = 4
M_STATE = 128
M_CONV = 4
CONV_DIM = M_WIDTH + 2 * M_GROUPS * M_STATE
MAIN_DIM = 2 * GM_WIDTH + M_WIDTH + CONV_DIM
N_EGROUPS = 4
EXPERTS_PER_GROUP = 4
N_EXPERTS = 16
D_EXPERT = 512
N_MOD = 6
EPS = 1e-6

ROW_BLOCK = 64
LANES = 128
VMEM_LIMIT = 56 * 1024 * 1024

F32 = jnp.float32
BF16 = jnp.bfloat16


def _cparams(n_axes):
    return pltpu.CompilerParams(
        dimension_semantics=("arbitrary",) * n_axes, vmem_limit_bytes=VMEM_LIMIT)


def _silu(x):
    return x * jax.nn.sigmoid(x)


def _gelu_tanh(x):
    c = math.sqrt(2.0 / math.pi)
    return 0.5 * x * (1.0 + jnp.tanh(c * (x + 0.044715 * (x * x * x))))


def _dot(a, b):
    return jnp.dot(a, b, preferred_element_type=F32)


def _dot_nt(a, b, precision=None):
    return lax.dot_general(a, b, (((1,), (1,)), ((), ())), precision=precision,
                           preferred_element_type=F32)


def _dot_split2(a, b):
    a0 = a.astype(BF16)
    a1 = (a - a0.astype(F32)).astype(BF16)
    b0 = b.astype(BF16)
    b1 = (b - b0.astype(F32)).astype(BF16)
    return _dot(a0, b0) + _dot(a0, b1) + _dot(a1, b0)


def _dot_tn(a, b):
    return lax.dot_general(a, b, (((0,), (0,)), ((), ())), preferred_element_type=F32)


def _ada_kernel(c_ref, w_ref, b_ref, o_ref):
    c = _silu(c_ref[...]).astype(BF16)
    o_ref[...] = _dot(c, w_ref[...].astype(BF16)) + b_ref[...]


def _ada(c_pad, w_ada, b_ada):
    rows = c_pad.shape[0]
    tn = 1024
    n_out = N_MOD * D_MODEL
    return pl.pallas_call(
        _ada_kernel,
        grid=(DEPTH, n_out // tn),
        in_specs=[
            pl.BlockSpec((rows, D_MODEL), lambda l, j: (0, 0)),
            pl.BlockSpec((None, D_MODEL, tn), lambda l, j: (l, 0, j)),
            pl.BlockSpec((None, 1, tn), lambda l, j: (l, 0, j)),
        ],
        out_specs=pl.BlockSpec((None, rows, tn), lambda l, j: (l, 0, j)),
        out_shape=jax.ShapeDtypeStruct((DEPTH, rows, n_out), F32),
        compiler_params=_cparams(2),
        name="ada_ln",
    )(c_pad, w_ada, b_ada.reshape(DEPTH, 1, n_out))


def _stream_specs(xs, block, col_of, n_j):
    if len(xs) == 1:
        return [pl.BlockSpec(block, lambda i, j: (i, col_of(j)))], None
    n_first = xs[0].shape[0] // block[0]

    def first(i, j):
        return jnp.minimum(i, n_first - 1), jnp.where(i < n_first, col_of(j), col_of(n_j - 1))

    def second(i, j):
        return jnp.maximum(i - n_first, 0), jnp.where(i < n_first, col_of(0), col_of(j))

    return [pl.BlockSpec(block, first), pl.BlockSpec(block, second)], n_first


def _stream_tile(x_refs, n_first, idx):
    if n_first is None:
        return x_refs[0][idx]
    return jnp.where(pl.program_id(0) < n_first, x_refs[0][idx], x_refs[1][idx])


def _inproj_kernel(n_first, *refs):
    n_x = 1 if n_first is None else 2
    x_refs = refs[:n_x]
    g_ref, shift_ref, scale_ref, w_ref, wdt_ref, proj_ref, dt_ref, h_ref = refs[n_x:]
    j = pl.program_id(1)
    n_blk = shift_ref.shape[0]

    @pl.when(j == 0)
    def _():
        g = g_ref[...]

        def body(r, carry):
            rows = pl.ds(pl.multiple_of(r * ROW_BLOCK, ROW_BLOCK), ROW_BLOCK)
            xf = _stream_tile(x_refs, n_first, (rows, slice(None)))
            y = xf * lax.rsqrt(jnp.mean(xf * xf, axis=-1, keepdims=True) + EPS) * g
            y = y * (1.0 + scale_ref[pl.ds(r, 1), :]) + shift_ref[pl.ds(r, 1), :]
            h_ref[rows, :] = y.astype(BF16)
            return carry

        lax.fori_loop(0, n_blk, body, 0)
        dt_ref[...] = _dot_nt(h_ref[...], wdt_ref[...].astype(BF16))

    proj_ref[...] = _dot_nt(h_ref[...], w_ref[...].astype(BF16)).astype(BF16)


def _inproj(xs, norm_g, modb, w_in_t, layer, tm=1024, tn=512):
    t_rows = sum(x.shape[0] for x in xs)
    nb = tm // ROW_BLOCK
    assert all(x.shape[0] % tm == 0 for x in xs)
    x_specs, n_first = _stream_specs(xs, (tm, D_MODEL), lambda j: 0, MAIN_DIM // tn)
    return pl.pallas_call(
        functools.partial(_inproj_kernel, n_first),
        grid=(t_rows // tm, MAIN_DIM // tn),
        in_specs=x_specs + [
            pl.BlockSpec((None, 1, D_MODEL), lambda i, j: (layer, 0, 0)),
            pl.BlockSpec((nb, D_MODEL), lambda i, j: (i, 0)),
            pl.BlockSpec((nb, D_MODEL), lambda i, j: (i, 1)),
            pl.BlockSpec((None, tn, D_MODEL), lambda i, j: (layer, j, 0)),
            pl.BlockSpec((None, M_HEADS, D_MODEL), lambda i, j: (layer, MAIN_DIM // M_HEADS, 0)),
        ],
        out_specs=[
            pl.BlockSpec((tm, tn), lambda i, j: (i, j)),
            pl.BlockSpec((tm, M_HEADS), lambda i, j: (i, 0)),
        ],
        out_shape=[
            jax.ShapeDtypeStruct((t_rows, MAIN_DIM), BF16),
            jax.ShapeDtypeStruct((t_rows, M_HEADS), F32),
        ],
        scratch_shapes=[pltpu.VMEM((tm, D_MODEL), BF16)],
        compiler_params=_cparams(2),
        name="in_proj",
    )(*xs, norm_g.reshape(DEPTH, 1, D_MODEL), modb, modb, w_in_t, w_in_t)


def _gmlp_kernel(n_prompt_tiles, u_ref, v_ref, lng_ref, lnb_ref, w_ref, b_ref, out_ref, vn_ref):
    i = pl.program_id(0)
    t_idx = lax.broadcasted_iota(jnp.int32, (GM_CHUNK, GM_CHUNK), 0)
    s_idx = lax.broadcasted_iota(jnp.int32, (GM_CHUNK, GM_CHUNK), 1)
    same_seq = (t_idx // ROW_BLOCK) == (s_idx // ROW_BLOCK)
    allowed = (s_idx <= t_idx) & (same_seq | (i < n_prompt_tiles))
    w_masked = [jnp.where(allowed, w_ref[h], 0.0).astype(BF16) for h in range(GM_HEADS)]
    bias = b_ref[...]
    for r0 in range(0, u_ref.shape[0], GM_CHUNK):
        rows = slice(r0, r0 + GM_CHUNK)
        u = _gelu_tanh(u_ref[rows, :].astype(F32))
        v = _gelu_tanh(v_ref[rows, :].astype(F32))
        mu = jnp.mean(v, axis=-1, keepdims=True)
        vc = v - mu
        vn = vc * lax.rsqrt(jnp.mean(vc * vc, axis=-1, keepdims=True) + EPS)
        vn = vn * lng_ref[...] + lnb_ref[...]
        vn_ref[rows, :] = vn
        vnb = vn.astype(BF16)
        for h in range(GM_HEADS):
            cols = slice(h * GM_HEAD_DIM, (h + 1) * GM_HEAD_DIM)
            mixed = _dot(w_masked[h], vnb[:, cols]) + bias[:, h:h + 1]
            out_ref[rows, cols] = (u[:, cols] * mixed).astype(BF16)


def _gmlp(proj, ln_g, ln_b, w_cat, b_cat, layer, n_prompt_tiles, n_sample_rows):
    t_rows = proj.shape[0]
    n_tiles = t_rows // GM_TILE
    kern = functools.partial(_gmlp_kernel, n_prompt_tiles)
    return pl.pallas_call(
        kern,
        grid=(n_tiles,),
        in_specs=[
            pl.BlockSpec((GM_TILE, GM_WIDTH), lambda i: (i, 0)),
            pl.BlockSpec((GM_TILE, GM_WIDTH), lambda i: (i, 1)),
            pl.BlockSpec((None, 1, GM_WIDTH), lambda i: (layer, 0, 0)),
            pl.BlockSpec((None, 1, GM_WIDTH), lambda i: (layer, 0, 0)),
            pl.BlockSpec((None, None, GM_HEADS, GM_CHUNK, GM_CHUNK),
                         lambda i: (layer, jnp.where(i < n_prompt_tiles, 0, 1), 0, 0, 0)),
            pl.BlockSpec((None, None, GM_CHUNK, GM_HEADS),
                         lambda i: (layer, jnp.where(i < n_prompt_tiles, 0, 1), 0, 0)),
        ],
        out_specs=[
            pl.BlockSpec((GM_TILE, GM_WIDTH), lambda i: (i, 0)),
            pl.BlockSpec((GM_TILE, GM_WIDTH), lambda i: (jnp.maximum(i - n_prompt_tiles, 0), 0)),
        ],
        out_shape=[
            jax.ShapeDtypeStruct((t_rows, GM_WIDTH), BF16),
            jax.ShapeDtypeStruct((n_sample_rows, GM_WIDTH), F32),
        ],
        compiler_params=_cparams(1),
        name="gmlp",
    )(proj, proj, ln_g.reshape(DEPTH, 1, GM_WIDTH), ln_b.reshape(DEPTH, 1, GM_WIDTH), w_cat, b_cat)


GM_TILE = 4 * GM_CHUNK
CONV_PAD = 8
SSD_PROMPT_CHUNK = 256


def _widen(v, onehot):
    p0 = v.astype(BF16)
    r1 = v - p0.astype(F32)
    p1 = r1.astype(BF16)
    p2 = (r1 - p1.astype(F32)).astype(BF16)
    return _dot(p0, onehot) + _dot(p1, onehot) + _dot(p2, onehot)


def _ssd_kernel(q, zero_init, *refs):
    z_ref, xa_ref, xb_ref, dt_ref = refs[:4]
    refs = refs[4:]
    h0_ref = c0_ref = None
    if not zero_init:
        h0_ref, c0_ref = refs[:2]
        refs = refs[2:]
    cw_ref, cb_ref, dtb_ref, alog_ref, dskip_ref, ng_ref = refs[:6]
    refs = refs[6:]
    out_ref, hfin_ref, cfin_ref, state_ref, win_ref, xs_ref, bc_ref, xdt_ref, xdd_ref, y_ref = refs
    c = pl.program_id(1)
    tail = M_CONV - 1
    lo = CONV_PAD - tail
    hp = M_HEADS * M_HEAD_DIM

    @pl.when(c == 0)
    def _():
        if zero_init:
            state_ref[...] = jnp.zeros_like(state_ref)
            win_ref[lo:CONV_PAD, :] = jnp.zeros((tail, CONV_DIM), F32)
        else:
            state_ref[...] = h0_ref[...].reshape(hp, M_STATE).T
            win_ref[lo:CONV_PAD, :] = c0_ref[...]

    half = CONV_DIM // 2
    win_ref[CONV_PAD:CONV_PAD + q, 0:half] = xa_ref[...].astype(F32)
    win_ref[CONV_PAD:CONV_PAD + q, half:CONV_DIM] = xb_ref[...].astype(F32)
    for r0 in range(0, q, ROW_BLOCK):
        conv = cb_ref[...]
        for k in range(M_CONV):
            conv = conv + win_ref[lo + k + r0:lo + k + r0 + ROW_BLOCK, :] * cw_ref[k:k + 1, :]
        act = _silu(conv)
        xs_ref[r0:r0 + ROW_BLOCK, :] = act[:, 0:M_WIDTH]
        bc_ref[r0:r0 + ROW_BLOCK, :] = act[:, M_WIDTH:CONV_DIM].astype(BF16)
    new_tail = win_ref[CONV_PAD + q - tail:CONV_PAD + q, :]
    cfin_ref[...] = new_tail
    win_ref[lo:CONV_PAD, :] = new_tail

    hi = lax.Precision.HIGHEST
    dt = jax.nn.softplus(dt_ref[...] + dtb_ref[...])
    a = -jnp.exp(alog_ref[...])
    r_idx = lax.broadcasted_iota(jnp.int32, (q, q), 0)
    c_idx = lax.broadcasted_iota(jnp.int32, (q, q), 1)
    causal = c_idx <= r_idx
    a_cs = jnp.dot(causal.astype(F32), dt * a, precision=hi, preferred_element_type=F32)
    eye = (lax.broadcasted_iota(jnp.int32, (M_HEADS, M_HEADS), 0)
           == lax.broadcasted_iota(jnp.int32, (M_HEADS, M_HEADS), 1)).astype(F32)
    a_cs_t = _dot_nt(eye, a_cs, precision=hi)
    widen = (lax.broadcasted_iota(jnp.int32, (M_HEADS, hp), 1) // M_HEAD_DIM
             == lax.broadcasted_iota(jnp.int32, (M_HEADS, hp), 0)).astype(BF16)
    dt_w = _widen(dt, widen)
    acs_w = _widen(a_cs, widen)
    dskip_w = _widen(jnp.broadcast_to(dskip_ref[...], (8, M_HEADS)), widen)[0:1, :]
    a_end_w = acs_w[q - 1:q, :]
    x_dt = xs_ref[...] * dt_w
    xdt_ref[...] = x_dt.astype(BF16)
    xdd_ref[...] = (x_dt * jnp.exp(a_end_w - acs_w)).astype(BF16)
    decay_blk_w = jnp.exp(a_end_w)

    heads_per_group = M_HEADS // M_GROUPS
    gw = M_WIDTH // M_GROUPS
    pair_lane = lax.broadcasted_iota(jnp.int32, (q, 2 * M_HEAD_DIM), 1)
    for g in range(M_GROUPS):
        gcols = slice(g * gw, (g + 1) * gw)
        b_g = bc_ref[:, g * M_STATE:(g + 1) * M_STATE]
        c_g = bc_ref[:, (M_GROUPS + g) * M_STATE:(M_GROUPS + g + 1) * M_STATE]
        cb = _dot_nt(c_g, b_g)
        st = state_ref[:, gcols]
        y_off = _dot(c_g, st.astype(BF16))
        state_ref[:, gcols] = st * decay_blk_w[:, gcols] + _dot_tn(b_g, xdd_ref[:, gcols])
        for pr in range(heads_per_group // 2):
            h0 = g * heads_per_group + 2 * pr
            pcols = slice(h0 * M_HEAD_DIM, (h0 + 2) * M_HEAD_DIM)
            x_pair = xdt_ref[:, pcols]
            ys = []
            for h in (h0, h0 + 1):
                seg = a_cs[:, h:h + 1] - a_cs_t[h:h + 1, :]
                scores = cb * jnp.exp(jnp.where(causal, seg, -jnp.inf))
                ys.append(_dot(scores.astype(BF16), x_pair))
            y_ref[:, pcols] = jnp.where(pair_lane < M_HEAD_DIM, ys[0], ys[1])
        y_ref[:, gcols] = (y_ref[:, gcols] + y_off * jnp.exp(acs_w[:, gcols])
                           + dskip_w[:, gcols] * xs_ref[:, gcols])

    @pl.when(c == pl.num_programs(1) - 1)
    def _():
        hfin_ref[...] = state_ref[...].T.reshape(M_HEADS, M_HEAD_DIM, M_STATE)

    gy = y_ref[...] * _silu(z_ref[...].astype(F32))
    ng = ng_ref[...]
    for g in range(M_GROUPS):
        cols = slice(g * gw, (g + 1) * gw)
        s = gy[:, cols]
        s = s * lax.rsqrt(jnp.mean(s * s, axis=-1, keepdims=True) + EPS)
        out_ref[:, cols] = (s * ng[:, cols]).astype(BF16)


def _ssd(proj, dt_raw, h0, c0, conv_w, conv_b, dt_bias, a_log, d_skip, mnorm_g, layer,
         row_start, n_seq, seq_len, q):
    n_c = seq_len // q
    blk0 = row_start // q
    half = CONV_DIM // 2
    z_blk = (2 * GM_WIDTH) // M_WIDTH
    xa_blk = (2 * GM_WIDTH + M_WIDTH) // half
    zero_init = h0 is None

    def rows(col):
        return lambda s, c: (blk0 + s * n_c + c, col)

    def per_layer(*shape):
        return pl.BlockSpec((None,) + shape, lambda s, c: (layer,) + (0,) * len(shape))

    in_specs = [
        pl.BlockSpec((q, M_WIDTH), rows(z_blk)),
        pl.BlockSpec((q, half), rows(xa_blk)),
        pl.BlockSpec((q, half), rows(xa_blk + 1)),
        pl.BlockSpec((q, M_HEADS), rows(0)),
    ]
    args = [proj, proj, proj, dt_raw]
    if not zero_init:
        in_specs += [
            pl.BlockSpec((None, None, M_HEADS, M_HEAD_DIM, M_STATE), lambda s, c: (layer, s, 0, 0, 0)),
            pl.BlockSpec((None, None, M_CONV - 1, CONV_DIM), lambda s, c: (layer, s, 0, 0)),
        ]
        args += [h0, c0]
    in_specs += [per_layer(M_CONV, CONV_DIM), per_layer(1, CONV_DIM), per_layer(1, M_HEADS),
                 per_layer(1, M_HEADS), per_layer(1, M_HEADS), per_layer(1, M_WIDTH)]
    args += [conv_w, conv_b.reshape(DEPTH, 1, CONV_DIM), dt_bias.reshape(DEPTH, 1, M_HEADS),
             a_log.reshape(DEPTH, 1, M_HEADS), d_skip.reshape(DEPTH, 1, M_HEADS),
             mnorm_g.reshape(DEPTH, 1, M_WIDTH)]
    return pl.pallas_call(
        functools.partial(_ssd_kernel, q, zero_init),
        grid=(n_seq, n_c),
        in_specs=in_specs,
        out_specs=[
            pl.BlockSpec((q, M_WIDTH), lambda s, c: (s * n_c + c, 0)),
            pl.BlockSpec((None, M_HEADS, M_HEAD_DIM, M_STATE), lambda s, c: (s, 0, 0, 0)),
            pl.BlockSpec((None, M_CONV - 1, CONV_DIM), lambda s, c: (s, 0, 0)),
        ],
        out_shape=[
            jax.ShapeDtypeStruct((n_seq * seq_len, M_WIDTH), BF16),
            jax.ShapeDtypeStruct((n_seq, M_HEADS, M_HEAD_DIM, M_STATE), F32),
            jax.ShapeDtypeStruct((n_seq, M_CONV - 1, CONV_DIM), F32),
        ],
        scratch_shapes=[
            pltpu.VMEM((M_STATE, M_WIDTH), F32),
            pltpu.VMEM((CONV_PAD + q, CONV_DIM), F32),
            pltpu.VMEM((q, M_WIDTH), F32),
            pltpu.VMEM((q, 2 * M_GROUPS * M_STATE), BF16),
            pltpu.VMEM((q, M_WIDTH), BF16),
            pltpu.VMEM((q, M_WIDTH), BF16),
            pltpu.VMEM((q, M_WIDTH), F32),
        ],
        compiler_params=_cparams(2),
        name="conv_ssd",
    )(*args)


def _outproj_kernel(n_first_ssm, n_first, gm_ref, wa_ref, wb_ref, gate_ref, *refs):
    n_ssm = 1 if n_first_ssm is None else 2
    ssm_refs, x_refs, o_ref = refs[:n_ssm], refs[n_ssm:-1], refs[-1]
    ssm = _stream_tile(ssm_refs, n_first_ssm, Ellipsis)
    acc = _dot(gm_ref[...], wa_ref[...].astype(BF16)) + _dot(ssm, wb_ref[...].astype(BF16))
    tm, tn = acc.shape
    nb = tm // ROW_BLOCK
    upd = acc.reshape(nb, ROW_BLOCK, tn) * gate_ref[...][:, None, :]
    o_ref[...] = _stream_tile(x_refs, n_first, Ellipsis) + upd.reshape(tm, tn)


def _outproj(gm, ssms, w_out, xs, modb, layer, tm=1024, tn=512):
    t_rows = gm.shape[0]
    nb = tm // ROW_BLOCK
    gate_blk = 2 * D_MODEL // tn
    assert all(x.shape[0] % tm == 0 for x in xs)
    assert all(x.shape[0] % tm == 0 for x in ssms)
    x_specs, n_first = _stream_specs(xs, (tm, tn), lambda j: j, D_MODEL // tn)
    ssm_specs, n_first_ssm = _stream_specs(ssms, (tm, M_WIDTH), lambda j: 0, D_MODEL // tn)
    return pl.pallas_call(
        functools.partial(_outproj_kernel, n_first_ssm, n_first),
        grid=(t_rows // tm, D_MODEL // tn),
        in_specs=[
            pl.BlockSpec((tm, GM_WIDTH), lambda i, j: (i, 0)),
            pl.BlockSpec((None, GM_WIDTH, tn), lambda i, j: (layer, 0, j)),
            pl.BlockSpec((None, M_WIDTH, tn), lambda i, j: (layer, 1, j)),
            pl.BlockSpec((nb, tn), lambda i, j: (i, gate_blk + j)),
        ] + ssm_specs + x_specs,
        out_specs=pl.BlockSpec((tm, tn), lambda i, j: (i, j)),
        out_shape=jax.ShapeDtypeStruct((t_rows, D_MODEL), F32),
        compiler_params=_cparams(2),
        name="out_proj",
    )(gm, w_out, w_out, modb, *ssms, *xs)


ROUTE_ROWS = 8
EXPERT_TILE = 1024
EXPERT_SUB = 256


def _row_argmax_first(rows):
    m = rows[0]
    for r in rows[1:]:
        m = jnp.maximum(m, r)
    idx = jnp.full(m.shape, len(rows), jnp.int32)
    for k in reversed(range(len(rows))):
        idx = jnp.where(rows[k] == m, k, idx)
    return m, idx


def _router_kernel(x_ref, g_ref, shift_ref, scale_ref, wr_ref, br_ref, h_ref, route_ref, count_ref, run_ref):
    n_blk = shift_ref.shape[0]
    g = g_ref[...]

    @pl.when(pl.program_id(0) == 0)
    def _():
        run_ref[...] = jnp.zeros_like(run_ref)

    def body(r, carry):
        rows = pl.ds(pl.multiple_of(r * ROW_BLOCK, ROW_BLOCK), ROW_BLOCK)
        xf = x_ref[rows, :]
        y = xf * lax.rsqrt(jnp.mean(xf * xf, axis=-1, keepdims=True) + EPS) * g
        h_ref[rows, 0:D_MODEL] = y * (1.0 + scale_ref[pl.ds(r, 1), :]) + shift_ref[pl.ds(r, 1), :]
        return carry

    lax.fori_loop(0, n_blk, body, 0)
    logits = (_dot_split2(h_ref[:, 0:D_MODEL], wr_ref[...]) + br_ref[...]).T
    lg = [logits[k:k + 1, :] for k in range(N_EGROUPS)]
    g_max, g_idx = _row_argmax_first(lg)
    g_den = lg[0] * 0.0
    for r in lg:
        g_den = g_den + jnp.exp(r - g_max)
    g_top = 1.0 / g_den
    e_in = []
    for k in range(EXPERTS_PER_GROUP):
        sel = lg[0] * 0.0
        for gi in range(N_EGROUPS):
            row = N_EGROUPS + gi * EXPERTS_PER_GROUP + k
            sel = jnp.where(g_idx == gi, logits[row:row + 1, :], sel)
        e_in.append(sel)
    e_max, i1 = _row_argmax_first(e_in)
    e_exp = [jnp.exp(r - e_max) for r in e_in]
    e_den = e_exp[0]
    for r in e_exp[1:]:
        e_den = e_den + r
    p = [r / e_den for r in e_exp]
    p1 = p[0] * 0.0
    for k in range(EXPERTS_PER_GROUP):
        p1 = jnp.where(i1 == k, p[k], p1)
    rest = [jnp.where(i1 == k, -1.0, p[k]) for k in range(EXPERTS_PER_GROUP)]
    p2, i2 = _row_argmax_first(rest)
    norm = p1 + p2
    w1 = p1 / norm * g_top
    w2 = p2 / norm * g_top
    tm = logits.shape[1]
    row = lax.broadcasted_iota(jnp.int32, (ROUTE_ROWS, tm), 0)
    onehot = (row == g_idx).astype(F32)
    before = (lax.broadcasted_iota(jnp.int32, (tm, tm), 0)
              < lax.broadcasted_iota(jnp.int32, (tm, tm), 1)).astype(BF16)
    earlier = _dot(onehot.astype(BF16), before)
    rank = jnp.sum(onehot * (earlier + run_ref[:, 0:1]), axis=0, keepdims=True)
    run_ref[...] += jnp.sum(onehot, axis=1, keepdims=True)
    count_ref[...] = run_ref[...]

    route_ref[0:1, :] = g_idx.astype(F32)
    route_ref[1:2, :] = rank
    for k in range(EXPERTS_PER_GROUP):
        route_ref[2 + k:3 + k, :] = jnp.where(i1 == k, w1, 0.0) + jnp.where(i2 == k, w2, 0.0)
    route_ref[2 + EXPERTS_PER_GROUP:ROUTE_ROWS, :] = jnp.zeros((ROUTE_ROWS - 2 - EXPERTS_PER_GROUP, tm), F32)
    place = (lax.broadcasted_iota(jnp.int32, (ROUTE_ROWS, LANES), 0)
             == lax.broadcasted_iota(jnp.int32, (ROUTE_ROWS, LANES), 1)).astype(F32)
    h_ref[:, D_MODEL:D_MODEL + LANES] = lax.dot_general(
        route_ref[...], place, (((0,), (0,)), ((), ())), precision=lax.Precision.HIGHEST,
        preferred_element_type=F32)


def _router(x, norm_g, modb, w_r, b_r, layer, tm=512):
    t_rows = x.shape[0]
    nb = tm // ROW_BLOCK
    return pl.pallas_call(
        _router_kernel,
        grid=(t_rows // tm,),
        in_specs=[
            pl.BlockSpec((tm, D_MODEL), lambda i: (i, 0)),
            pl.BlockSpec((None, 1, D_MODEL), lambda i: (layer, 0, 0)),
            pl.BlockSpec((nb, D_MODEL), lambda i: (i, 3)),
            pl.BlockSpec((nb, D_MODEL), lambda i: (i, 4)),
            pl.BlockSpec((D_MODEL, LANES), lambda i: (0, 0)),
            pl.BlockSpec((1, LANES), lambda i: (0, 0)),
        ],
        out_specs=[
            pl.BlockSpec((tm, D_MODEL + LANES), lambda i: (i, 0)),
            pl.BlockSpec((ROUTE_ROWS, tm), lambda i: (0, i)),
            pl.BlockSpec((ROUTE_ROWS, LANES), lambda i: (0, 0)),
        ],
        out_shape=[
            jax.ShapeDtypeStruct((t_rows, D_MODEL + LANES), F32),
            jax.ShapeDtypeStruct((ROUTE_ROWS, t_rows), F32),
            jax.ShapeDtypeStruct((ROUTE_ROWS, LANES), F32),
        ],
        scratch_shapes=[pltpu.VMEM((ROUTE_ROWS, LANES), F32)],
        compiler_params=_cparams(1),
        name="router",
    )(x, norm_g.reshape(DEPTH, 1, D_MODEL), modb, modb, w_r, b_r)


def _dispatch_tables(route, counts, t_rows, tm, n_tiles):
    gid = route[0].astype(jnp.int32)
    rank = route[1].astype(jnp.int32)
    cnt = counts[:N_EGROUPS, 0].astype(jnp.int32)
    padded = (cnt + tm - 1) // tm * tm
    ends = jnp.cumsum(padded)
    starts = ends - padded
    pos = starts[gid] + rank
    src = jnp.zeros((n_tiles * tm,), jnp.int32).at[pos].set(jnp.arange(t_rows, dtype=jnp.int32))
    tile_start = jnp.arange(n_tiles, dtype=jnp.int32) * tm
    tile_group = jnp.sum((tile_start[:, None] >= ends[None, :]).astype(jnp.int32), axis=1)
    tile_group = jnp.minimum(tile_group, N_EGROUPS - 1)
    tile_rows = jnp.clip(cnt[tile_group] - (tile_start - starts[tile_group]), 0, tm)
    tile_nsub = (tile_rows + EXPERT_SUB - 1) // EXPERT_SUB
    return pos, src, tile_group, tile_nsub


def _start_row_gather(table_ref, idx_ref, idx_base, buf_ref, sem):
    def issue(r, carry):
        pltpu.make_async_copy(table_ref.at[pl.ds(idx_ref[idx_base + r], 1), :],
                              buf_ref.at[pl.ds(r, 1), :], sem).start()
        return carry

    lax.fori_loop(0, buf_ref.shape[0], issue, 0, unroll=8)


def _start_rows_unrolled(table_ref, idx_ref, idx_base, buf_ref, row0, n_rows, sem):
    for r in range(n_rows):
        pltpu.make_async_copy(table_ref.at[pl.ds(idx_ref[idx_base + r], 1), :],
                              buf_ref.at[pl.ds(row0 + r, 1), :], sem).start()


def _wait_row_gather(table_ref, buf_ref, sem):
    pltpu.make_async_copy(table_ref.at[pl.ds(0, buf_ref.shape[0]), :], buf_ref, sem).wait()


def _experts_kernel(f_split, tg_ref, ns_ref, src_ref, wg_ref, wu_ref, wd_ref, h_ref, y_ref,
                    xbuf_ref, xb_ref, rec_ref, wgb_ref, wub_ref, wdb_ref, sem):
    i = pl.program_id(0)
    s = pl.program_id(1)
    n_tiles = pl.num_programs(0)
    tm = xb_ref.shape[0]
    n_steps = pl.num_programs(1)
    n_sub = ns_ref[i]
    step_rows = tm // (EXPERTS_PER_GROUP * f_split)
    nxt = jnp.minimum(i + 1, n_tiles - 1)

    @pl.when((s == 0) & (i == 0))
    def _():
        _start_row_gather(h_ref, src_ref, 0, xbuf_ref, sem)

    @pl.when(s == 0)
    def _():
        y_ref[...] = jnp.zeros_like(y_ref)

        @pl.when((i == 0) | (ns_ref[jnp.maximum(i - 1, 0)] > 0))
        def _():
            _wait_row_gather(h_ref, xbuf_ref, sem)

        def keep(b, carry):
            rows = pl.ds(pl.multiple_of(b * EXPERT_SUB, EXPERT_SUB), EXPERT_SUB)
            xb_ref[rows, :] = xbuf_ref[rows, 0:D_MODEL].astype(BF16)
            rec_ref[rows, :] = xbuf_ref[rows, D_MODEL:D_MODEL + LANES]
            return carry

        lax.fori_loop(0, n_sub, keep, 0)

    @pl.when(n_sub > 0)
    def _():
        wgb_ref[...] = wg_ref[...].astype(BF16)
        wub_ref[...] = wu_ref[...].astype(BF16)
        wdb_ref[...] = wd_ref[...].astype(BF16)
        _start_rows_unrolled(h_ref, src_ref, nxt * tm + s * step_rows, xbuf_ref, s * step_rows, step_rows, sem)

    for k in range(1, tm // EXPERT_SUB + 1):
        @pl.when(n_sub == k)
        def _(k=k):
            e = s // f_split
            rows = slice(0, k * EXPERT_SUB)
            x = xb_ref[rows, :]
            gate_act = _dot(x, wgb_ref[...])
            up = _dot(x, wub_ref[...])
            record = rec_ref[rows, :]
            lane = lax.broadcasted_iota(jnp.int32, record.shape, 1)
            gate_col = jnp.sum(jnp.where(lane == 2 + e, record, 0.0), axis=1, keepdims=True)
            hid = (_silu(gate_act) * up * gate_col).astype(BF16)
            y_ref[rows, :] += _dot(hid, wdb_ref[...])

    @pl.when((i == n_tiles - 1) & (s == n_steps - 1) & (n_sub > 0))
    def _():
        _wait_row_gather(h_ref, xbuf_ref, sem)


def _experts(h, src, tile_group, tile_nsub, w_gate, w_up, w_down, layer, tm, f_split=2):
    n_rows = src.shape[0]
    fb = D_EXPERT // f_split
    last = EXPERTS_PER_GROUP * f_split - 1

    def expert_of(i, s, tg, ns, sr):
        return tg[i] * EXPERTS_PER_GROUP + jnp.where(ns[i] > 0, s, last) // f_split

    def half_of(i, s, tg, ns, sr):
        return jnp.where(ns[i] > 0, s, last) % f_split

    grid_spec = pltpu.PrefetchScalarGridSpec(
        num_scalar_prefetch=3,
        grid=(n_rows // tm, EXPERTS_PER_GROUP * f_split),
        in_specs=[
            pl.BlockSpec((None, None, D_MODEL, fb), lambda i, s, *p: (layer, expert_of(i, s, *p), 0, half_of(i, s, *p))),
            pl.BlockSpec((None, None, D_MODEL, fb), lambda i, s, *p: (layer, expert_of(i, s, *p), 0, half_of(i, s, *p))),
            pl.BlockSpec((None, None, fb, D_MODEL), lambda i, s, *p: (layer, expert_of(i, s, *p), half_of(i, s, *p), 0)),
            pl.BlockSpec(memory_space=pl.ANY),
        ],
        out_specs=pl.BlockSpec((tm, D_MODEL), lambda i, s, *_: (i, 0)),
        scratch_shapes=[
            pltpu.VMEM((tm, D_MODEL + LANES), F32),
            pltpu.VMEM((tm, D_MODEL), BF16),
            pltpu.VMEM((tm, LANES), F32),
            pltpu.VMEM((D_MODEL, fb), BF16),
            pltpu.VMEM((D_MODEL, fb), BF16),
            pltpu.VMEM((fb, D_MODEL), BF16),
            pltpu.SemaphoreType.DMA(()),
        ],
    )
    return pl.pallas_call(
        functools.partial(_experts_kernel, f_split),
        grid_spec=grid_spec,
        out_shape=jax.ShapeDtypeStruct((n_rows, D_MODEL), F32),
        compiler_params=_cparams(2),
        name="experts",
    )(tile_group, tile_nsub, src, w_gate, w_up, w_down, h)


def _combine_kernel(final_norm, n_first, pos_ref, x_ref, gate_ref, fg_ref, ys_ref, *refs):
    o_refs, buf_ref, sem = refs[:-2], refs[-2], refs[-1]
    tm, d = x_ref.shape
    i = pl.program_id(0)
    slot = i % 2

    @pl.when(i == 0)
    def _():
        _start_row_gather(ys_ref, pos_ref, 0, buf_ref.at[0], sem.at[0])

    _wait_row_gather(ys_ref, buf_ref.at[slot], sem.at[slot])
    n_tiles = pl.num_programs(0)
    nxt = jnp.minimum(i + 1, n_tiles - 1)
    _start_rows_unrolled(ys_ref, pos_ref, nxt * tm, buf_ref.at[1 - slot], 0, tm, sem.at[1 - slot])

    nb = tm // ROW_BLOCK
    upd = buf_ref[slot].reshape(nb, ROW_BLOCK, d) * gate_ref[...][:, None, :]
    x = x_ref[...] + upd.reshape(tm, d)
    if final_norm:
        x = x * lax.rsqrt(jnp.mean(x * x, axis=-1, keepdims=True) + EPS) * fg_ref[...]
    if n_first is None:
        o_refs[0][...] = x
    else:
        @pl.when(pl.program_id(0) < n_first)
        def _():
            o_refs[0][...] = x

        @pl.when(pl.program_id(0) >= n_first)
        def _():
            o_refs[1][...] = x

    @pl.when(i == n_tiles - 1)
    def _():
        _wait_row_gather(ys_ref, buf_ref.at[1 - slot], sem.at[1 - slot])


def _combine(x, ys, pos, modb, final_g, final_norm, split_rows=None, tm=512):
    t_rows = x.shape[0]
    nb = tm // ROW_BLOCK
    if split_rows is None:
        n_first = None
        out_specs = pl.BlockSpec((tm, D_MODEL), lambda i, p: (i, 0))
        out_shape = jax.ShapeDtypeStruct((t_rows, D_MODEL), F32)
    else:
        assert split_rows % tm == 0 and (t_rows - split_rows) % tm == 0
        n_first = split_rows // tm
        out_specs = [pl.BlockSpec((tm, D_MODEL), lambda i, p: (jnp.minimum(i, n_first - 1), 0)),
                     pl.BlockSpec((tm, D_MODEL), lambda i, p: (jnp.maximum(i - n_first, 0), 0))]
        out_shape = [jax.ShapeDtypeStruct((split_rows, D_MODEL), F32),
                     jax.ShapeDtypeStruct((t_rows - split_rows, D_MODEL), F32)]
    grid_spec = pltpu.PrefetchScalarGridSpec(
        num_scalar_prefetch=1,
        grid=(t_rows // tm,),
        in_specs=[
            pl.BlockSpec((tm, D_MODEL), lambda i, p: (i, 0)),
            pl.BlockSpec((nb, D_MODEL), lambda i, p: (i, 5)),
            pl.BlockSpec((1, D_MODEL), lambda i, p: (0, 0)),
            pl.BlockSpec(memory_space=pl.ANY),
        ],
        out_specs=out_specs,
        scratch_shapes=[pltpu.VMEM((2, tm, D_MODEL), F32), pltpu.SemaphoreType.DMA((2,))],
    )
    return pl.pallas_call(
        functools.partial(_combine_kernel, final_norm, n_first),
        grid_spec=grid_spec,
        out_shape=out_shape,
        compiler_params=_cparams(1),
        name="combine",
    )(pos, x, modb, final_g.reshape(1, D_MODEL), ys)


def kernel(x_prompt, x_sample, state_ssm, state_conv, c_prompt, c_sample, norm1_g, norm2_g, w_ada, b_ada,
           w_in, gm_ln_g, gm_ln_b, gm_w_s, gm_b_s, conv_w, conv_b, dt_bias, a_log, d_skip, mnorm_g, w_out,
           w_router_group, b_router_group, w_router_expert, b_router_expert, w_gate, w_up, w_down,
           final_norm_g):
    n_p, len_p, _ = x_prompt.shape
    n_s, len_s, _ = x_sample.shape
    assert len_p % GM_TILE == 0 and len_s == ROW_BLOCK and (n_s * len_s) % GM_TILE == 0
    rows_p, rows_s = n_p * len_p, n_s * len_s
    t_rows = rows_p + rows_s
    n_seq = n_p + n_s
    blk_per_prompt = len_p // ROW_BLOCK
    assert t_rows % EXPERT_TILE == 0
    n_expert_tiles = t_rows // EXPERT_TILE + N_EGROUPS

    xs = (x_prompt.reshape(rows_p, D_MODEL), x_sample.reshape(rows_s, D_MODEL))

    c_all = jnp.concatenate([c_prompt, c_sample], axis=0)
    c_pad = jnp.pad(c_all, ((0, (-n_seq) % 8), (0, 0)))
    mod = _ada(c_pad, w_ada, b_ada)
    modb_all = jnp.concatenate(
        [jnp.repeat(mod[:, :n_p], blk_per_prompt, axis=1), mod[:, n_p:n_seq]], axis=1)

    half_w = gm_w_s[:, :, :ROW_BLOCK, :ROW_BLOCK]
    w_cat = jnp.stack([gm_w_s, jnp.tile(half_w, (1, 1, 2, 2))], axis=1)
    b_t = jnp.swapaxes(gm_b_s, 1, 2)
    b_cat = jnp.stack([b_t, jnp.tile(b_t[:, :ROW_BLOCK], (1, 2, 1))], axis=1)

    w_in_t = jnp.swapaxes(w_in, 1, 2)
    w_r = jnp.concatenate([w_router_group, w_router_expert], axis=2)
    w_r = jnp.pad(w_r, ((0, 0), (0, 0), (0, LANES - w_r.shape[2])))
    b_r = jnp.concatenate([b_router_group, b_router_expert], axis=1)
    b_r = jnp.pad(b_r, ((0, 0), (0, LANES - b_r.shape[1])))[:, None, :]

    ssm_p, conv_p, ssm_s, conv_s, v_rows = [], [], [], [], []
    for l in range(DEPTH):
        last = l == DEPTH - 1
        modb = modb_all[l]
        proj, dt_raw = _inproj(xs, norm1_g, modb, w_in_t, l, tn=512 if len(xs) == 2 else 1024)
        gm_out, vn = _gmlp(proj, gm_ln_g, gm_ln_b, w_cat, b_cat, l, rows_p // GM_TILE, rows_s)
        ssm_args = (conv_w, conv_b, dt_bias, a_log, d_skip, mnorm_g, l)
        ssm_out_p, hfin_p, cfin_p = _ssd(proj, dt_raw, None, None, *ssm_args, 0, n_p, len_p, SSD_PROMPT_CHUNK)
        ssm_out_s, hfin_s, cfin_s = _ssd(proj, dt_raw, state_ssm, state_conv, *ssm_args, rows_p, n_s, len_s, len_s)
        x = _outproj(gm_out, (ssm_out_p, ssm_out_s), w_out, xs, modb, l)
        h2, route, counts = _router(x, norm2_g, modb, w_r[l], b_r[l], l)
        pos, src, tile_group, tile_nsub = _dispatch_tables(route, counts, t_rows, EXPERT_TILE, n_expert_tiles)
        ys = _experts(h2, src, tile_group, tile_nsub, w_gate, w_up, w_down, l, EXPERT_TILE)
        out = _combine(x, ys, pos, modb, final_norm_g, last, rows_p if last else None)
        xs = tuple(out) if last else (out,)
        ssm_p.append(hfin_p)
        conv_p.append(cfin_p)
        ssm_s.append(hfin_s)
        conv_s.append(cfin_s)
        v_rows.append(vn.reshape(n_s, len_s, GM_WIDTH))

    y_prompt = xs[0].reshape(n_p, len_p, D_MODEL)
    y_sample = xs[1].reshape(n_s, len_s, D_MODEL)
    return (y_prompt, y_sample, jnp.stack(ssm_p), jnp.stack(conv_p), jnp.stack(ssm_s), jnp.stack(conv_s),
            jnp.stack(v_rows))
```

```python
import functools
import math

import jax
import jax.numpy as jnp
import numpy as np
from jax import lax
from jax.experimental import pallas as pl
from jax.experimental.pallas import tpu as pltpu

D_MODEL = 2048
DEPTH = 2
GM_WIDTH = 1024
GM_HEADS = 8
GM_HEAD_DIM = 128
GM_CHUNK = 128
M_WIDTH = 1024
M_HEAD_DIM = 64
M_HEADS = 16
M_GROUPS = 4
M_STATE = 128
M_CONV = 4
CONV_DIM = M_WIDTH + 2 * M_GROUPS * M_STATE
MAIN_DIM = 2 * GM_WIDTH + M_WIDTH + CONV_DIM
N_EGROUPS = 4
EXPERTS_PER_GROUP = 4
N_EXPERTS = 16
D_EXPERT = 512
N_MOD = 6
EPS = 1e-6

ROW_BLOCK = 64
LANES = 128
VMEM_LIMIT = 56 * 1024 * 1024

F32 = jnp.float32
BF16 = jnp.bfloat16


def _cparams(n_axes):
    return pltpu.CompilerParams(
        dimension_semantics=("arbitrary",) * n_axes, vmem_limit_bytes=VMEM_LIMIT)


def _silu(x):
    return x * jax.nn.sigmoid(x)


def _gelu_tanh(x):
    c = math.sqrt(2.0 / math.pi)
    return 0.5 * x * (1.0 + jnp.tanh(c * (x + 0.044715 * (x * x * x))))


def _dot(a, b):
    return jnp.dot(a, b, preferred_element_type=F32)


def _dot_nt(a, b, precision=None):
    return lax.dot_general(a, b, (((1,), (1,)), ((), ())), precision=precision,
                           preferred_element_type=F32)


def _dot_split2(a, b):
    a0 = a.astype(BF16)
    a1 = (a - a0.astype(F32)).astype(BF16)
    b0 = b.astype(BF16)
    b1 = (b - b0.astype(F32)).astype(BF16)
    return _dot(a0, b0) + _dot(a0, b1) + _dot(a1, b0)


def _dot_tn(a, b):
    return lax.dot_general(a, b, (((0,), (0,)), ((), ())), preferred_element_type=F32)


def _ada_kernel(c_ref, w_ref, b_ref, o_ref):
    c = _silu(c_ref[...]).astype(BF16)
    o_ref[...] = _dot(c, w_ref[...].astype(BF16)) + b_ref[...]


def _ada(c_pad, w_ada, b_ada):
    rows = c_pad.shape[0]
    tn = 1024
    n_out = N_MOD * D_MODEL
    return pl.pallas_call(
        _ada_kernel,
        grid=(DEPTH, n_out // tn),
        in_specs=[
            pl.BlockSpec((rows, D_MODEL), lambda l, j: (0, 0)),
            pl.BlockSpec((None, D_MODEL, tn), lambda l, j: (l, 0, j)),
            pl.BlockSpec((None, 1, tn), lambda l, j: (l, 0, j)),
        ],
        out_specs=pl.BlockSpec((None, rows, tn), lambda l, j: (l, 0, j)),
        out_shape=jax.ShapeDtypeStruct((DEPTH, rows, n_out), F32),
        compiler_params=_cparams(2),
        name="ada_ln",
    )(c_pad, w_ada, b_ada.reshape(DEPTH, 1, n_out))


def _stream_specs(xs, block, col_of, n_j):
    if len(xs) == 1:
        return [pl.BlockSpec(block, lambda i, j: (i, col_of(j)))], None
    n_first = xs[0].shape[0] // block[0]

    def first(i, j):
        return jnp.minimum(i, n_first - 1), jnp.where(i < n_first, col_of(j), col_of(n_j - 1))

    def second(i, j):
        return jnp.maximum(i - n_first, 0), jnp.where(i < n_first, col_of(0), col_of(j))

    return [pl.BlockSpec(block, first), pl.BlockSpec(block, second)], n_first


def _stream_tile(x_refs, n_first, idx):
    if n_first is None:
        return x_refs[0][idx]
    return jnp.where(pl.program_id(0) < n_first, x_refs[0][idx], x_refs[1][idx])


def _inproj_kernel(n_first, *refs):
    n_x = 1 if n_first is None else 2
    x_refs = refs[:n_x]
    g_ref, shift_ref, scale_ref, w_ref, wdt_ref, proj_ref, dt_ref, h_ref = refs[n_x:]
    j = pl.program_id(1)
    n_blk = shift_ref.shape[0]

    @pl.when(j == 0)
    def _():
        g = g_ref[...]

        def body(r, carry):
            rows = pl.ds(pl.multiple_of(r * ROW_BLOCK, ROW_BLOCK), ROW_BLOCK)
            xf = _stream_tile(x_refs, n_first, (rows, slice(None)))
            y = xf * lax.rsqrt(jnp.mean(xf * xf, axis=-1, keepdims=True) + EPS) * g
            y = y * (1.0 + scale_ref[pl.ds(r, 1), :]) + shift_ref[pl.ds(r, 1), :]
            h_ref[rows, :] = y.astype(BF16)
            return carry

        lax.fori_loop(0, n_blk, body, 0)
        dt_ref[...] = _dot_nt(h_ref[...], wdt_ref[...].astype(BF16))

    proj_ref[...] = _dot_nt(h_ref[...], w_ref[...].astype(BF16)).astype(BF16)


def _inproj(xs, norm_g, modb, w_in_t, layer, tm=1024, tn=512):
    t_rows = sum(x.shape[0] for x in xs)
    nb = tm // ROW_BLOCK
    assert all(x.shape[0] % tm == 0 for x in xs)
    x_specs, n_first = _stream_specs(xs, (tm, D_MODEL), lambda j: 0, MAIN_DIM // tn)
    return pl.pallas_call(
        functools.partial(_inproj_kernel, n_first),
        grid=(t_rows // tm, MAIN_DIM // tn),
        in_specs=x_specs + [
            pl.BlockSpec((None, 1, D_MODEL), lambda i, j: (layer, 0, 0)),
            pl.BlockSpec((nb, D_MODEL), lambda i, j: (i, 0)),
            pl.BlockSpec((nb, D_MODEL), lambda i, j: (i, 1)),
            pl.BlockSpec((None, tn, D_MODEL), lambda i, j: (layer, j, 0)),
            pl.BlockSpec((None, M_HEADS, D_MODEL), lambda i, j: (layer, MAIN_DIM // M_HEADS, 0)),
        ],
        out_specs=[
            pl.BlockSpec((tm, tn), lambda i, j: (i, j)),
            pl.BlockSpec((tm, M_HEADS), lambda i, j: (i, 0)),
        ],
        out_shape=[
            jax.ShapeDtypeStruct((t_rows, MAIN_DIM), BF16),
            jax.ShapeDtypeStruct((t_rows, M_HEADS), F32),
        ],
        scratch_shapes=[pltpu.VMEM((tm, D_MODEL), BF16)],
        compiler_params=_cparams(2),
        name="in_proj",
    )(*xs, norm_g.reshape(DEPTH, 1, D_MODEL), modb, modb, w_in_t, w_in_t)


def _gmlp_kernel(n_prompt_tiles, u_ref, v_ref, lng_ref, lnb_ref, w_ref, b_ref, out_ref, vn_ref):
    i = pl.program_id(0)
    t_idx = lax.broadcasted_iota(jnp.int32, (GM_CHUNK, GM_CHUNK), 0)
    s_idx = lax.broadcasted_iota(jnp.int32, (GM_CHUNK, GM_CHUNK), 1)
    same_seq = (t_idx // ROW_BLOCK) == (s_idx // ROW_BLOCK)
    allowed = (s_idx <= t_idx) & (same_seq | (i < n_prompt_tiles))
    w_masked = [jnp.where(allowed, w_ref[h], 0.0).astype(BF16) for h in range(GM_HEADS)]
    bias = b_ref[...]
    for r0 in range(0, u_ref.shape[0], GM_CHUNK):
        rows = slice(r0, r0 + GM_CHUNK)
        u = _gelu_tanh(u_ref[rows, :].astype(F32))
        v = _gelu_tanh(v_ref[rows, :].astype(F32))
        mu = jnp.mean(v, axis=-1, keepdims=True)
        vc = v - mu
        vn = vc * lax.rsqrt(jnp.mean(vc * vc, axis=-1, keepdims=True) + EPS)
        vn = vn * lng_ref[...] + lnb_ref[...]
        vn_ref[rows, :] = vn
        vnb = vn.astype(BF16)
        for h in range(GM_HEADS):
            cols = slice(h * GM_HEAD_DIM, (h + 1) * GM_HEAD_DIM)
            mixed = _dot(w_masked[h], vnb[:, cols]) + bias[:, h:h + 1]
            out_ref[rows, cols] = (u[:, cols] * mixed).astype(BF16)


def _gmlp(proj, ln_g, ln_b, w_cat, b_cat, layer, n_prompt_tiles, n_sample_rows):
    t_rows = proj.shape[0]
    n_tiles = t_rows // GM_TILE
    kern = functools.partial(_gmlp_kernel, n_prompt_tiles)
    return pl.pallas_call(
        kern,
        grid=(n_tiles,),
        in_specs=[
            pl.BlockSpec((GM_TILE, GM_WIDTH), lambda i: (i, 0)),
            pl.BlockSpec((GM_TILE, GM_WIDTH), lambda i: (i, 1)),
            pl.BlockSpec((None, 1, GM_WIDTH), lambda i: (layer, 0, 0)),
            pl.BlockSpec((None, 1, GM_WIDTH), lambda i: (layer, 0, 0)),
            pl.BlockSpec((None, None, GM_HEADS, GM_CHUNK, GM_CHUNK),
                         lambda i: (layer, jnp.where(i < n_prompt_tiles, 0, 1), 0, 0, 0)),
            pl.BlockSpec((None, None, GM_CHUNK, GM_HEADS),
                         lambda i: (layer, jnp.where(i < n_prompt_tiles, 0, 1), 0, 0)),
        ],
        out_specs=[
            pl.BlockSpec((GM_TILE, GM_WIDTH), lambda i: (i, 0)),
            pl.BlockSpec((GM_TILE, GM_WIDTH), lambda i: (jnp.maximum(i - n_prompt_tiles, 0), 0)),
        ],
        out_shape=[
            jax.ShapeDtypeStruct((t_rows, GM_WIDTH), BF16),
            jax.ShapeDtypeStruct((n_sample_rows, GM_WIDTH), F32),
        ],
        compiler_params=_cparams(1),
        name="gmlp",
    )(proj, proj, ln_g.reshape(DEPTH, 1, GM_WIDTH), ln_b.reshape(DEPTH, 1, GM_WIDTH), w_cat, b_cat)


GM_TILE = 4 * GM_CHUNK
CONV_PAD = 8
SSD_PROMPT_CHUNK = 256


def _widen(v, onehot):
    p0 = v.astype(BF16)
    r1 = v - p0.astype(F32)
    p1 = r1.astype(BF16)
    p2 = (r1 - p1.astype(F32)).astype(BF16)
    return _dot(p0, onehot) + _dot(p1, onehot) + _dot(p2, onehot)


def _ssd_kernel(q, zero_init, *refs):
    z_ref, xa_ref, xb_ref, dt_ref = refs[:4]
    refs = refs[4:]
    h0_ref = c0_ref = None
    if not zero_init:
        h0_ref, c0_ref = refs[:2]
        refs = refs[2:]
    cw_ref, cb_ref, dtb_ref, alog_ref, dskip_ref, ng_ref = refs[:6]
    refs = refs[6:]
    out_ref, hfin_ref, cfin_ref, state_ref, win_ref, xs_ref, bc_ref, xdt_ref, xdd_ref, y_ref = refs
    c = pl.program_id(1)
    tail = M_CONV - 1
    lo = CONV_PAD - tail
    hp = M_HEADS * M_HEAD_DIM

    @pl.when(c == 0)
    def _():
        if zero_init:
            state_ref[...] = jnp.zeros_like(state_ref)
            win_ref[lo:CONV_PAD, :] = jnp.zeros((tail, CONV_DIM), F32)
        else:
            state_ref[...] = h0_ref[...].reshape(hp, M_STATE).T
            win_ref[lo:CONV_PAD, :] = c0_ref[...]

    half = CONV_DIM // 2
    win_ref[CONV_PAD:CONV_PAD + q, 0:half] = xa_ref[...].astype(F32)
    win_ref[CONV_PAD:CONV_PAD + q, half:CONV_DIM] = xb_ref[...].astype(F32)
    for r0 in range(0, q, ROW_BLOCK):
        conv = cb_ref[...]
        for k in range(M_CONV):
            conv = conv + win_ref[lo + k + r0:lo + k + r0 + ROW_BLOCK, :] * cw_ref[k:k + 1, :]
        act = _silu(conv)
        xs_ref[r0:r0 + ROW_BLOCK, :] = act[:, 0:M_WIDTH]
        bc_ref[r0:r0 + ROW_BLOCK, :] = act[:, M_WIDTH:CONV_DIM].astype(BF16)
    new_tail = win_ref[CONV_PAD + q - tail:CONV_PAD + q, :]
    cfin_ref[...] = new_tail
    win_ref[lo:CONV_PAD, :] = new_tail

    hi = lax.Precision.HIGHEST
    dt = jax.nn.softplus(dt_ref[...] + dtb_ref[...])
    a = -jnp.exp(alog_ref[...])
    r_idx = lax.broadcasted_iota(jnp.int32, (q, q), 0)
    c_idx = lax.broadcasted_iota(jnp.int32, (q, q), 1)
    causal = c_idx <= r_idx
    a_cs = jnp.dot(causal.astype(F32), dt * a, precision=hi, preferred_element_type=F32)
    eye = (lax.broadcasted_iota(jnp.int32, (M_HEADS, M_HEADS), 0)
           == lax.broadcasted_iota(jnp.int32, (M_HEADS, M_HEADS), 1)).astype(F32)
    a_cs_t = _dot_nt(eye, a_cs, precision=hi)
    widen = (lax.broadcasted_iota(jnp.int32, (M_HEADS, hp), 1) // M_HEAD_DIM
             == lax.broadcasted_iota(jnp.int32, (M_HEADS, hp), 0)).astype(BF16)
    dt_w = _widen(dt, widen)
    acs_w = _widen(a_cs, widen)
    dskip_w = _widen(jnp.broadcast_to(dskip_ref[...], (8, M_HEADS)), widen)[0:1, :]
    a_end_w = acs_w[q - 1:q, :]
    x_dt = xs_ref[...] * dt_w
    xdt_ref[...] = x_dt.astype(BF16)
    xdd_ref[...] = (x_dt * jnp.exp(a_end_w - acs_w)).astype(BF16)
    decay_blk_w = jnp.exp(a_end_w)

    heads_per_group = M_HEADS // M_GROUPS
    gw = M_WIDTH // M_GROUPS
    pair_lane = lax.broadcasted_iota(jnp.int32, (q, 2 * M_HEAD_DIM), 1)
    for g in range(M_GROUPS):
        gcols = slice(g * gw, (g + 1) * gw)
        b_g = bc_ref[:, g * M_STATE:(g + 1) * M_STATE]
        c_g = bc_ref[:, (M_GROUPS + g) * M_STATE:(M_GROUPS + g + 1) * M_STATE]
        cb = _dot_nt(c_g, b_g)
        st = state_ref[:, gcols]
        y_off = _dot(c_g, st.astype(BF16))
        state_ref[:, gcols] = st * decay_blk_w[:, gcols] + _dot_tn(b_g, xdd_ref[:, gcols])
        for pr in range(heads_per_group // 2):
            h0 = g * heads_per_group + 2 * pr
            pcols = slice(h0 * M_HEAD_DIM, (h0 + 2) * M_HEAD_DIM)
            x_pair = xdt_ref[:, pcols]
            ys = []
            for h in (h0, h0 + 1):
                seg = a_cs[:, h:h + 1] - a_cs_t[h:h + 1, :]
                scores = cb * jnp.exp(jnp.where(causal, seg, -jnp.inf))
                ys.append(_dot(scores.astype(BF16), x_pair))
            y_ref[:, pcols] = jnp.where(pair_lane < M_HEAD_DIM, ys[0], ys[1])
        y_ref[:, gcols] = (y_ref[:, gcols] + y_off * jnp.exp(acs_w[:, gcols])
                           + dskip_w[:, gcols] * xs_ref[:, gcols])

    @pl.when(c == pl.num_programs(1) - 1)
    def _():
        hfin_ref[...] = state_ref[...].T.reshape(M_HEADS, M_HEAD_DIM, M_STATE)

    gy = y_ref[...] * _silu(z_ref[...].astype(F32))
    ng = ng_ref[...]
    for g in range(M_GROUPS):
        cols = slice(g * gw, (g + 1) * gw)
        s = gy[:, cols]
        s = s * lax.rsqrt(jnp.mean(s * s, axis=-1, keepdims=True) + EPS)
        out_ref[:, cols] = (s * ng[:, cols]).astype(BF16)


def _ssd(proj, dt_raw, h0, c0, conv_w, conv_b, dt_bias, a_log, d_skip, mnorm_g, layer,
         row_start, n_seq, seq_len, q):
    n_c = seq_len // q
    blk0 = row_start // q
    half = CONV_DIM // 2
    z_blk = (2 * GM_WIDTH) // M_WIDTH
    xa_blk = (2 * GM_WIDTH + M_WIDTH) // half
    zero_init = h0 is None

    def rows(col):
        return lambda s, c: (blk0 + s * n_c + c, col)

    def per_layer(*shape):
        return pl.BlockSpec((None,) + shape, lambda s, c: (layer,) + (0,) * len(shape))

    in_specs = [
        pl.BlockSpec((q, M_WIDTH), rows(z_blk)),
        pl.BlockSpec((q, half), rows(xa_blk)),
        pl.BlockSpec((q, half), rows(xa_blk + 1)),
        pl.BlockSpec((q, M_HEADS), rows(0)),
    ]
    args = [proj, proj, proj, dt_raw]
    if not zero_init:
        in_specs += [
            pl.BlockSpec((None, None, M_HEADS, M_HEAD_DIM, M_STATE), lambda s, c: (layer, s, 0, 0, 0)),
            pl.BlockSpec((None, None, M_CONV - 1, CONV_DIM), lambda s, c: (layer, s, 0, 0)),
        ]
        args += [h0, c0]
    in_specs += [per_layer(M_CONV, CONV_DIM), per_layer(1, CONV_DIM), per_layer(1, M_HEADS),
                 per_layer(1, M_HEADS), per_layer(1, M_HEADS), per_layer(1, M_WIDTH)]
    args += [conv_w, conv_b.reshape(DEPTH, 1, CONV_DIM), dt_bias.reshape(DEPTH, 1, M_HEADS),
             a_log.reshape(DEPTH, 1, M_HEADS), d_skip.reshape(DEPTH, 1, M_HEADS),
             mnorm_g.reshape(DEPTH, 1, M_WIDTH)]
    return pl.pallas_call(
        functools.partial(_ssd_kernel, q, zero_init),
        grid=(n_seq, n_c),
        in_specs=in_specs,
        out_specs=[
            pl.BlockSpec((q, M_WIDTH), lambda s, c: (s * n_c + c, 0)),
            pl.BlockSpec((None, M_HEADS, M_HEAD_DIM, M_STATE), lambda s, c: (s, 0, 0, 0)),
            pl.BlockSpec((None, M_CONV - 1, CONV_DIM), lambda s, c: (s, 0, 0)),
        ],
        out_shape=[
            jax.ShapeDtypeStruct((n_seq * seq_len, M_WIDTH), BF16),
            jax.ShapeDtypeStruct((n_seq, M_HEADS, M_HEAD_DIM, M_STATE), F32),
            jax.ShapeDtypeStruct((n_seq, M_CONV - 1, CONV_DIM), F32),
        ],
        scratch_shapes=[
            pltpu.VMEM((M_STATE, M_WIDTH), F32),
            pltpu.VMEM((CONV_PAD + q, CONV_DIM), F32),
            pltpu.VMEM((q, M_WIDTH), F32),
            pltpu.VMEM((q, 2 * M_GROUPS * M_STATE), BF16),
            pltpu.VMEM((q, M_WIDTH), BF16),
            pltpu.VMEM((q, M_WIDTH), BF16),
            pltpu.VMEM((q, M_WIDTH), F32),
        ],
        compiler_params=_cparams(2),
        name="conv_ssd",
    )(*args)


def _outproj_kernel(n_first_ssm, n_first, gm_ref, wa_ref, wb_ref, gate_ref, *refs):
    n_ssm = 1 if n_first_ssm is None else 2
    ssm_refs, x_refs, o_ref = refs[:n_ssm], refs[n_ssm:-1], refs[-1]
    ssm = _stream_tile(ssm_refs, n_first_ssm, Ellipsis)
    acc = _dot(gm_ref[...], wa_ref[...].astype(BF16)) + _dot(ssm, wb_ref[...].astype(BF16))
    tm, tn = acc.shape
    nb = tm // ROW_BLOCK
    upd = acc.reshape(nb, ROW_BLOCK, tn) * gate_ref[...][:, None, :]
    o_ref[...] = _stream_tile(x_refs, n_first, Ellipsis) + upd.reshape(tm, tn)


def _outproj(gm, ssms, w_out, xs, modb, layer, tm=1024, tn=512):
    t_rows = gm.shape[0]
    nb = tm // ROW_BLOCK
    gate_blk = 2 * D_MODEL // tn
    assert all(x.shape[0] % tm == 0 for x in xs)
    assert all(x.shape[0] % tm == 0 for x in ssms)
    x_specs, n_first = _stream_specs(xs, (tm, tn), lambda j: j, D_MODEL // tn)
    ssm_specs, n_first_ssm = _stream_specs(ssms, (tm, M_WIDTH), lambda j: 0, D_MODEL // tn)
    return pl.pallas_call(
        functools.partial(_outproj_kernel, n_first_ssm, n_first),
        grid=(t_rows // tm, D_MODEL // tn),
        in_specs=[
            pl.BlockSpec((tm, GM_WIDTH), lambda i, j: (i, 0)),
            pl.BlockSpec((None, GM_WIDTH, tn), lambda i, j: (layer, 0, j)),
            pl.BlockSpec((None, M_WIDTH, tn), lambda i, j: (layer, 1, j)),
            pl.BlockSpec((nb, tn), lambda i, j: (i, gate_blk + j)),
        ] + ssm_specs + x_specs,
        out_specs=pl.BlockSpec((tm, tn), lambda i, j: (i, j)),
        out_shape=jax.ShapeDtypeStruct((t_rows, D_MODEL), F32),
        compiler_params=_cparams(2),
        name="out_proj",
    )(gm, w_out, w_out, modb, *ssms, *xs)


ROUTE_ROWS = 8
EXPERT_TILE = 1024
EXPERT_SUB = 256


def _row_argmax_first(rows):
    m = rows[0]
    for r in rows[1:]:
        m = jnp.maximum(m, r)
    idx = jnp.full(m.shape, len(rows), jnp.int32)
    for k in reversed(range(len(rows))):
        idx = jnp.where(rows[k] == m, k, idx)
    return m, idx


def _router_kernel(x_ref, g_ref, shift_ref, scale_ref, wr_ref, br_ref, h_ref, route_ref, count_ref, run_ref):
    n_blk = shift_ref.shape[0]
    g = g_ref[...]

    @pl.when(pl.program_id(0) == 0)
    def _():
        run_ref[...] = jnp.zeros_like(run_ref)

    def body(r, carry):
        rows = pl.ds(pl.multiple_of(r * ROW_BLOCK, ROW_BLOCK), ROW_BLOCK)
        xf = x_ref[rows, :]
        y = xf * lax.rsqrt(jnp.mean(xf * xf, axis=-1, keepdims=True) + EPS) * g
        h_ref[rows, 0:D_MODEL] = y * (1.0 + scale_ref[pl.ds(r, 1), :]) + shift_ref[pl.ds(r, 1), :]
        return carry

    lax.fori_loop(0, n_blk, body, 0)
    logits = (_dot_split2(h_ref[:, 0:D_MODEL], wr_ref[...]) + br_ref[...]).T
    lg = [logits[k:k + 1, :] for k in range(N_EGROUPS)]
    g_max, g_idx = _row_argmax_first(lg)
    g_den = lg[0] * 0.0
    for r in lg:
        g_den = g_den + jnp.exp(r - g_max)
    g_top = 1.0 / g_den
    e_in = []
    for k in range(EXPERTS_PER_GROUP):
        sel = lg[0] * 0.0
        for gi in range(N_EGROUPS):
            row = N_EGROUPS + gi * EXPERTS_PER_GROUP + k
            sel = jnp.where(g_idx == gi, logits[row:row + 1, :], sel)
        e_in.append(sel)
    e_max, i1 = _row_argmax_first(e_in)
    e_exp = [jnp.exp(r - e_max) for r in e_in]
    e_den = e_exp[0]
    for r in e_exp[1:]:
        e_den = e_den + r
    p = [r / e_den for r in e_exp]
    p1 = p[0] * 0.0
    for k in range(EXPERTS_PER_GROUP):
        p1 = jnp.where(i1 == k, p[k], p1)
    rest = [jnp.where(i1 == k, -1.0, p[k]) for k in range(EXPERTS_PER_GROUP)]
    p2, i2 = _row_argmax_first(rest)
    norm = p1 + p2
    w1 = p1 / norm * g_top
    w2 = p2 / norm * g_top
    tm = logits.shape[1]
    row = lax.broadcasted_iota(jnp.int32, (ROUTE_ROWS, tm), 0)
    onehot = (row == g_idx).astype(F32)
    before = (lax.broadcasted_iota(jnp.int32, (tm, tm), 0)
              < lax.broadcasted_iota(jnp.int32, (tm, tm), 1)).astype(BF16)
    earlier = _dot(onehot.astype(BF16), before)
    rank = jnp.sum(onehot * (earlier + run_ref[:, 0:1]), axis=0, keepdims=True)
    run_ref[...] += jnp.sum(onehot, axis=1, keepdims=True)
    count_ref[...] = run_ref[...]

    route_ref[0:1, :] = g_idx.astype(F32)
    route_ref[1:2, :] = rank
    for k in range(EXPERTS_PER_GROUP):
        route_ref[2 + k:3 + k, :] = jnp.where(i1 == k, w1, 0.0) + jnp.where(i2 == k, w2, 0.0)
    route_ref[2 + EXPERTS_PER_GROUP:ROUTE_ROWS, :] = jnp.zeros((ROUTE_ROWS - 2 - EXPERTS_PER_GROUP, tm), F32)
    place = (lax.broadcasted_iota(jnp.int32, (ROUTE_ROWS, LANES), 0)
             == lax.broadcasted_iota(jnp.int32, (ROUTE_ROWS, LANES), 1)).astype(F32)
    h_ref[:, D_MODEL:D_MODEL + LANES] = lax.dot_general(
        route_ref[...], place, (((0,), (0,)), ((), ())), precision=lax.Precision.HIGHEST,
        preferred_element_type=F32)


def _router(x, norm_g, modb, w_r, b_r, layer, tm=512):
    t_rows = x.shape[0]
    nb = tm // ROW_BLOCK
    return pl.pallas_call(
        _router_kernel,
        grid=(t_rows // tm,),
        in_specs=[
            pl.BlockSpec((tm, D_MODEL), lambda i: (i, 0)),
            pl.BlockSpec((None, 1, D_MODEL), lambda i: (layer, 0, 0)),
            pl.BlockSpec((nb, D_MODEL), lambda i: (i, 3)),
            pl.BlockSpec((nb, D_MODEL), lambda i: (i, 4)),
            pl.BlockSpec((D_MODEL, LANES), lambda i: (0, 0)),
            pl.BlockSpec((1, LANES), lambda i: (0, 0)),
        ],
        out_specs=[
            pl.BlockSpec((tm, D_MODEL + LANES), lambda i: (i, 0)),
            pl.BlockSpec((ROUTE_ROWS, tm), lambda i: (0, i)),
            pl.BlockSpec((ROUTE_ROWS, LANES), lambda i: (0, 0)),
        ],
        out_shape=[
            jax.ShapeDtypeStruct((t_rows, D_MODEL + LANES), F32),
            jax.ShapeDtypeStruct((ROUTE_ROWS, t_rows), F32),
            jax.ShapeDtypeStruct((ROUTE_ROWS, LANES), F32),
        ],
        scratch_shapes=[pltpu.VMEM((ROUTE_ROWS, LANES), F32)],
        compiler_params=_cparams(1),
        name="router",
    )(x, norm_g.reshape(DEPTH, 1, D_MODEL), modb, modb, w_r, b_r)


def _dispatch_tables(route, counts, t_rows, tm, n_tiles):
    gid = route[0].astype(jnp.int32)
    rank = route[1].astype(jnp.int32)
    cnt = counts[:N_EGROUPS, 0].astype(jnp.int32)
    padded = (cnt + tm - 1) // tm * tm
    ends = jnp.cumsum(padded)
    starts = ends - padded
    pos = starts[gid] + rank
    src = jnp.zeros((n_tiles * tm,), jnp.int32).at[pos].set(jnp.arange(t_rows, dtype=jnp.int32))
    tile_start = jnp.arange(n_tiles, dtype=jnp.int32) * tm
    tile_group = jnp.sum((tile_start[:, None] >= ends[None, :]).astype(jnp.int32), axis=1)
    tile_group = jnp.minimum(tile_group, N_EGROUPS - 1)
    tile_rows = jnp.clip(cnt[tile_group] - (tile_start - starts[tile_group]), 0, tm)
    tile_nsub = (tile_rows + EXPERT_SUB - 1) // EXPERT_SUB
    return pos, src, tile_group, tile_nsub


def _start_row_gather(table_ref, idx_ref, idx_base, buf_ref, sem):
    def issue(r, carry):
        pltpu.make_async_copy(table_ref.at[pl.ds(idx_ref[idx_base + r], 1), :],
                              buf_ref.at[pl.ds(r, 1), :], sem).start()
        return carry

    lax.fori_loop(0, buf_ref.shape[0], issue, 0, unroll=8)


def _start_rows_unrolled(table_ref, idx_ref, idx_base, buf_ref, row0, n_rows, sem):
    for r in range(n_rows):
        pltpu.make_async_copy(table_ref.at[pl.ds(idx_ref[idx_base + r], 1), :],
                              buf_ref.at[pl.ds(row0 + r, 1), :], sem).start()


def _wait_row_gather(table_ref, buf_ref, sem):
    pltpu.make_async_copy(table_ref.at[pl.ds(0, buf_ref.shape[0]), :], buf_ref, sem).wait()


def _experts_kernel(f_split, tg_ref, ns_ref, src_ref, wg_ref, wu_ref, wd_ref, h_ref, y_ref,
                    xbuf_ref, xb_ref, rec_ref, wgb_ref, wub_ref, wdb_ref, sem):
    i = pl.program_id(0)
    s = pl.program_id(1)
    n_tiles = pl.num_programs(0)
    tm = xb_ref.shape[0]
    n_steps = pl.num_programs(1)
    n_sub = ns_ref[i]
    step_rows = tm // (EXPERTS_PER_GROUP * f_split)
    nxt = jnp.minimum(i + 1, n_tiles - 1)

    @pl.when((s == 0) & (i == 0))
    def _():
        _start_row_gather(h_ref, src_ref, 0, xbuf_ref, sem)

    @pl.when(s == 0)
    def _():
        y_ref[...] = jnp.zeros_like(y_ref)

        @pl.when((i == 0) | (ns_ref[jnp.maximum(i - 1, 0)] > 0))
        def _():
            _wait_row_gather(h_ref, xbuf_ref, sem)

        def keep(b, carry):
            rows = pl.ds(pl.multiple_of(b * EXPERT_SUB, EXPERT_SUB), EXPERT_SUB)
            xb_ref[rows, :] = xbuf_ref[rows, 0:D_MODEL].astype(BF16)
            rec_ref[rows, :] = xbuf_ref[rows, D_MODEL:D_MODEL + LANES]
            return carry

        lax.fori_loop(0, n_sub, keep, 0)

    @pl.when(n_sub > 0)
    def _():
        wgb_ref[...] = wg_ref[...].astype(BF16)
        wub_ref[...] = wu_ref[...].astype(BF16)
        wdb_ref[...] = wd_ref[...].astype(BF16)

    for k in range(1, tm // EXPERT_SUB + 1):
        @pl.when(n_sub == k)
        def _(k=k):
            _start_rows_unrolled(h_ref, src_ref, nxt * tm + s * step_rows, xbuf_ref, s * step_rows,
                                 step_rows, sem)
            e = s // f_split
            rows = slice(0, k * EXPERT_SUB)
            x = xb_ref[rows, :]
            gate_act = _dot(x, wgb_ref[...])
            up = _dot(x, wub_ref[...])
            record = rec_ref[rows, :]
            lane = lax.broadcasted_iota(jnp.int32, record.shape, 1)
            gate_col = jnp.sum(jnp.where(lane == 2 + e, record, 0.0), axis=1, keepdims=True)
            hid = (_silu(gate_act) * up * gate_col).astype(BF16)
            y_ref[rows, :] += _dot(hid, wdb_ref[...])

    @pl.when((i == n_tiles - 1) & (s == n_steps - 1) & (n_sub > 0))
    def _():
        _wait_row_gather(h_ref, xbuf_ref, sem)


def _experts(h, src, tile_group, tile_nsub, w_gate, w_up, w_down, layer, tm, f_split=2):
    n_rows = src.shape[0]
    fb = D_EXPERT // f_split
    last = EXPERTS_PER_GROUP * f_split - 1

    def expert_of(i, s, tg, ns, sr):
        return tg[i] * EXPERTS_PER_GROUP + jnp.where(ns[i] > 0, s, last) // f_split

    def half_of(i, s, tg, ns, sr):
        return jnp.where(ns[i] > 0, s, last) % f_split

    grid_spec = pltpu.PrefetchScalarGridSpec(
        num_scalar_prefetch=3,
        grid=(n_rows // tm, EXPERTS_PER_GROUP * f_split),
        in_specs=[
            pl.BlockSpec((None, None, D_MODEL, fb), lambda i, s, *p: (layer, expert_of(i, s, *p), 0, half_of(i, s, *p))),
            pl.BlockSpec((None, None, D_MODEL, fb), lambda i, s, *p: (layer, expert_of(i, s, *p), 0, half_of(i, s, *p))),
            pl.BlockSpec((None, None, fb, D_MODEL), lambda i, s, *p: (layer, expert_of(i, s, *p), half_of(i, s, *p), 0)),
            pl.BlockSpec(memory_space=pl.ANY),
        ],
        out_specs=pl.BlockSpec((tm, D_MODEL), lambda i, s, *_: (i, 0)),
        scratch_shapes=[
            pltpu.VMEM((tm, D_MODEL + LANES), F32),
            pltpu.VMEM((tm, D_MODEL), BF16),
            pltpu.VMEM((tm, LANES), F32),
            pltpu.VMEM((D_MODEL, fb), BF16),
            pltpu.VMEM((D_MODEL, fb), BF16),
            pltpu.VMEM((fb, D_MODEL), BF16),
            pltpu.SemaphoreType.DMA(()),
        ],
    )
    return pl.pallas_call(
        functools.partial(_experts_kernel, f_split),
        grid_spec=grid_spec,
        out_shape=jax.ShapeDtypeStruct((n_rows, D_MODEL), F32),
        compiler_params=_cparams(2),
        name="experts",
    )(tile_group, tile_nsub, src, w_gate, w_up, w_down, h)


def _combine_kernel(final_norm, n_first, pos_ref, x_ref, gate_ref, fg_ref, ys_ref, *refs):
    o_refs, buf_ref, sem = refs[:-2], refs[-2], refs[-1]
    tm, d = x_ref.shape
    i = pl.program_id(0)
    slot = i % 2

    @pl.when(i == 0)
    def _():
        _start_row_gather(ys_ref, pos_ref, 0, buf_ref.at[0], sem.at[0])

    _wait_row_gather(ys_ref, buf_ref.at[slot], sem.at[slot])
    n_tiles = pl.num_programs(0)
    nxt = jnp.minimum(i + 1, n_tiles - 1)
    _start_rows_unrolled(ys_ref, pos_ref, nxt * tm, buf_ref.at[1 - slot], 0, tm, sem.at[1 - slot])

    nb = tm // ROW_BLOCK
    upd = buf_ref[slot].reshape(nb, ROW_BLOCK, d) * gate_ref[...][:, None, :]
    x = x_ref[...] + upd.reshape(tm, d)
    if final_norm:
        x = x * lax.rsqrt(jnp.mean(x * x, axis=-1, keepdims=True) + EPS) * fg_ref[...]
    if n_first is None:
        o_refs[0][...] = x
    else:
        @pl.when(pl.program_id(0) < n_first)
        def _():
            o_refs[0][...] = x

        @pl.when(pl.program_id(0) >= n_first)
        def _():
            o_refs[1][...] = x

    @pl.when(i == n_tiles - 1)
    def _():
        _wait_row_gather(ys_ref, buf_ref.at[1 - slot], sem.at[1 - slot])


def _combine(x, ys, pos, modb, final_g, final_norm, split_rows=None, tm=512):
    t_rows = x.shape[0]
    nb = tm // ROW_BLOCK
    if split_rows is None:
        n_first = None
        out_specs = pl.BlockSpec((tm, D_MODEL), lambda i, p: (i, 0))
        out_shape = jax.ShapeDtypeStruct((t_rows, D_MODEL), F32)
    else:
        assert split_rows % tm == 0 and (t_rows - split_rows) % tm == 0
        n_first = split_rows // tm
        out_specs = [pl.BlockSpec((tm, D_MODEL), lambda i, p: (jnp.minimum(i, n_first - 1), 0)),
                     pl.BlockSpec((tm, D_MODEL), lambda i, p: (jnp.maximum(i - n_first, 0), 0))]
        out_shape = [jax.ShapeDtypeStruct((split_rows, D_MODEL), F32),
                     jax.ShapeDtypeStruct((t_rows - split_rows, D_MODEL), F32)]
    grid_spec = pltpu.PrefetchScalarGridSpec(
        num_scalar_prefetch=1,
        grid=(t_rows // tm,),
        in_specs=[
            pl.BlockSpec((tm, D_MODEL), lambda i, p: (i, 0)),
            pl.BlockSpec((nb, D_MODEL), lambda i, p: (i, 5)),
            pl.BlockSpec((1, D_MODEL), lambda i, p: (0, 0)),
            pl.BlockSpec(memory_space=pl.ANY),
        ],
        out_specs=out_specs,
        scratch_shapes=[pltpu.VMEM((2, tm, D_MODEL), F32), pltpu.SemaphoreType.DMA((2,))],
    )
    return pl.pallas_call(
        functools.partial(_combine_kernel, final_norm, n_first),
        grid_spec=grid_spec,
        out_shape=out_shape,
        compiler_params=_cparams(1),
        name="combine",
    )(pos, x, modb, final_g.reshape(1, D_MODEL), ys)


def kernel(x_prompt, x_sample, state_ssm, state_conv, c_prompt, c_sample, norm1_g, norm2_g, w_ada, b_ada,
           w_in, gm_ln_g, gm_ln_b, gm_w_s, gm_b_s, conv_w, conv_b, dt_bias, a_log, d_skip, mnorm_g, w_out,
           w_router_group, b_router_group, w_router_expert, b_router_expert, w_gate, w_up, w_down,
           final_norm_g):
    n_p, len_p, _ = x_prompt.shape
    n_s, len_s, _ = x_sample.shape
    assert len_p % GM_TILE == 0 and len_s == ROW_BLOCK and (n_s * len_s) % GM_TILE == 0
    rows_p, rows_s = n_p * len_p, n_s * len_s
    t_rows = rows_p + rows_s
    n_seq = n_p + n_s
    blk_per_prompt = len_p // ROW_BLOCK
    assert t_rows % EXPERT_TILE == 0
    n_expert_tiles = t_rows // EXPERT_TILE + N_EGROUPS

    xs = (x_prompt.reshape(rows_p, D_MODEL), x_sample.reshape(rows_s, D_MODEL))

    c_all = jnp.concatenate([c_prompt, c_sample], axis=0)
    c_pad = jnp.pad(c_all, ((0, (-n_seq) % 8), (0, 0)))
    mod = _ada(c_pad, w_ada, b_ada)
    modb_all = jnp.concatenate(
        [jnp.repeat(mod[:, :n_p], blk_per_prompt, axis=1), mod[:, n_p:n_seq]], axis=1)

    half_w = gm_w_s[:, :, :ROW_BLOCK, :ROW_BLOCK]
    w_cat = jnp.stack([gm_w_s, jnp.tile(half_w, (1, 1, 2, 2))], axis=1)
    b_t = jnp.swapaxes(gm_b_s, 1, 2)
    b_cat = jnp.stack([b_t, jnp.tile(b_t[:, :ROW_BLOCK], (1, 2, 1))], axis=1)

    w_in_t = jnp.swapaxes(w_in, 1, 2)
    w_r = jnp.concatenate([w_router_group, w_router_expert], axis=2)
    w_r = jnp.pad(w_r, ((0, 0), (0, 0), (0, LANES - w_r.shape[2])))
    b_r = jnp.concatenate([b_router_group, b_router_expert], axis=1)
    b_r = jnp.pad(b_r, ((0, 0), (0, LANES - b_r.shape[1])))[:, None, :]

    ssm_p, conv_p, ssm_s, conv_s, v_rows = [], [], [], [], []
    for l in range(DEPTH):
        last = l == DEPTH - 1
        modb = modb_all[l]
        proj, dt_raw = _inproj(xs, norm1_g, modb, w_in_t, l, tn=512 if len(xs) == 2 else 1024)
        gm_out, vn = _gmlp(proj, gm_ln_g, gm_ln_b, w_cat, b_cat, l, rows_p // GM_TILE, rows_s)
        ssm_args = (conv_w, conv_b, dt_bias, a_log, d_skip, mnorm_g, l)
        ssm_out_p, hfin_p, cfin_p = _ssd(proj, dt_raw, None, None, *ssm_args, 0, n_p, len_p, SSD_PROMPT_CHUNK)
        ssm_out_s, hfin_s, cfin_s = _ssd(proj, dt_raw, state_ssm, state_conv, *ssm_args, rows_p, n_s, len_s, len_s)
        x = _outproj(gm_out, (ssm_out_p, ssm_out_s), w_out, xs, modb, l)
        h2, route, counts = _router(x, norm2_g, modb, w_r[l], b_r[l], l)
        pos, src, tile_group, tile_nsub = _dispatch_tables(route, counts, t_rows, EXPERT_TILE, n_expert_tiles)
        ys = _experts(h2, src, tile_group, tile_nsub, w_gate, w_up, w_down, l, EXPERT_TILE)
        out = _combine(x, ys, pos, modb, final_norm_g, last, rows_p if last else None)
        xs = tuple(out) if last else (out,)
        ssm_p.append(hfin_p)
        conv_p.append(cfin_p)
        ssm_s.append(hfin_s)
        conv_s.append(cfin_s)
        v_rows.append(vn.reshape(n_s, len_s, GM_WIDTH))

    y_prompt = xs[0].reshape(n_p, len_p, D_MODEL)
    y_sample = xs[1].reshape(n_s, len_s, D_MODEL)
    return (y_prompt, y_sample, jnp.stack(ssm_p), jnp.stack(conv_p), jnp.stack(ssm_s), jnp.stack(conv_s),
            jnp.stack(v_rows))
```

```python
import functools
import math

import jax
import jax.numpy as jnp
import numpy as np
from jax import lax
from jax.experimental import pallas as pl
from jax.experimental.pallas import tpu as pltpu

D_MODEL = 2048
DEPTH = 2
GM_WIDTH = 1024
GM_HEADS = 8
GM_HEAD_DIM = 128
GM_CHUNK = 128
M_WIDTH = 1024
M_HEAD_DIM = 64
M_HEADS = 16
M_GROUPS = 4
M_STATE = 128
M_CONV = 4
CONV_DIM = M_WIDTH + 2 * M_GROUPS * M_STATE
MAIN_DIM = 2 * GM_WIDTH + M_WIDTH + CONV_DIM
N_EGROUPS = 4
EXPERTS_PER_GROUP = 4
N_EXPERTS = 16
D_EXPERT = 512
N_MOD = 6
EPS = 1e-6

ROW_BLOCK = 64
LANES = 128
VMEM_LIMIT = 56 * 1024 * 1024

F32 = jnp.float32
BF16 = jnp.bfloat16


def _cparams(n_axes):
    return pltpu.CompilerParams(
        dimension_semantics=("arbitrary",) * n_axes, vmem_limit_bytes=VMEM_LIMIT)


def _silu(x):
    return x * jax.nn.sigmoid(x)


def _gelu_tanh(x):
    c = math.sqrt(2.0 / math.pi)
    return 0.5 * x * (1.0 + jnp.tanh(c * (x + 0.044715 * (x * x * x))))


def _dot(a, b):
    return jnp.dot(a, b, preferred_element_type=F32)


def _dot_nt(a, b, precision=None):
    return lax.dot_general(a, b, (((1,), (1,)), ((), ())), precision=precision,
                           preferred_element_type=F32)


def _dot_split2(a, b):
    a0 = a.astype(BF16)
    a1 = (a - a0.astype(F32)).astype(BF16)
    b0 = b.astype(BF16)
    b1 = (b - b0.astype(F32)).astype(BF16)
    return _dot(a0, b0) + _dot(a0, b1) + _dot(a1, b0)


def _dot_tn(a, b):
    return lax.dot_general(a, b, (((0,), (0,)), ((), ())), preferred_element_type=F32)


def _ada_kernel(c_ref, w_ref, b_ref, o_ref):
    c = _silu(c_ref[...]).astype(BF16)
    o_ref[...] = _dot(c, w_ref[...].astype(BF16)) + b_ref[...]


def _ada(c_pad, w_ada, b_ada):
    rows = c_pad.shape[0]
    tn = 1024
    n_out = N_MOD * D_MODEL
    return pl.pallas_call(
        _ada_kernel,
        grid=(DEPTH, n_out // tn),
        in_specs=[
            pl.BlockSpec((rows, D_MODEL), lambda l, j: (0, 0)),
            pl.BlockSpec((None, D_MODEL, tn), lambda l, j: (l, 0, j)),
            pl.BlockSpec((None, 1, tn), lambda l, j: (l, 0, j)),
        ],
        out_specs=pl.BlockSpec((None, rows, tn), lambda l, j: (l, 0, j)),
        out_shape=jax.ShapeDtypeStruct((DEPTH, rows, n_out), F32),
        compiler_params=_cparams(2),
        name="ada_ln",
    )(c_pad, w_ada, b_ada.reshape(DEPTH, 1, n_out))


def _stream_specs(xs, block, col_of, n_j):
    if len(xs) == 1:
        return [pl.BlockSpec(block, lambda i, j: (i, col_of(j)))], None
    n_first = xs[0].shape[0] // block[0]

    def first(i, j):
        return jnp.minimum(i, n_first - 1), jnp.where(i < n_first, col_of(j), col_of(n_j - 1))

    def second(i, j):
        return jnp.maximum(i - n_first, 0), jnp.where(i < n_first, col_of(0), col_of(j))

    return [pl.BlockSpec(block, first), pl.BlockSpec(block, second, pipeline_mode=pl.Buffered(1))], n_first


def _stream_tile(x_refs, n_first, idx):
    if n_first is None:
        return x_refs[0][idx]
    return jnp.where(pl.program_id(0) < n_first, x_refs[0][idx], x_refs[1][idx])


def _inproj_kernel(n_first, *refs):
    n_x = 1 if n_first is None else 2
    x_refs = refs[:n_x]
    g_ref, shift_ref, scale_ref, w_ref, wdt_ref, proj_ref, dt_ref, h_ref = refs[n_x:]
    j = pl.program_id(1)
    n_blk = shift_ref.shape[0]

    @pl.when(j == 0)
    def _():
        g = g_ref[...]

        def body(r, carry):
            rows = pl.ds(pl.multiple_of(r * ROW_BLOCK, ROW_BLOCK), ROW_BLOCK)
            xf = _stream_tile(x_refs, n_first, (rows, slice(None)))
            y = xf * lax.rsqrt(jnp.mean(xf * xf, axis=-1, keepdims=True) + EPS) * g
            y = y * (1.0 + scale_ref[pl.ds(r, 1), :]) + shift_ref[pl.ds(r, 1), :]
            h_ref[rows, :] = y.astype(BF16)
            return carry

        lax.fori_loop(0, n_blk, body, 0)
        dt_ref[...] = _dot_nt(h_ref[...], wdt_ref[...].astype(BF16))

    proj_ref[...] = _dot_nt(h_ref[...], w_ref[...].astype(BF16)).astype(BF16)


def _inproj(xs, norm_g, modb, w_in_t, layer, tm=1024, tn=512):
    t_rows = sum(x.shape[0] for x in xs)
    nb = tm // ROW_BLOCK
    assert all(x.shape[0] % tm == 0 for x in xs)
    x_specs, n_first = _stream_specs(xs, (tm, D_MODEL), lambda j: 0, MAIN_DIM // tn)
    return pl.pallas_call(
        functools.partial(_inproj_kernel, n_first),
        grid=(t_rows // tm, MAIN_DIM // tn),
        in_specs=x_specs + [
            pl.BlockSpec((None, 1, D_MODEL), lambda i, j: (layer, 0, 0)),
            pl.BlockSpec((nb, D_MODEL), lambda i, j: (i, 0)),
            pl.BlockSpec((nb, D_MODEL), lambda i, j: (i, 1)),
            pl.BlockSpec((None, tn, D_MODEL), lambda i, j: (layer, j, 0)),
            pl.BlockSpec((None, M_HEADS, D_MODEL), lambda i, j: (layer, MAIN_DIM // M_HEADS, 0)),
        ],
        out_specs=[
            pl.BlockSpec((tm, tn), lambda i, j: (i, j)),
            pl.BlockSpec((tm, M_HEADS), lambda i, j: (i, 0)),
        ],
        out_shape=[
            jax.ShapeDtypeStruct((t_rows, MAIN_DIM), BF16),
            jax.ShapeDtypeStruct((t_rows, M_HEADS), F32),
        ],
        scratch_shapes=[pltpu.VMEM((tm, D_MODEL), BF16)],
        compiler_params=_cparams(2),
        name="in_proj",
    )(*xs, norm_g.reshape(DEPTH, 1, D_MODEL), modb, modb, w_in_t, w_in_t)


def _gmlp_kernel(n_prompt_tiles, u_ref, v_ref, lng_ref, lnb_ref, w_ref, b_ref, out_ref, vn_ref):
    i = pl.program_id(0)
    t_idx = lax.broadcasted_iota(jnp.int32, (GM_CHUNK, GM_CHUNK), 0)
    s_idx = lax.broadcasted_iota(jnp.int32, (GM_CHUNK, GM_CHUNK), 1)
    same_seq = (t_idx // ROW_BLOCK) == (s_idx // ROW_BLOCK)
    allowed = (s_idx <= t_idx) & (same_seq | (i < n_prompt_tiles))
    w_masked = [jnp.where(allowed, w_ref[h], 0.0).astype(BF16) for h in range(GM_HEADS)]
    bias = b_ref[...]
    for r0 in range(0, u_ref.shape[0], GM_CHUNK):
        rows = slice(r0, r0 + GM_CHUNK)
        u = _gelu_tanh(u_ref[rows, :].astype(F32))
        v = _gelu_tanh(v_ref[rows, :].astype(F32))
        mu = jnp.mean(v, axis=-1, keepdims=True)
        vc = v - mu
        vn = vc * lax.rsqrt(jnp.mean(vc * vc, axis=-1, keepdims=True) + EPS)
        vn = vn * lng_ref[...] + lnb_ref[...]
        vn_ref[rows, :] = vn
        vnb = vn.astype(BF16)
        for h in range(GM_HEADS):
            cols = slice(h * GM_HEAD_DIM, (h + 1) * GM_HEAD_DIM)
            mixed = _dot(w_masked[h], vnb[:, cols]) + bias[:, h:h + 1]
            out_ref[rows, cols] = (u[:, cols] * mixed).astype(BF16)


def _gmlp(proj, ln_g, ln_b, w_cat, b_cat, layer, n_prompt_tiles, n_sample_rows):
    t_rows = proj.shape[0]
    n_tiles = t_rows // GM_TILE
    kern = functools.partial(_gmlp_kernel, n_prompt_tiles)
    return pl.pallas_call(
        kern,
        grid=(n_tiles,),
        in_specs=[
            pl.BlockSpec((GM_TILE, GM_WIDTH), lambda i: (i, 0)),
            pl.BlockSpec((GM_TILE, GM_WIDTH), lambda i: (i, 1)),
            pl.BlockSpec((None, 1, GM_WIDTH), lambda i: (layer, 0, 0)),
            pl.BlockSpec((None, 1, GM_WIDTH), lambda i: (layer, 0, 0)),
            pl.BlockSpec((None, None, GM_HEADS, GM_CHUNK, GM_CHUNK),
                         lambda i: (layer, jnp.where(i < n_prompt_tiles, 0, 1), 0, 0, 0)),
            pl.BlockSpec((None, None, GM_CHUNK, GM_HEADS),
                         lambda i: (layer, jnp.where(i < n_prompt_tiles, 0, 1), 0, 0)),
        ],
        out_specs=[
            pl.BlockSpec((GM_TILE, GM_WIDTH), lambda i: (i, 0)),
            pl.BlockSpec((GM_TILE, GM_WIDTH), lambda i: (jnp.maximum(i - n_prompt_tiles, 0), 0)),
        ],
        out_shape=[
            jax.ShapeDtypeStruct((t_rows, GM_WIDTH), BF16),
            jax.ShapeDtypeStruct((n_sample_rows, GM_WIDTH), F32),
        ],
        compiler_params=_cparams(1),
        name="gmlp",
    )(proj, proj, ln_g.reshape(DEPTH, 1, GM_WIDTH), ln_b.reshape(DEPTH, 1, GM_WIDTH), w_cat, b_cat)


GM_TILE = 4 * GM_CHUNK
CONV_PAD = 8
SSD_PROMPT_CHUNK = 256


def _widen(v, onehot):
    p0 = v.astype(BF16)
    r1 = v - p0.astype(F32)
    p1 = r1.astype(BF16)
    p2 = (r1 - p1.astype(F32)).astype(BF16)
    return _dot(p0, onehot) + _dot(p1, onehot) + _dot(p2, onehot)


def _ssd_kernel(q, zero_init, *refs):
    z_ref, xa_ref, xb_ref, dt_ref = refs[:4]
    refs = refs[4:]
    h0_ref = c0_ref = None
    if not zero_init:
        h0_ref, c0_ref = refs[:2]
        refs = refs[2:]
    cw_ref, cb_ref, dtb_ref, alog_ref, dskip_ref, ng_ref = refs[:6]
    refs = refs[6:]
    out_ref, hfin_ref, cfin_ref, state_ref, win_ref, xs_ref, bc_ref, xdt_ref, xdd_ref, y_ref = refs
    c = pl.program_id(1)
    tail = M_CONV - 1
    lo = CONV_PAD - tail
    hp = M_HEADS * M_HEAD_DIM

    @pl.when(c == 0)
    def _():
        if zero_init:
            state_ref[...] = jnp.zeros_like(state_ref)
            win_ref[lo:CONV_PAD, :] = jnp.zeros((tail, CONV_DIM), F32)
        else:
            state_ref[...] = h0_ref[...].reshape(hp, M_STATE).T
            win_ref[lo:CONV_PAD, :] = c0_ref[...]

    half = CONV_DIM // 2
    win_ref[CONV_PAD:CONV_PAD + q, 0:half] = xa_ref[...].astype(F32)
    win_ref[CONV_PAD:CONV_PAD + q, half:CONV_DIM] = xb_ref[...].astype(F32)
    for r0 in range(0, q, ROW_BLOCK):
        conv = cb_ref[...]
        for k in range(M_CONV):
            conv = conv + win_ref[lo + k + r0:lo + k + r0 + ROW_BLOCK, :] * cw_ref[k:k + 1, :]
        act = _silu(conv)
        xs_ref[r0:r0 + ROW_BLOCK, :] = act[:, 0:M_WIDTH]
        bc_ref[r0:r0 + ROW_BLOCK, :] = act[:, M_WIDTH:CONV_DIM].astype(BF16)
    new_tail = win_ref[CONV_PAD + q - tail:CONV_PAD + q, :]
    cfin_ref[...] = new_tail
    win_ref[lo:CONV_PAD, :] = new_tail

    hi = lax.Precision.HIGHEST
    dt = jax.nn.softplus(dt_ref[...] + dtb_ref[...])
    a = -jnp.exp(alog_ref[...])
    r_idx = lax.broadcasted_iota(jnp.int32, (q, q), 0)
    c_idx = lax.broadcasted_iota(jnp.int32, (q, q), 1)
    causal = c_idx <= r_idx
    a_cs = jnp.dot(causal.astype(F32), dt * a, precision=hi, preferred_element_type=F32)
    eye = (lax.broadcasted_iota(jnp.int32, (M_HEADS, M_HEADS), 0)
           == lax.broadcasted_iota(jnp.int32, (M_HEADS, M_HEADS), 1)).astype(F32)
    a_cs_t = _dot_nt(eye, a_cs, precision=hi)
    widen = (lax.broadcasted_iota(jnp.int32, (M_HEADS, hp), 1) // M_HEAD_DIM
             == lax.broadcasted_iota(jnp.int32, (M_HEADS, hp), 0)).astype(BF16)
    dt_w = _widen(dt, widen)
    acs_w = _widen(a_cs, widen)
    dskip_w = _widen(jnp.broadcast_to(dskip_ref[...], (8, M_HEADS)), widen)[0:1, :]
    a_end_w = acs_w[q - 1:q, :]
    x_dt = xs_ref[...] * dt_w
    xdt_ref[...] = x_dt.astype(BF16)
    xdd_ref[...] = (x_dt * jnp.exp(a_end_w - acs_w)).astype(BF16)
    decay_blk_w = jnp.exp(a_end_w)

    heads_per_group = M_HEADS // M_GROUPS
    gw = M_WIDTH // M_GROUPS
    pair_lane = lax.broadcasted_iota(jnp.int32, (q, 2 * M_HEAD_DIM), 1)
    for g in range(M_GROUPS):
        gcols = slice(g * gw, (g + 1) * gw)
        b_g = bc_ref[:, g * M_STATE:(g + 1) * M_STATE]
        c_g = bc_ref[:, (M_GROUPS + g) * M_STATE:(M_GROUPS + g + 1) * M_STATE]
        cb = _dot_nt(c_g, b_g)
        st = state_ref[:, gcols]
        y_off = _dot(c_g, st.astype(BF16))
        state_ref[:, gcols] = st * decay_blk_w[:, gcols] + _dot_tn(b_g, xdd_ref[:, gcols])
        for pr in range(heads_per_group // 2):
            h0 = g * heads_per_group + 2 * pr
            pcols = slice(h0 * M_HEAD_DIM, (h0 + 2) * M_HEAD_DIM)
            x_pair = xdt_ref[:, pcols]
            ys = []
            for h in (h0, h0 + 1):
                seg = a_cs[:, h:h + 1] - a_cs_t[h:h + 1, :]
                scores = cb * jnp.exp(jnp.where(causal, seg, -jnp.inf))
                ys.append(_dot(scores.astype(BF16), x_pair))
            y_ref[:, pcols] = jnp.where(pair_lane < M_HEAD_DIM, ys[0], ys[1])
        y_ref[:, gcols] = (y_ref[:, gcols] + y_off * jnp.exp(acs_w[:, gcols])
                           + dskip_w[:, gcols] * xs_ref[:, gcols])

    @pl.when(c == pl.num_programs(1) - 1)
    def _():
        hfin_ref[...] = state_ref[...].T.reshape(M_HEADS, M_HEAD_DIM, M_STATE)

    gy = y_ref[...] * _silu(z_ref[...].astype(F32))
    ng = ng_ref[...]
    for g in range(M_GROUPS):
        cols = slice(g * gw, (g + 1) * gw)
        s = gy[:, cols]
        s = s * lax.rsqrt(jnp.mean(s * s, axis=-1, keepdims=True) + EPS)
        out_ref[:, cols] = (s * ng[:, cols]).astype(BF16)


def _ssd(proj, dt_raw, h0, c0, conv_w, conv_b, dt_bias, a_log, d_skip, mnorm_g, layer,
         row_start, n_seq, seq_len, q):
    n_c = seq_len // q
    blk0 = row_start // q
    half = CONV_DIM // 2
    z_blk = (2 * GM_WIDTH) // M_WIDTH
    xa_blk = (2 * GM_WIDTH + M_WIDTH) // half
    zero_init = h0 is None

    def rows(col):
        return lambda s, c: (blk0 + s * n_c + c, col)

    def per_layer(*shape):
        return pl.BlockSpec((None,) + shape, lambda s, c: (layer,) + (0,) * len(shape))

    in_specs = [
        pl.BlockSpec((q, M_WIDTH), rows(z_blk)),
        pl.BlockSpec((q, half), rows(xa_blk)),
        pl.BlockSpec((q, half), rows(xa_blk + 1)),
        pl.BlockSpec((q, M_HEADS), rows(0)),
    ]
    args = [proj, proj, proj, dt_raw]
    if not zero_init:
        in_specs += [
            pl.BlockSpec((None, None, M_HEADS, M_HEAD_DIM, M_STATE), lambda s, c: (layer, s, 0, 0, 0)),
            pl.BlockSpec((None, None, M_CONV - 1, CONV_DIM), lambda s, c: (layer, s, 0, 0)),
        ]
        args += [h0, c0]
    in_specs += [per_layer(M_CONV, CONV_DIM), per_layer(1, CONV_DIM), per_layer(1, M_HEADS),
                 per_layer(1, M_HEADS), per_layer(1, M_HEADS), per_layer(1, M_WIDTH)]
    args += [conv_w, conv_b.reshape(DEPTH, 1, CONV_DIM), dt_bias.reshape(DEPTH, 1, M_HEADS),
             a_log.reshape(DEPTH, 1, M_HEADS), d_skip.reshape(DEPTH, 1, M_HEADS),
             mnorm_g.reshape(DEPTH, 1, M_WIDTH)]
    return pl.pallas_call(
        functools.partial(_ssd_kernel, q, zero_init),
        grid=(n_seq, n_c),
        in_specs=in_specs,
        out_specs=[
            pl.BlockSpec((q, M_WIDTH), lambda s, c: (s * n_c + c, 0)),
            pl.BlockSpec((None, M_HEADS, M_HEAD_DIM, M_STATE), lambda s, c: (s, 0, 0, 0)),
            pl.BlockSpec((None, M_CONV - 1, CONV_DIM), lambda s, c: (s, 0, 0)),
        ],
        out_shape=[
            jax.ShapeDtypeStruct((n_seq * seq_len, M_WIDTH), BF16),
            jax.ShapeDtypeStruct((n_seq, M_HEADS, M_HEAD_DIM, M_STATE), F32),
            jax.ShapeDtypeStruct((n_seq, M_CONV - 1, CONV_DIM), F32),
        ],
        scratch_shapes=[
            pltpu.VMEM((M_STATE, M_WIDTH), F32),
            pltpu.VMEM((CONV_PAD + q, CONV_DIM), F32),
            pltpu.VMEM((q, M_WIDTH), F32),
            pltpu.VMEM((q, 2 * M_GROUPS * M_STATE), BF16),
            pltpu.VMEM((q, M_WIDTH), BF16),
            pltpu.VMEM((q, M_WIDTH), BF16),
            pltpu.VMEM((q, M_WIDTH), F32),
        ],
        compiler_params=_cparams(2),
        name="conv_ssd",
    )(*args)


def _outproj_kernel(n_first_ssm, n_first, gm_ref, wa_ref, wb_ref, gate_ref, g2_ref, shift2_ref, scale2_ref,
                    wr_ref, br_ref, *refs):
    n_ssm = 1 if n_first_ssm is None else 2
    ssm_refs, x_refs = refs[:n_ssm], refs[n_ssm:-6]
    o_ref, h_ref, route_ref, count_ref, x1_ref, run_ref = refs[-6:]
    j = pl.program_id(1)
    n_j = pl.num_programs(1)
    ssm = _stream_tile(ssm_refs, n_first_ssm, Ellipsis)
    acc = _dot(gm_ref[...], wa_ref[...].astype(BF16)) + _dot(ssm, wb_ref[...].astype(BF16))
    tm, tn = acc.shape
    nb = tm // ROW_BLOCK
    upd = acc.reshape(nb, ROW_BLOCK, tn) * gate_ref[...][:, None, :]
    x1 = _stream_tile(x_refs, n_first, Ellipsis) + upd.reshape(tm, tn)
    o_ref[...] = x1
    for jj in range(x1_ref.shape[1] // tn):
        @pl.when(j == jj)
        def _(jj=jj):
            x1_ref[:, jj * tn:(jj + 1) * tn] = x1

    @pl.when(j == n_j - 1)
    def _():
        _route_tile(x1_ref, g2_ref, shift2_ref, scale2_ref, wr_ref, br_ref, h_ref, route_ref, count_ref, run_ref)


def _outproj_route(gm, ssms, w_out, xs, modb, norm2_g, w_r, b_r, layer, tm=1024, tn=512):
    t_rows = gm.shape[0]
    nb = tm // ROW_BLOCK
    gate_blk = 2 * D_MODEL // tn
    assert all(x.shape[0] % tm == 0 for x in xs)
    assert all(x.shape[0] % tm == 0 for x in ssms)
    x_specs, n_first = _stream_specs(xs, (tm, tn), lambda j: j, D_MODEL // tn)
    ssm_specs, n_first_ssm = _stream_specs(ssms, (tm, M_WIDTH), lambda j: 0, D_MODEL // tn)
    once = dict(pipeline_mode=pl.Buffered(1))
    return pl.pallas_call(
        functools.partial(_outproj_kernel, n_first_ssm, n_first),
        grid=(t_rows // tm, D_MODEL // tn),
        in_specs=[
            pl.BlockSpec((tm, GM_WIDTH), lambda i, j: (i, 0)),
            pl.BlockSpec((None, GM_WIDTH, tn), lambda i, j: (layer, 0, j)),
            pl.BlockSpec((None, M_WIDTH, tn), lambda i, j: (layer, 1, j)),
            pl.BlockSpec((nb, tn), lambda i, j: (i, gate_blk + j)),
            pl.BlockSpec((None, 1, D_MODEL), lambda i, j: (layer, 0, 0)),
            pl.BlockSpec((nb, D_MODEL), lambda i, j: (i, 3)),
            pl.BlockSpec((nb, D_MODEL), lambda i, j: (i, 4)),
            pl.BlockSpec((D_MODEL, LANES), lambda i, j: (0, 0)),
            pl.BlockSpec((1, LANES), lambda i, j: (0, 0)),
        ] + ssm_specs + x_specs,
        out_specs=[
            pl.BlockSpec((tm, tn), lambda i, j: (i, j)),
            pl.BlockSpec((tm, D_MODEL + LANES), lambda i, j: (i, 0), **once),
            pl.BlockSpec((ROUTE_ROWS, tm), lambda i, j: (0, i)),
            pl.BlockSpec((ROUTE_ROWS, LANES), lambda i, j: (0, 0)),
        ],
        out_shape=[
            jax.ShapeDtypeStruct((t_rows, D_MODEL), F32),
            jax.ShapeDtypeStruct((t_rows, D_MODEL + LANES), F32),
            jax.ShapeDtypeStruct((ROUTE_ROWS, t_rows), F32),
            jax.ShapeDtypeStruct((ROUTE_ROWS, LANES), F32),
        ],
        scratch_shapes=[pltpu.VMEM((tm, D_MODEL), F32), pltpu.VMEM((ROUTE_ROWS, LANES), F32)],
        compiler_params=_cparams(2),
        name="out_proj_route",
    )(gm, w_out, w_out, modb, norm2_g.reshape(DEPTH, 1, D_MODEL), modb, modb, w_r, b_r, *ssms, *xs)


ROUTE_ROWS = 8
EXPERT_TILE = 1024
EXPERT_SUB = 256


def _row_argmax_first(rows):
    m = rows[0]
    for r in rows[1:]:
        m = jnp.maximum(m, r)
    idx = jnp.full(m.shape, len(rows), jnp.int32)
    for k in reversed(range(len(rows))):
        idx = jnp.where(rows[k] == m, k, idx)
    return m, idx


def _route_tile(x_ref, g_ref, shift_ref, scale_ref, wr_ref, br_ref, h_ref, route_ref, count_ref, run_ref):
    n_blk = shift_ref.shape[0]
    g = g_ref[...]

    @pl.when(pl.program_id(0) == 0)
    def _():
        run_ref[...] = jnp.zeros_like(run_ref)

    def body(r, carry):
        rows = pl.ds(pl.multiple_of(r * ROW_BLOCK, ROW_BLOCK), ROW_BLOCK)
        xf = x_ref[rows, :]
        y = xf * lax.rsqrt(jnp.mean(xf * xf, axis=-1, keepdims=True) + EPS) * g
        h_ref[rows, 0:D_MODEL] = y * (1.0 + scale_ref[pl.ds(r, 1), :]) + shift_ref[pl.ds(r, 1), :]
        return carry

    lax.fori_loop(0, n_blk, body, 0)
    logits = (_dot_split2(h_ref[:, 0:D_MODEL], wr_ref[...]) + br_ref[...]).T
    lg = [logits[k:k + 1, :] for k in range(N_EGROUPS)]
    g_max, g_idx = _row_argmax_first(lg)
    g_den = lg[0] * 0.0
    for r in lg:
        g_den = g_den + jnp.exp(r - g_max)
    g_top = 1.0 / g_den
    e_in = []
    for k in range(EXPERTS_PER_GROUP):
        sel = lg[0] * 0.0
        for gi in range(N_EGROUPS):
            row = N_EGROUPS + gi * EXPERTS_PER_GROUP + k
            sel = jnp.where(g_idx == gi, logits[row:row + 1, :], sel)
        e_in.append(sel)
    e_max, i1 = _row_argmax_first(e_in)
    e_exp = [jnp.exp(r - e_max) for r in e_in]
    e_den = e_exp[0]
    for r in e_exp[1:]:
        e_den = e_den + r
    p = [r / e_den for r in e_exp]
    p1 = p[0] * 0.0
    for k in range(EXPERTS_PER_GROUP):
        p1 = jnp.where(i1 == k, p[k], p1)
    rest = [jnp.where(i1 == k, -1.0, p[k]) for k in range(EXPERTS_PER_GROUP)]
    p2, i2 = _row_argmax_first(rest)
    norm = p1 + p2
    w1 = p1 / norm * g_top
    w2 = p2 / norm * g_top
    tm = logits.shape[1]
    row = lax.broadcasted_iota(jnp.int32, (ROUTE_ROWS, tm), 0)
    onehot = (row == g_idx).astype(F32)
    before = (lax.broadcasted_iota(jnp.int32, (tm, tm), 0)
              < lax.broadcasted_iota(jnp.int32, (tm, tm), 1)).astype(BF16)
    earlier = _dot(onehot.astype(BF16), before)
    rank = jnp.sum(onehot * (earlier + run_ref[:, 0:1]), axis=0, keepdims=True)
    run_ref[...] += jnp.sum(onehot, axis=1, keepdims=True)
    count_ref[...] = run_ref[...]

    route_ref[0:1, :] = g_idx.astype(F32)
    route_ref[1:2, :] = rank
    for k in range(EXPERTS_PER_GROUP):
        route_ref[2 + k:3 + k, :] = jnp.where(i1 == k, w1, 0.0) + jnp.where(i2 == k, w2, 0.0)
    route_ref[2 + EXPERTS_PER_GROUP:ROUTE_ROWS, :] = jnp.zeros((ROUTE_ROWS - 2 - EXPERTS_PER_GROUP, tm), F32)
    place = (lax.broadcasted_iota(jnp.int32, (ROUTE_ROWS, LANES), 0)
             == lax.broadcasted_iota(jnp.int32, (ROUTE_ROWS, LANES), 1)).astype(F32)
    h_ref[:, D_MODEL:D_MODEL + LANES] = lax.dot_general(
        route_ref[...], place, (((0,), (0,)), ((), ())), precision=lax.Precision.HIGHEST,
        preferred_element_type=F32)


def _dispatch_tables(route, counts, t_rows, tm, n_tiles):
    gid = route[0].astype(jnp.int32)
    rank = route[1].astype(jnp.int32)
    cnt = counts[:N_EGROUPS, 0].astype(jnp.int32)
    padded = (cnt + tm - 1) // tm * tm
    ends = jnp.cumsum(padded)
    starts = ends - padded
    pos = starts[gid] + rank
    src = jnp.zeros((n_tiles * tm,), jnp.int32).at[pos].set(jnp.arange(t_rows, dtype=jnp.int32))
    tile_start = jnp.arange(n_tiles, dtype=jnp.int32) * tm
    tile_group = jnp.sum((tile_start[:, None] >= ends[None, :]).astype(jnp.int32), axis=1)
    tile_group = jnp.minimum(tile_group, N_EGROUPS - 1)
    tile_rows = jnp.clip(cnt[tile_group] - (tile_start - starts[tile_group]), 0, tm)
    tile_nsub = (tile_rows + EXPERT_SUB - 1) // EXPERT_SUB
    return pos, src, tile_group, tile_nsub


def _start_row_gather(table_ref, idx_ref, idx_base, buf_ref, sem):
    def issue(r, carry):
        pltpu.make_async_copy(table_ref.at[pl.ds(idx_ref[idx_base + r], 1), :],
                              buf_ref.at[pl.ds(r, 1), :], sem).start()
        return carry

    lax.fori_loop(0, buf_ref.shape[0], issue, 0, unroll=8)


def _start_rows_unrolled(table_ref, idx_ref, idx_base, buf_ref, row0, n_rows, sem):
    for r in range(n_rows):
        pltpu.make_async_copy(table_ref.at[pl.ds(idx_ref[idx_base + r], 1), :],
                              buf_ref.at[pl.ds(row0 + r, 1), :], sem).start()


def _wait_row_gather(table_ref, buf_ref, sem):
    pltpu.make_async_copy(table_ref.at[pl.ds(0, buf_ref.shape[0]), :], buf_ref, sem).wait()


def _experts_kernel(f_split, tg_ref, ns_ref, src_ref, wg_ref, wu_ref, wd_ref, h_ref, y_ref,
                    xbuf_ref, xb_ref, rec_ref, wgb_ref, wub_ref, wdb_ref, sem):
    i = pl.program_id(0)
    s = pl.program_id(1)
    n_tiles = pl.num_programs(0)
    tm = xb_ref.shape[0]
    n_steps = pl.num_programs(1)
    n_sub = ns_ref[i]
    step_rows = tm // (EXPERTS_PER_GROUP * f_split)
    nxt = jnp.minimum(i + 1, n_tiles - 1)

    @pl.when((s == 0) & (i == 0))
    def _():
        _start_row_gather(h_ref, src_ref, 0, xbuf_ref, sem)

    @pl.when(s == 0)
    def _():
        y_ref[...] = jnp.zeros_like(y_ref)

        @pl.when((i == 0) | (ns_ref[jnp.maximum(i - 1, 0)] > 0))
        def _():
            _wait_row_gather(h_ref, xbuf_ref, sem)

        def keep(b, carry):
            rows = pl.ds(pl.multiple_of(b * EXPERT_SUB, EXPERT_SUB), EXPERT_SUB)
            xb_ref[rows, :] = xbuf_ref[rows, 0:D_MODEL].astype(BF16)
            rec_ref[rows, :] = xbuf_ref[rows, D_MODEL:D_MODEL + LANES]
            return carry

        lax.fori_loop(0, n_sub, keep, 0)

    @pl.when(n_sub > 0)
    def _():
        wgb_ref[...] = wg_ref[...].astype(BF16)
        wub_ref[...] = wu_ref[...].astype(BF16)
        wdb_ref[...] = wd_ref[...].astype(BF16)

    for k in range(1, tm // EXPERT_SUB + 1):
        @pl.when(n_sub == k)
        def _(k=k):
            _start_rows_unrolled(h_ref, src_ref, nxt * tm + s * step_rows, xbuf_ref, s * step_rows,
                                 step_rows, sem)
            e = s // f_split
            rows = slice(0, k * EXPERT_SUB)
            x = xb_ref[rows, :]
            gate_act = _dot(x, wgb_ref[...])
            up = _dot(x, wub_ref[...])
            record = rec_ref[rows, :]
            lane = lax.broadcasted_iota(jnp.int32, record.shape, 1)
            gate_col = jnp.sum(jnp.where(lane == 2 + e, record, 0.0), axis=1, keepdims=True)
            hid = (_silu(gate_act) * up * gate_col).astype(BF16)
            y_ref[rows, :] += _dot(hid, wdb_ref[...])

    @pl.when((i == n_tiles - 1) & (s == n_steps - 1) & (n_sub > 0))
    def _():
        _wait_row_gather(h_ref, xbuf_ref, sem)


def _experts(h, src, tile_group, tile_nsub, w_gate, w_up, w_down, layer, tm, f_split=2):
    n_rows = src.shape[0]
    fb = D_EXPERT // f_split
    last = EXPERTS_PER_GROUP * f_split - 1

    def expert_of(i, s, tg, ns, sr):
        return tg[i] * EXPERTS_PER_GROUP + jnp.where(ns[i] > 0, s, last) // f_split

    def half_of(i, s, tg, ns, sr):
        return jnp.where(ns[i] > 0, s, last) % f_split

    grid_spec = pltpu.PrefetchScalarGridSpec(
        num_scalar_prefetch=3,
        grid=(n_rows // tm, EXPERTS_PER_GROUP * f_split),
        in_specs=[
            pl.BlockSpec((None, None, D_MODEL, fb), lambda i, s, *p: (layer, expert_of(i, s, *p), 0, half_of(i, s, *p))),
            pl.BlockSpec((None, None, D_MODEL, fb), lambda i, s, *p: (layer, expert_of(i, s, *p), 0, half_of(i, s, *p))),
            pl.BlockSpec((None, None, fb, D_MODEL), lambda i, s, *p: (layer, expert_of(i, s, *p), half_of(i, s, *p), 0)),
            pl.BlockSpec(memory_space=pl.ANY),
        ],
        out_specs=pl.BlockSpec((tm, D_MODEL), lambda i, s, *_: (i, 0)),
        scratch_shapes=[
            pltpu.VMEM((tm, D_MODEL + LANES), F32),
            pltpu.VMEM((tm, D_MODEL), BF16),
            pltpu.VMEM((tm, LANES), F32),
            pltpu.VMEM((D_MODEL, fb), BF16),
            pltpu.VMEM((D_MODEL, fb), BF16),
            pltpu.VMEM((fb, D_MODEL), BF16),
            pltpu.SemaphoreType.DMA(()),
        ],
    )
    return pl.pallas_call(
        functools.partial(_experts_kernel, f_split),
        grid_spec=grid_spec,
        out_shape=jax.ShapeDtypeStruct((n_rows, D_MODEL), F32),
        compiler_params=_cparams(2),
        name="experts",
    )(tile_group, tile_nsub, src, w_gate, w_up, w_down, h)


def _combine_kernel(final_norm, n_first, pos_ref, x_ref, gate_ref, fg_ref, ys_ref, *refs):
    o_refs, buf_ref, sem = refs[:-2], refs[-2], refs[-1]
    tm, d = x_ref.shape
    i = pl.program_id(0)
    slot = i % 2

    @pl.when(i == 0)
    def _():
        _start_row_gather(ys_ref, pos_ref, 0, buf_ref.at[0], sem.at[0])

    _wait_row_gather(ys_ref, buf_ref.at[slot], sem.at[slot])
    n_tiles = pl.num_programs(0)
    nxt = jnp.minimum(i + 1, n_tiles - 1)
    _start_rows_unrolled(ys_ref, pos_ref, nxt * tm, buf_ref.at[1 - slot], 0, tm, sem.at[1 - slot])

    nb = tm // ROW_BLOCK
    upd = buf_ref[slot].reshape(nb, ROW_BLOCK, d) * gate_ref[...][:, None, :]
    x = x_ref[...] + upd.reshape(tm, d)
    if final_norm:
        x = x * lax.rsqrt(jnp.mean(x * x, axis=-1, keepdims=True) + EPS) * fg_ref[...]
    if n_first is None:
        o_refs[0][...] = x
    else:
        @pl.when(pl.program_id(0) < n_first)
        def _():
            o_refs[0][...] = x

        @pl.when(pl.program_id(0) >= n_first)
        def _():
            o_refs[1][...] = x

    @pl.when(i == n_tiles - 1)
    def _():
        _wait_row_gather(ys_ref, buf_ref.at[1 - slot], sem.at[1 - slot])


def _combine(x, ys, pos, modb, final_g, final_norm, split_rows=None, tm=512):
    t_rows = x.shape[0]
    nb = tm // ROW_BLOCK
    if split_rows is None:
        n_first = None
        out_specs = pl.BlockSpec((tm, D_MODEL), lambda i, p: (i, 0))
        out_shape = jax.ShapeDtypeStruct((t_rows, D_MODEL), F32)
    else:
        assert split_rows % tm == 0 and (t_rows - split_rows) % tm == 0
        n_first = split_rows // tm
        out_specs = [pl.BlockSpec((tm, D_MODEL), lambda i, p: (jnp.minimum(i, n_first - 1), 0)),
                     pl.BlockSpec((tm, D_MODEL), lambda i, p: (jnp.maximum(i - n_first, 0), 0))]
        out_shape = [jax.ShapeDtypeStruct((split_rows, D_MODEL), F32),
                     jax.ShapeDtypeStruct((t_rows - split_rows, D_MODEL), F32)]
    grid_spec = pltpu.PrefetchScalarGridSpec(
        num_scalar_prefetch=1,
        grid=(t_rows // tm,),
        in_specs=[
            pl.BlockSpec((tm, D_MODEL), lambda i, p: (i, 0)),
            pl.BlockSpec((nb, D_MODEL), lambda i, p: (i, 5)),
            pl.BlockSpec((1, D_MODEL), lambda i, p: (0, 0)),
            pl.BlockSpec(memory_space=pl.ANY),
        ],
        out_specs=out_specs,
        scratch_shapes=[pltpu.VMEM((2, tm, D_MODEL), F32), pltpu.SemaphoreType.DMA((2,))],
    )
    return pl.pallas_call(
        functools.partial(_combine_kernel, final_norm, n_first),
        grid_spec=grid_spec,
        out_shape=out_shape,
        compiler_params=_cparams(1),
        name="combine",
    )(pos, x, modb, final_g.reshape(1, D_MODEL), ys)


def kernel(x_prompt, x_sample, state_ssm, state_conv, c_prompt, c_sample, norm1_g, norm2_g, w_ada, b_ada,
           w_in, gm_ln_g, gm_ln_b, gm_w_s, gm_b_s, conv_w, conv_b, dt_bias, a_log, d_skip, mnorm_g, w_out,
           w_router_group, b_router_group, w_router_expert, b_router_expert, w_gate, w_up, w_down,
           final_norm_g):
    n_p, len_p, _ = x_prompt.shape
    n_s, len_s, _ = x_sample.shape
    assert len_p % GM_TILE == 0 and len_s == ROW_BLOCK and (n_s * len_s) % GM_TILE == 0
    rows_p, rows_s = n_p * len_p, n_s * len_s
    t_rows = rows_p + rows_s
    n_seq = n_p + n_s
    blk_per_prompt = len_p // ROW_BLOCK
    assert t_rows % EXPERT_TILE == 0
    n_expert_tiles = t_rows // EXPERT_TILE + N_EGROUPS

    xs = (x_prompt.reshape(rows_p, D_MODEL), x_sample.reshape(rows_s, D_MODEL))

    c_all = jnp.concatenate([c_prompt, c_sample], axis=0)
    c_pad = jnp.pad(c_all, ((0, (-n_seq) % 8), (0, 0)))
    mod = _ada(c_pad, w_ada, b_ada)
    modb_all = jnp.concatenate(
        [jnp.repeat(mod[:, :n_p], blk_per_prompt, axis=1), mod[:, n_p:n_seq]], axis=1)

    half_w = gm_w_s[:, :, :ROW_BLOCK, :ROW_BLOCK]
    w_cat = jnp.stack([gm_w_s, jnp.tile(half_w, (1, 1, 2, 2))], axis=1)
    b_t = jnp.swapaxes(gm_b_s, 1, 2)
    b_cat = jnp.stack([b_t, jnp.tile(b_t[:, :ROW_BLOCK], (1, 2, 1))], axis=1)

    w_in_t = jnp.swapaxes(w_in, 1, 2)
    w_r = jnp.concatenate([w_router_group, w_router_expert], axis=2)
    w_r = jnp.pad(w_r, ((0, 0), (0, 0), (0, LANES - w_r.shape[2])))
    b_r = jnp.concatenate([b_router_group, b_router_expert], axis=1)
    b_r = jnp.pad(b_r, ((0, 0), (0, LANES - b_r.shape[1])))[:, None, :]

    ssm_p, conv_p, ssm_s, conv_s, v_rows = [], [], [], [], []
    for l in range(DEPTH):
        last = l == DEPTH - 1
        modb = modb_all[l]
        proj, dt_raw = _inproj(xs, norm1_g, modb, w_in_t, l, tn=512 if len(xs) == 2 else 1024)
        gm_out, vn = _gmlp(proj, gm_ln_g, gm_ln_b, w_cat, b_cat, l, rows_p // GM_TILE, rows_s)
        ssm_args = (conv_w, conv_b, dt_bias, a_log, d_skip, mnorm_g, l)
        ssm_out_p, hfin_p, cfin_p = _ssd(proj, dt_raw, None, None, *ssm_args, 0, n_p, len_p, SSD_PROMPT_CHUNK)
        ssm_out_s, hfin_s, cfin_s = _ssd(proj, dt_raw, state_ssm, state_conv, *ssm_args, rows_p, n_s, len_s, len_s)
        x, h2, route, counts = _outproj_route(gm_out, (ssm_out_p, ssm_out_s), w_out, xs, modb, norm2_g,
                                              w_r[l], b_r[l], l)
        pos, src, tile_group, tile_nsub = _dispatch_tables(route, counts, t_rows, EXPERT_TILE, n_expert_tiles)
        ys = _experts(h2, src, tile_group, tile_nsub, w_gate, w_up, w_down, l, EXPERT_TILE)
        out = _combine(x, ys, pos, modb, final_norm_g, last, rows_p if last else None)
        xs = tuple(out) if last else (out,)
        ssm_p.append(hfin_p)
        conv_p.append(cfin_p)
        ssm_s.append(hfin_s)
        conv_s.append(cfin_s)
        v_rows.append(vn.reshape(n_s, len_s, GM_WIDTH))

    y_prompt = xs[0].reshape(n_p, len_p, D_MODEL)
    y_sample = xs[1].reshape(n_s, len_s, D_MODEL)
    return (y_prompt, y_sample, jnp.stack(ssm_p), jnp.stack(conv_p), jnp.stack(ssm_s), jnp.stack(conv_s),
            jnp.stack(v_rows))
```

```python
import functools
import math

import jax
import jax.numpy as jnp
import numpy as np
from jax import lax
from jax.experimental import pallas as pl
from jax.experimental.pallas import tpu as pltpu

D_MODEL = 2048
DEPTH = 2
GM_WIDTH = 1024
GM_HEADS = 8
GM_HEAD_DIM = 128
GM_CHUNK = 128
M_WIDTH = 1024
M_HEAD_DIM = 64
M_HEADS = 16
M_GROUPS = 4
M_STATE = 128
M_CONV = 4
CONV_DIM = M_WIDTH + 2 * M_GROUPS * M_STATE
MAIN_DIM = 2 * GM_WIDTH + M_WIDTH + CONV_DIM
N_EGROUPS = 4
EXPERTS_PER_GROUP = 4
N_EXPERTS = 16
D_EXPERT = 512
N_MOD = 6
EPS = 1e-6

ROW_BLOCK = 64
LANES = 128
VMEM_LIMIT = 56 * 1024 * 1024

F32 = jnp.float32
BF16 = jnp.bfloat16


def _cparams(n_axes):
    return pltpu.CompilerParams(
        dimension_semantics=("arbitrary",) * n_axes, vmem_limit_bytes=VMEM_LIMIT)


def _silu(x):
    return x * jax.nn.sigmoid(x)


def _gelu_tanh(x):
    c = math.sqrt(2.0 / math.pi)
    return 0.5 * x * (1.0 + jnp.tanh(c * (x + 0.044715 * (x * x * x))))


def _dot(a, b):
    return jnp.dot(a, b, preferred_element_type=F32)


def _dot_nt(a, b, precision=None):
    return lax.dot_general(a, b, (((1,), (1,)), ((), ())), precision=precision,
                           preferred_element_type=F32)


def _pack_bf16_pairs(hi, lo):
    hi_bits = pltpu.bitcast(hi.astype(BF16).astype(F32), jnp.uint32)
    lo_bits = pltpu.bitcast(lo.astype(BF16).astype(F32), jnp.uint32)
    return hi_bits | (lo_bits >> 16)


def _unpack_bf16_pairs(words):
    hi = pltpu.bitcast(words & jnp.uint32(0xFFFF0000), F32)
    lo = pltpu.bitcast(words << 16, F32)
    return hi, lo


def _dot_split2(a, b):
    a0 = a.astype(BF16)
    a1 = (a - a0.astype(F32)).astype(BF16)
    b0 = b.astype(BF16)
    b1 = (b - b0.astype(F32)).astype(BF16)
    return _dot(a0, b0) + _dot(a0, b1) + _dot(a1, b0)


def _dot_tn(a, b):
    return lax.dot_general(a, b, (((0,), (0,)), ((), ())), preferred_element_type=F32)


def _ada_kernel(c_ref, w_ref, b_ref, o_ref):
    c = _silu(c_ref[...]).astype(BF16)
    o_ref[...] = _dot(c, w_ref[...].astype(BF16)) + b_ref[...]


def _ada(c_pad, w_ada, b_ada):
    rows = c_pad.shape[0]
    tn = 1024
    n_out = N_MOD * D_MODEL
    return pl.pallas_call(
        _ada_kernel,
        grid=(DEPTH, n_out // tn),
        in_specs=[
            pl.BlockSpec((rows, D_MODEL), lambda l, j: (0, 0)),
            pl.BlockSpec((None, D_MODEL, tn), lambda l, j: (l, 0, j)),
            pl.BlockSpec((None, 1, tn), lambda l, j: (l, 0, j)),
        ],
        out_specs=pl.BlockSpec((None, rows, tn), lambda l, j: (l, 0, j)),
        out_shape=jax.ShapeDtypeStruct((DEPTH, rows, n_out), F32),
        compiler_params=_cparams(2),
        name="ada_ln",
    )(c_pad, w_ada, b_ada.reshape(DEPTH, 1, n_out))


def _stream_specs(xs, block, col_of, n_j):
    if len(xs) == 1:
        return [pl.BlockSpec(block, lambda i, j: (i, col_of(j)))], None
    n_first = xs[0].shape[0] // block[0]

    def first(i, j):
        return jnp.minimum(i, n_first - 1), jnp.where(i < n_first, col_of(j), col_of(n_j - 1))

    def second(i, j):
        return jnp.maximum(i - n_first, 0), jnp.where(i < n_first, col_of(0), col_of(j))

    return [pl.BlockSpec(block, first), pl.BlockSpec(block, second)], n_first


def _stream_tile(x_refs, n_first, idx):
    if n_first is None:
        return x_refs[0][idx]
    return jnp.where(pl.program_id(0) < n_first, x_refs[0][idx], x_refs[1][idx])


def _inproj_kernel(n_first, *refs):
    n_x = 1 if n_first is None else 2
    x_refs = refs[:n_x]
    g_ref, shift_ref, scale_ref, w_ref, wdt_ref, proj_ref, dt_ref, h_ref = refs[n_x:]
    j = pl.program_id(1)
    n_blk = shift_ref.shape[0]

    @pl.when(j == 0)
    def _():
        g = g_ref[...]

        def body(r, carry):
            rows = pl.ds(pl.multiple_of(r * ROW_BLOCK, ROW_BLOCK), ROW_BLOCK)
            xf = _stream_tile(x_refs, n_first, (rows, slice(None)))
            y = xf * lax.rsqrt(jnp.mean(xf * xf, axis=-1, keepdims=True) + EPS) * g
            y = y * (1.0 + scale_ref[pl.ds(r, 1), :]) + shift_ref[pl.ds(r, 1), :]
            h_ref[rows, :] = y.astype(BF16)
            return carry

        lax.fori_loop(0, n_blk, body, 0)
        dt_ref[...] = _dot_nt(h_ref[...], wdt_ref[...].astype(BF16))

    proj_ref[...] = _dot_nt(h_ref[...], w_ref[...].astype(BF16)).astype(BF16)


def _inproj(xs, norm_g, modb, w_in_t, layer, tm=1024, tn=512):
    t_rows = sum(x.shape[0] for x in xs)
    nb = tm // ROW_BLOCK
    assert all(x.shape[0] % tm == 0 for x in xs)
    x_specs, n_first = _stream_specs(xs, (tm, D_MODEL), lambda j: 0, MAIN_DIM // tn)
    return pl.pallas_call(
        functools.partial(_inproj_kernel, n_first),
        grid=(t_rows // tm, MAIN_DIM // tn),
        in_specs=x_specs + [
            pl.BlockSpec((None, 1, D_MODEL), lambda i, j: (layer, 0, 0)),
            pl.BlockSpec((nb, D_MODEL), lambda i, j: (i, 0)),
            pl.BlockSpec((nb, D_MODEL), lambda i, j: (i, 1)),
            pl.BlockSpec((None, tn, D_MODEL), lambda i, j: (layer, j, 0)),
            pl.BlockSpec((None, M_HEADS, D_MODEL), lambda i, j: (layer, MAIN_DIM // M_HEADS, 0)),
        ],
        out_specs=[
            pl.BlockSpec((tm, tn), lambda i, j: (i, j)),
            pl.BlockSpec((tm, M_HEADS), lambda i, j: (i, 0)),
        ],
        out_shape=[
            jax.ShapeDtypeStruct((t_rows, MAIN_DIM), BF16),
            jax.ShapeDtypeStruct((t_rows, M_HEADS), F32),
        ],
        scratch_shapes=[pltpu.VMEM((tm, D_MODEL), BF16)],
        compiler_params=_cparams(2),
        name="in_proj",
    )(*xs, norm_g.reshape(DEPTH, 1, D_MODEL), modb, modb, w_in_t, w_in_t)


def _gmlp_kernel(n_prompt_tiles, u_ref, v_ref, lng_ref, lnb_ref, w_ref, b_ref, out_ref, vn_ref):
    i = pl.program_id(0)
    t_idx = lax.broadcasted_iota(jnp.int32, (GM_CHUNK, GM_CHUNK), 0)
    s_idx = lax.broadcasted_iota(jnp.int32, (GM_CHUNK, GM_CHUNK), 1)
    same_seq = (t_idx // ROW_BLOCK) == (s_idx // ROW_BLOCK)
    allowed = (s_idx <= t_idx) & (same_seq | (i < n_prompt_tiles))
    w_masked = [jnp.where(allowed, w_ref[h], 0.0).astype(BF16) for h in range(GM_HEADS)]
    bias = b_ref[...]
    for r0 in range(0, u_ref.shape[0], GM_CHUNK):
        rows = slice(r0, r0 + GM_CHUNK)
        u = _gelu_tanh(u_ref[rows, :].astype(F32))
        v = _gelu_tanh(v_ref[rows, :].astype(F32))
        mu = jnp.mean(v, axis=-1, keepdims=True)
        vc = v - mu
        vn = vc * lax.rsqrt(jnp.mean(vc * vc, axis=-1, keepdims=True) + EPS)
        vn = vn * lng_ref[...] + lnb_ref[...]
        vn_ref[rows, :] = vn
        vnb = vn.astype(BF16)
        for h in range(GM_HEADS):
            cols = slice(h * GM_HEAD_DIM, (h + 1) * GM_HEAD_DIM)
            mixed = _dot(w_masked[h], vnb[:, cols]) + bias[:, h:h + 1]
            out_ref[rows, cols] = (u[:, cols] * mixed).astype(BF16)


def _gmlp(proj, ln_g, ln_b, w_cat, b_cat, layer, n_prompt_tiles, n_sample_rows):
    t_rows = proj.shape[0]
    n_tiles = t_rows // GM_TILE
    kern = functools.partial(_gmlp_kernel, n_prompt_tiles)
    return pl.pallas_call(
        kern,
        grid=(n_tiles,),
        in_specs=[
            pl.BlockSpec((GM_TILE, GM_WIDTH), lambda i: (i, 0)),
            pl.BlockSpec((GM_TILE, GM_WIDTH), lambda i: (i, 1)),
            pl.BlockSpec((None, 1, GM_WIDTH), lambda i: (layer, 0, 0)),
            pl.BlockSpec((None, 1, GM_WIDTH), lambda i: (layer, 0, 0)),
            pl.BlockSpec((None, None, GM_HEADS, GM_CHUNK, GM_CHUNK),
                         lambda i: (layer, jnp.where(i < n_prompt_tiles, 0, 1), 0, 0, 0)),
            pl.BlockSpec((None, None, GM_CHUNK, GM_HEADS),
                         lambda i: (layer, jnp.where(i < n_prompt_tiles, 0, 1), 0, 0)),
        ],
        out_specs=[
            pl.BlockSpec((GM_TILE, GM_WIDTH), lambda i: (i, 0)),
            pl.BlockSpec((GM_TILE, GM_WIDTH), lambda i: (jnp.maximum(i - n_prompt_tiles, 0), 0)),
        ],
        out_shape=[
            jax.ShapeDtypeStruct((t_rows, GM_WIDTH), BF16),
            jax.ShapeDtypeStruct((n_sample_rows, GM_WIDTH), F32),
        ],
        compiler_params=_cparams(1),
        name="gmlp",
    )(proj, proj, ln_g.reshape(DEPTH, 1, GM_WIDTH), ln_b.reshape(DEPTH, 1, GM_WIDTH), w_cat, b_cat)


GM_TILE = 4 * GM_CHUNK
CONV_PAD = 8
SSD_PROMPT_CHUNK = 256


def _widen(v, onehot):
    p0 = v.astype(BF16)
    r1 = v - p0.astype(F32)
    p1 = r1.astype(BF16)
    p2 = (r1 - p1.astype(F32)).astype(BF16)
    return _dot(p0, onehot) + _dot(p1, onehot) + _dot(p2, onehot)


def _ssd_kernel(q, zero_init, *refs):
    z_ref, xa_ref, xb_ref, dt_ref = refs[:4]
    refs = refs[4:]
    h0_ref = c0_ref = None
    if not zero_init:
        h0_ref, c0_ref = refs[:2]
        refs = refs[2:]
    cw_ref, cb_ref, dtb_ref, alog_ref, dskip_ref, ng_ref = refs[:6]
    refs = refs[6:]
    out_ref, hfin_ref, cfin_ref, state_ref, win_ref, xs_ref, bc_ref, xdt_ref, xdd_ref, y_ref = refs
    c = pl.program_id(1)
    tail = M_CONV - 1
    lo = CONV_PAD - tail
    hp = M_HEADS * M_HEAD_DIM

    @pl.when(c == 0)
    def _():
        if zero_init:
            state_ref[...] = jnp.zeros_like(state_ref)
            win_ref[lo:CONV_PAD, :] = jnp.zeros((tail, CONV_DIM), F32)
        else:
            state_ref[...] = h0_ref[...].reshape(hp, M_STATE).T
            win_ref[lo:CONV_PAD, :] = c0_ref[...]

    half = CONV_DIM // 2
    win_ref[CONV_PAD:CONV_PAD + q, 0:half] = xa_ref[...].astype(F32)
    win_ref[CONV_PAD:CONV_PAD + q, half:CONV_DIM] = xb_ref[...].astype(F32)
    for r0 in range(0, q, ROW_BLOCK):
        conv = cb_ref[...]
        for k in range(M_CONV):
            conv = conv + win_ref[lo + k + r0:lo + k + r0 + ROW_BLOCK, :] * cw_ref[k:k + 1, :]
        act = _silu(conv)
        xs_ref[r0:r0 + ROW_BLOCK, :] = act[:, 0:M_WIDTH]
        bc_ref[r0:r0 + ROW_BLOCK, :] = act[:, M_WIDTH:CONV_DIM].astype(BF16)
    new_tail = win_ref[CONV_PAD + q - tail:CONV_PAD + q, :]
    cfin_ref[...] = new_tail
    win_ref[lo:CONV_PAD, :] = new_tail

    hi = lax.Precision.HIGHEST
    dt = jax.nn.softplus(dt_ref[...] + dtb_ref[...])
    a = -jnp.exp(alog_ref[...])
    r_idx = lax.broadcasted_iota(jnp.int32, (q, q), 0)
    c_idx = lax.broadcasted_iota(jnp.int32, (q, q), 1)
    causal = c_idx <= r_idx
    a_cs = jnp.dot(causal.astype(F32), dt * a, precision=hi, preferred_element_type=F32)
    eye = (lax.broadcasted_iota(jnp.int32, (M_HEADS, M_HEADS), 0)
           == lax.broadcasted_iota(jnp.int32, (M_HEADS, M_HEADS), 1)).astype(F32)
    a_cs_t = _dot_nt(eye, a_cs, precision=hi)
    widen = (lax.broadcasted_iota(jnp.int32, (M_HEADS, hp), 1) // M_HEAD_DIM
             == lax.broadcasted_iota(jnp.int32, (M_HEADS, hp), 0)).astype(BF16)
    dt_w = _widen(dt, widen)
    acs_w = _widen(a_cs, widen)
    dskip_w = _widen(jnp.broadcast_to(dskip_ref[...], (8, M_HEADS)), widen)[0:1, :]
    a_end_w = acs_w[q - 1:q, :]
    x_dt = xs_ref[...] * dt_w
    xdt_ref[...] = x_dt.astype(BF16)
    xdd_ref[...] = (x_dt * jnp.exp(a_end_w - acs_w)).astype(BF16)
    decay_blk_w = jnp.exp(a_end_w)

    heads_per_group = M_HEADS // M_GROUPS
    gw = M_WIDTH // M_GROUPS
    pair_lane = lax.broadcasted_iota(jnp.int32, (q, 2 * M_HEAD_DIM), 1)
    for g in range(M_GROUPS):
        gcols = slice(g * gw, (g + 1) * gw)
        b_g = bc_ref[:, g * M_STATE:(g + 1) * M_STATE]
        c_g = bc_ref[:, (M_GROUPS + g) * M_STATE:(M_GROUPS + g + 1) * M_STATE]
        cb = _dot_nt(c_g, b_g)
        st = state_ref[:, gcols]
        y_off = _dot(c_g, st.astype(BF16))
        state_ref[:, gcols] = st * decay_blk_w[:, gcols] + _dot_tn(b_g, xdd_ref[:, gcols])
        for pr in range(heads_per_group // 2):
            h0 = g * heads_per_group + 2 * pr
            pcols = slice(h0 * M_HEAD_DIM, (h0 + 2) * M_HEAD_DIM)
            x_pair = xdt_ref[:, pcols]
            ys = []
            for h in (h0, h0 + 1):
                seg = a_cs[:, h:h + 1] - a_cs_t[h:h + 1, :]
                scores = cb * jnp.exp(jnp.where(causal, seg, -jnp.inf))
                ys.append(_dot(scores.astype(BF16), x_pair))
            y_ref[:, pcols] = jnp.where(pair_lane < M_HEAD_DIM, ys[0], ys[1])
        y_ref[:, gcols] = (y_ref[:, gcols] + y_off * jnp.exp(acs_w[:, gcols])
                           + dskip_w[:, gcols] * xs_ref[:, gcols])

    @pl.when(c == pl.num_programs(1) - 1)
    def _():
        hfin_ref[...] = state_ref[...].T.reshape(M_HEADS, M_HEAD_DIM, M_STATE)

    gy = y_ref[...] * _silu(z_ref[...].astype(F32))
    ng = ng_ref[...]
    for g in range(M_GROUPS):
        cols = slice(g * gw, (g + 1) * gw)
        s = gy[:, cols]
        s = s * lax.rsqrt(jnp.mean(s * s, axis=-1, keepdims=True) + EPS)
        out_ref[:, cols] = (s * ng[:, cols]).astype(BF16)


def _ssd(proj, dt_raw, h0, c0, conv_w, conv_b, dt_bias, a_log, d_skip, mnorm_g, layer,
         row_start, n_seq, seq_len, q):
    n_c = seq_len // q
    blk0 = row_start // q
    half = CONV_DIM // 2
    z_blk = (2 * GM_WIDTH) // M_WIDTH
    xa_blk = (2 * GM_WIDTH + M_WIDTH) // half
    zero_init = h0 is None

    def rows(col):
        return lambda s, c: (blk0 + s * n_c + c, col)

    def per_layer(*shape):
        return pl.BlockSpec((None,) + shape, lambda s, c: (layer,) + (0,) * len(shape))

    in_specs = [
        pl.BlockSpec((q, M_WIDTH), rows(z_blk)),
        pl.BlockSpec((q, half), rows(xa_blk)),
        pl.BlockSpec((q, half), rows(xa_blk + 1)),
        pl.BlockSpec((q, M_HEADS), rows(0)),
    ]
    args = [proj, proj, proj, dt_raw]
    if not zero_init:
        in_specs += [
            pl.BlockSpec((None, None, M_HEADS, M_HEAD_DIM, M_STATE), lambda s, c: (layer, s, 0, 0, 0)),
            pl.BlockSpec((None, None, M_CONV - 1, CONV_DIM), lambda s, c: (layer, s, 0, 0)),
        ]
        args += [h0, c0]
    in_specs += [per_layer(M_CONV, CONV_DIM), per_layer(1, CONV_DIM), per_layer(1, M_HEADS),
                 per_layer(1, M_HEADS), per_layer(1, M_HEADS), per_layer(1, M_WIDTH)]
    args += [conv_w, conv_b.reshape(DEPTH, 1, CONV_DIM), dt_bias.reshape(DEPTH, 1, M_HEADS),
             a_log.reshape(DEPTH, 1, M_HEADS), d_skip.reshape(DEPTH, 1, M_HEADS),
             mnorm_g.reshape(DEPTH, 1, M_WIDTH)]
    return pl.pallas_call(
        functools.partial(_ssd_kernel, q, zero_init),
        grid=(n_seq, n_c),
        in_specs=in_specs,
        out_specs=[
            pl.BlockSpec((q, M_WIDTH), lambda s, c: (s * n_c + c, 0)),
            pl.BlockSpec((None, M_HEADS, M_HEAD_DIM, M_STATE), lambda s, c: (s, 0, 0, 0)),
            pl.BlockSpec((None, M_CONV - 1, CONV_DIM), lambda s, c: (s, 0, 0)),
        ],
        out_shape=[
            jax.ShapeDtypeStruct((n_seq * seq_len, M_WIDTH), BF16),
            jax.ShapeDtypeStruct((n_seq, M_HEADS, M_HEAD_DIM, M_STATE), F32),
            jax.ShapeDtypeStruct((n_seq, M_CONV - 1, CONV_DIM), F32),
        ],
        scratch_shapes=[
            pltpu.VMEM((M_STATE, M_WIDTH), F32),
            pltpu.VMEM((CONV_PAD + q, CONV_DIM), F32),
            pltpu.VMEM((q, M_WIDTH), F32),
            pltpu.VMEM((q, 2 * M_GROUPS * M_STATE), BF16),
            pltpu.VMEM((q, M_WIDTH), BF16),
            pltpu.VMEM((q, M_WIDTH), BF16),
            pltpu.VMEM((q, M_WIDTH), F32),
        ],
        compiler_params=_cparams(2),
        name="conv_ssd",
    )(*args)


def _outproj_kernel(n_first_ssm, n_first, gm_ref, wa_ref, wb_ref, gate_ref, *refs):
    n_ssm = 1 if n_first_ssm is None else 2
    ssm_refs, x_refs, o_ref = refs[:n_ssm], refs[n_ssm:-1], refs[-1]
    ssm = _stream_tile(ssm_refs, n_first_ssm, Ellipsis)
    acc = _dot(gm_ref[...], wa_ref[...].astype(BF16)) + _dot(ssm, wb_ref[...].astype(BF16))
    tm, tn = acc.shape
    nb = tm // ROW_BLOCK
    upd = acc.reshape(nb, ROW_BLOCK, tn) * gate_ref[...][:, None, :]
    o_ref[...] = _stream_tile(x_refs, n_first, Ellipsis) + upd.reshape(tm, tn)


def _outproj(gm, ssms, w_out, xs, modb, layer, tm=1024, tn=512):
    t_rows = gm.shape[0]
    nb = tm // ROW_BLOCK
    gate_blk = 2 * D_MODEL // tn
    assert all(x.shape[0] % tm == 0 for x in xs)
    assert all(x.shape[0] % tm == 0 for x in ssms)
    x_specs, n_first = _stream_specs(xs, (tm, tn), lambda j: j, D_MODEL // tn)
    ssm_specs, n_first_ssm = _stream_specs(ssms, (tm, M_WIDTH), lambda j: 0, D_MODEL // tn)
    return pl.pallas_call(
        functools.partial(_outproj_kernel, n_first_ssm, n_first),
        grid=(t_rows // tm, D_MODEL // tn),
        in_specs=[
            pl.BlockSpec((tm, GM_WIDTH), lambda i, j: (i, 0)),
            pl.BlockSpec((None, GM_WIDTH, tn), lambda i, j: (layer, 0, j)),
            pl.BlockSpec((None, M_WIDTH, tn), lambda i, j: (layer, 1, j)),
            pl.BlockSpec((nb, tn), lambda i, j: (i, gate_blk + j)),
        ] + ssm_specs + x_specs,
        out_specs=pl.BlockSpec((tm, tn), lambda i, j: (i, j)),
        out_shape=jax.ShapeDtypeStruct((t_rows, D_MODEL), F32),
        compiler_params=_cparams(2),
        name="out_proj",
    )(gm, w_out, w_out, modb, *ssms, *xs)


HALF_D = D_MODEL // 2
PACKED_ROW = HALF_D + LANES
ROUTE_ROWS = 8
EXPERT_TILE = 1024
EXPERT_SUB = 256


def _row_argmax_first(rows):
    m = rows[0]
    for r in rows[1:]:
        m = jnp.maximum(m, r)
    idx = jnp.full(m.shape, len(rows), jnp.int32)
    for k in reversed(range(len(rows))):
        idx = jnp.where(rows[k] == m, k, idx)
    return m, idx


def _router_kernel(x_ref, g_ref, shift_ref, scale_ref, wr_ref, br_ref, h_ref, route_ref, count_ref,
                   hf_ref, run_ref):
    n_blk = shift_ref.shape[0]
    g = g_ref[...]

    @pl.when(pl.program_id(0) == 0)
    def _():
        run_ref[...] = jnp.zeros_like(run_ref)

    def body(r, carry):
        rows = pl.ds(pl.multiple_of(r * ROW_BLOCK, ROW_BLOCK), ROW_BLOCK)
        xf = x_ref[rows, :]
        y = xf * lax.rsqrt(jnp.mean(xf * xf, axis=-1, keepdims=True) + EPS) * g
        y = y * (1.0 + scale_ref[pl.ds(r, 1), :]) + shift_ref[pl.ds(r, 1), :]
        hf_ref[rows, :] = y
        h_ref[rows, 0:HALF_D] = _pack_bf16_pairs(y[:, 0:HALF_D], y[:, HALF_D:D_MODEL])
        return carry

    lax.fori_loop(0, n_blk, body, 0)
    logits = (_dot_split2(hf_ref[...], wr_ref[...]) + br_ref[...]).T
    lg = [logits[k:k + 1, :] for k in range(N_EGROUPS)]
    g_max, g_idx = _row_argmax_first(lg)
    g_den = lg[0] * 0.0
    for r in lg:
        g_den = g_den + jnp.exp(r - g_max)
    g_top = 1.0 / g_den
    e_in = []
    for k in range(EXPERTS_PER_GROUP):
        sel = lg[0] * 0.0
        for gi in range(N_EGROUPS):
            row = N_EGROUPS + gi * EXPERTS_PER_GROUP + k
            sel = jnp.where(g_idx == gi, logits[row:row + 1, :], sel)
        e_in.append(sel)
    e_max, i1 = _row_argmax_first(e_in)
    e_exp = [jnp.exp(r - e_max) for r in e_in]
    e_den = e_exp[0]
    for r in e_exp[1:]:
        e_den = e_den + r
    p = [r / e_den for r in e_exp]
    p1 = p[0] * 0.0
    for k in range(EXPERTS_PER_GROUP):
        p1 = jnp.where(i1 == k, p[k], p1)
    rest = [jnp.where(i1 == k, -1.0, p[k]) for k in range(EXPERTS_PER_GROUP)]
    p2, i2 = _row_argmax_first(rest)
    norm = p1 + p2
    w1 = p1 / norm * g_top
    w2 = p2 / norm * g_top
    tm = logits.shape[1]
    row = lax.broadcasted_iota(jnp.int32, (ROUTE_ROWS, tm), 0)
    onehot = (row == g_idx).astype(F32)
    before = (lax.broadcasted_iota(jnp.int32, (tm, tm), 0)
              < lax.broadcasted_iota(jnp.int32, (tm, tm), 1)).astype(BF16)
    earlier = _dot(onehot.astype(BF16), before)
    rank = jnp.sum(onehot * (earlier + run_ref[:, 0:1]), axis=0, keepdims=True)
    run_ref[...] += jnp.sum(onehot, axis=1, keepdims=True)
    count_ref[...] = run_ref[...]

    route_ref[0:1, :] = g_idx.astype(F32)
    route_ref[1:2, :] = rank
    for k in range(EXPERTS_PER_GROUP):
        route_ref[2 + k:3 + k, :] = jnp.where(i1 == k, w1, 0.0) + jnp.where(i2 == k, w2, 0.0)
    route_ref[2 + EXPERTS_PER_GROUP:ROUTE_ROWS, :] = jnp.zeros((ROUTE_ROWS - 2 - EXPERTS_PER_GROUP, tm), F32)
    place = (lax.broadcasted_iota(jnp.int32, (ROUTE_ROWS, LANES), 0)
             == lax.broadcasted_iota(jnp.int32, (ROUTE_ROWS, LANES), 1)).astype(F32)
    record = lax.dot_general(route_ref[...], place, (((0,), (0,)), ((), ())),
                             precision=lax.Precision.HIGHEST, preferred_element_type=F32)
    h_ref[:, HALF_D:PACKED_ROW] = pltpu.bitcast(record, jnp.uint32)


def _router(x, norm_g, modb, w_r, b_r, layer, tm=512):
    t_rows = x.shape[0]
    nb = tm // ROW_BLOCK
    return pl.pallas_call(
        _router_kernel,
        grid=(t_rows // tm,),
        in_specs=[
            pl.BlockSpec((tm, D_MODEL), lambda i: (i, 0)),
            pl.BlockSpec((None, 1, D_MODEL), lambda i: (layer, 0, 0)),
            pl.BlockSpec((nb, D_MODEL), lambda i: (i, 3)),
            pl.BlockSpec((nb, D_MODEL), lambda i: (i, 4)),
            pl.BlockSpec((D_MODEL, LANES), lambda i: (0, 0)),
            pl.BlockSpec((1, LANES), lambda i: (0, 0)),
        ],
        out_specs=[
            pl.BlockSpec((tm, PACKED_ROW), lambda i: (i, 0)),
            pl.BlockSpec((ROUTE_ROWS, tm), lambda i: (0, i)),
            pl.BlockSpec((ROUTE_ROWS, LANES), lambda i: (0, 0)),
        ],
        out_shape=[
            jax.ShapeDtypeStruct((t_rows, PACKED_ROW), jnp.uint32),
            jax.ShapeDtypeStruct((ROUTE_ROWS, t_rows), F32),
            jax.ShapeDtypeStruct((ROUTE_ROWS, LANES), F32),
        ],
        scratch_shapes=[pltpu.VMEM((tm, D_MODEL), F32), pltpu.VMEM((ROUTE_ROWS, LANES), F32)],
        compiler_params=_cparams(1),
        name="router",
    )(x, norm_g.reshape(DEPTH, 1, D_MODEL), modb, modb, w_r, b_r)


def _dispatch_tables(route, counts, t_rows, tm, n_tiles):
    gid = route[0].astype(jnp.int32)
    rank = route[1].astype(jnp.int32)
    cnt = counts[:N_EGROUPS, 0].astype(jnp.int32)
    padded = (cnt + tm - 1) // tm * tm
    ends = jnp.cumsum(padded)
    starts = ends - padded
    pos = starts[gid] + rank
    src = jnp.zeros((n_tiles * tm,), jnp.int32).at[pos].set(jnp.arange(t_rows, dtype=jnp.int32))
    tile_start = jnp.arange(n_tiles, dtype=jnp.int32) * tm
    tile_group = jnp.sum((tile_start[:, None] >= ends[None, :]).astype(jnp.int32), axis=1)
    tile_group = jnp.minimum(tile_group, N_EGROUPS - 1)
    tile_rows = jnp.clip(cnt[tile_group] - (tile_start - starts[tile_group]), 0, tm)
    tile_nsub = (tile_rows + EXPERT_SUB - 1) // EXPERT_SUB
    return pos, src, tile_group, tile_nsub


def _start_row_gather(table_ref, idx_ref, idx_base, buf_ref, sem):
    def issue(r, carry):
        pltpu.make_async_copy(table_ref.at[pl.ds(idx_ref[idx_base + r], 1), :],
                              buf_ref.at[pl.ds(r, 1), :], sem).start()
        return carry

    lax.fori_loop(0, buf_ref.shape[0], issue, 0, unroll=8)


def _start_rows_unrolled(table_ref, idx_ref, idx_base, buf_ref, row0, n_rows, sem):
    for r in range(n_rows):
        pltpu.make_async_copy(table_ref.at[pl.ds(idx_ref[idx_base + r], 1), :],
                              buf_ref.at[pl.ds(row0 + r, 1), :], sem).start()


def _wait_row_gather(table_ref, buf_ref, sem):
    pltpu.make_async_copy(table_ref.at[pl.ds(0, buf_ref.shape[0]), :], buf_ref, sem).wait()


def _experts_kernel(f_split, tg_ref, ns_ref, src_ref, wg_ref, wu_ref, wd_ref, h_ref, y_ref,
                    xbuf_ref, xb_ref, rec_ref, acc_ref, wgb_ref, wub_ref, wdb_ref, sem):
    i = pl.program_id(0)
    s = pl.program_id(1)
    n_tiles = pl.num_programs(0)
    tm = xb_ref.shape[0]
    n_steps = pl.num_programs(1)
    n_sub = ns_ref[i]
    step_rows = tm // (EXPERTS_PER_GROUP * f_split)
    nxt = jnp.minimum(i + 1, n_tiles - 1)

    @pl.when((s == 0) & (i == 0))
    def _():
        _start_row_gather(h_ref, src_ref, 0, xbuf_ref, sem)

    @pl.when(s == 0)
    def _():
        acc_ref[...] = jnp.zeros_like(acc_ref)

        @pl.when((i == 0) | (ns_ref[jnp.maximum(i - 1, 0)] > 0))
        def _():
            _wait_row_gather(h_ref, xbuf_ref, sem)

        def keep(b, carry):
            rows = pl.ds(pl.multiple_of(b * EXPERT_SUB, EXPERT_SUB), EXPERT_SUB)
            hi, lo = _unpack_bf16_pairs(xbuf_ref[rows, 0:HALF_D])
            xb_ref[rows, 0:HALF_D] = hi.astype(BF16)
            xb_ref[rows, HALF_D:D_MODEL] = lo.astype(BF16)
            rec_ref[rows, :] = pltpu.bitcast(xbuf_ref[rows, HALF_D:PACKED_ROW], F32)
            return carry

        lax.fori_loop(0, n_sub, keep, 0)

    @pl.when(n_sub > 0)
    def _():
        wgb_ref[...] = wg_ref[...].astype(BF16)
        wub_ref[...] = wu_ref[...].astype(BF16)
        wdb_ref[...] = wd_ref[...].astype(BF16)

    for k in range(1, tm // EXPERT_SUB + 1):
        @pl.when(n_sub == k)
        def _(k=k):
            _start_rows_unrolled(h_ref, src_ref, nxt * tm + s * step_rows, xbuf_ref, s * step_rows,
                                 step_rows, sem)
            e = s // f_split
            rows = slice(0, k * EXPERT_SUB)
            x = xb_ref[rows, :]
            gate_act = _dot(x, wgb_ref[...])
            up = _dot(x, wub_ref[...])
            record = rec_ref[rows, :]
            lane = lax.broadcasted_iota(jnp.int32, record.shape, 1)
            gate_col = jnp.sum(jnp.where(lane == 2 + e, record, 0.0), axis=1, keepdims=True)
            hid = (_silu(gate_act) * up * gate_col).astype(BF16)
            acc_ref[rows, :] += _dot(hid, wdb_ref[...])

    @pl.when(s == n_steps - 1)
    def _():
        for r0 in range(0, tm, EXPERT_SUB):
            rows = slice(r0, r0 + EXPERT_SUB)
            y_ref[rows, :] = _pack_bf16_pairs(acc_ref[rows, 0:HALF_D], acc_ref[rows, HALF_D:D_MODEL])

    @pl.when((i == n_tiles - 1) & (s == n_steps - 1) & (n_sub > 0))
    def _():
        _wait_row_gather(h_ref, xbuf_ref, sem)


def _experts(h, src, tile_group, tile_nsub, w_gate, w_up, w_down, layer, tm, f_split=2):
    n_rows = src.shape[0]
    fb = D_EXPERT // f_split
    last = EXPERTS_PER_GROUP * f_split - 1

    def expert_of(i, s, tg, ns, sr):
        return tg[i] * EXPERTS_PER_GROUP + jnp.where(ns[i] > 0, s, last) // f_split

    def half_of(i, s, tg, ns, sr):
        return jnp.where(ns[i] > 0, s, last) % f_split

    grid_spec = pltpu.PrefetchScalarGridSpec(
        num_scalar_prefetch=3,
        grid=(n_rows // tm, EXPERTS_PER_GROUP * f_split),
        in_specs=[
            pl.BlockSpec((None, None, D_MODEL, fb), lambda i, s, *p: (layer, expert_of(i, s, *p), 0, half_of(i, s, *p))),
            pl.BlockSpec((None, None, D_MODEL, fb), lambda i, s, *p: (layer, expert_of(i, s, *p), 0, half_of(i, s, *p))),
            pl.BlockSpec((None, None, fb, D_MODEL), lambda i, s, *p: (layer, expert_of(i, s, *p), half_of(i, s, *p), 0)),
            pl.BlockSpec(memory_space=pl.ANY),
        ],
        out_specs=pl.BlockSpec((tm, HALF_D), lambda i, s, *_: (i, 0)),
        scratch_shapes=[
            pltpu.VMEM((tm, PACKED_ROW), jnp.uint32),
            pltpu.VMEM((tm, D_MODEL), BF16),
            pltpu.VMEM((tm, LANES), F32),
            pltpu.VMEM((tm, D_MODEL), F32),
            pltpu.VMEM((D_MODEL, fb), BF16),
            pltpu.VMEM((D_MODEL, fb), BF16),
            pltpu.VMEM((fb, D_MODEL), BF16),
            pltpu.SemaphoreType.DMA(()),
        ],
    )
    return pl.pallas_call(
        functools.partial(_experts_kernel, f_split),
        grid_spec=grid_spec,
        out_shape=jax.ShapeDtypeStruct((n_rows, HALF_D), jnp.uint32),
        compiler_params=_cparams(2),
        name="experts",
    )(tile_group, tile_nsub, src, w_gate, w_up, w_down, h)


def _combine_kernel(final_norm, n_first, pos_ref, x_ref, gate_ref, fg_ref, ys_ref, *refs):
    o_refs, buf_ref, sem = refs[:-2], refs[-2], refs[-1]
    tm, d = x_ref.shape
    i = pl.program_id(0)
    slot = i % 2

    @pl.when(i == 0)
    def _():
        _start_row_gather(ys_ref, pos_ref, 0, buf_ref.at[0], sem.at[0])

    _wait_row_gather(ys_ref, buf_ref.at[slot], sem.at[slot])
    n_tiles = pl.num_programs(0)
    nxt = jnp.minimum(i + 1, n_tiles - 1)
    _start_rows_unrolled(ys_ref, pos_ref, nxt * tm, buf_ref.at[1 - slot], 0, tm, sem.at[1 - slot])

    nb = tm // ROW_BLOCK
    hi, lo = _unpack_bf16_pairs(buf_ref[slot])
    y = jnp.concatenate([hi, lo], axis=1)
    upd = y.reshape(nb, ROW_BLOCK, d) * gate_ref[...][:, None, :]
    x = x_ref[...] + upd.reshape(tm, d)
    if final_norm:
        x = x * lax.rsqrt(jnp.mean(x * x, axis=-1, keepdims=True) + EPS) * fg_ref[...]
    if n_first is None:
        o_refs[0][...] = x
    else:
        @pl.when(pl.program_id(0) < n_first)
        def _():
            o_refs[0][...] = x

        @pl.when(pl.program_id(0) >= n_first)
        def _():
            o_refs[1][...] = x

    @pl.when(i == n_tiles - 1)
    def _():
        _wait_row_gather(ys_ref, buf_ref.at[1 - slot], sem.at[1 - slot])


def _combine(x, ys, pos, modb, final_g, final_norm, split_rows=None, tm=512):
    t_rows = x.shape[0]
    nb = tm // ROW_BLOCK
    if split_rows is None:
        n_first = None
        out_specs = pl.BlockSpec((tm, D_MODEL), lambda i, p: (i, 0))
        out_shape = jax.ShapeDtypeStruct((t_rows, D_MODEL), F32)
    else:
        assert split_rows % tm == 0 and (t_rows - split_rows) % tm == 0
        n_first = split_rows // tm
        out_specs = [pl.BlockSpec((tm, D_MODEL), lambda i, p: (jnp.minimum(i, n_first - 1), 0)),
                     pl.BlockSpec((tm, D_MODEL), lambda i, p: (jnp.maximum(i - n_first, 0), 0))]
        out_shape = [jax.ShapeDtypeStruct((split_rows, D_MODEL), F32),
                     jax.ShapeDtypeStruct((t_rows - split_rows, D_MODEL), F32)]
    grid_spec = pltpu.PrefetchScalarGridSpec(
        num_scalar_prefetch=1,
        grid=(t_rows // tm,),
        in_specs=[
            pl.BlockSpec((tm, D_MODEL), lambda i, p: (i, 0)),
            pl.BlockSpec((nb, D_MODEL), lambda i, p: (i, 5)),
            pl.BlockSpec((1, D_MODEL), lambda i, p: (0, 0)),
            pl.BlockSpec(memory_space=pl.ANY),
        ],
        out_specs=out_specs,
        scratch_shapes=[pltpu.VMEM((2, tm, HALF_D), jnp.uint32), pltpu.SemaphoreType.DMA((2,))],
    )
    return pl.pallas_call(
        functools.partial(_combine_kernel, final_norm, n_first),
        grid_spec=grid_spec,
        out_shape=out_shape,
        compiler_params=_cparams(1),
        name="combine",
    )(pos, x, modb, final_g.reshape(1, D_MODEL), ys)


def kernel(x_prompt, x_sample, state_ssm, state_conv, c_prompt, c_sample, norm1_g, norm2_g, w_ada, b_ada,
           w_in, gm_ln_g, gm_ln_b, gm_w_s, gm_b_s, conv_w, conv_b, dt_bias, a_log, d_skip, mnorm_g, w_out,
           w_router_group, b_router_group, w_router_expert, b_router_expert, w_gate, w_up, w_down,
           final_norm_g):
    n_p, len_p, _ = x_prompt.shape
    n_s, len_s, _ = x_sample.shape
    assert len_p % GM_TILE == 0 and len_s == ROW_BLOCK and (n_s * len_s) % GM_TILE == 0
    rows_p, rows_s = n_p * len_p, n_s * len_s
    t_rows = rows_p + rows_s
    n_seq = n_p + n_s
    blk_per_prompt = len_p // ROW_BLOCK
    assert t_rows % EXPERT_TILE == 0
    n_expert_tiles = t_rows // EXPERT_TILE + N_EGROUPS

    xs = (x_prompt.reshape(rows_p, D_MODEL), x_sample.reshape(rows_s, D_MODEL))

    c_all = jnp.concatenate([c_prompt, c_sample], axis=0)
    c_pad = jnp.pad(c_all, ((0, (-n_seq) % 8), (0, 0)))
    mod = _ada(c_pad, w_ada, b_ada)
    modb_all = jnp.concatenate(
        [jnp.repeat(mod[:, :n_p], blk_per_prompt, axis=1), mod[:, n_p:n_seq]], axis=1)

    half_w = gm_w_s[:, :, :ROW_BLOCK, :ROW_BLOCK]
    w_cat = jnp.stack([gm_w_s, jnp.tile(half_w, (1, 1, 2, 2))], axis=1)
    b_t = jnp.swapaxes(gm_b_s, 1, 2)
    b_cat = jnp.stack([b_t, jnp.tile(b_t[:, :ROW_BLOCK], (1, 2, 1))], axis=1)

    w_in_t = jnp.swapaxes(w_in, 1, 2)
    w_r = jnp.concatenate([w_router_group, w_router_expert], axis=2)
    w_r = jnp.pad(w_r, ((0, 0), (0, 0), (0, LANES - w_r.shape[2])))
    b_r = jnp.concatenate([b_router_group, b_router_expert], axis=1)
    b_r = jnp.pad(b_r, ((0, 0), (0, LANES - b_r.shape[1])))[:, None, :]

    ssm_p, conv_p, ssm_s, conv_s, v_rows = [], [], [], [], []
    for l in range(DEPTH):
        last = l == DEPTH - 1
        modb = modb_all[l]
        proj, dt_raw = _inproj(xs, norm1_g, modb, w_in_t, l, tn=512 if len(xs) == 2 else 1024)
        gm_out, vn = _gmlp(proj, gm_ln_g, gm_ln_b, w_cat, b_cat, l, rows_p // GM_TILE, rows_s)
        ssm_args = (conv_w, conv_b, dt_bias, a_log, d_skip, mnorm_g, l)
        ssm_out_p, hfin_p, cfin_p = _ssd(proj, dt_raw, None, None, *ssm_args, 0, n_p, len_p, SSD_PROMPT_CHUNK)
        ssm_out_s, hfin_s, cfin_s = _ssd(proj, dt_raw, state_ssm, state_conv, *ssm_args, rows_p, n_s, len_s, len_s)
        x = _outproj(gm_out, (ssm_out_p, ssm_out_s), w_out, xs, modb, l)
        h2, route, counts = _router(x, norm2_g, modb, w_r[l], b_r[l], l)
        pos, src, tile_group, tile_nsub = _dispatch_tables(route, counts, t_rows, EXPERT_TILE, n_expert_tiles)
        ys = _experts(h2, src, tile_group, tile_nsub, w_gate, w_up, w_down, l, EXPERT_TILE)
        out = _combine(x, ys, pos, modb, final_norm_g, last, rows_p if last else None)
        xs = tuple(out) if last else (out,)
        ssm_p.append(hfin_p)
        conv_p.append(cfin_p)
        ssm_s.append(hfin_s)
        conv_s.append(cfin_s)
        v_rows.append(vn.reshape(n_s, len_s, GM_WIDTH))

    y_prompt = xs[0].reshape(n_p, len_p, D_MODEL)
    y_sample = xs[1].reshape(n_s, len_s, D_MODEL)
    return (y_prompt, y_sample, jnp.stack(ssm_p), jnp.stack(conv_p), jnp.stack(ssm_s), jnp.stack(conv_s),
            jnp.stack(v_rows))
```

```python
import functools
import math

import jax
import jax.numpy as jnp
from jax import lax
from jax.experimental import pallas as pl
from jax.experimental.pallas import tpu as pltpu

D_MODEL = 2048
DEPTH = 2
GM_WIDTH = 1024
GM_HEADS = 8
GM_HEAD_DIM = 128
GM_CHUNK = 128
M_WIDTH = 1024
M_HEAD_DIM = 64
M_HEADS = 16
M_GROUPS = 4
M_STATE = 128
M_CONV = 4
CONV_DIM = M_WIDTH + 2 * M_GROUPS * M_STATE
MAIN_DIM = 2 * GM_WIDTH + M_WIDTH + CONV_DIM
N_EGROUPS = 4
EXPERTS_PER_GROUP = 4
N_EXPERTS = 16
D_EXPERT = 512
N_MOD = 6
EPS = 1e-6

ROW_BLOCK = 64
LANES = 128
VMEM_LIMIT = 56 * 1024 * 1024

F32 = jnp.float32
BF16 = jnp.bfloat16


def _cparams(n_axes):
    return pltpu.CompilerParams(
        dimension_semantics=("arbitrary",) * n_axes, vmem_limit_bytes=VMEM_LIMIT)


def _silu(x):
    return x * jax.nn.sigmoid(x)


def _gelu_tanh(x):
    c = math.sqrt(2.0 / math.pi)
    return 0.5 * x * (1.0 + jnp.tanh(c * (x + 0.044715 * (x * x * x))))


def _dot(a, b):
    return jnp.dot(a, b, preferred_element_type=F32)


def _dot_nt(a, b, precision=None):
    return lax.dot_general(a, b, (((1,), (1,)), ((), ())), precision=precision,
                           preferred_element_type=F32)


def _dot_split2(a, b):
    a0 = a.astype(BF16)
    a1 = (a - a0.astype(F32)).astype(BF16)
    b0 = b.astype(BF16)
    b1 = (b - b0.astype(F32)).astype(BF16)
    return _dot(a0, b0) + _dot(a0, b1) + _dot(a1, b0)


def _dot_tn(a, b):
    return lax.dot_general(a, b, (((0,), (0,)), ((), ())), preferred_element_type=F32)


def _ada_kernel(c_ref, w_ref, b_ref, o_ref):
    c = _silu(c_ref[...]).astype(BF16)
    o_ref[...] = _dot(c, w_ref[...].astype(BF16)) + b_ref[...]


def _ada(c_pad, w_ada, b_ada):
    rows = c_pad.shape[0]
    tn = 1024
    n_out = N_MOD * D_MODEL
    return pl.pallas_call(
        _ada_kernel,
        grid=(DEPTH, n_out // tn),
        in_specs=[
            pl.BlockSpec((rows, D_MODEL), lambda l, j: (0, 0)),
            pl.BlockSpec((None, D_MODEL, tn), lambda l, j: (l, 0, j)),
            pl.BlockSpec((None, 1, tn), lambda l, j: (l, 0, j)),
        ],
        out_specs=pl.BlockSpec((None, rows, tn), lambda l, j: (l, 0, j)),
        out_shape=jax.ShapeDtypeStruct((DEPTH, rows, n_out), F32),
        compiler_params=_cparams(2),
        name="ada_ln",
    )(c_pad, w_ada, b_ada.reshape(DEPTH, 1, n_out))


def _stream_specs(xs, block, col_of, n_j):
    if len(xs) == 1:
        return [pl.BlockSpec(block, lambda i, j: (i, col_of(j)))], None
    n_first = xs[0].shape[0] // block[0]

    def first(i, j):
        return jnp.minimum(i, n_first - 1), jnp.where(i < n_first, col_of(j), col_of(n_j - 1))

    def second(i, j):
        return jnp.maximum(i - n_first, 0), jnp.where(i < n_first, col_of(0), col_of(j))

    return [pl.BlockSpec(block, first), pl.BlockSpec(block, second)], n_first


def _stream_tile(x_refs, n_first, idx):
    if n_first is None:
        return x_refs[0][idx]
    return jnp.where(pl.program_id(0) < n_first, x_refs[0][idx], x_refs[1][idx])


def _inproj_kernel(n_first, *refs):
    n_x = 1 if n_first is None else 2
    x_refs = refs[:n_x]
    g_ref, shift_ref, scale_ref, w_ref, wdt_ref, proj_ref, dt_ref, h_ref = refs[n_x:]
    j = pl.program_id(1)
    n_blk = shift_ref.shape[0]

    @pl.when(j == 0)
    def _():
        g = g_ref[...]

        def body(r, carry):
            rows = pl.ds(pl.multiple_of(r * ROW_BLOCK, ROW_BLOCK), ROW_BLOCK)
            xf = _stream_tile(x_refs, n_first, (rows, slice(None)))
            y = xf * lax.rsqrt(jnp.mean(xf * xf, axis=-1, keepdims=True) + EPS) * g
            y = y * (1.0 + scale_ref[pl.ds(r, 1), :]) + shift_ref[pl.ds(r, 1), :]
            h_ref[rows, :] = y.astype(BF16)
            return carry

        lax.fori_loop(0, n_blk, body, 0)
        dt_ref[...] = _dot_nt(h_ref[...], wdt_ref[...].astype(BF16))

    proj_ref[...] = _dot_nt(h_ref[...], w_ref[...].astype(BF16)).astype(BF16)


def _inproj(xs, norm_g, modb, w_in_t, layer, tm=1024, tn=512):
    t_rows = sum(x.shape[0] for x in xs)
    nb = tm // ROW_BLOCK
    assert all(x.shape[0] % tm == 0 for x in xs)
    x_specs, n_first = _stream_specs(xs, (tm, D_MODEL), lambda j: 0, MAIN_DIM // tn)
    return pl.pallas_call(
        functools.partial(_inproj_kernel, n_first),
        grid=(t_rows // tm, MAIN_DIM // tn),
        in_specs=x_specs + [
            pl.BlockSpec((None, 1, D_MODEL), lambda i, j: (layer, 0, 0)),
            pl.BlockSpec((nb, D_MODEL), lambda i, j: (i, 0)),
            pl.BlockSpec((nb, D_MODEL), lambda i, j: (i, 1)),
            pl.BlockSpec((None, tn, D_MODEL), lambda i, j: (layer, j, 0)),
            pl.BlockSpec((None, M_HEADS, D_MODEL), lambda i, j: (layer, MAIN_DIM // M_HEADS, 0)),
        ],
        out_specs=[
            pl.BlockSpec((tm, tn), lambda i, j: (i, j)),
            pl.BlockSpec((tm, M_HEADS), lambda i, j: (i, 0)),
        ],
        out_shape=[
            jax.ShapeDtypeStruct((t_rows, MAIN_DIM), BF16),
            jax.ShapeDtypeStruct((t_rows, M_HEADS), F32),
        ],
        scratch_shapes=[pltpu.VMEM((tm, D_MODEL), BF16)],
        compiler_params=_cparams(2),
        name="in_proj",
    )(*xs, norm_g.reshape(DEPTH, 1, D_MODEL), modb, modb, w_in_t, w_in_t)


def _gmlp_kernel(n_prompt_tiles, u_ref, v_ref, lng_ref, lnb_ref, w_ref, b_ref, out_ref, vn_ref):
    i = pl.program_id(0)
    t_idx = lax.broadcasted_iota(jnp.int32, (GM_CHUNK, GM_CHUNK), 0)
    s_idx = lax.broadcasted_iota(jnp.int32, (GM_CHUNK, GM_CHUNK), 1)
    same_seq = (t_idx // ROW_BLOCK) == (s_idx // ROW_BLOCK)
    allowed = (s_idx <= t_idx) & (same_seq | (i < n_prompt_tiles))
    w_masked = [jnp.where(allowed, w_ref[h], 0.0).astype(BF16) for h in range(GM_HEADS)]
    bias = b_ref[...]
    for r0 in range(0, u_ref.shape[0], GM_CHUNK):
        rows = slice(r0, r0 + GM_CHUNK)
        u = _gelu_tanh(u_ref[rows, :].astype(F32))
        v = _gelu_tanh(v_ref[rows, :].astype(F32))
        mu = jnp.mean(v, axis=-1, keepdims=True)
        vc = v - mu
        vn = vc * lax.rsqrt(jnp.mean(vc * vc, axis=-1, keepdims=True) + EPS)
        vn = vn * lng_ref[...] + lnb_ref[...]
        vn_ref[rows, :] = vn
        vnb = vn.astype(BF16)
        for h in range(GM_HEADS):
            cols = slice(h * GM_HEAD_DIM, (h + 1) * GM_HEAD_DIM)
            mixed = _dot(w_masked[h], vnb[:, cols]) + bias[:, h:h + 1]
            out_ref[rows, cols] = (u[:, cols] * mixed).astype(BF16)


def _gmlp(proj, ln_g, ln_b, w_cat, b_cat, layer, n_prompt_tiles, n_sample_rows):
    t_rows = proj.shape[0]
    n_tiles = t_rows // GM_TILE
    kern = functools.partial(_gmlp_kernel, n_prompt_tiles)
    return pl.pallas_call(
        kern,
        grid=(n_tiles,),
        in_specs=[
            pl.BlockSpec((GM_TILE, GM_WIDTH), lambda i: (i, 0)),
            pl.BlockSpec((GM_TILE, GM_WIDTH), lambda i: (i, 1)),
            pl.BlockSpec((None, 1, GM_WIDTH), lambda i: (layer, 0, 0)),
            pl.BlockSpec((None, 1, GM_WIDTH), lambda i: (layer, 0, 0)),
            pl.BlockSpec((None, None, GM_HEADS, GM_CHUNK, GM_CHUNK),
                         lambda i: (layer, jnp.where(i < n_prompt_tiles, 0, 1), 0, 0, 0)),
            pl.BlockSpec((None, None, GM_CHUNK, GM_HEADS),
                         lambda i: (layer, jnp.where(i < n_prompt_tiles, 0, 1), 0, 0)),
        ],
        out_specs=[
            pl.BlockSpec((GM_TILE, GM_WIDTH), lambda i: (i, 0)),
            pl.BlockSpec((GM_TILE, GM_WIDTH), lambda i: (jnp.maximum(i - n_prompt_tiles, 0), 0)),
        ],
        out_shape=[
            jax.ShapeDtypeStruct((t_rows, GM_WIDTH), BF16),
            jax.ShapeDtypeStruct((n_sample_rows, GM_WIDTH), F32),
        ],
        compiler_params=_cparams(1),
        name="gmlp",
    )(proj, proj, ln_g.reshape(DEPTH, 1, GM_WIDTH), ln_b.reshape(DEPTH, 1, GM_WIDTH), w_cat, b_cat)


GM_TILE = 4 * GM_CHUNK
CONV_PAD = 8
SSD_PROMPT_CHUNK = 256


def _widen(v, onehot):
    p0 = v.astype(BF16)
    r1 = v - p0.astype(F32)
    p1 = r1.astype(BF16)
    p2 = (r1 - p1.astype(F32)).astype(BF16)
    return _dot(p0, onehot) + _dot(p1, onehot) + _dot(p2, onehot)


def _ssd_kernel(q, zero_init, *refs):
    z_ref, xa_ref, xb_ref, dt_ref = refs[:4]
    refs = refs[4:]
    h0_ref = c0_ref = None
    if not zero_init:
        h0_ref, c0_ref = refs[:2]
        refs = refs[2:]
    cw_ref, cb_ref, dtb_ref, alog_ref, dskip_ref, ng_ref = refs[:6]
    refs = refs[6:]
    out_ref, hfin_ref, cfin_ref, state_ref, win_ref, xs_ref, bc_ref, xdt_ref, xdd_ref, y_ref = refs
    c = pl.program_id(1)
    tail = M_CONV - 1
    lo = CONV_PAD - tail
    hp = M_HEADS * M_HEAD_DIM

    @pl.when(c == 0)
    def _():
        if zero_init:
            state_ref[...] = jnp.zeros_like(state_ref)
            win_ref[lo:CONV_PAD, :] = jnp.zeros((tail, CONV_DIM), F32)
        else:
            state_ref[...] = h0_ref[...].reshape(hp, M_STATE).T
            win_ref[lo:CONV_PAD, :] = c0_ref[...]

    half = CONV_DIM // 2
    win_ref[CONV_PAD:CONV_PAD + q, 0:half] = xa_ref[...].astype(F32)
    win_ref[CONV_PAD:CONV_PAD + q, half:CONV_DIM] = xb_ref[...].astype(F32)
    for r0 in range(0, q, ROW_BLOCK):
        conv = cb_ref[...]
        for k in range(M_CONV):
            conv = conv + win_ref[lo + k + r0:lo + k + r0 + ROW_BLOCK, :] * cw_ref[k:k + 1, :]
        act = _silu(conv)
        xs_ref[r0:r0 + ROW_BLOCK, :] = act[:, 0:M_WIDTH]
        bc_ref[r0:r0 + ROW_BLOCK, :] = act[:, M_WIDTH:CONV_DIM].astype(BF16)
    new_tail = win_ref[CONV_PAD + q - tail:CONV_PAD + q, :]
    cfin_ref[...] = new_tail
    win_ref[lo:CONV_PAD, :] = new_tail

    hi = lax.Precision.HIGHEST
    dt = jax.nn.softplus(dt_ref[...] + dtb_ref[...])
    a = -jnp.exp(alog_ref[...])
    r_idx = lax.broadcasted_iota(jnp.int32, (q, q), 0)
    c_idx = lax.broadcasted_iota(jnp.int32, (q, q), 1)
    causal = c_idx <= r_idx
    a_cs = jnp.dot(causal.astype(F32), dt * a, precision=hi, preferred_element_type=F32)
    eye = (lax.broadcasted_iota(jnp.int32, (M_HEADS, M_HEADS), 0)
           == lax.broadcasted_iota(jnp.int32, (M_HEADS, M_HEADS), 1)).astype(F32)
    a_cs_t = _dot_nt(eye, a_cs, precision=hi)
    widen = (lax.broadcasted_iota(jnp.int32, (M_HEADS, hp), 1) // M_HEAD_DIM
             == lax.broadcasted_iota(jnp.int32, (M_HEADS, hp), 0)).astype(BF16)
    dt_w = _widen(dt, widen)
    acs_w = _widen(a_cs, widen)
    dskip_w = _widen(jnp.broadcast_to(dskip_ref[...], (8, M_HEADS)), widen)[0:1, :]
    a_end_w = acs_w[q - 1:q, :]
    x_dt = xs_ref[...] * dt_w
    xdt_ref[...] = x_dt.astype(BF16)
    xdd_ref[...] = (x_dt * jnp.exp(a_end_w - acs_w)).astype(BF16)
    decay_blk_w = jnp.exp(a_end_w)

    heads_per_group = M_HEADS // M_GROUPS
    gw = M_WIDTH // M_GROUPS
    pair_lane = lax.broadcasted_iota(jnp.int32, (q, 2 * M_HEAD_DIM), 1)
    for g in range(M_GROUPS):
        gcols = slice(g * gw, (g + 1) * gw)
        b_g = bc_ref[:, g * M_STATE:(g + 1) * M_STATE]
        c_g = bc_ref[:, (M_GROUPS + g) * M_STATE:(M_GROUPS + g + 1) * M_STATE]
        cb = _dot_nt(c_g, b_g)
        st = state_ref[:, gcols]
        y_off = _dot(c_g, st.astype(BF16))
        state_ref[:, gcols] = st * decay_blk_w[:, gcols] + _dot_tn(b_g, xdd_ref[:, gcols])
        for pr in range(heads_per_group // 2):
            h0 = g * heads_per_group + 2 * pr
            pcols = slice(h0 * M_HEAD_DIM, (h0 + 2) * M_HEAD_DIM)
            x_pair = xdt_ref[:, pcols]
            ys = []
            for h in (h0, h0 + 1):
                seg = a_cs[:, h:h + 1] - a_cs_t[h:h + 1, :]
                scores = cb * jnp.exp(jnp.where(causal, seg, -jnp.inf))
                ys.append(_dot(scores.astype(BF16), x_pair))
            y_ref[:, pcols] = jnp.where(pair_lane < M_HEAD_DIM, ys[0], ys[1])
        y_ref[:, gcols] = (y_ref[:, gcols] + y_off * jnp.exp(acs_w[:, gcols])
                           + dskip_w[:, gcols] * xs_ref[:, gcols])

    @pl.when(c == pl.num_programs(1) - 1)
    def _():
        hfin_ref[...] = state_ref[...].T.reshape(M_HEADS, M_HEAD_DIM, M_STATE)

    gy = y_ref[...] * _silu(z_ref[...].astype(F32))
    ng = ng_ref[...]
    for g in range(M_GROUPS):
        cols = slice(g * gw, (g + 1) * gw)
        s = gy[:, cols]
        s = s * lax.rsqrt(jnp.mean(s * s, axis=-1, keepdims=True) + EPS)
        out_ref[:, cols] = (s * ng[:, cols]).astype(BF16)


def _ssd(proj, dt_raw, h0, c0, conv_w, conv_b, dt_bias, a_log, d_skip, mnorm_g, layer,
         row_start, n_seq, seq_len, q):
    n_c = seq_len // q
    blk0 = row_start // q
    half = CONV_DIM // 2
    z_blk = (2 * GM_WIDTH) // M_WIDTH
    xa_blk = (2 * GM_WIDTH + M_WIDTH) // half
    zero_init = h0 is None

    def rows(col):
        return lambda s, c: (blk0 + s * n_c + c, col)

    def per_layer(*shape):
        return pl.BlockSpec((None,) + shape, lambda s, c: (layer,) + (0,) * len(shape))

    in_specs = [
        pl.BlockSpec((q, M_WIDTH), rows(z_blk)),
        pl.BlockSpec((q, half), rows(xa_blk)),
        pl.BlockSpec((q, half), rows(xa_blk + 1)),
        pl.BlockSpec((q, M_HEADS), rows(0)),
    ]
    args = [proj, proj, proj, dt_raw]
    if not zero_init:
        in_specs += [
            pl.BlockSpec((None, None, M_HEADS, M_HEAD_DIM, M_STATE), lambda s, c: (layer, s, 0, 0, 0)),
            pl.BlockSpec((None, None, M_CONV - 1, CONV_DIM), lambda s, c: (layer, s, 0, 0)),
        ]
        args += [h0, c0]
    in_specs += [per_layer(M_CONV, CONV_DIM), per_layer(1, CONV_DIM), per_layer(1, M_HEADS),
                 per_layer(1, M_HEADS), per_layer(1, M_HEADS), per_layer(1, M_WIDTH)]
    args += [conv_w, conv_b.reshape(DEPTH, 1, CONV_DIM), dt_bias.reshape(DEPTH, 1, M_HEADS),
             a_log.reshape(DEPTH, 1, M_HEADS), d_skip.reshape(DEPTH, 1, M_HEADS),
             mnorm_g.reshape(DEPTH, 1, M_WIDTH)]
    return pl.pallas_call(
        functools.partial(_ssd_kernel, q, zero_init),
        grid=(n_seq, n_c),
        in_specs=in_specs,
        out_specs=[
            pl.BlockSpec((q, M_WIDTH), lambda s, c: (s * n_c + c, 0)),
            pl.BlockSpec((None, M_HEADS, M_HEAD_DIM, M_STATE), lambda s, c: (s, 0, 0, 0)),
            pl.BlockSpec((None, M_CONV - 1, CONV_DIM), lambda s, c: (s, 0, 0)),
        ],
        out_shape=[
            jax.ShapeDtypeStruct((n_seq * seq_len, M_WIDTH), BF16),
            jax.ShapeDtypeStruct((n_seq, M_HEADS, M_HEAD_DIM, M_STATE), F32),
            jax.ShapeDtypeStruct((n_seq, M_CONV - 1, CONV_DIM), F32),
        ],
        scratch_shapes=[
            pltpu.VMEM((M_STATE, M_WIDTH), F32),
            pltpu.VMEM((CONV_PAD + q, CONV_DIM), F32),
            pltpu.VMEM((q, M_WIDTH), F32),
            pltpu.VMEM((q, 2 * M_GROUPS * M_STATE), BF16),
            pltpu.VMEM((q, M_WIDTH), BF16),
            pltpu.VMEM((q, M_WIDTH), BF16),
            pltpu.VMEM((q, M_WIDTH), F32),
        ],
        compiler_params=_cparams(2),
        name="conv_ssd",
    )(*args)


def _outproj_kernel(n_first_ssm, n_first, gm_ref, wa_ref, wb_ref, gate_ref, *refs):
    n_ssm = 1 if n_first_ssm is None else 2
    ssm_refs, x_refs, o_ref = refs[:n_ssm], refs[n_ssm:-1], refs[-1]
    ssm = _stream_tile(ssm_refs, n_first_ssm, Ellipsis)
    acc = _dot(gm_ref[...], wa_ref[...].astype(BF16)) + _dot(ssm, wb_ref[...].astype(BF16))
    tm, tn = acc.shape
    nb = tm // ROW_BLOCK
    upd = acc.reshape(nb, ROW_BLOCK, tn) * gate_ref[...][:, None, :]
    o_ref[...] = _stream_tile(x_refs, n_first, Ellipsis) + upd.reshape(tm, tn)


def _outproj(gm, ssms, w_out, xs, modb, layer, tm=1024, tn=512):
    t_rows = gm.shape[0]
    nb = tm // ROW_BLOCK
    gate_blk = 2 * D_MODEL // tn
    assert all(x.shape[0] % tm == 0 for x in xs)
    assert all(x.shape[0] % tm == 0 for x in ssms)
    x_specs, n_first = _stream_specs(xs, (tm, tn), lambda j: j, D_MODEL // tn)
    ssm_specs, n_first_ssm = _stream_specs(ssms, (tm, M_WIDTH), lambda j: 0, D_MODEL // tn)
    return pl.pallas_call(
        functools.partial(_outproj_kernel, n_first_ssm, n_first),
        grid=(t_rows // tm, D_MODEL // tn),
        in_specs=[
            pl.BlockSpec((tm, GM_WIDTH), lambda i, j: (i, 0)),
            pl.BlockSpec((None, GM_WIDTH, tn), lambda i, j: (layer, 0, j)),
            pl.BlockSpec((None, M_WIDTH, tn), lambda i, j: (layer, 1, j)),
            pl.BlockSpec((nb, tn), lambda i, j: (i, gate_blk + j)),
        ] + ssm_specs + x_specs,
        out_specs=pl.BlockSpec((tm, tn), lambda i, j: (i, j)),
        out_shape=jax.ShapeDtypeStruct((t_rows, D_MODEL), F32),
        compiler_params=_cparams(2),
        name="out_proj",
    )(gm, w_out, w_out, modb, *ssms, *xs)


ROUTE_ROWS = 8
EXPERT_TILE = 1024
EXPERT_SUB = 256


def _row_argmax_first(rows):
    m = rows[0]
    for r in rows[1:]:
        m = jnp.maximum(m, r)
    idx = jnp.full(m.shape, len(rows), jnp.int32)
    for k in reversed(range(len(rows))):
        idx = jnp.where(rows[k] == m, k, idx)
    return m, idx


def _router_kernel(x_ref, g_ref, shift_ref, scale_ref, wr_ref, br_ref, h_ref, route_ref, count_ref, run_ref):
    n_blk = shift_ref.shape[0]
    g = g_ref[...]

    @pl.when(pl.program_id(0) == 0)
    def _():
        run_ref[...] = jnp.zeros_like(run_ref)

    def body(r, carry):
        rows = pl.ds(pl.multiple_of(r * ROW_BLOCK, ROW_BLOCK), ROW_BLOCK)
        xf = x_ref[rows, :]
        y = xf * lax.rsqrt(jnp.mean(xf * xf, axis=-1, keepdims=True) + EPS) * g
        h_ref[rows, 0:D_MODEL] = y * (1.0 + scale_ref[pl.ds(r, 1), :]) + shift_ref[pl.ds(r, 1), :]
        return carry

    lax.fori_loop(0, n_blk, body, 0)
    logits = (_dot_split2(h_ref[:, 0:D_MODEL], wr_ref[...]) + br_ref[...]).T
    lg = [logits[k:k + 1, :] for k in range(N_EGROUPS)]
    g_max, g_idx = _row_argmax_first(lg)
    g_den = lg[0] * 0.0
    for r in lg:
        g_den = g_den + jnp.exp(r - g_max)
    g_top = 1.0 / g_den
    e_in = []
    for k in range(EXPERTS_PER_GROUP):
        sel = lg[0] * 0.0
        for gi in range(N_EGROUPS):
            row = N_EGROUPS + gi * EXPERTS_PER_GROUP + k
            sel = jnp.where(g_idx == gi, logits[row:row + 1, :], sel)
        e_in.append(sel)
    e_max, i1 = _row_argmax_first(e_in)
    e_exp = [jnp.exp(r - e_max) for r in e_in]
    e_den = e_exp[0]
    for r in e_exp[1:]:
        e_den = e_den + r
    p = [r / e_den for r in e_exp]
    p1 = p[0] * 0.0
    for k in range(EXPERTS_PER_GROUP):
        p1 = jnp.where(i1 == k, p[k], p1)
    rest = [jnp.where(i1 == k, -1.0, p[k]) for k in range(EXPERTS_PER_GROUP)]
    p2, i2 = _row_argmax_first(rest)
    norm = p1 + p2
    w1 = p1 / norm * g_top
    w2 = p2 / norm * g_top
    tm = logits.shape[1]
    row = lax.broadcasted_iota(jnp.int32, (ROUTE_ROWS, tm), 0)
    onehot = (row == g_idx).astype(F32)
    before = (lax.broadcasted_iota(jnp.int32, (tm, tm), 0)
              < lax.broadcasted_iota(jnp.int32, (tm, tm), 1)).astype(BF16)
    earlier = _dot(onehot.astype(BF16), before)
    rank = jnp.sum(onehot * (earlier + run_ref[:, 0:1]), axis=0, keepdims=True)
    run_ref[...] += jnp.sum(onehot, axis=1, keepdims=True)
    count_ref[...] = run_ref[...]

    route_ref[0:1, :] = g_idx.astype(F32)
    route_ref[1:2, :] = rank
    for k in range(EXPERTS_PER_GROUP):
        route_ref[2 + k:3 + k, :] = jnp.where(i1 == k, w1, 0.0) + jnp.where(i2 == k, w2, 0.0)
    route_ref[2 + EXPERTS_PER_GROUP:ROUTE_ROWS, :] = jnp.zeros((ROUTE_ROWS - 2 - EXPERTS_PER_GROUP, tm), F32)
    place = (lax.broadcasted_iota(jnp.int32, (ROUTE_ROWS, LANES), 0)
             == lax.broadcasted_iota(jnp.int32, (ROUTE_ROWS, LANES), 1)).astype(F32)
    h_ref[:, D_MODEL:D_MODEL + LANES] = lax.dot_general(
        route_ref[...], place, (((0,), (0,)), ((), ())), precision=lax.Precision.HIGHEST,
        preferred_element_type=F32)


def _router(x, norm_g, modb, w_r, b_r, layer, tm=512):
    t_rows = x.shape[0]
    nb = tm // ROW_BLOCK
    return pl.pallas_call(
        _router_kernel,
        grid=(t_rows // tm,),
        in_specs=[
            pl.BlockSpec((tm, D_MODEL), lambda i: (i, 0)),
            pl.BlockSpec((None, 1, D_MODEL), lambda i: (layer, 0, 0)),
            pl.BlockSpec((nb, D_MODEL), lambda i: (i, 3)),
            pl.BlockSpec((nb, D_MODEL), lambda i: (i, 4)),
            pl.BlockSpec((D_MODEL, LANES), lambda i: (0, 0)),
            pl.BlockSpec((1, LANES), lambda i: (0, 0)),
        ],
        out_specs=[
            pl.BlockSpec((tm, D_MODEL + LANES), lambda i: (i, 0)),
            pl.BlockSpec((ROUTE_ROWS, tm), lambda i: (0, i)),
            pl.BlockSpec((ROUTE_ROWS, LANES), lambda i: (0, 0)),
        ],
        out_shape=[
            jax.ShapeDtypeStruct((t_rows, D_MODEL + LANES), F32),
            jax.ShapeDtypeStruct((ROUTE_ROWS, t_rows), F32),
            jax.ShapeDtypeStruct((ROUTE_ROWS, LANES), F32),
        ],
        scratch_shapes=[pltpu.VMEM((ROUTE_ROWS, LANES), F32)],
        compiler_params=_cparams(1),
        name="router",
    )(x, norm_g.reshape(DEPTH, 1, D_MODEL), modb, modb, w_r, b_r)


def _dispatch_tables(route, counts, t_rows, tm, n_tiles):
    gid = route[0].astype(jnp.int32)
    rank = route[1].astype(jnp.int32)
    cnt = counts[:N_EGROUPS, 0].astype(jnp.int32)
    padded = (cnt + tm - 1) // tm * tm
    ends = jnp.cumsum(padded)
    starts = ends - padded
    pos = starts[gid] + rank
    src = jnp.zeros((n_tiles * tm,), jnp.int32).at[pos].set(jnp.arange(t_rows, dtype=jnp.int32))
    tile_start = jnp.arange(n_tiles, dtype=jnp.int32) * tm
    tile_group = jnp.sum((tile_start[:, None] >= ends[None, :]).astype(jnp.int32), axis=1)
    tile_group = jnp.minimum(tile_group, N_EGROUPS - 1)
    tile_rows = jnp.clip(cnt[tile_group] - (tile_start - starts[tile_group]), 0, tm)
    tile_nsub = (tile_rows + EXPERT_SUB - 1) // EXPERT_SUB
    return pos, src, tile_group, tile_nsub


def _start_row_gather(table_ref, idx_ref, idx_base, buf_ref, sem):
    def issue(r, carry):
        pltpu.make_async_copy(table_ref.at[pl.ds(idx_ref[idx_base + r], 1), :],
                              buf_ref.at[pl.ds(r, 1), :], sem).start()
        return carry

    lax.fori_loop(0, buf_ref.shape[0], issue, 0, unroll=8)


def _start_rows_unrolled(table_ref, idx_ref, idx_base, buf_ref, row0, n_rows, sem):
    for r in range(n_rows):
        pltpu.make_async_copy(table_ref.at[pl.ds(idx_ref[idx_base + r], 1), :],
                              buf_ref.at[pl.ds(row0 + r, 1), :], sem).start()


def _wait_row_gather(table_ref, buf_ref, sem):
    pltpu.make_async_copy(table_ref.at[pl.ds(0, buf_ref.shape[0]), :], buf_ref, sem).wait()


def _experts_kernel(f_split, tg_ref, ns_ref, src_ref, wg_ref, wu_ref, wd_ref, h_ref, y_ref,
                    xbuf_ref, xb_ref, rec_ref, wgb_ref, wub_ref, wdb_ref, sem):
    i = pl.program_id(0)
    s = pl.program_id(1)
    n_tiles = pl.num_programs(0)
    tm = xb_ref.shape[0]
    n_steps = pl.num_programs(1)
    n_sub = ns_ref[i]
    step_rows = tm // (EXPERTS_PER_GROUP * f_split)
    nxt = jnp.minimum(i + 1, n_tiles - 1)

    @pl.when((s == 0) & (i == 0))
    def _():
        _start_row_gather(h_ref, src_ref, 0, xbuf_ref, sem)

    @pl.when(s == 0)
    def _():
        y_ref[...] = jnp.zeros_like(y_ref)

        @pl.when((i == 0) | (ns_ref[jnp.maximum(i - 1, 0)] > 0))
        def _():
            _wait_row_gather(h_ref, xbuf_ref, sem)

        def keep(b, carry):
            rows = pl.ds(pl.multiple_of(b * EXPERT_SUB, EXPERT_SUB), EXPERT_SUB)
            xb_ref[rows, :] = xbuf_ref[rows, 0:D_MODEL].astype(BF16)
            rec_ref[rows, :] = xbuf_ref[rows, D_MODEL:D_MODEL + LANES]
            return carry

        lax.fori_loop(0, n_sub, keep, 0)

    @pl.when(n_sub > 0)
    def _():
        wgb_ref[...] = wg_ref[...].astype(BF16)
        wub_ref[...] = wu_ref[...].astype(BF16)
        wdb_ref[...] = wd_ref[...].astype(BF16)

    for k in range(1, tm // EXPERT_SUB + 1):
        @pl.when(n_sub == k)
        def _(k=k):
            _start_rows_unrolled(h_ref, src_ref, nxt * tm + s * step_rows, xbuf_ref, s * step_rows,
                                 step_rows, sem)
            e = s // f_split
            rows = slice(0, k * EXPERT_SUB)
            x = xb_ref[rows, :]
            gate_act = _dot(x, wgb_ref[...])
            up = _dot(x, wub_ref[...])
            record = rec_ref[rows, :]
            lane = lax.broadcasted_iota(jnp.int32, record.shape, 1)
            gate_col = jnp.sum(jnp.where(lane == 2 + e, record, 0.0), axis=1, keepdims=True)
            hid = (_silu(gate_act) * up * gate_col).astype(BF16)
            y_ref[rows, :] += _dot(hid, wdb_ref[...])

    @pl.when((i == n_tiles - 1) & (s == n_steps - 1) & (n_sub > 0))
    def _():
        _wait_row_gather(h_ref, xbuf_ref, sem)


def _experts(h, src, tile_group, tile_nsub, w_gate, w_up, w_down, layer, tm, f_split=2):
    n_rows = src.shape[0]
    fb = D_EXPERT // f_split
    last = EXPERTS_PER_GROUP * f_split - 1

    def expert_of(i, s, tg, ns, sr):
        return tg[i] * EXPERTS_PER_GROUP + jnp.where(ns[i] > 0, s, last) // f_split

    def half_of(i, s, tg, ns, sr):
        return jnp.where(ns[i] > 0, s, last) % f_split

    grid_spec = pltpu.PrefetchScalarGridSpec(
        num_scalar_prefetch=3,
        grid=(n_rows // tm, EXPERTS_PER_GROUP * f_split),
        in_specs=[
            pl.BlockSpec((None, None, D_MODEL, fb), lambda i, s, *p: (layer, expert_of(i, s, *p), 0, half_of(i, s, *p))),
            pl.BlockSpec((None, None, D_MODEL, fb), lambda i, s, *p: (layer, expert_of(i, s, *p), 0, half_of(i, s, *p))),
            pl.BlockSpec((None, None, fb, D_MODEL), lambda i, s, *p: (layer, expert_of(i, s, *p), half_of(i, s, *p), 0)),
            pl.BlockSpec(memory_space=pl.ANY),
        ],
        out_specs=pl.BlockSpec((tm, D_MODEL), lambda i, s, *_: (i, 0)),
        scratch_shapes=[
            pltpu.VMEM((tm, D_MODEL + LANES), F32),
            pltpu.VMEM((tm, D_MODEL), BF16),
            pltpu.VMEM((tm, LANES), F32),
            pltpu.VMEM((D_MODEL, fb), BF16),
            pltpu.VMEM((D_MODEL, fb), BF16),
            pltpu.VMEM((fb, D_MODEL), BF16),
            pltpu.SemaphoreType.DMA(()),
        ],
    )
    return pl.pallas_call(
        functools.partial(_experts_kernel, f_split),
        grid_spec=grid_spec,
        out_shape=jax.ShapeDtypeStruct((n_rows, D_MODEL), F32),
        compiler_params=_cparams(2),
        name="experts",
    )(tile_group, tile_nsub, src, w_gate, w_up, w_down, h)


def _combine_kernel(final_norm, n_first, pos_ref, x_ref, gate_ref, fg_ref, ys_ref, *refs):
    o_refs, buf_ref, sem = refs[:-2], refs[-2], refs[-1]
    tm, d = x_ref.shape
    i = pl.program_id(0)
    slot = i % 2

    @pl.when(i == 0)
    def _():
        _start_row_gather(ys_ref, pos_ref, 0, buf_ref.at[0], sem.at[0])

    _wait_row_gather(ys_ref, buf_ref.at[slot], sem.at[slot])
    n_tiles = pl.num_programs(0)
    nxt = jnp.minimum(i + 1, n_tiles - 1)
    _start_rows_unrolled(ys_ref, pos_ref, nxt * tm, buf_ref.at[1 - slot], 0, tm, sem.at[1 - slot])

    nb = tm // ROW_BLOCK
    upd = buf_ref[slot].reshape(nb, ROW_BLOCK, d) * gate_ref[...][:, None, :]
    x = x_ref[...] + upd.reshape(tm, d)
    if final_norm:
        x = x * lax.rsqrt(jnp.mean(x * x, axis=-1, keepdims=True) + EPS) * fg_ref[...]
    if n_first is None:
        o_refs[0][...] = x
    else:
        @pl.when(pl.program_id(0) < n_first)
        def _():
            o_refs[0][...] = x

        @pl.when(pl.program_id(0) >= n_first)
        def _():
            o_refs[1][...] = x

    @pl.when(i == n_tiles - 1)
    def _():
        _wait_row_gather(ys_ref, buf_ref.at[1 - slot], sem.at[1 - slot])


def _combine(x, ys, pos, modb, final_g, final_norm, split_rows=None, tm=512):
    t_rows = x.shape[0]
    nb = tm // ROW_BLOCK
    if split_rows is None:
        n_first = None
        out_specs = pl.BlockSpec((tm, D_MODEL), lambda i, p: (i, 0))
        out_shape = jax.ShapeDtypeStruct((t_rows, D_MODEL), F32)
    else:
        assert split_rows % tm == 0 and (t_rows - split_rows) % tm == 0
        n_first = split_rows // tm
        out_specs = [pl.BlockSpec((tm, D_MODEL), lambda i, p: (jnp.minimum(i, n_first - 1), 0)),
                     pl.BlockSpec((tm, D_MODEL), lambda i, p: (jnp.maximum(i - n_first, 0), 0))]
        out_shape = [jax.ShapeDtypeStruct((split_rows, D_MODEL), F32),
                     jax.ShapeDtypeStruct((t_rows - split_rows, D_MODEL), F32)]
    grid_spec = pltpu.PrefetchScalarGridSpec(
        num_scalar_prefetch=1,
        grid=(t_rows // tm,),
        in_specs=[
            pl.BlockSpec((tm, D_MODEL), lambda i, p: (i, 0)),
            pl.BlockSpec((nb, D_MODEL), lambda i, p: (i, 5)),
            pl.BlockSpec((1, D_MODEL), lambda i, p: (0, 0)),
            pl.BlockSpec(memory_space=pl.ANY),
        ],
        out_specs=out_specs,
        scratch_shapes=[pltpu.VMEM((2, tm, D_MODEL), F32), pltpu.SemaphoreType.DMA((2,))],
    )
    return pl.pallas_call(
        functools.partial(_combine_kernel, final_norm, n_first),
        grid_spec=grid_spec,
        out_shape=out_shape,
        compiler_params=_cparams(1),
        name="combine",
    )(pos, x, modb, final_g.reshape(1, D_MODEL), ys)


def kernel(x_prompt, x_sample, state_ssm, state_conv, c_prompt, c_sample, norm1_g, norm2_g, w_ada, b_ada,
           w_in, gm_ln_g, gm_ln_b, gm_w_s, gm_b_s, conv_w, conv_b, dt_bias, a_log, d_skip, mnorm_g, w_out,
           w_router_group, b_router_group, w_router_expert, b_router_expert, w_gate, w_up, w_down,
           final_norm_g):
    n_p, len_p, _ = x_prompt.shape
    n_s, len_s, _ = x_sample.shape
    assert len_p % GM_TILE == 0 and len_s == ROW_BLOCK and (n_s * len_s) % GM_TILE == 0
    rows_p, rows_s = n_p * len_p, n_s * len_s
    t_rows = rows_p + rows_s
    n_seq = n_p + n_s
    blk_per_prompt = len_p // ROW_BLOCK
    assert t_rows % EXPERT_TILE == 0
    n_expert_tiles = t_rows // EXPERT_TILE + N_EGROUPS

    xs = (x_prompt.reshape(rows_p, D_MODEL), x_sample.reshape(rows_s, D_MODEL))

    c_all = jnp.concatenate([c_prompt, c_sample], axis=0)
    c_pad = jnp.pad(c_all, ((0, (-n_seq) % 8), (0, 0)))
    mod = _ada(c_pad, w_ada, b_ada)
    modb_all = jnp.concatenate(
        [jnp.repeat(mod[:, :n_p], blk_per_prompt, axis=1), mod[:, n_p:n_seq]], axis=1)

    half_w = gm_w_s[:, :, :ROW_BLOCK, :ROW_BLOCK]
    w_cat = jnp.stack([gm_w_s, jnp.tile(half_w, (1, 1, 2, 2))], axis=1)
    b_t = jnp.swapaxes(gm_b_s, 1, 2)
    b_cat = jnp.stack([b_t, jnp.tile(b_t[:, :ROW_BLOCK], (1, 2, 1))], axis=1)

    w_in_t = jnp.swapaxes(w_in, 1, 2).astype(BF16)
    w_r = jnp.concatenate([w_router_group, w_router_expert], axis=2)
    w_r = jnp.pad(w_r, ((0, 0), (0, 0), (0, LANES - w_r.shape[2])))
    b_r = jnp.concatenate([b_router_group, b_router_expert], axis=1)
    b_r = jnp.pad(b_r, ((0, 0), (0, LANES - b_r.shape[1])))[:, None, :]

    ssm_p, conv_p, ssm_s, conv_s, v_rows = [], [], [], [], []
    for l in range(DEPTH):
        last = l == DEPTH - 1
        modb = modb_all[l]
        proj, dt_raw = _inproj(xs, norm1_g, modb, w_in_t, l, tn=512 if len(xs) == 2 else 1024)
        gm_out, vn = _gmlp(proj, gm_ln_g, gm_ln_b, w_cat, b_cat, l, rows_p // GM_TILE, rows_s)
        ssm_args = (conv_w, conv_b, dt_bias, a_log, d_skip, mnorm_g, l)
        ssm_out_p, hfin_p, cfin_p = _ssd(proj, dt_raw, None, None, *ssm_args, 0, n_p, len_p, SSD_PROMPT_CHUNK)
        ssm_out_s, hfin_s, cfin_s = _ssd(proj, dt_raw, state_ssm, state_conv, *ssm_args, rows_p, n_s, len_s, len_s)
        if l == 0:
            x = _outproj(gm_out, (ssm_out_p, ssm_out_s), w_out, xs, modb, l)
        else:
            x = _outproj(gm_out, (ssm_out_p, ssm_out_s), w_out[l:l + 1].astype(BF16), xs, modb, 0)
        h2, route, counts = _router(x, norm2_g, modb, w_r[l], b_r[l], l)
        pos, src, tile_group, tile_nsub = _dispatch_tables(route, counts, t_rows, EXPERT_TILE, n_expert_tiles)
        ys = _experts(h2, src, tile_group, tile_nsub, w_gate, w_up, w_down, l, EXPERT_TILE)
        out = _combine(x, ys, pos, modb, final_norm_g, last, rows_p if last else None)
        xs = tuple(out) if last else (out,)
        ssm_p.append(hfin_p)
        conv_p.append(cfin_p)
        ssm_s.append(hfin_s)
        conv_s.append(cfin_s)
        v_rows.append(vn.reshape(n_s, len_s, GM_WIDTH))

    y_prompt = xs[0].reshape(n_p, len_p, D_MODEL)
    y_sample = xs[1].reshape(n_s, len_s, D_MODEL)
    return (y_prompt, y_sample, jnp.stack(ssm_p), jnp.stack(conv_p), jnp.stack(ssm_s), jnp.stack(conv_s),
            jnp.stack(v_rows))
```

```python
import functools
import math

import jax
import jax.numpy as jnp
from jax import lax
from jax.experimental import pallas as pl
from jax.experimental.pallas import tpu as pltpu

D_MODEL = 2048
DEPTH = 2
GM_WIDTH = 1024
GM_HEADS = 8
GM_HEAD_DIM = 128
GM_CHUNK = 128
M_WIDTH = 1024
M_HEAD_DIM = 64
M_HEADS = 16
M_GROUPS = 4
M_STATE = 128
M_CONV = 4
CONV_DIM = M_WIDTH + 2 * M_GROUPS * M_STATE
MAIN_DIM = 2 * GM_WIDTH + M_WIDTH + CONV_DIM
N_EGROUPS = 4
EXPERTS_PER_GROUP = 4
N_EXPERTS = 16
D_EXPERT = 512
N_MOD = 6
EPS = 1e-6

ROW_BLOCK = 64
LANES = 128
VMEM_LIMIT = 56 * 1024 * 1024

F32 = jnp.float32
BF16 = jnp.bfloat16


def _cparams(n_axes):
    return pltpu.CompilerParams(
        dimension_semantics=("arbitrary",) * n_axes, vmem_limit_bytes=VMEM_LIMIT)


def _silu(x):
    return x * jax.nn.sigmoid(x)


def _gelu_tanh(x):
    c = math.sqrt(2.0 / math.pi)
    return 0.5 * x * (1.0 + jnp.tanh(c * (x + 0.044715 * (x * x * x))))


def _dot(a, b):
    return jnp.dot(a, b, preferred_element_type=F32)


def _dot_nt(a, b, precision=None):
    return lax.dot_general(a, b, (((1,), (1,)), ((), ())), precision=precision,
                           preferred_element_type=F32)


def _dot_split2(a, b):
    a0 = a.astype(BF16)
    a1 = (a - a0.astype(F32)).astype(BF16)
    b0 = b.astype(BF16)
    b1 = (b - b0.astype(F32)).astype(BF16)
    return _dot(a0, b0) + _dot(a0, b1) + _dot(a1, b0)


def _dot_tn(a, b):
    return lax.dot_general(a, b, (((0,), (0,)), ((), ())), preferred_element_type=F32)


def _ada_kernel(c_ref, w_ref, b_ref, o_ref):
    c = _silu(c_ref[...]).astype(BF16)
    o_ref[...] = _dot(c, w_ref[...].astype(BF16)) + b_ref[...]


def _ada(c_pad, w_ada, b_ada):
    rows = c_pad.shape[0]
    tn = 1024
    n_out = N_MOD * D_MODEL
    return pl.pallas_call(
        _ada_kernel,
        grid=(DEPTH, n_out // tn),
        in_specs=[
            pl.BlockSpec((rows, D_MODEL), lambda l, j: (0, 0)),
            pl.BlockSpec((None, D_MODEL, tn), lambda l, j: (l, 0, j)),
            pl.BlockSpec((None, 1, tn), lambda l, j: (l, 0, j)),
        ],
        out_specs=pl.BlockSpec((None, rows, tn), lambda l, j: (l, 0, j)),
        out_shape=jax.ShapeDtypeStruct((DEPTH, rows, n_out), F32),
        compiler_params=_cparams(2),
        name="ada_ln",
    )(c_pad, w_ada, b_ada.reshape(DEPTH, 1, n_out))


def _stream_specs(xs, block, col_of, n_j):
    if len(xs) == 1:
        return [pl.BlockSpec(block, lambda i, j: (i, col_of(j)))], None
    n_first = xs[0].shape[0] // block[0]

    def first(i, j):
        return jnp.minimum(i, n_first - 1), jnp.where(i < n_first, col_of(j), col_of(n_j - 1))

    def second(i, j):
        return jnp.maximum(i - n_first, 0), jnp.where(i < n_first, col_of(0), col_of(j))

    return [pl.BlockSpec(block, first), pl.BlockSpec(block, second)], n_first


def _stream_tile(x_refs, n_first, idx):
    if n_first is None:
        return x_refs[0][idx]
    return jnp.where(pl.program_id(0) < n_first, x_refs[0][idx], x_refs[1][idx])


def _inproj_kernel(n_first, *refs):
    n_x = 1 if n_first is None else 2
    x_refs = refs[:n_x]
    g_ref, shift_ref, scale_ref, w_ref, wdt_ref, proj_ref, dt_ref, h_ref = refs[n_x:]
    j = pl.program_id(1)
    n_blk = shift_ref.shape[0]

    @pl.when(j == 0)
    def _():
        g = g_ref[...]

        def body(r, carry):
            rows = pl.ds(pl.multiple_of(r * ROW_BLOCK, ROW_BLOCK), ROW_BLOCK)
            xf = _stream_tile(x_refs, n_first, (rows, slice(None)))
            y = xf * lax.rsqrt(jnp.mean(xf * xf, axis=-1, keepdims=True) + EPS) * g
            y = y * (1.0 + scale_ref[pl.ds(r, 1), :]) + shift_ref[pl.ds(r, 1), :]
            h_ref[rows, :] = y.astype(BF16)
            return carry

        lax.fori_loop(0, n_blk, body, 0)
        dt_ref[...] = _dot_nt(h_ref[...], wdt_ref[...].astype(BF16))

    proj_ref[...] = _dot_nt(h_ref[...], w_ref[...].astype(BF16)).astype(BF16)


def _inproj(xs, norm_g, modb, w_in_t, layer, tm=1024, tn=512):
    t_rows = sum(x.shape[0] for x in xs)
    nb = tm // ROW_BLOCK
    assert all(x.shape[0] % tm == 0 for x in xs)
    x_specs, n_first = _stream_specs(xs, (tm, D_MODEL), lambda j: 0, MAIN_DIM // tn)
    return pl.pallas_call(
        functools.partial(_inproj_kernel, n_first),
        grid=(t_rows // tm, MAIN_DIM // tn),
        in_specs=x_specs + [
            pl.BlockSpec((None, 1, D_MODEL), lambda i, j: (layer, 0, 0)),
            pl.BlockSpec((nb, D_MODEL), lambda i, j: (i, 0)),
            pl.BlockSpec((nb, D_MODEL), lambda i, j: (i, 1)),
            pl.BlockSpec((None, tn, D_MODEL), lambda i, j: (layer, j, 0)),
            pl.BlockSpec((None, M_HEADS, D_MODEL), lambda i, j: (layer, MAIN_DIM // M_HEADS, 0)),
        ],
        out_specs=[
            pl.BlockSpec((tm, tn), lambda i, j: (i, j)),
            pl.BlockSpec((tm, M_HEADS), lambda i, j: (i, 0)),
        ],
        out_shape=[
            jax.ShapeDtypeStruct((t_rows, MAIN_DIM), BF16),
            jax.ShapeDtypeStruct((t_rows, M_HEADS), F32),
        ],
        scratch_shapes=[pltpu.VMEM((tm, D_MODEL), BF16)],
        compiler_params=_cparams(2),
        name="in_proj",
    )(*xs, norm_g.reshape(DEPTH, 1, D_MODEL), modb, modb, w_in_t, w_in_t)


def _gmlp_kernel(n_prompt_tiles, u_ref, v_ref, lng_ref, lnb_ref, w_ref, b_ref, out_ref, vn_ref):
    i = pl.program_id(0)
    t_idx = lax.broadcasted_iota(jnp.int32, (GM_CHUNK, GM_CHUNK), 0)
    s_idx = lax.broadcasted_iota(jnp.int32, (GM_CHUNK, GM_CHUNK), 1)
    same_seq = (t_idx // ROW_BLOCK) == (s_idx // ROW_BLOCK)
    allowed = (s_idx <= t_idx) & (same_seq | (i < n_prompt_tiles))
    w_masked = [jnp.where(allowed, w_ref[h], 0.0).astype(BF16) for h in range(GM_HEADS)]
    bias = b_ref[...]
    for r0 in range(0, u_ref.shape[0], GM_CHUNK):
        rows = slice(r0, r0 + GM_CHUNK)
        u = _gelu_tanh(u_ref[rows, :].astype(F32))
        v = _gelu_tanh(v_ref[rows, :].astype(F32))
        mu = jnp.mean(v, axis=-1, keepdims=True)
        vc = v - mu
        vn = vc * lax.rsqrt(jnp.mean(vc * vc, axis=-1, keepdims=True) + EPS)
        vn = vn * lng_ref[...] + lnb_ref[...]
        vn_ref[rows, :] = vn
        vnb = vn.astype(BF16)
        for h in range(GM_HEADS):
            cols = slice(h * GM_HEAD_DIM, (h + 1) * GM_HEAD_DIM)
            mixed = _dot(w_masked[h], vnb[:, cols]) + bias[:, h:h + 1]
            out_ref[rows, cols] = (u[:, cols] * mixed).astype(BF16)


def _gmlp(proj, ln_g, ln_b, w_cat, b_cat, layer, n_prompt_tiles, n_sample_rows):
    t_rows = proj.shape[0]
    n_tiles = t_rows // GM_TILE
    kern = functools.partial(_gmlp_kernel, n_prompt_tiles)
    return pl.pallas_call(
        kern,
        grid=(n_tiles,),
        in_specs=[
            pl.BlockSpec((GM_TILE, GM_WIDTH), lambda i: (i, 0)),
            pl.BlockSpec((GM_TILE, GM_WIDTH), lambda i: (i, 1)),
            pl.BlockSpec((None, 1, GM_WIDTH), lambda i: (layer, 0, 0)),
            pl.BlockSpec((None, 1, GM_WIDTH), lambda i: (layer, 0, 0)),
            pl.BlockSpec((None, None, GM_HEADS, GM_CHUNK, GM_CHUNK),
                         lambda i: (layer, jnp.where(i < n_prompt_tiles, 0, 1), 0, 0, 0)),
            pl.BlockSpec((None, None, GM_CHUNK, GM_HEADS),
                         lambda i: (layer, jnp.where(i < n_prompt_tiles, 0, 1), 0, 0)),
        ],
        out_specs=[
            pl.BlockSpec((GM_TILE, GM_WIDTH), lambda i: (i, 0)),
            pl.BlockSpec((GM_TILE, GM_WIDTH), lambda i: (jnp.maximum(i - n_prompt_tiles, 0), 0)),
        ],
        out_shape=[
            jax.ShapeDtypeStruct((t_rows, GM_WIDTH), BF16),
            jax.ShapeDtypeStruct((n_sample_rows, GM_WIDTH), F32),
        ],
        compiler_params=_cparams(1),
        name="gmlp",
    )(proj, proj, ln_g.reshape(DEPTH, 1, GM_WIDTH), ln_b.reshape(DEPTH, 1, GM_WIDTH), w_cat, b_cat)


GM_TILE = 4 * GM_CHUNK
CONV_PAD = 8
SSD_PROMPT_CHUNK = 256


def _widen(v, onehot):
    p0 = v.astype(BF16)
    r1 = v - p0.astype(F32)
    p1 = r1.astype(BF16)
    p2 = (r1 - p1.astype(F32)).astype(BF16)
    return _dot(p0, onehot) + _dot(p1, onehot) + _dot(p2, onehot)


def _ssd_kernel(q, zero_init, *refs):
    z_ref, xa_ref, xb_ref, dt_ref = refs[:4]
    refs = refs[4:]
    h0_ref = c0_ref = None
    if not zero_init:
        h0_ref, c0_ref = refs[:2]
        refs = refs[2:]
    cw_ref, cb_ref, dtb_ref, alog_ref, dskip_ref, ng_ref = refs[:6]
    refs = refs[6:]
    out_ref, hfin_ref, cfin_ref, state_ref, win_ref, xs_ref, bc_ref, xdt_ref, xdd_ref, y_ref = refs
    c = pl.program_id(1)
    tail = M_CONV - 1
    lo = CONV_PAD - tail
    hp = M_HEADS * M_HEAD_DIM

    @pl.when(c == 0)
    def _():
        if zero_init:
            state_ref[...] = jnp.zeros_like(state_ref)
            win_ref[lo:CONV_PAD, :] = jnp.zeros((tail, CONV_DIM), F32)
        else:
            state_ref[...] = h0_ref[...].reshape(hp, M_STATE).T
            win_ref[lo:CONV_PAD, :] = c0_ref[...]

    half = CONV_DIM // 2
    win_ref[CONV_PAD:CONV_PAD + q, 0:half] = xa_ref[...].astype(F32)
    win_ref[CONV_PAD:CONV_PAD + q, half:CONV_DIM] = xb_ref[...].astype(F32)
    for r0 in range(0, q, ROW_BLOCK):
        conv = cb_ref[...]
        for k in range(M_CONV):
            conv = conv + win_ref[lo + k + r0:lo + k + r0 + ROW_BLOCK, :] * cw_ref[k:k + 1, :]
        act = _silu(conv)
        xs_ref[r0:r0 + ROW_BLOCK, :] = act[:, 0:M_WIDTH]
        bc_ref[r0:r0 + ROW_BLOCK, :] = act[:, M_WIDTH:CONV_DIM].astype(BF16)
    new_tail = win_ref[CONV_PAD + q - tail:CONV_PAD + q, :]
    cfin_ref[...] = new_tail
    win_ref[lo:CONV_PAD, :] = new_tail

    hi = lax.Precision.HIGHEST
    dt = jax.nn.softplus(dt_ref[...] + dtb_ref[...])
    a = -jnp.exp(alog_ref[...])
    r_idx = lax.broadcasted_iota(jnp.int32, (q, q), 0)
    c_idx = lax.broadcasted_iota(jnp.int32, (q, q), 1)
    causal = c_idx <= r_idx
    a_cs = jnp.dot(causal.astype(F32), dt * a, precision=hi, preferred_element_type=F32)
    eye = (lax.broadcasted_iota(jnp.int32, (M_HEADS, M_HEADS), 0)
           == lax.broadcasted_iota(jnp.int32, (M_HEADS, M_HEADS), 1)).astype(F32)
    a_cs_t = _dot_nt(eye, a_cs, precision=hi)
    widen = (lax.broadcasted_iota(jnp.int32, (M_HEADS, hp), 1) // M_HEAD_DIM
             == lax.broadcasted_iota(jnp.int32, (M_HEADS, hp), 0)).astype(BF16)
    dt_w = _widen(dt, widen)
    acs_w = _widen(a_cs, widen)
    dskip_w = _widen(jnp.broadcast_to(dskip_ref[...], (8, M_HEADS)), widen)[0:1, :]
    a_end_w = acs_w[q - 1:q, :]
    x_dt = xs_ref[...] * dt_w
    xdt_ref[...] = x_dt.astype(BF16)
    xdd_ref[...] = (x_dt * jnp.exp(a_end_w - acs_w)).astype(BF16)
    decay_blk_w = jnp.exp(a_end_w)

    heads_per_group = M_HEADS // M_GROUPS
    gw = M_WIDTH // M_GROUPS
    pair_lane = lax.broadcasted_iota(jnp.int32, (q, 2 * M_HEAD_DIM), 1)
    for g in range(M_GROUPS):
        gcols = slice(g * gw, (g + 1) * gw)
        b_g = bc_ref[:, g * M_STATE:(g + 1) * M_STATE]
        c_g = bc_ref[:, (M_GROUPS + g) * M_STATE:(M_GROUPS + g + 1) * M_STATE]
        cb = _dot_nt(c_g, b_g)
        st = state_ref[:, gcols]
        y_off = _dot(c_g, st.astype(BF16))
        state_ref[:, gcols] = st * decay_blk_w[:, gcols] + _dot_tn(b_g, xdd_ref[:, gcols])
        for pr in range(heads_per_group // 2):
            h0 = g * heads_per_group + 2 * pr
            pcols = slice(h0 * M_HEAD_DIM, (h0 + 2) * M_HEAD_DIM)
            x_pair = xdt_ref[:, pcols]
            ys = []
            for h in (h0, h0 + 1):
                seg = a_cs[:, h:h + 1] - a_cs_t[h:h + 1, :]
                scores = cb * jnp.exp(jnp.where(causal, seg, -jnp.inf))
                ys.append(_dot(scores.astype(BF16), x_pair))
            y_ref[:, pcols] = jnp.where(pair_lane < M_HEAD_DIM, ys[0], ys[1])
        y_ref[:, gcols] = (y_ref[:, gcols] + y_off * jnp.exp(acs_w[:, gcols])
                           + dskip_w[:, gcols] * xs_ref[:, gcols])

    @pl.when(c == pl.num_programs(1) - 1)
    def _():
        hfin_ref[...] = state_ref[...].T.reshape(M_HEADS, M_HEAD_DIM, M_STATE)

    gy = y_ref[...] * _silu(z_ref[...].astype(F32))
    ng = ng_ref[...]
    for g in range(M_GROUPS):
        cols = slice(g * gw, (g + 1) * gw)
        s = gy[:, cols]
        s = s * lax.rsqrt(jnp.mean(s * s, axis=-1, keepdims=True) + EPS)
        out_ref[:, cols] = (s * ng[:, cols]).astype(BF16)


def _ssd(proj, dt_raw, h0, c0, conv_w, conv_b, dt_bias, a_log, d_skip, mnorm_g, layer,
         row_start, n_seq, seq_len, q):
    n_c = seq_len // q
    blk0 = row_start // q
    half = CONV_DIM // 2
    z_blk = (2 * GM_WIDTH) // M_WIDTH
    xa_blk = (2 * GM_WIDTH + M_WIDTH) // half
    zero_init = h0 is None

    def rows(col):
        return lambda s, c: (blk0 + s * n_c + c, col)

    def per_layer(*shape):
        return pl.BlockSpec((None,) + shape, lambda s, c: (layer,) + (0,) * len(shape))

    in_specs = [
        pl.BlockSpec((q, M_WIDTH), rows(z_blk)),
        pl.BlockSpec((q, half), rows(xa_blk)),
        pl.BlockSpec((q, half), rows(xa_blk + 1)),
        pl.BlockSpec((q, M_HEADS), rows(0)),
    ]
    args = [proj, proj, proj, dt_raw]
    if not zero_init:
        in_specs += [
            pl.BlockSpec((None, None, M_HEADS, M_HEAD_DIM, M_STATE), lambda s, c: (layer, s, 0, 0, 0)),
            pl.BlockSpec((None, None, M_CONV - 1, CONV_DIM), lambda s, c: (layer, s, 0, 0)),
        ]
        args += [h0, c0]
    in_specs += [per_layer(M_CONV, CONV_DIM), per_layer(1, CONV_DIM), per_layer(1, M_HEADS),
                 per_layer(1, M_HEADS), per_layer(1, M_HEADS), per_layer(1, M_WIDTH)]
    args += [conv_w, conv_b.reshape(DEPTH, 1, CONV_DIM), dt_bias.reshape(DEPTH, 1, M_HEADS),
             a_log.reshape(DEPTH, 1, M_HEADS), d_skip.reshape(DEPTH, 1, M_HEADS),
             mnorm_g.reshape(DEPTH, 1, M_WIDTH)]
    return pl.pallas_call(
        functools.partial(_ssd_kernel, q, zero_init),
        grid=(n_seq, n_c),
        in_specs=in_specs,
        out_specs=[
            pl.BlockSpec((q, M_WIDTH), lambda s, c: (s * n_c + c, 0)),
            pl.BlockSpec((None, M_HEADS, M_HEAD_DIM, M_STATE), lambda s, c: (s, 0, 0, 0)),
            pl.BlockSpec((None, M_CONV - 1, CONV_DIM), lambda s, c: (s, 0, 0)),
        ],
        out_shape=[
            jax.ShapeDtypeStruct((n_seq * seq_len, M_WIDTH), BF16),
            jax.ShapeDtypeStruct((n_seq, M_HEADS, M_HEAD_DIM, M_STATE), F32),
            jax.ShapeDtypeStruct((n_seq, M_CONV - 1, CONV_DIM), F32),
        ],
        scratch_shapes=[
            pltpu.VMEM((M_STATE, M_WIDTH), F32),
            pltpu.VMEM((CONV_PAD + q, CONV_DIM), F32),
            pltpu.VMEM((q, M_WIDTH), F32),
            pltpu.VMEM((q, 2 * M_GROUPS * M_STATE), BF16),
            pltpu.VMEM((q, M_WIDTH), BF16),
            pltpu.VMEM((q, M_WIDTH), BF16),
            pltpu.VMEM((q, M_WIDTH), F32),
        ],
        compiler_params=_cparams(2),
        name="conv_ssd",
    )(*args)


def _outproj_kernel(n_first_ssm, n_first, gm_ref, wa_ref, wb_ref, gate_ref, *refs):
    n_ssm = 1 if n_first_ssm is None else 2
    ssm_refs, x_refs, o_ref = refs[:n_ssm], refs[n_ssm:-1], refs[-1]
    ssm = _stream_tile(ssm_refs, n_first_ssm, Ellipsis)
    acc = _dot(gm_ref[...], wa_ref[...].astype(BF16)) + _dot(ssm, wb_ref[...].astype(BF16))
    tm, tn = acc.shape
    nb = tm // ROW_BLOCK
    upd = acc.reshape(nb, ROW_BLOCK, tn) * gate_ref[...][:, None, :]
    o_ref[...] = _stream_tile(x_refs, n_first, Ellipsis) + upd.reshape(tm, tn)


def _outproj(gm, ssms, w_out, xs, modb, layer, tm=1024, tn=512):
    t_rows = gm.shape[0]
    nb = tm // ROW_BLOCK
    gate_blk = 2 * D_MODEL // tn
    assert all(x.shape[0] % tm == 0 for x in xs)
    assert all(x.shape[0] % tm == 0 for x in ssms)
    x_specs, n_first = _stream_specs(xs, (tm, tn), lambda j: j, D_MODEL // tn)
    ssm_specs, n_first_ssm = _stream_specs(ssms, (tm, M_WIDTH), lambda j: 0, D_MODEL // tn)
    return pl.pallas_call(
        functools.partial(_outproj_kernel, n_first_ssm, n_first),
        grid=(t_rows // tm, D_MODEL // tn),
        in_specs=[
            pl.BlockSpec((tm, GM_WIDTH), lambda i, j: (i, 0)),
            pl.BlockSpec((None, GM_WIDTH, tn), lambda i, j: (layer, 0, j)),
            pl.BlockSpec((None, M_WIDTH, tn), lambda i, j: (layer, 1, j)),
            pl.BlockSpec((nb, tn), lambda i, j: (i, gate_blk + j)),
        ] + ssm_specs + x_specs,
        out_specs=pl.BlockSpec((tm, tn), lambda i, j: (i, j)),
        out_shape=jax.ShapeDtypeStruct((t_rows, D_MODEL), F32),
        compiler_params=_cparams(2),
        name="out_proj",
    )(gm, w_out, w_out, modb, *ssms, *xs)


ROUTE_ROWS = 8
EXPERT_TILE = 1536
EXPERT_SUB = 256


def _row_argmax_first(rows):
    m = rows[0]
    for r in rows[1:]:
        m = jnp.maximum(m, r)
    idx = jnp.full(m.shape, len(rows), jnp.int32)
    for k in reversed(range(len(rows))):
        idx = jnp.where(rows[k] == m, k, idx)
    return m, idx


def _router_kernel(x_ref, g_ref, shift_ref, scale_ref, wr_ref, br_ref, h_ref, route_ref, count_ref, run_ref):
    n_blk = shift_ref.shape[0]
    g = g_ref[...]

    @pl.when(pl.program_id(0) == 0)
    def _():
        run_ref[...] = jnp.zeros_like(run_ref)

    def body(r, carry):
        rows = pl.ds(pl.multiple_of(r * ROW_BLOCK, ROW_BLOCK), ROW_BLOCK)
        xf = x_ref[rows, :]
        y = xf * lax.rsqrt(jnp.mean(xf * xf, axis=-1, keepdims=True) + EPS) * g
        h_ref[rows, 0:D_MODEL] = y * (1.0 + scale_ref[pl.ds(r, 1), :]) + shift_ref[pl.ds(r, 1), :]
        return carry

    lax.fori_loop(0, n_blk, body, 0)
    logits = (_dot_split2(h_ref[:, 0:D_MODEL], wr_ref[...]) + br_ref[...]).T
    lg = [logits[k:k + 1, :] for k in range(N_EGROUPS)]
    g_max, g_idx = _row_argmax_first(lg)
    g_den = lg[0] * 0.0
    for r in lg:
        g_den = g_den + jnp.exp(r - g_max)
    g_top = 1.0 / g_den
    e_in = []
    for k in range(EXPERTS_PER_GROUP):
        sel = lg[0] * 0.0
        for gi in range(N_EGROUPS):
            row = N_EGROUPS + gi * EXPERTS_PER_GROUP + k
            sel = jnp.where(g_idx == gi, logits[row:row + 1, :], sel)
        e_in.append(sel)
    e_max, i1 = _row_argmax_first(e_in)
    e_exp = [jnp.exp(r - e_max) for r in e_in]
    e_den = e_exp[0]
    for r in e_exp[1:]:
        e_den = e_den + r
    p = [r / e_den for r in e_exp]
    p1 = p[0] * 0.0
    for k in range(EXPERTS_PER_GROUP):
        p1 = jnp.where(i1 == k, p[k], p1)
    rest = [jnp.where(i1 == k, -1.0, p[k]) for k in range(EXPERTS_PER_GROUP)]
    p2, i2 = _row_argmax_first(rest)
    norm = p1 + p2
    w1 = p1 / norm * g_top
    w2 = p2 / norm * g_top
    tm = logits.shape[1]
    row = lax.broadcasted_iota(jnp.int32, (ROUTE_ROWS, tm), 0)
    onehot = (row == g_idx).astype(F32)
    before = (lax.broadcasted_iota(jnp.int32, (tm, tm), 0)
              < lax.broadcasted_iota(jnp.int32, (tm, tm), 1)).astype(BF16)
    earlier = _dot(onehot.astype(BF16), before)
    rank = jnp.sum(onehot * (earlier + run_ref[:, 0:1]), axis=0, keepdims=True)
    run_ref[...] += jnp.sum(onehot, axis=1, keepdims=True)
    count_ref[...] = run_ref[...]

    route_ref[0:1, :] = g_idx.astype(F32)
    route_ref[1:2, :] = rank
    for k in range(EXPERTS_PER_GROUP):
        route_ref[2 + k:3 + k, :] = jnp.where(i1 == k, w1, 0.0) + jnp.where(i2 == k, w2, 0.0)
    route_ref[2 + EXPERTS_PER_GROUP:ROUTE_ROWS, :] = jnp.zeros((ROUTE_ROWS - 2 - EXPERTS_PER_GROUP, tm), F32)
    place = (lax.broadcasted_iota(jnp.int32, (ROUTE_ROWS, LANES), 0)
             == lax.broadcasted_iota(jnp.int32, (ROUTE_ROWS, LANES), 1)).astype(F32)
    h_ref[:, D_MODEL:D_MODEL + LANES] = lax.dot_general(
        route_ref[...], place, (((0,), (0,)), ((), ())), precision=lax.Precision.HIGHEST,
        preferred_element_type=F32)


def _router(x, norm_g, modb, w_r, b_r, layer, tm=512):
    t_rows = x.shape[0]
    nb = tm // ROW_BLOCK
    return pl.pallas_call(
        _router_kernel,
        grid=(t_rows // tm,),
        in_specs=[
            pl.BlockSpec((tm, D_MODEL), lambda i: (i, 0)),
            pl.BlockSpec((None, 1, D_MODEL), lambda i: (layer, 0, 0)),
            pl.BlockSpec((nb, D_MODEL), lambda i: (i, 3)),
            pl.BlockSpec((nb, D_MODEL), lambda i: (i, 4)),
            pl.BlockSpec((D_MODEL, LANES), lambda i: (0, 0)),
            pl.BlockSpec((1, LANES), lambda i: (0, 0)),
        ],
        out_specs=[
            pl.BlockSpec((tm, D_MODEL + LANES), lambda i: (i, 0)),
            pl.BlockSpec((ROUTE_ROWS, tm), lambda i: (0, i)),
            pl.BlockSpec((ROUTE_ROWS, LANES), lambda i: (0, 0)),
        ],
        out_shape=[
            jax.ShapeDtypeStruct((t_rows, D_MODEL + LANES), F32),
            jax.ShapeDtypeStruct((ROUTE_ROWS, t_rows), F32),
            jax.ShapeDtypeStruct((ROUTE_ROWS, LANES), F32),
        ],
        scratch_shapes=[pltpu.VMEM((ROUTE_ROWS, LANES), F32)],
        compiler_params=_cparams(1),
        name="router",
    )(x, norm_g.reshape(DEPTH, 1, D_MODEL), modb, modb, w_r, b_r)


def _dispatch_tables(route, counts, t_rows, tm, n_tiles):
    gid = route[0].astype(jnp.int32)
    rank = route[1].astype(jnp.int32)
    cnt = counts[:N_EGROUPS, 0].astype(jnp.int32)
    padded = (cnt + tm - 1) // tm * tm
    ends = jnp.cumsum(padded)
    starts = ends - padded
    pos = starts[gid] + rank
    src = jnp.zeros((n_tiles * tm,), jnp.int32).at[pos].set(jnp.arange(t_rows, dtype=jnp.int32))
    tile_start = jnp.arange(n_tiles, dtype=jnp.int32) * tm
    tile_group = jnp.sum((tile_start[:, None] >= ends[None, :]).astype(jnp.int32), axis=1)
    tile_group = jnp.minimum(tile_group, N_EGROUPS - 1)
    tile_rows = jnp.clip(cnt[tile_group] - (tile_start - starts[tile_group]), 0, tm)
    tile_nsub = (tile_rows + EXPERT_SUB - 1) // EXPERT_SUB
    return pos, src, tile_group, tile_nsub


def _start_row_gather(table_ref, idx_ref, idx_base, buf_ref, sem):
    def issue(r, carry):
        pltpu.make_async_copy(table_ref.at[pl.ds(idx_ref[idx_base + r], 1), :],
                              buf_ref.at[pl.ds(r, 1), :], sem).start()
        return carry

    lax.fori_loop(0, buf_ref.shape[0], issue, 0, unroll=8)


def _start_rows_unrolled(table_ref, idx_ref, idx_base, buf_ref, row0, n_rows, sem):
    for r in range(n_rows):
        pltpu.make_async_copy(table_ref.at[pl.ds(idx_ref[idx_base + r], 1), :],
                              buf_ref.at[pl.ds(row0 + r, 1), :], sem).start()


def _wait_row_gather(table_ref, buf_ref, sem):
    pltpu.make_async_copy(table_ref.at[pl.ds(0, buf_ref.shape[0]), :], buf_ref, sem).wait()


def _experts_kernel(f_split, tg_ref, ns_ref, src_ref, wg_ref, wu_ref, wd_ref, h_ref, y_ref,
                    xbuf_ref, xb_ref, rec_ref, wgb_ref, wub_ref, wdb_ref, sem):
    i = pl.program_id(0)
    s = pl.program_id(1)
    n_tiles = pl.num_programs(0)
    tm = xb_ref.shape[0]
    n_steps = pl.num_programs(1)
    n_sub = ns_ref[i]
    step_rows = tm // (EXPERTS_PER_GROUP * f_split)
    nxt = jnp.minimum(i + 1, n_tiles - 1)

    @pl.when((s == 0) & (i == 0))
    def _():
        _start_row_gather(h_ref, src_ref, 0, xbuf_ref, sem)

    @pl.when(s == 0)
    def _():
        y_ref[...] = jnp.zeros_like(y_ref)

        @pl.when((i == 0) | (ns_ref[jnp.maximum(i - 1, 0)] > 0))
        def _():
            _wait_row_gather(h_ref, xbuf_ref, sem)

        def keep(b, carry):
            rows = pl.ds(pl.multiple_of(b * EXPERT_SUB, EXPERT_SUB), EXPERT_SUB)
            xb_ref[rows, :] = xbuf_ref[rows, 0:D_MODEL].astype(BF16)
            rec_ref[rows, :] = xbuf_ref[rows, D_MODEL:D_MODEL + LANES]
            return carry

        lax.fori_loop(0, n_sub, keep, 0)

    @pl.when(n_sub > 0)
    def _():
        wgb_ref[...] = wg_ref[...].astype(BF16)
        wub_ref[...] = wu_ref[...].astype(BF16)
        wdb_ref[...] = wd_ref[...].astype(BF16)

    for k in range(1, tm // EXPERT_SUB + 1):
        @pl.when(n_sub == k)
        def _(k=k):
            _start_rows_unrolled(h_ref, src_ref, nxt * tm + s * step_rows, xbuf_ref, s * step_rows,
                                 step_rows, sem)
            e = s // f_split
            rows = slice(0, k * EXPERT_SUB)
            x = xb_ref[rows, :]
            gate_act = _dot(x, wgb_ref[...])
            up = _dot(x, wub_ref[...])
            record = rec_ref[rows, :]
            lane = lax.broadcasted_iota(jnp.int32, record.shape, 1)
            gate_col = jnp.sum(jnp.where(lane == 2 + e, record, 0.0), axis=1, keepdims=True)
            hid = (_silu(gate_act) * up * gate_col).astype(BF16)
            y_ref[rows, :] += _dot(hid, wdb_ref[...])

    @pl.when((i == n_tiles - 1) & (s == n_steps - 1) & (n_sub > 0))
    def _():
        _wait_row_gather(h_ref, xbuf_ref, sem)


def _experts(h, src, tile_group, tile_nsub, w_gate, w_up, w_down, layer, tm, f_split=2):
    n_rows = src.shape[0]
    fb = D_EXPERT // f_split
    last = EXPERTS_PER_GROUP * f_split - 1

    def expert_of(i, s, tg, ns, sr):
        return tg[i] * EXPERTS_PER_GROUP + jnp.where(ns[i] > 0, s, last) // f_split

    def half_of(i, s, tg, ns, sr):
        return jnp.where(ns[i] > 0, s, last) % f_split

    grid_spec = pltpu.PrefetchScalarGridSpec(
        num_scalar_prefetch=3,
        grid=(n_rows // tm, EXPERTS_PER_GROUP * f_split),
        in_specs=[
            pl.BlockSpec((None, None, D_MODEL, fb), lambda i, s, *p: (layer, expert_of(i, s, *p), 0, half_of(i, s, *p))),
            pl.BlockSpec((None, None, D_MODEL, fb), lambda i, s, *p: (layer, expert_of(i, s, *p), 0, half_of(i, s, *p))),
            pl.BlockSpec((None, None, fb, D_MODEL), lambda i, s, *p: (layer, expert_of(i, s, *p), half_of(i, s, *p), 0)),
            pl.BlockSpec(memory_space=pl.ANY),
        ],
        out_specs=pl.BlockSpec((tm, D_MODEL), lambda i, s, *_: (i, 0), pipeline_mode=pl.Buffered(1)),
        scratch_shapes=[
            pltpu.VMEM((tm, D_MODEL + LANES), F32),
            pltpu.VMEM((tm, D_MODEL), BF16),
            pltpu.VMEM((tm, LANES), F32),
            pltpu.VMEM((D_MODEL, fb), BF16),
            pltpu.VMEM((D_MODEL, fb), BF16),
            pltpu.VMEM((fb, D_MODEL), BF16),
            pltpu.SemaphoreType.DMA(()),
        ],
    )
    return pl.pallas_call(
        functools.partial(_experts_kernel, f_split),
        grid_spec=grid_spec,
        out_shape=jax.ShapeDtypeStruct((n_rows, D_MODEL), F32),
        compiler_params=_cparams(2),
        name="experts",
    )(tile_group, tile_nsub, src, w_gate, w_up, w_down, h)


def _combine_kernel(final_norm, n_first, pos_ref, x_ref, gate_ref, fg_ref, ys_ref, *refs):
    o_refs, buf_ref, sem = refs[:-2], refs[-2], refs[-1]
    tm, d = x_ref.shape
    i = pl.program_id(0)
    slot = i % 2

    @pl.when(i == 0)
    def _():
        _start_row_gather(ys_ref, pos_ref, 0, buf_ref.at[0], sem.at[0])

    _wait_row_gather(ys_ref, buf_ref.at[slot], sem.at[slot])
    n_tiles = pl.num_programs(0)
    nxt = jnp.minimum(i + 1, n_tiles - 1)
    _start_rows_unrolled(ys_ref, pos_ref, nxt * tm, buf_ref.at[1 - slot], 0, tm, sem.at[1 - slot])

    nb = tm // ROW_BLOCK
    upd = buf_ref[slot].reshape(nb, ROW_BLOCK, d) * gate_ref[...][:, None, :]
    x = x_ref[...] + upd.reshape(tm, d)
    if final_norm:
        x = x * lax.rsqrt(jnp.mean(x * x, axis=-1, keepdims=True) + EPS) * fg_ref[...]
    if n_first is None:
        o_refs[0][...] = x
    else:
        @pl.when(pl.program_id(0) < n_first)
        def _():
            o_refs[0][...] = x

        @pl.when(pl.program_id(0) >= n_first)
        def _():
            o_refs[1][...] = x

    @pl.when(i == n_tiles - 1)
    def _():
        _wait_row_gather(ys_ref, buf_ref.at[1 - slot], sem.at[1 - slot])


def _combine(x, ys, pos, modb, final_g, final_norm, split_rows=None, tm=512):
    t_rows = x.shape[0]
    nb = tm // ROW_BLOCK
    if split_rows is None:
        n_first = None
        out_specs = pl.BlockSpec((tm, D_MODEL), lambda i, p: (i, 0))
        out_shape = jax.ShapeDtypeStruct((t_rows, D_MODEL), F32)
    else:
        assert split_rows % tm == 0 and (t_rows - split_rows) % tm == 0
        n_first = split_rows // tm
        out_specs = [pl.BlockSpec((tm, D_MODEL), lambda i, p: (jnp.minimum(i, n_first - 1), 0)),
                     pl.BlockSpec((tm, D_MODEL), lambda i, p: (jnp.maximum(i - n_first, 0), 0))]
        out_shape = [jax.ShapeDtypeStruct((split_rows, D_MODEL), F32),
                     jax.ShapeDtypeStruct((t_rows - split_rows, D_MODEL), F32)]
    grid_spec = pltpu.PrefetchScalarGridSpec(
        num_scalar_prefetch=1,
        grid=(t_rows // tm,),
        in_specs=[
            pl.BlockSpec((tm, D_MODEL), lambda i, p: (i, 0)),
            pl.BlockSpec((nb, D_MODEL), lambda i, p: (i, 5)),
            pl.BlockSpec((1, D_MODEL), lambda i, p: (0, 0)),
            pl.BlockSpec(memory_space=pl.ANY),
        ],
        out_specs=out_specs,
        scratch_shapes=[pltpu.VMEM((2, tm, D_MODEL), F32), pltpu.SemaphoreType.DMA((2,))],
    )
    return pl.pallas_call(
        functools.partial(_combine_kernel, final_norm, n_first),
        grid_spec=grid_spec,
        out_shape=out_shape,
        compiler_params=_cparams(1),
        name="combine",
    )(pos, x, modb, final_g.reshape(1, D_MODEL), ys)


def kernel(x_prompt, x_sample, state_ssm, state_conv, c_prompt, c_sample, norm1_g, norm2_g, w_ada, b_ada,
           w_in, gm_ln_g, gm_ln_b, gm_w_s, gm_b_s, conv_w, conv_b, dt_bias, a_log, d_skip, mnorm_g, w_out,
           w_router_group, b_router_group, w_router_expert, b_router_expert, w_gate, w_up, w_down,
           final_norm_g):
    n_p, len_p, _ = x_prompt.shape
    n_s, len_s, _ = x_sample.shape
    assert len_p % GM_TILE == 0 and len_s == ROW_BLOCK and (n_s * len_s) % GM_TILE == 0
    rows_p, rows_s = n_p * len_p, n_s * len_s
    t_rows = rows_p + rows_s
    n_seq = n_p + n_s
    blk_per_prompt = len_p // ROW_BLOCK
    assert t_rows % EXPERT_TILE == 0
    n_expert_tiles = t_rows // EXPERT_TILE + N_EGROUPS

    xs = (x_prompt.reshape(rows_p, D_MODEL), x_sample.reshape(rows_s, D_MODEL))

    c_all = jnp.concatenate([c_prompt, c_sample], axis=0)
    c_pad = jnp.pad(c_all, ((0, (-n_seq) % 8), (0, 0)))
    mod = _ada(c_pad, w_ada, b_ada)
    modb_all = jnp.concatenate(
        [jnp.repeat(mod[:, :n_p], blk_per_prompt, axis=1), mod[:, n_p:n_seq]], axis=1)

    half_w = gm_w_s[:, :, :ROW_BLOCK, :ROW_BLOCK]
    w_cat = jnp.stack([gm_w_s, jnp.tile(half_w, (1, 1, 2, 2))], axis=1)
    b_t = jnp.swapaxes(gm_b_s, 1, 2)
    b_cat = jnp.stack([b_t, jnp.tile(b_t[:, :ROW_BLOCK], (1, 2, 1))], axis=1)

    w_in_t = jnp.swapaxes(w_in, 1, 2)
    w_r = jnp.concatenate([w_router_group, w_router_expert], axis=2)
    w_r = jnp.pad(w_r, ((0, 0), (0, 0), (0, LANES - w_r.shape[2])))
    b_r = jnp.concatenate([b_router_group, b_router_expert], axis=1)
    b_r = jnp.pad(b_r, ((0, 0), (0, LANES - b_r.shape[1])))[:, None, :]

    ssm_p, conv_p, ssm_s, conv_s, v_rows = [], [], [], [], []
    for l in range(DEPTH):
        last = l == DEPTH - 1
        modb = modb_all[l]
        proj, dt_raw = _inproj(xs, norm1_g, modb, w_in_t, l, tn=512 if len(xs) == 2 else 1024)
        gm_out, vn = _gmlp(proj, gm_ln_g, gm_ln_b, w_cat, b_cat, l, rows_p // GM_TILE, rows_s)
        ssm_args = (conv_w, conv_b, dt_bias, a_log, d_skip, mnorm_g, l)
        ssm_out_p, hfin_p, cfin_p = _ssd(proj, dt_raw, None, None, *ssm_args, 0, n_p, len_p, SSD_PROMPT_CHUNK)
        ssm_out_s, hfin_s, cfin_s = _ssd(proj, dt_raw, state_ssm, state_conv, *ssm_args, rows_p, n_s, len_s, len_s)
        x = _outproj(gm_out, (ssm_out_p, ssm_out_s), w_out, xs, modb, l)
        h2, route, counts = _router(x, norm2_g, modb, w_r[l], b_r[l], l)
        pos, src, tile_group, tile_nsub = _dispatch_tables(route, counts, t_rows, EXPERT_TILE, n_expert_tiles)
        ys = _experts(h2, src, tile_group, tile_nsub, w_gate, w_up, w_down, l, EXPERT_TILE)
        out = _combine(x, ys, pos, modb, final_norm_g, last, rows_p if last else None)
        xs = tuple(out) if last else (out,)
        ssm_p.append(hfin_p)
        conv_p.append(cfin_p)
        ssm_s.append(hfin_s)
        conv_s.append(cfin_s)
        v_rows.append(vn.reshape(n_s, len_s, GM_WIDTH))

    y_prompt = xs[0].reshape(n_p, len_p, D_MODEL)
    y_sample = xs[1].reshape(n_s, len_s, D_MODEL)
    return (y_prompt, y_sample, jnp.stack(ssm_p), jnp.stack(conv_p), jnp.stack(ssm_s), jnp.stack(conv_s),
            jnp.stack(v_rows))
```

```python
import functools
import math

import jax
import jax.numpy as jnp
import numpy as np
from jax import lax
from jax.experimental import pallas as pl
from jax.experimental.pallas import tpu as pltpu

D_MODEL = 2048
DEPTH = 2
GM_WIDTH = 1024
GM_HEADS = 8
GM_HEAD_DIM = 128
GM_CHUNK = 128
M_WIDTH = 1024
M_HEAD_DIM = 64
M_HEADS = 16
M_GROUPS = 4
M_STATE = 128
M_CONV = 4
CONV_DIM = M_WIDTH + 2 * M_GROUPS * M_STATE
MAIN_DIM = 2 * GM_WIDTH + M_WIDTH + CONV_DIM
N_EGROUPS = 4
EXPERTS_PER_GROUP = 4
N_EXPERTS = 16
D_EXPERT = 512
N_MOD = 6
EPS = 1e-6

ROW_BLOCK = 64
LANES = 128
VMEM_LIMIT = 56 * 1024 * 1024

F32 = jnp.float32
BF16 = jnp.bfloat16


def _cparams(n_axes):
    return pltpu.CompilerParams(
        dimension_semantics=("arbitrary",) * n_axes, vmem_limit_bytes=VMEM_LIMIT)


def _silu(x):
    return x * jax.nn.sigmoid(x)


def _gelu_tanh(x):
    c = math.sqrt(2.0 / math.pi)
    return 0.5 * x * (1.0 + jnp.tanh(c * (x + 0.044715 * (x * x * x))))


def _dot(a, b):
    return jnp.dot(a, b, preferred_element_type=F32)


def _dot_nt(a, b, precision=None):
    return lax.dot_general(a, b, (((1,), (1,)), ((), ())), precision=precision,
                           preferred_element_type=F32)


def _dot_split2(a, b):
    a0 = a.astype(BF16)
    a1 = (a - a0.astype(F32)).astype(BF16)
    b0 = b.astype(BF16)
    b1 = (b - b0.astype(F32)).astype(BF16)
    return _dot(a0, b0) + _dot(a0, b1) + _dot(a1, b0)


def _dot_tn(a, b):
    return lax.dot_general(a, b, (((0,), (0,)), ((), ())), preferred_element_type=F32)


def _ada_kernel(c_ref, w_ref, b_ref, o_ref):
    c = _silu(c_ref[...]).astype(BF16)
    o_ref[...] = _dot(c, w_ref[...].astype(BF16)) + b_ref[...]


def _ada(c_pad, w_ada, b_ada):
    rows = c_pad.shape[0]
    tn = 1024
    n_out = N_MOD * D_MODEL
    return pl.pallas_call(
        _ada_kernel,
        grid=(DEPTH, n_out // tn),
        in_specs=[
            pl.BlockSpec((rows, D_MODEL), lambda l, j: (0, 0)),
            pl.BlockSpec((None, D_MODEL, tn), lambda l, j: (l, 0, j)),
            pl.BlockSpec((None, 1, tn), lambda l, j: (l, 0, j)),
        ],
        out_specs=pl.BlockSpec((None, rows, tn), lambda l, j: (l, 0, j)),
        out_shape=jax.ShapeDtypeStruct((DEPTH, rows, n_out), F32),
        compiler_params=_cparams(2),
        name="ada_ln",
    )(c_pad, w_ada, b_ada.reshape(DEPTH, 1, n_out))


def _stream_specs(xs, block, col_of, n_j):
    if len(xs) == 1:
        return [pl.BlockSpec(block, lambda i, j: (i, col_of(j)))], None
    n_first = xs[0].shape[0] // block[0]

    def first(i, j):
        return jnp.minimum(i, n_first - 1), jnp.where(i < n_first, col_of(j), col_of(n_j - 1))

    def second(i, j):
        return jnp.maximum(i - n_first, 0), jnp.where(i < n_first, col_of(0), col_of(j))

    return [pl.BlockSpec(block, first), pl.BlockSpec(block, second)], n_first


def _stream_tile(x_refs, n_first, idx):
    if n_first is None:
        return x_refs[0][idx]
    return jnp.where(pl.program_id(0) < n_first, x_refs[0][idx], x_refs[1][idx])


def _inproj_kernel(n_first, *refs):
    n_x = 1 if n_first is None else 2
    x_refs = refs[:n_x]
    g_ref, shift_ref, scale_ref, w_ref, wdt_ref, proj_ref, dt_ref, h_ref = refs[n_x:]
    j = pl.program_id(1)
    n_blk = shift_ref.shape[0]

    @pl.when(j == 0)
    def _():
        g = g_ref[...]

        def body(r, carry):
            rows = pl.ds(pl.multiple_of(r * ROW_BLOCK, ROW_BLOCK), ROW_BLOCK)
            xf = _stream_tile(x_refs, n_first, (rows, slice(None)))
            y = xf * lax.rsqrt(jnp.mean(xf * xf, axis=-1, keepdims=True) + EPS) * g
            y = y * (1.0 + scale_ref[pl.ds(r, 1), :]) + shift_ref[pl.ds(r, 1), :]
            h_ref[rows, :] = y.astype(BF16)
            return carry

        lax.fori_loop(0, n_blk, body, 0)
        dt_ref[...] = _dot_nt(h_ref[...], wdt_ref[...].astype(BF16))

    proj_ref[...] = _dot_nt(h_ref[...], w_ref[...].astype(BF16)).astype(BF16)


def _inproj(xs, norm_g, modb, w_in_t, layer, tm=1024, tn=512):
    t_rows = sum(x.shape[0] for x in xs)
    nb = tm // ROW_BLOCK
    assert all(x.shape[0] % tm == 0 for x in xs)
    x_specs, n_first = _stream_specs(xs, (tm, D_MODEL), lambda j: 0, MAIN_DIM // tn)
    return pl.pallas_call(
        functools.partial(_inproj_kernel, n_first),
        grid=(t_rows // tm, MAIN_DIM // tn),
        in_specs=x_specs + [
            pl.BlockSpec((None, 1, D_MODEL), lambda i, j: (layer, 0, 0)),
            pl.BlockSpec((nb, D_MODEL), lambda i, j: (i, 0)),
            pl.BlockSpec((nb, D_MODEL), lambda i, j: (i, 1)),
            pl.BlockSpec((None, tn, D_MODEL), lambda i, j: (layer, j, 0)),
            pl.BlockSpec((None, M_HEADS, D_MODEL), lambda i, j: (layer, MAIN_DIM // M_HEADS, 0)),
        ],
        out_specs=[
            pl.BlockSpec((tm, tn), lambda i, j: (i, j)),
            pl.BlockSpec((tm, M_HEADS), lambda i, j: (i, 0)),
        ],
        out_shape=[
            jax.ShapeDtypeStruct((t_rows, MAIN_DIM), BF16),
            jax.ShapeDtypeStruct((t_rows, M_HEADS), F32),
        ],
        scratch_shapes=[pltpu.VMEM((tm, D_MODEL), BF16)],
        compiler_params=_cparams(2),
        name="in_proj",
    )(*xs, norm_g.reshape(DEPTH, 1, D_MODEL), modb, modb, w_in_t, w_in_t)


def _gmlp_kernel(n_prompt_tiles, u_ref, v_ref, lng_ref, lnb_ref, w_ref, b_ref, out_ref, vn_ref):
    i = pl.program_id(0)
    t_idx = lax.broadcasted_iota(jnp.int32, (GM_CHUNK, GM_CHUNK), 0)
    s_idx = lax.broadcasted_iota(jnp.int32, (GM_CHUNK, GM_CHUNK), 1)
    same_seq = (t_idx // ROW_BLOCK) == (s_idx // ROW_BLOCK)
    allowed = (s_idx <= t_idx) & (same_seq | (i < n_prompt_tiles))
    w_masked = [jnp.where(allowed, w_ref[h], 0.0).astype(BF16) for h in range(GM_HEADS)]
    bias = b_ref[...]
    for r0 in range(0, u_ref.shape[0], GM_CHUNK):
        rows = slice(r0, r0 + GM_CHUNK)
        u = _gelu_tanh(u_ref[rows, :].astype(F32))
        v = _gelu_tanh(v_ref[rows, :].astype(F32))
        mu = jnp.mean(v, axis=-1, keepdims=True)
        vc = v - mu
        vn = vc * lax.rsqrt(jnp.mean(vc * vc, axis=-1, keepdims=True) + EPS)
        vn = vn * lng_ref[...] + lnb_ref[...]
        vn_ref[rows, :] = vn
        vnb = vn.astype(BF16)
        for h in range(GM_HEADS):
            cols = slice(h * GM_HEAD_DIM, (h + 1) * GM_HEAD_DIM)
            mixed = _dot(w_masked[h], vnb[:, cols]) + bias[:, h:h + 1]
            out_ref[rows, cols] = (u[:, cols] * mixed).astype(BF16)


def _gmlp(proj, ln_g, ln_b, w_cat, b_cat, layer, n_prompt_tiles, n_sample_rows):
    t_rows = proj.shape[0]
    n_tiles = t_rows // GM_TILE
    kern = functools.partial(_gmlp_kernel, n_prompt_tiles)
    return pl.pallas_call(
        kern,
        grid=(n_tiles,),
        in_specs=[
            pl.BlockSpec((GM_TILE, GM_WIDTH), lambda i: (i, 0)),
            pl.BlockSpec((GM_TILE, GM_WIDTH), lambda i: (i, 1)),
            pl.BlockSpec((None, 1, GM_WIDTH), lambda i: (layer, 0, 0)),
            pl.BlockSpec((None, 1, GM_WIDTH), lambda i: (layer, 0, 0)),
            pl.BlockSpec((None, None, GM_HEADS, GM_CHUNK, GM_CHUNK),
                         lambda i: (layer, jnp.where(i < n_prompt_tiles, 0, 1), 0, 0, 0)),
            pl.BlockSpec((None, None, GM_CHUNK, GM_HEADS),
                         lambda i: (layer, jnp.where(i < n_prompt_tiles, 0, 1), 0, 0)),
        ],
        out_specs=[
            pl.BlockSpec((GM_TILE, GM_WIDTH), lambda i: (i, 0)),
            pl.BlockSpec((GM_TILE, GM_WIDTH), lambda i: (jnp.maximum(i - n_prompt_tiles, 0), 0)),
        ],
        out_shape=[
            jax.ShapeDtypeStruct((t_rows, GM_WIDTH), BF16),
            jax.ShapeDtypeStruct((n_sample_rows, GM_WIDTH), F32),
        ],
        compiler_params=_cparams(1),
        name="gmlp",
    )(proj, proj, ln_g.reshape(DEPTH, 1, GM_WIDTH), ln_b.reshape(DEPTH, 1, GM_WIDTH), w_cat, b_cat)


GM_TILE = 4 * GM_CHUNK
CONV_PAD = 8
SSD_PROMPT_CHUNK = 256


def _widen(v, onehot):
    p0 = v.astype(BF16)
    r1 = v - p0.astype(F32)
    p1 = r1.astype(BF16)
    p2 = (r1 - p1.astype(F32)).astype(BF16)
    return _dot(p0, onehot) + _dot(p1, onehot) + _dot(p2, onehot)


def _ssd_kernel(q, zero_init, *refs):
    z_ref, xa_ref, xb_ref, dt_ref = refs[:4]
    refs = refs[4:]
    h0_ref = c0_ref = None
    if not zero_init:
        h0_ref, c0_ref = refs[:2]
        refs = refs[2:]
    cw_ref, cb_ref, dtb_ref, alog_ref, dskip_ref, ng_ref = refs[:6]
    refs = refs[6:]
    out_ref, hfin_ref, cfin_ref, state_ref, win_ref, xs_ref, bc_ref, xdt_ref, xdd_ref, y_ref = refs
    c = pl.program_id(1)
    tail = M_CONV - 1
    lo = CONV_PAD - tail
    hp = M_HEADS * M_HEAD_DIM

    @pl.when(c == 0)
    def _():
        if zero_init:
            state_ref[...] = jnp.zeros_like(state_ref)
            win_ref[lo:CONV_PAD, :] = jnp.zeros((tail, CONV_DIM), F32)
        else:
            state_ref[...] = h0_ref[...].reshape(hp, M_STATE).T
            win_ref[lo:CONV_PAD, :] = c0_ref[...]

    half = CONV_DIM // 2
    win_ref[CONV_PAD:CONV_PAD + q, 0:half] = xa_ref[...].astype(F32)
    win_ref[CONV_PAD:CONV_PAD + q, half:CONV_DIM] = xb_ref[...].astype(F32)
    for r0 in range(0, q, ROW_BLOCK):
        conv = cb_ref[...]
        for k in range(M_CONV):
            conv = conv + win_ref[lo + k + r0:lo + k + r0 + ROW_BLOCK, :] * cw_ref[k:k + 1, :]
        act = _silu(conv)
        xs_ref[r0:r0 + ROW_BLOCK, :] = act[:, 0:M_WIDTH]
        bc_ref[r0:r0 + ROW_BLOCK, :] = act[:, M_WIDTH:CONV_DIM].astype(BF16)
    new_tail = win_ref[CONV_PAD + q - tail:CONV_PAD + q, :]
    cfin_ref[...] = new_tail
    win_ref[lo:CONV_PAD, :] = new_tail

    hi = lax.Precision.HIGHEST
    dt = jax.nn.softplus(dt_ref[...] + dtb_ref[...])
    a = -jnp.exp(alog_ref[...])
    r_idx = lax.broadcasted_iota(jnp.int32, (q, q), 0)
    c_idx = lax.broadcasted_iota(jnp.int32, (q, q), 1)
    causal = c_idx <= r_idx
    a_cs = jnp.dot(causal.astype(F32), dt * a, precision=hi, preferred_element_type=F32)
    eye = (lax.broadcasted_iota(jnp.int32, (M_HEADS, M_HEADS), 0)
           == lax.broadcasted_iota(jnp.int32, (M_HEADS, M_HEADS), 1)).astype(F32)
    a_cs_t = _dot_nt(eye, a_cs, precision=hi)
    widen = (lax.broadcasted_iota(jnp.int32, (M_HEADS, hp), 1) // M_HEAD_DIM
             == lax.broadcasted_iota(jnp.int32, (M_HEADS, hp), 0)).astype(BF16)
    dt_w = _widen(dt, widen)
    acs_w = _widen(a_cs, widen)
    dskip_w = _widen(jnp.broadcast_to(dskip_ref[...], (8, M_HEADS)), widen)[0:1, :]
    a_end_w = acs_w[q - 1:q, :]
    x_dt = xs_ref[...] * dt_w
    xdt_ref[...] = x_dt.astype(BF16)
    xdd_ref[...] = (x_dt * jnp.exp(a_end_w - acs_w)).astype(BF16)
    decay_blk_w = jnp.exp(a_end_w)

    heads_per_group = M_HEADS // M_GROUPS
    gw = M_WIDTH // M_GROUPS
    pair_lane = lax.broadcasted_iota(jnp.int32, (q, 2 * M_HEAD_DIM), 1)
    for g in range(M_GROUPS):
        gcols = slice(g * gw, (g + 1) * gw)
        b_g = bc_ref[:, g * M_STATE:(g + 1) * M_STATE]
        c_g = bc_ref[:, (M_GROUPS + g) * M_STATE:(M_GROUPS + g + 1) * M_STATE]
        cb = _dot_nt(c_g, b_g)
        st = state_ref[:, gcols]
        y_off = _dot(c_g, st.astype(BF16))
        state_ref[:, gcols] = st * decay_blk_w[:, gcols] + _dot_tn(b_g, xdd_ref[:, gcols])
        for pr in range(heads_per_group // 2):
            h0 = g * heads_per_group + 2 * pr
            pcols = slice(h0 * M_HEAD_DIM, (h0 + 2) * M_HEAD_DIM)
            x_pair = xdt_ref[:, pcols]
            ys = []
            for h in (h0, h0 + 1):
                seg = a_cs[:, h:h + 1] - a_cs_t[h:h + 1, :]
                scores = cb * jnp.exp(jnp.where(causal, seg, -jnp.inf))
                ys.append(_dot(scores.astype(BF16), x_pair))
            y_ref[:, pcols] = jnp.where(pair_lane < M_HEAD_DIM, ys[0], ys[1])
        y_ref[:, gcols] = (y_ref[:, gcols] + y_off * jnp.exp(acs_w[:, gcols])
                           + dskip_w[:, gcols] * xs_ref[:, gcols])

    @pl.when(c == pl.num_programs(1) - 1)
    def _():
        hfin_ref[...] = state_ref[...].T.reshape(M_HEADS, M_HEAD_DIM, M_STATE)

    gy = y_ref[...] * _silu(z_ref[...].astype(F32))
    ng = ng_ref[...]
    for g in range(M_GROUPS):
        cols = slice(g * gw, (g + 1) * gw)
        s = gy[:, cols]
        s = s * lax.rsqrt(jnp.mean(s * s, axis=-1, keepdims=True) + EPS)
        out_ref[:, cols] = (s * ng[:, cols]).astype(BF16)


def _ssd(proj, dt_raw, h0, c0, conv_w, conv_b, dt_bias, a_log, d_skip, mnorm_g, layer,
         row_start, n_seq, seq_len, q):
    n_c = seq_len // q
    blk0 = row_start // q
    half = CONV_DIM // 2
    z_blk = (2 * GM_WIDTH) // M_WIDTH
    xa_blk = (2 * GM_WIDTH + M_WIDTH) // half
    zero_init = h0 is None

    def rows(col):
        return lambda s, c: (blk0 + s * n_c + c, col)

    def per_layer(*shape):
        return pl.BlockSpec((None,) + shape, lambda s, c: (layer,) + (0,) * len(shape))

    in_specs = [
        pl.BlockSpec((q, M_WIDTH), rows(z_blk)),
        pl.BlockSpec((q, half), rows(xa_blk)),
        pl.BlockSpec((q, half), rows(xa_blk + 1)),
        pl.BlockSpec((q, M_HEADS), rows(0)),
    ]
    args = [proj, proj, proj, dt_raw]
    if not zero_init:
        in_specs += [
            pl.BlockSpec((None, None, M_HEADS, M_HEAD_DIM, M_STATE), lambda s, c: (layer, s, 0, 0, 0)),
            pl.BlockSpec((None, None, M_CONV - 1, CONV_DIM), lambda s, c: (layer, s, 0, 0)),
        ]
        args += [h0, c0]
    in_specs += [per_layer(M_CONV, CONV_DIM), per_layer(1, CONV_DIM), per_layer(1, M_HEADS),
                 per_layer(1, M_HEADS), per_layer(1, M_HEADS), per_layer(1, M_WIDTH)]
    args += [conv_w, conv_b.reshape(DEPTH, 1, CONV_DIM), dt_bias.reshape(DEPTH, 1, M_HEADS),
             a_log.reshape(DEPTH, 1, M_HEADS), d_skip.reshape(DEPTH, 1, M_HEADS),
             mnorm_g.reshape(DEPTH, 1, M_WIDTH)]
    return pl.pallas_call(
        functools.partial(_ssd_kernel, q, zero_init),
        grid=(n_seq, n_c),
        in_specs=in_specs,
        out_specs=[
            pl.BlockSpec((q, M_WIDTH), lambda s, c: (s * n_c + c, 0)),
            pl.BlockSpec((None, M_HEADS, M_HEAD_DIM, M_STATE), lambda s, c: (s, 0, 0, 0)),
            pl.BlockSpec((None, M_CONV - 1, CONV_DIM), lambda s, c: (s, 0, 0)),
        ],
        out_shape=[
            jax.ShapeDtypeStruct((n_seq * seq_len, M_WIDTH), BF16),
            jax.ShapeDtypeStruct((n_seq, M_HEADS, M_HEAD_DIM, M_STATE), F32),
            jax.ShapeDtypeStruct((n_seq, M_CONV - 1, CONV_DIM), F32),
        ],
        scratch_shapes=[
            pltpu.VMEM((M_STATE, M_WIDTH), F32),
            pltpu.VMEM((CONV_PAD + q, CONV_DIM), F32),
            pltpu.VMEM((q, M_WIDTH), F32),
            pltpu.VMEM((q, 2 * M_GROUPS * M_STATE), BF16),
            pltpu.VMEM((q, M_WIDTH), BF16),
            pltpu.VMEM((q, M_WIDTH), BF16),
            pltpu.VMEM((q, M_WIDTH), F32),
        ],
        compiler_params=_cparams(2),
        name="conv_ssd",
    )(*args)


def _outproj_kernel(n_first_ssm, n_first, gm_ref, wa_ref, wb_ref, gate_ref, *refs):
    n_ssm = 1 if n_first_ssm is None else 2
    ssm_refs, x_refs, o_ref = refs[:n_ssm], refs[n_ssm:-1], refs[-1]
    ssm = _stream_tile(ssm_refs, n_first_ssm, Ellipsis)
    acc = _dot(gm_ref[...], wa_ref[...].astype(BF16)) + _dot(ssm, wb_ref[...].astype(BF16))
    tm, tn = acc.shape
    nb = tm // ROW_BLOCK
    upd = acc.reshape(nb, ROW_BLOCK, tn) * gate_ref[...][:, None, :]
    o_ref[...] = _stream_tile(x_refs, n_first, Ellipsis) + upd.reshape(tm, tn)


def _outproj(gm, ssms, w_out, xs, modb, layer, tm=1024, tn=512):
    t_rows = gm.shape[0]
    nb = tm // ROW_BLOCK
    gate_blk = 2 * D_MODEL // tn
    assert all(x.shape[0] % tm == 0 for x in xs)
    assert all(x.shape[0] % tm == 0 for x in ssms)
    x_specs, n_first = _stream_specs(xs, (tm, tn), lambda j: j, D_MODEL // tn)
    ssm_specs, n_first_ssm = _stream_specs(ssms, (tm, M_WIDTH), lambda j: 0, D_MODEL // tn)
    return pl.pallas_call(
        functools.partial(_outproj_kernel, n_first_ssm, n_first),
        grid=(t_rows // tm, D_MODEL // tn),
        in_specs=[
            pl.BlockSpec((tm, GM_WIDTH), lambda i, j: (i, 0)),
            pl.BlockSpec((None, GM_WIDTH, tn), lambda i, j: (layer, 0, j)),
            pl.BlockSpec((None, M_WIDTH, tn), lambda i, j: (layer, 1, j)),
            pl.BlockSpec((nb, tn), lambda i, j: (i, gate_blk + j)),
        ] + ssm_specs + x_specs,
        out_specs=pl.BlockSpec((tm, tn), lambda i, j: (i, j)),
        out_shape=jax.ShapeDtypeStruct((t_rows, D_MODEL), F32),
        compiler_params=_cparams(2),
        name="out_proj",
    )(gm, w_out, w_out, modb, *ssms, *xs)


ROUTE_ROWS = 8
PAIRS_PER_GROUP = EXPERTS_PER_GROUP * (EXPERTS_PER_GROUP - 1) // 2
N_BUCKETS = N_EGROUPS * PAIRS_PER_GROUP
BUCKET_ROWS = 32
EXPERT_TILE = 512
EXPERT_SUB = 128
PAIR_LO = (0, 0, 0, 1, 1, 2)
PAIR_HI = (1, 2, 3, 2, 3, 3)


def _row_argmax_first(rows):
    m = rows[0]
    for r in rows[1:]:
        m = jnp.maximum(m, r)
    idx = jnp.full(m.shape, len(rows), jnp.int32)
    for k in reversed(range(len(rows))):
        idx = jnp.where(rows[k] == m, k, idx)
    return m, idx


def _router_kernel(x_ref, g_ref, shift_ref, scale_ref, wr_ref, br_ref, h_ref, route_ref, count_ref, run_ref):
    n_blk = shift_ref.shape[0]
    g = g_ref[...]

    @pl.when(pl.program_id(0) == 0)
    def _():
        run_ref[...] = jnp.zeros_like(run_ref)

    def body(r, carry):
        rows = pl.ds(pl.multiple_of(r * ROW_BLOCK, ROW_BLOCK), ROW_BLOCK)
        xf = x_ref[rows, :]
        y = xf * lax.rsqrt(jnp.mean(xf * xf, axis=-1, keepdims=True) + EPS) * g
        h_ref[rows, 0:D_MODEL] = y * (1.0 + scale_ref[pl.ds(r, 1), :]) + shift_ref[pl.ds(r, 1), :]
        return carry

    lax.fori_loop(0, n_blk, body, 0)
    logits = (_dot_split2(h_ref[:, 0:D_MODEL], wr_ref[...]) + br_ref[...]).T
    lg = [logits[k:k + 1, :] for k in range(N_EGROUPS)]
    g_max, g_idx = _row_argmax_first(lg)
    g_den = lg[0] * 0.0
    for r in lg:
        g_den = g_den + jnp.exp(r - g_max)
    g_top = 1.0 / g_den
    e_in = []
    for k in range(EXPERTS_PER_GROUP):
        sel = lg[0] * 0.0
        for gi in range(N_EGROUPS):
            row = N_EGROUPS + gi * EXPERTS_PER_GROUP + k
            sel = jnp.where(g_idx == gi, logits[row:row + 1, :], sel)
        e_in.append(sel)
    e_max, i1 = _row_argmax_first(e_in)
    e_exp = [jnp.exp(r - e_max) for r in e_in]
    e_den = e_exp[0]
    for r in e_exp[1:]:
        e_den = e_den + r
    p = [r / e_den for r in e_exp]
    p1 = p[0] * 0.0
    for k in range(EXPERTS_PER_GROUP):
        p1 = jnp.where(i1 == k, p[k], p1)
    rest = [jnp.where(i1 == k, -1.0, p[k]) for k in range(EXPERTS_PER_GROUP)]
    p2, i2 = _row_argmax_first(rest)
    norm = p1 + p2
    w1 = p1 / norm * g_top
    w2 = p2 / norm * g_top
    e_lo = jnp.minimum(i1, i2)
    e_hi = jnp.maximum(i1, i2)
    pair = jnp.where(e_lo == 0, e_hi - 1, jnp.where(e_lo == 1, e_hi + 1, PAIRS_PER_GROUP - 1))
    bucket = g_idx * PAIRS_PER_GROUP + pair
    w_lo = jnp.where(i1 < i2, w1, w2)
    w_hi = jnp.where(i1 < i2, w2, w1)
    tm = logits.shape[1]
    row = lax.broadcasted_iota(jnp.int32, (BUCKET_ROWS, tm), 0)
    onehot = (row == bucket).astype(F32)
    before = (lax.broadcasted_iota(jnp.int32, (tm, tm), 0)
              < lax.broadcasted_iota(jnp.int32, (tm, tm), 1)).astype(BF16)
    earlier = _dot(onehot.astype(BF16), before)
    rank = jnp.sum(onehot * (earlier + run_ref[:, 0:1]), axis=0, keepdims=True)
    run_ref[...] += jnp.sum(onehot, axis=1, keepdims=True)
    count_ref[...] = run_ref[...]

    route_ref[0:1, :] = bucket.astype(F32)
    route_ref[1:2, :] = rank
    route_ref[2:3, :] = w_lo
    route_ref[3:4, :] = w_hi
    route_ref[4:ROUTE_ROWS, :] = jnp.zeros((ROUTE_ROWS - 4, tm), F32)
    place = (lax.broadcasted_iota(jnp.int32, (ROUTE_ROWS, LANES), 0)
             == lax.broadcasted_iota(jnp.int32, (ROUTE_ROWS, LANES), 1)).astype(F32)
    h_ref[:, D_MODEL:D_MODEL + LANES] = lax.dot_general(
        route_ref[...], place, (((0,), (0,)), ((), ())), precision=lax.Precision.HIGHEST,
        preferred_element_type=F32)


def _router(x, norm_g, modb, w_r, b_r, layer, tm=512):
    t_rows = x.shape[0]
    nb = tm // ROW_BLOCK
    return pl.pallas_call(
        _router_kernel,
        grid=(t_rows // tm,),
        in_specs=[
            pl.BlockSpec((tm, D_MODEL), lambda i: (i, 0)),
            pl.BlockSpec((None, 1, D_MODEL), lambda i: (layer, 0, 0)),
            pl.BlockSpec((nb, D_MODEL), lambda i: (i, 3)),
            pl.BlockSpec((nb, D_MODEL), lambda i: (i, 4)),
            pl.BlockSpec((D_MODEL, LANES), lambda i: (0, 0)),
            pl.BlockSpec((1, LANES), lambda i: (0, 0)),
        ],
        out_specs=[
            pl.BlockSpec((tm, D_MODEL + LANES), lambda i: (i, 0)),
            pl.BlockSpec((ROUTE_ROWS, tm), lambda i: (0, i)),
            pl.BlockSpec((BUCKET_ROWS, LANES), lambda i: (0, 0)),
        ],
        out_shape=[
            jax.ShapeDtypeStruct((t_rows, D_MODEL + LANES), F32),
            jax.ShapeDtypeStruct((ROUTE_ROWS, t_rows), F32),
            jax.ShapeDtypeStruct((BUCKET_ROWS, LANES), F32),
        ],
        scratch_shapes=[pltpu.VMEM((BUCKET_ROWS, LANES), F32)],
        compiler_params=_cparams(1),
        name="router",
    )(x, norm_g.reshape(DEPTH, 1, D_MODEL), modb, modb, w_r, b_r)


def _dispatch_tables(route, counts, t_rows, tm, n_tiles):
    gid = route[0].astype(jnp.int32)
    rank = route[1].astype(jnp.int32)
    cnt = counts[:N_BUCKETS, 0].astype(jnp.int32)
    padded = (cnt + tm - 1) // tm * tm
    ends = jnp.cumsum(padded)
    starts = ends - padded
    pos = starts[gid] + rank
    src = jnp.zeros((n_tiles * tm,), jnp.int32).at[pos].set(jnp.arange(t_rows, dtype=jnp.int32))
    tile_start = jnp.arange(n_tiles, dtype=jnp.int32) * tm
    tile_bucket = jnp.sum((tile_start[:, None] >= ends[None, :]).astype(jnp.int32), axis=1)
    tile_bucket = jnp.minimum(tile_bucket, N_BUCKETS - 1)
    tile_rows = jnp.clip(cnt[tile_bucket] - (tile_start - starts[tile_bucket]), 0, tm)
    tile_nsub = (tile_rows + EXPERT_SUB - 1) // EXPERT_SUB
    first = (tile_bucket // PAIRS_PER_GROUP) * EXPERTS_PER_GROUP
    tile_lo = first + jnp.asarray(PAIR_LO, jnp.int32)[tile_bucket % PAIRS_PER_GROUP]
    tile_hi = first + jnp.asarray(PAIR_HI, jnp.int32)[tile_bucket % PAIRS_PER_GROUP]
    return pos, src, tile_lo, tile_hi, tile_nsub


def _start_row_gather(table_ref, idx_ref, idx_base, buf_ref, sem):
    def issue(r, carry):
        pltpu.make_async_copy(table_ref.at[pl.ds(idx_ref[idx_base + r], 1), :],
                              buf_ref.at[pl.ds(r, 1), :], sem).start()
        return carry

    lax.fori_loop(0, buf_ref.shape[0], issue, 0, unroll=8)


def _start_rows_unrolled(table_ref, idx_ref, idx_base, buf_ref, row0, n_rows, sem):
    for r in range(n_rows):
        pltpu.make_async_copy(table_ref.at[pl.ds(idx_ref[idx_base + r], 1), :],
                              buf_ref.at[pl.ds(row0 + r, 1), :], sem).start()


def _wait_row_gather(table_ref, buf_ref, sem):
    pltpu.make_async_copy(table_ref.at[pl.ds(0, buf_ref.shape[0]), :], buf_ref, sem).wait()


def _experts_kernel(f_split, lo_ref, hi_ref, ns_ref, src_ref, wg_ref, wu_ref, wd_ref, h_ref, y_ref,
                    xbuf_ref, xb_ref, rec_ref, wgb_ref, wub_ref, wdb_ref, sem):
    i = pl.program_id(0)
    s = pl.program_id(1)
    n_tiles = pl.num_programs(0)
    tm = xb_ref.shape[0]
    n_steps = pl.num_programs(1)
    n_sub = ns_ref[i]
    step_rows = tm // (2 * f_split)
    nxt = jnp.minimum(i + 1, n_tiles - 1)

    @pl.when((s == 0) & (i == 0))
    def _():
        _start_row_gather(h_ref, src_ref, 0, xbuf_ref, sem)

    @pl.when(s == 0)
    def _():
        y_ref[...] = jnp.zeros_like(y_ref)

        @pl.when((i == 0) | (ns_ref[jnp.maximum(i - 1, 0)] > 0))
        def _():
            _wait_row_gather(h_ref, xbuf_ref, sem)

        def keep(b, carry):
            rows = pl.ds(pl.multiple_of(b * EXPERT_SUB, EXPERT_SUB), EXPERT_SUB)
            xb_ref[rows, :] = xbuf_ref[rows, 0:D_MODEL].astype(BF16)
            rec_ref[rows, :] = xbuf_ref[rows, D_MODEL:D_MODEL + LANES]
            return carry

        lax.fori_loop(0, n_sub, keep, 0)

    @pl.when(n_sub > 0)
    def _():
        wgb_ref[...] = wg_ref[...].astype(BF16)
        wub_ref[...] = wu_ref[...].astype(BF16)
        wdb_ref[...] = wd_ref[...].astype(BF16)

    for k in range(1, tm // EXPERT_SUB + 1):
        @pl.when(n_sub == k)
        def _(k=k):
            _start_rows_unrolled(h_ref, src_ref, nxt * tm + s * step_rows, xbuf_ref, s * step_rows,
                                 step_rows, sem)
            e = s // f_split
            rows = slice(0, k * EXPERT_SUB)
            x = xb_ref[rows, :]
            gate_act = _dot(x, wgb_ref[...])
            up = _dot(x, wub_ref[...])
            record = rec_ref[rows, :]
            lane = lax.broadcasted_iota(jnp.int32, record.shape, 1)
            gate_col = jnp.sum(jnp.where(lane == 2 + e, record, 0.0), axis=1, keepdims=True)
            hid = (_silu(gate_act) * up * gate_col).astype(BF16)
            y_ref[rows, :] += _dot(hid, wdb_ref[...])

    @pl.when((i == n_tiles - 1) & (s == n_steps - 1) & (n_sub > 0))
    def _():
        _wait_row_gather(h_ref, xbuf_ref, sem)


def _experts(h, src, tile_lo, tile_hi, tile_nsub, w_gate, w_up, w_down, layer, tm, f_split=2):
    n_rows = src.shape[0]
    fb = D_EXPERT // f_split
    last = 2 * f_split - 1

    def expert_of(i, s, lo, hi, ns, sr):
        return jnp.where(jnp.where(ns[i] > 0, s, last) // f_split == 0, lo[i], hi[i])

    def half_of(i, s, lo, hi, ns, sr):
        return jnp.where(ns[i] > 0, s, last) % f_split

    grid_spec = pltpu.PrefetchScalarGridSpec(
        num_scalar_prefetch=4,
        grid=(n_rows // tm, 2 * f_split),
        in_specs=[
            pl.BlockSpec((None, None, D_MODEL, fb), lambda i, s, *p: (layer, expert_of(i, s, *p), 0, half_of(i, s, *p))),
            pl.BlockSpec((None, None, D_MODEL, fb), lambda i, s, *p: (layer, expert_of(i, s, *p), 0, half_of(i, s, *p))),
            pl.BlockSpec((None, None, fb, D_MODEL), lambda i, s, *p: (layer, expert_of(i, s, *p), half_of(i, s, *p), 0)),
            pl.BlockSpec(memory_space=pl.ANY),
        ],
        out_specs=pl.BlockSpec((tm, D_MODEL), lambda i, s, *_: (i, 0)),
        scratch_shapes=[
            pltpu.VMEM((tm, D_MODEL + LANES), F32),
            pltpu.VMEM((tm, D_MODEL), BF16),
            pltpu.VMEM((tm, LANES), F32),
            pltpu.VMEM((D_MODEL, fb), BF16),
            pltpu.VMEM((D_MODEL, fb), BF16),
            pltpu.VMEM((fb, D_MODEL), BF16),
            pltpu.SemaphoreType.DMA(()),
        ],
    )
    return pl.pallas_call(
        functools.partial(_experts_kernel, f_split),
        grid_spec=grid_spec,
        out_shape=jax.ShapeDtypeStruct((n_rows, D_MODEL), F32),
        compiler_params=_cparams(2),
        name="experts",
    )(tile_lo, tile_hi, tile_nsub, src, w_gate, w_up, w_down, h)


def _combine_kernel(final_norm, n_first, pos_ref, x_ref, gate_ref, fg_ref, ys_ref, *refs):
    o_refs, buf_ref, sem = refs[:-2], refs[-2], refs[-1]
    tm, d = x_ref.shape
    i = pl.program_id(0)
    slot = i % 2

    @pl.when(i == 0)
    def _():
        _start_row_gather(ys_ref, pos_ref, 0, buf_ref.at[0], sem.at[0])

    _wait_row_gather(ys_ref, buf_ref.at[slot], sem.at[slot])
    n_tiles = pl.num_programs(0)
    nxt = jnp.minimum(i + 1, n_tiles - 1)
    _start_rows_unrolled(ys_ref, pos_ref, nxt * tm, buf_ref.at[1 - slot], 0, tm, sem.at[1 - slot])

    nb = tm // ROW_BLOCK
    upd = buf_ref[slot].reshape(nb, ROW_BLOCK, d) * gate_ref[...][:, None, :]
    x = x_ref[...] + upd.reshape(tm, d)
    if final_norm:
        x = x * lax.rsqrt(jnp.mean(x * x, axis=-1, keepdims=True) + EPS) * fg_ref[...]
    if n_first is None:
        o_refs[0][...] = x
    else:
        @pl.when(pl.program_id(0) < n_first)
        def _():
            o_refs[0][...] = x

        @pl.when(pl.program_id(0) >= n_first)
        def _():
            o_refs[1][...] = x

    @pl.when(i == n_tiles - 1)
    def _():
        _wait_row_gather(ys_ref, buf_ref.at[1 - slot], sem.at[1 - slot])


def _combine(x, ys, pos, modb, final_g, final_norm, split_rows=None, tm=512):
    t_rows = x.shape[0]
    nb = tm // ROW_BLOCK
    if split_rows is None:
        n_first = None
        out_specs = pl.BlockSpec((tm, D_MODEL), lambda i, p: (i, 0))
        out_shape = jax.ShapeDtypeStruct((t_rows, D_MODEL), F32)
    else:
        assert split_rows % tm == 0 and (t_rows - split_rows) % tm == 0
        n_first = split_rows // tm
        out_specs = [pl.BlockSpec((tm, D_MODEL), lambda i, p: (jnp.minimum(i, n_first - 1), 0)),
                     pl.BlockSpec((tm, D_MODEL), lambda i, p: (jnp.maximum(i - n_first, 0), 0))]
        out_shape = [jax.ShapeDtypeStruct((split_rows, D_MODEL), F32),
                     jax.ShapeDtypeStruct((t_rows - split_rows, D_MODEL), F32)]
    grid_spec = pltpu.PrefetchScalarGridSpec(
        num_scalar_prefetch=1,
        grid=(t_rows // tm,),
        in_specs=[
            pl.BlockSpec((tm, D_MODEL), lambda i, p: (i, 0)),
            pl.BlockSpec((nb, D_MODEL), lambda i, p: (i, 5)),
            pl.BlockSpec((1, D_MODEL), lambda i, p: (0, 0)),
            pl.BlockSpec(memory_space=pl.ANY),
        ],
        out_specs=out_specs,
        scratch_shapes=[pltpu.VMEM((2, tm, D_MODEL), F32), pltpu.SemaphoreType.DMA((2,))],
    )
    return pl.pallas_call(
        functools.partial(_combine_kernel, final_norm, n_first),
        grid_spec=grid_spec,
        out_shape=out_shape,
        compiler_params=_cparams(1),
        name="combine",
    )(pos, x, modb, final_g.reshape(1, D_MODEL), ys)


def kernel(x_prompt, x_sample, state_ssm, state_conv, c_prompt, c_sample, norm1_g, norm2_g, w_ada, b_ada,
           w_in, gm_ln_g, gm_ln_b, gm_w_s, gm_b_s, conv_w, conv_b, dt_bias, a_log, d_skip, mnorm_g, w_out,
           w_router_group, b_router_group, w_router_expert, b_router_expert, w_gate, w_up, w_down,
           final_norm_g):
    n_p, len_p, _ = x_prompt.shape
    n_s, len_s, _ = x_sample.shape
    assert len_p % GM_TILE == 0 and len_s == ROW_BLOCK and (n_s * len_s) % GM_TILE == 0
    rows_p, rows_s = n_p * len_p, n_s * len_s
    t_rows = rows_p + rows_s
    n_seq = n_p + n_s
    blk_per_prompt = len_p // ROW_BLOCK
    assert t_rows % EXPERT_TILE == 0
    n_expert_tiles = t_rows // EXPERT_TILE + N_BUCKETS

    xs = (x_prompt.reshape(rows_p, D_MODEL), x_sample.reshape(rows_s, D_MODEL))

    c_all = jnp.concatenate([c_prompt, c_sample], axis=0)
    c_pad = jnp.pad(c_all, ((0, (-n_seq) % 8), (0, 0)))
    mod = _ada(c_pad, w_ada, b_ada)
    modb_all = jnp.concatenate(
        [jnp.repeat(mod[:, :n_p], blk_per_prompt, axis=1), mod[:, n_p:n_seq]], axis=1)

    half_w = gm_w_s[:, :, :ROW_BLOCK, :ROW_BLOCK]
    w_cat = jnp.stack([gm_w_s, jnp.tile(half_w, (1, 1, 2, 2))], axis=1)
    b_t = jnp.swapaxes(gm_b_s, 1, 2)
    b_cat = jnp.stack([b_t, jnp.tile(b_t[:, :ROW_BLOCK], (1, 2, 1))], axis=1)

    w_in_t = jnp.swapaxes(w_in, 1, 2)
    w_r = jnp.concatenate([w_router_group, w_router_expert], axis=2)
    w_r = jnp.pad(w_r, ((0, 0), (0, 0), (0, LANES - w_r.shape[2])))
    b_r = jnp.concatenate([b_router_group, b_router_expert], axis=1)
    b_r = jnp.pad(b_r, ((0, 0), (0, LANES - b_r.shape[1])))[:, None, :]

    ssm_p, conv_p, ssm_s, conv_s, v_rows = [], [], [], [], []
    for l in range(DEPTH):
        last = l == DEPTH - 1
        modb = modb_all[l]
        proj, dt_raw = _inproj(xs, norm1_g, modb, w_in_t, l, tn=512 if len(xs) == 2 else 1024)
        gm_out, vn = _gmlp(proj, gm_ln_g, gm_ln_b, w_cat, b_cat, l, rows_p // GM_TILE, rows_s)
        ssm_args = (conv_w, conv_b, dt_bias, a_log, d_skip, mnorm_g, l)
        ssm_out_p, hfin_p, cfin_p = _ssd(proj, dt_raw, None, None, *ssm_args, 0, n_p, len_p, SSD_PROMPT_CHUNK)
        ssm_out_s, hfin_s, cfin_s = _ssd(proj, dt_raw, state_ssm, state_conv, *ssm_args, rows_p, n_s, len_s, len_s)
        x = _outproj(gm_out, (ssm_out_p, ssm_out_s), w_out, xs, modb, l)
        h2, route, counts = _router(x, norm2_g, modb, w_r[l], b_r[l], l)
        pos, src, tile_lo, tile_hi, tile_nsub = _dispatch_tables(route, counts, t_rows, EXPERT_TILE,
                                                                 n_expert_tiles)
        ys = _experts(h2, src, tile_lo, tile_hi, tile_nsub, w_gate, w_up, w_down, l, EXPERT_TILE)
        out = _combine(x, ys, pos, modb, final_norm_g, last, rows_p if last else None)
        xs = tuple(out) if last else (out,)
        ssm_p.append(hfin_p)
        conv_p.append(cfin_p)
        ssm_s.append(hfin_s)
        conv_s.append(cfin_s)
        v_rows.append(vn.reshape(n_s, len_s, GM_WIDTH))

    y_prompt = xs[0].reshape(n_p, len_p, D_MODEL)
    y_sample = xs[1].reshape(n_s, len_s, D_MODEL)
    return (y_prompt, y_sample, jnp.stack(ssm_p), jnp.stack(conv_p), jnp.stack(ssm_s), jnp.stack(conv_s),
            jnp.stack(v_rows))
```

```python
import functools
import math

import jax
import jax.numpy as jnp
import numpy as np
from jax import lax
from jax.experimental import pallas as pl
from jax.experimental.pallas import tpu as pltpu

D_MODEL = 2048
DEPTH = 2
GM_WIDTH = 1024
GM_HEADS = 8
GM_HEAD_DIM = 128
GM_CHUNK = 128
M_WIDTH = 1024
M_HEAD_DIM = 64
M_HEADS = 16
M_GROUPS = 4
M_STATE = 128
M_CONV = 4
CONV_DIM = M_WIDTH + 2 * M_GROUPS * M_STATE
MAIN_DIM = 2 * GM_WIDTH + M_WIDTH + CONV_DIM
N_EGROUPS = 4
EXPERTS_PER_GROUP = 4
N_EXPERTS = 16
D_EXPERT = 512
N_MOD = 6
EPS = 1e-6

ROW_BLOCK = 64
LANES = 128
VMEM_LIMIT = 56 * 1024 * 1024

F32 = jnp.float32
BF16 = jnp.bfloat16


def _cparams(n_axes):
    return pltpu.CompilerParams(
        dimension_semantics=("arbitrary",) * n_axes, vmem_limit_bytes=VMEM_LIMIT)


def _silu(x):
    return x * jax.nn.sigmoid(x)


def _gelu_tanh(x):
    c = math.sqrt(2.0 / math.pi)
    return 0.5 * x * (1.0 + jnp.tanh(c * (x + 0.044715 * (x * x * x))))


def _dot(a, b):
    return jnp.dot(a, b, preferred_element_type=F32)


def _dot_nt(a, b, precision=None):
    return lax.dot_general(a, b, (((1,), (1,)), ((), ())), precision=precision,
                           preferred_element_type=F32)


def _dot_split2(a, b):
    a0 = a.astype(BF16)
    a1 = (a - a0.astype(F32)).astype(BF16)
    b0 = b.astype(BF16)
    b1 = (b - b0.astype(F32)).astype(BF16)
    return _dot(a0, b0) + _dot(a0, b1) + _dot(a1, b0)


def _dot_tn(a, b):
    return lax.dot_general(a, b, (((0,), (0,)), ((), ())), preferred_element_type=F32)


def _ada_kernel(c_ref, w_ref, b_ref, o_ref):
    c = _silu(c_ref[...]).astype(BF16)
    o_ref[...] = _dot(c, w_ref[...].astype(BF16)) + b_ref[...]


def _ada(c_pad, w_ada, b_ada):
    rows = c_pad.shape[0]
    tn = 1024
    n_out = N_MOD * D_MODEL
    return pl.pallas_call(
        _ada_kernel,
        grid=(DEPTH, n_out // tn),
        in_specs=[
            pl.BlockSpec((rows, D_MODEL), lambda l, j: (0, 0)),
            pl.BlockSpec((None, D_MODEL, tn), lambda l, j: (l, 0, j)),
            pl.BlockSpec((None, 1, tn), lambda l, j: (l, 0, j)),
        ],
        out_specs=pl.BlockSpec((None, rows, tn), lambda l, j: (l, 0, j)),
        out_shape=jax.ShapeDtypeStruct((DEPTH, rows, n_out), F32),
        compiler_params=_cparams(2),
        name="ada_ln",
    )(c_pad, w_ada, b_ada.reshape(DEPTH, 1, n_out))


def _stream_specs(xs, block, col_of, n_j):
    if len(xs) == 1:
        return [pl.BlockSpec(block, lambda i, j: (i, col_of(j)))], None
    n_first = xs[0].shape[0] // block[0]

    def first(i, j):
        return jnp.minimum(i, n_first - 1), jnp.where(i < n_first, col_of(j), col_of(n_j - 1))

    def second(i, j):
        return jnp.maximum(i - n_first, 0), jnp.where(i < n_first, col_of(0), col_of(j))

    return [pl.BlockSpec(block, first), pl.BlockSpec(block, second)], n_first


def _stream_tile(x_refs, n_first, idx):
    if n_first is None:
        return x_refs[0][idx]
    return jnp.where(pl.program_id(0) < n_first, x_refs[0][idx], x_refs[1][idx])


def _inproj_kernel(n_first, *refs):
    n_x = 1 if n_first is None else 2
    x_refs = refs[:n_x]
    g_ref, shift_ref, scale_ref, w_ref, wdt_ref, proj_ref, dt_ref, h_ref = refs[n_x:]
    j = pl.program_id(1)
    n_blk = shift_ref.shape[0]

    @pl.when(j == 0)
    def _():
        g = g_ref[...]

        def body(r, carry):
            rows = pl.ds(pl.multiple_of(r * ROW_BLOCK, ROW_BLOCK), ROW_BLOCK)
            xf = _stream_tile(x_refs, n_first, (rows, slice(None)))
            y = xf * lax.rsqrt(jnp.mean(xf * xf, axis=-1, keepdims=True) + EPS) * g
            y = y * (1.0 + scale_ref[pl.ds(r, 1), :]) + shift_ref[pl.ds(r, 1), :]
            h_ref[rows, :] = y.astype(BF16)
            return carry

        lax.fori_loop(0, n_blk, body, 0)
        dt_ref[...] = _dot_nt(h_ref[...], wdt_ref[...].astype(BF16))

    proj_ref[...] = _dot_nt(h_ref[...], w_ref[...].astype(BF16)).astype(BF16)


def _inproj(xs, norm_g, modb, w_in_t, layer, tm=1024, tn=512):
    t_rows = sum(x.shape[0] for x in xs)
    nb = tm // ROW_BLOCK
    assert all(x.shape[0] % tm == 0 for x in xs)
    x_specs, n_first = _stream_specs(xs, (tm, D_MODEL), lambda j: 0, MAIN_DIM // tn)
    return pl.pallas_call(
        functools.partial(_inproj_kernel, n_first),
        grid=(t_rows // tm, MAIN_DIM // tn),
        in_specs=x_specs + [
            pl.BlockSpec((None, 1, D_MODEL), lambda i, j: (layer, 0, 0)),
            pl.BlockSpec((nb, D_MODEL), lambda i, j: (i, 0)),
            pl.BlockSpec((nb, D_MODEL), lambda i, j: (i, 1)),
            pl.BlockSpec((None, tn, D_MODEL), lambda i, j: (layer, j, 0)),
            pl.BlockSpec((None, M_HEADS, D_MODEL), lambda i, j: (layer, MAIN_DIM // M_HEADS, 0)),
        ],
        out_specs=[
            pl.BlockSpec((tm, tn), lambda i, j: (i, j)),
            pl.BlockSpec((tm, M_HEADS), lambda i, j: (i, 0)),
        ],
        out_shape=[
            jax.ShapeDtypeStruct((t_rows, MAIN_DIM), BF16),
            jax.ShapeDtypeStruct((t_rows, M_HEADS), F32),
        ],
        scratch_shapes=[pltpu.VMEM((tm, D_MODEL), BF16)],
        compiler_params=_cparams(2),
        name="in_proj",
    )(*xs, norm_g.reshape(DEPTH, 1, D_MODEL), modb, modb, w_in_t, w_in_t)


def _gmlp_kernel(n_prompt_tiles, u_ref, v_ref, lng_ref, lnb_ref, w_ref, b_ref, out_ref, vn_ref):
    i = pl.program_id(0)
    t_idx = lax.broadcasted_iota(jnp.int32, (GM_CHUNK, GM_CHUNK), 0)
    s_idx = lax.broadcasted_iota(jnp.int32, (GM_CHUNK, GM_CHUNK), 1)
    same_seq = (t_idx // ROW_BLOCK) == (s_idx // ROW_BLOCK)
    allowed = (s_idx <= t_idx) & (same_seq | (i < n_prompt_tiles))
    w_masked = [jnp.where(allowed, w_ref[h], 0.0).astype(BF16) for h in range(GM_HEADS)]
    bias = b_ref[...]
    for r0 in range(0, u_ref.shape[0], GM_CHUNK):
        rows = slice(r0, r0 + GM_CHUNK)
        u = _gelu_tanh(u_ref[rows, :].astype(F32))
        v = _gelu_tanh(v_ref[rows, :].astype(F32))
        mu = jnp.mean(v, axis=-1, keepdims=True)
        vc = v - mu
        vn = vc * lax.rsqrt(jnp.mean(vc * vc, axis=-1, keepdims=True) + EPS)
        vn = vn * lng_ref[...] + lnb_ref[...]
        vn_ref[rows, :] = vn
        vnb = vn.astype(BF16)
        for h in range(GM_HEADS):
            cols = slice(h * GM_HEAD_DIM, (h + 1) * GM_HEAD_DIM)
            mixed = _dot(w_masked[h], vnb[:, cols]) + bias[:, h:h + 1]
            out_ref[rows, cols] = (u[:, cols] * mixed).astype(BF16)


def _gmlp(proj, ln_g, ln_b, w_cat, b_cat, layer, n_prompt_tiles, n_sample_rows):
    t_rows = proj.shape[0]
    n_tiles = t_rows // GM_TILE
    kern = functools.partial(_gmlp_kernel, n_prompt_tiles)
    return pl.pallas_call(
        kern,
        grid=(n_tiles,),
        in_specs=[
            pl.BlockSpec((GM_TILE, GM_WIDTH), lambda i: (i, 0)),
            pl.BlockSpec((GM_TILE, GM_WIDTH), lambda i: (i, 1)),
            pl.BlockSpec((None, 1, GM_WIDTH), lambda i: (layer, 0, 0)),
            pl.BlockSpec((None, 1, GM_WIDTH), lambda i: (layer, 0, 0)),
            pl.BlockSpec((None, None, GM_HEADS, GM_CHUNK, GM_CHUNK),
                         lambda i: (layer, jnp.where(i < n_prompt_tiles, 0, 1), 0, 0, 0)),
            pl.BlockSpec((None, None, GM_CHUNK, GM_HEADS),
                         lambda i: (layer, jnp.where(i < n_prompt_tiles, 0, 1), 0, 0)),
        ],
        out_specs=[
            pl.BlockSpec((GM_TILE, GM_WIDTH), lambda i: (i, 0)),
            pl.BlockSpec((GM_TILE, GM_WIDTH), lambda i: (jnp.maximum(i - n_prompt_tiles, 0), 0)),
        ],
        out_shape=[
            jax.ShapeDtypeStruct((t_rows, GM_WIDTH), BF16),
            jax.ShapeDtypeStruct((n_sample_rows, GM_WIDTH), F32),
        ],
        compiler_params=_cparams(1),
        name="gmlp",
    )(proj, proj, ln_g.reshape(DEPTH, 1, GM_WIDTH), ln_b.reshape(DEPTH, 1, GM_WIDTH), w_cat, b_cat)


GM_TILE = 4 * GM_CHUNK
CONV_PAD = 8
SSD_PROMPT_CHUNK = 256


def _widen(v, onehot):
    p0 = v.astype(BF16)
    r1 = v - p0.astype(F32)
    p1 = r1.astype(BF16)
    p2 = (r1 - p1.astype(F32)).astype(BF16)
    return _dot(p0, onehot) + _dot(p1, onehot) + _dot(p2, onehot)


def _ssd_kernel(q, zero_init, *refs):
    z_ref, xa_ref, xb_ref, dt_ref = refs[:4]
    refs = refs[4:]
    h0_ref = c0_ref = None
    if not zero_init:
        h0_ref, c0_ref = refs[:2]
        refs = refs[2:]
    cw_ref, cb_ref, dtb_ref, alog_ref, dskip_ref, ng_ref = refs[:6]
    refs = refs[6:]
    out_ref, hfin_ref, cfin_ref, state_ref, win_ref, xs_ref, bc_ref, xdt_ref, xdd_ref, y_ref = refs
    c = pl.program_id(1)
    tail = M_CONV - 1
    lo = CONV_PAD - tail
    hp = M_HEADS * M_HEAD_DIM

    @pl.when(c == 0)
    def _():
        if zero_init:
            state_ref[...] = jnp.zeros_like(state_ref)
            win_ref[lo:CONV_PAD, :] = jnp.zeros((tail, CONV_DIM), F32)
        else:
            state_ref[...] = h0_ref[...].reshape(hp, M_STATE).T
            win_ref[lo:CONV_PAD, :] = c0_ref[...]

    half = CONV_DIM // 2
    win_ref[CONV_PAD:CONV_PAD + q, 0:half] = xa_ref[...].astype(F32)
    win_ref[CONV_PAD:CONV_PAD + q, half:CONV_DIM] = xb_ref[...].astype(F32)
    for r0 in range(0, q, ROW_BLOCK):
        conv = cb_ref[...]
        for k in range(M_CONV):
            conv = conv + win_ref[lo + k + r0:lo + k + r0 + ROW_BLOCK, :] * cw_ref[k:k + 1, :]
        act = _silu(conv)
        xs_ref[r0:r0 + ROW_BLOCK, :] = act[:, 0:M_WIDTH]
        bc_ref[r0:r0 + ROW_BLOCK, :] = act[:, M_WIDTH:CONV_DIM].astype(BF16)
    new_tail = win_ref[CONV_PAD + q - tail:CONV_PAD + q, :]
    cfin_ref[...] = new_tail
    win_ref[lo:CONV_PAD, :] = new_tail

    hi = lax.Precision.HIGHEST
    dt = jax.nn.softplus(dt_ref[...] + dtb_ref[...])
    a = -jnp.exp(alog_ref[...])
    r_idx = lax.broadcasted_iota(jnp.int32, (q, q), 0)
    c_idx = lax.broadcasted_iota(jnp.int32, (q, q), 1)
    causal = c_idx <= r_idx
    a_cs = jnp.dot(causal.astype(F32), dt * a, precision=hi, preferred_element_type=F32)
    eye = (lax.broadcasted_iota(jnp.int32, (M_HEADS, M_HEADS), 0)
           == lax.broadcasted_iota(jnp.int32, (M_HEADS, M_HEADS), 1)).astype(F32)
    a_cs_t = _dot_nt(eye, a_cs, precision=hi)
    widen = (lax.broadcasted_iota(jnp.int32, (M_HEADS, hp), 1) // M_HEAD_DIM
             == lax.broadcasted_iota(jnp.int32, (M_HEADS, hp), 0)).astype(BF16)
    dt_w = _widen(dt, widen)
    acs_w = _widen(a_cs, widen)
    dskip_w = _widen(jnp.broadcast_to(dskip_ref[...], (8, M_HEADS)), widen)[0:1, :]
    a_end_w = acs_w[q - 1:q, :]
    x_dt = xs_ref[...] * dt_w
    xdt_ref[...] = x_dt.astype(BF16)
    xdd_ref[...] = (x_dt * jnp.exp(a_end_w - acs_w)).astype(BF16)
    decay_blk_w = jnp.exp(a_end_w)

    heads_per_group = M_HEADS // M_GROUPS
    gw = M_WIDTH // M_GROUPS
    pair_lane = lax.broadcasted_iota(jnp.int32, (q, 2 * M_HEAD_DIM), 1)
    for g in range(M_GROUPS):
        gcols = slice(g * gw, (g + 1) * gw)
        b_g = bc_ref[:, g * M_STATE:(g + 1) * M_STATE]
        c_g = bc_ref[:, (M_GROUPS + g) * M_STATE:(M_GROUPS + g + 1) * M_STATE]
        cb = _dot_nt(c_g, b_g)
        st = state_ref[:, gcols]
        y_off = _dot(c_g, st.astype(BF16))
        state_ref[:, gcols] = st * decay_blk_w[:, gcols] + _dot_tn(b_g, xdd_ref[:, gcols])
        for pr in range(heads_per_group // 2):
            h0 = g * heads_per_group + 2 * pr
            pcols = slice(h0 * M_HEAD_DIM, (h0 + 2) * M_HEAD_DIM)
            x_pair = xdt_ref[:, pcols]
            ys = []
            for h in (h0, h0 + 1):
                seg = a_cs[:, h:h + 1] - a_cs_t[h:h + 1, :]
                scores = cb * jnp.exp(jnp.where(causal, seg, -jnp.inf))
                ys.append(_dot(scores.astype(BF16), x_pair))
            y_ref[:, pcols] = jnp.where(pair_lane < M_HEAD_DIM, ys[0], ys[1])
        y_ref[:, gcols] = (y_ref[:, gcols] + y_off * jnp.exp(acs_w[:, gcols])
                           + dskip_w[:, gcols] * xs_ref[:, gcols])

    @pl.when(c == pl.num_programs(1) - 1)
    def _():
        hfin_ref[...] = state_ref[...].T.reshape(M_HEADS, M_HEAD_DIM, M_STATE)

    gy = y_ref[...] * _silu(z_ref[...].astype(F32))
    ng = ng_ref[...]
    for g in range(M_GROUPS):
        cols = slice(g * gw, (g + 1) * gw)
        s = gy[:, cols]
        s = s * lax.rsqrt(jnp.mean(s * s, axis=-1, keepdims=True) + EPS)
        out_ref[:, cols] = (s * ng[:, cols]).astype(BF16)


def _ssd(proj, dt_raw, h0, c0, conv_w, conv_b, dt_bias, a_log, d_skip, mnorm_g, layer,
         row_start, n_seq, seq_len, q):
    n_c = seq_len // q
    blk0 = row_start // q
    half = CONV_DIM // 2
    z_blk = (2 * GM_WIDTH) // M_WIDTH
    xa_blk = (2 * GM_WIDTH + M_WIDTH) // half
    zero_init = h0 is None

    def rows(col):
        return lambda s, c: (blk0 + s * n_c + c, col)

    def per_layer(*shape):
        return pl.BlockSpec((None,) + shape, lambda s, c: (layer,) + (0,) * len(shape))

    in_specs = [
        pl.BlockSpec((q, M_WIDTH), rows(z_blk)),
        pl.BlockSpec((q, half), rows(xa_blk)),
        pl.BlockSpec((q, half), rows(xa_blk + 1)),
        pl.BlockSpec((q, M_HEADS), rows(0)),
    ]
    args = [proj, proj, proj, dt_raw]
    if not zero_init:
        in_specs += [
            pl.BlockSpec((None, None, M_HEADS, M_HEAD_DIM, M_STATE), lambda s, c: (layer, s, 0, 0, 0)),
            pl.BlockSpec((None, None, M_CONV - 1, CONV_DIM), lambda s, c: (layer, s, 0, 0)),
        ]
        args += [h0, c0]
    in_specs += [per_layer(M_CONV, CONV_DIM), per_layer(1, CONV_DIM), per_layer(1, M_HEADS),
                 per_layer(1, M_HEADS), per_layer(1, M_HEADS), per_layer(1, M_WIDTH)]
    args += [conv_w, conv_b.reshape(DEPTH, 1, CONV_DIM), dt_bias.reshape(DEPTH, 1, M_HEADS),
             a_log.reshape(DEPTH, 1, M_HEADS), d_skip.reshape(DEPTH, 1, M_HEADS),
             mnorm_g.reshape(DEPTH, 1, M_WIDTH)]
    return pl.pallas_call(
        functools.partial(_ssd_kernel, q, zero_init),
        grid=(n_seq, n_c),
        in_specs=in_specs,
        out_specs=[
            pl.BlockSpec((q, M_WIDTH), lambda s, c: (s * n_c + c, 0)),
            pl.BlockSpec((None, M_HEADS, M_HEAD_DIM, M_STATE), lambda s, c: (s, 0, 0, 0)),
            pl.BlockSpec((None, M_CONV - 1, CONV_DIM), lambda s, c: (s, 0, 0)),
        ],
        out_shape=[
            jax.ShapeDtypeStruct((n_seq * seq_len, M_WIDTH), BF16),
            jax.ShapeDtypeStruct((n_seq, M_HEADS, M_HEAD_DIM, M_STATE), F32),
            jax.ShapeDtypeStruct((n_seq, M_CONV - 1, CONV_DIM), F32),
        ],
        scratch_shapes=[
            pltpu.VMEM((M_STATE, M_WIDTH), F32),
            pltpu.VMEM((CONV_PAD + q, CONV_DIM), F32),
            pltpu.VMEM((q, M_WIDTH), F32),
            pltpu.VMEM((q, 2 * M_GROUPS * M_STATE), BF16),
            pltpu.VMEM((q, M_WIDTH), BF16),
            pltpu.VMEM((q, M_WIDTH), BF16),
            pltpu.VMEM((q, M_WIDTH), F32),
        ],
        compiler_params=_cparams(2),
        name="conv_ssd",
    )(*args)


def _outproj_kernel(n_first_ssm, n_first, gm_ref, wa_ref, wb_ref, gate_ref, *refs):
    n_ssm = 1 if n_first_ssm is None else 2
    ssm_refs, x_refs, o_ref = refs[:n_ssm], refs[n_ssm:-1], refs[-1]
    ssm = _stream_tile(ssm_refs, n_first_ssm, Ellipsis)
    acc = _dot(gm_ref[...], wa_ref[...].astype(BF16)) + _dot(ssm, wb_ref[...].astype(BF16))
    tm, tn = acc.shape
    nb = tm // ROW_BLOCK
    upd = acc.reshape(nb, ROW_BLOCK, tn) * gate_ref[...][:, None, :]
    o_ref[...] = _stream_tile(x_refs, n_first, Ellipsis) + upd.reshape(tm, tn)


def _outproj(gm, ssms, w_out, xs, modb, layer, tm=1024, tn=512):
    t_rows = gm.shape[0]
    nb = tm // ROW_BLOCK
    gate_blk = 2 * D_MODEL // tn
    assert all(x.shape[0] % tm == 0 for x in xs)
    assert all(x.shape[0] % tm == 0 for x in ssms)
    x_specs, n_first = _stream_specs(xs, (tm, tn), lambda j: j, D_MODEL // tn)
    ssm_specs, n_first_ssm = _stream_specs(ssms, (tm, M_WIDTH), lambda j: 0, D_MODEL // tn)
    return pl.pallas_call(
        functools.partial(_outproj_kernel, n_first_ssm, n_first),
        grid=(t_rows // tm, D_MODEL // tn),
        in_specs=[
            pl.BlockSpec((tm, GM_WIDTH), lambda i, j: (i, 0)),
            pl.BlockSpec((None, GM_WIDTH, tn), lambda i, j: (layer, 0, j)),
            pl.BlockSpec((None, M_WIDTH, tn), lambda i, j: (layer, 1, j)),
            pl.BlockSpec((nb, tn), lambda i, j: (i, gate_blk + j)),
        ] + ssm_specs + x_specs,
        out_specs=pl.BlockSpec((tm, tn), lambda i, j: (i, j)),
        out_shape=jax.ShapeDtypeStruct((t_rows, D_MODEL), F32),
        compiler_params=_cparams(2),
        name="out_proj",
    )(gm, w_out, w_out, modb, *ssms, *xs)


ROUTE_ROWS = 8
PAIRS_PER_GROUP = EXPERTS_PER_GROUP * (EXPERTS_PER_GROUP - 1) // 2
N_BUCKETS = N_EGROUPS * PAIRS_PER_GROUP
BUCKET_ROWS = 32
EXPERT_TILE = 512
EXPERT_SUB = 128
PAIR_LO = (0, 0, 0, 1, 1, 2)
PAIR_HI = (1, 2, 3, 2, 3, 3)


def _row_argmax_first(rows):
    m = rows[0]
    for r in rows[1:]:
        m = jnp.maximum(m, r)
    idx = jnp.full(m.shape, len(rows), jnp.int32)
    for k in reversed(range(len(rows))):
        idx = jnp.where(rows[k] == m, k, idx)
    return m, idx


def _router_kernel(x_ref, g_ref, shift_ref, scale_ref, wr_ref, br_ref, h_ref, route_ref, count_ref, run_ref):
    n_blk = shift_ref.shape[0]
    g = g_ref[...]

    @pl.when(pl.program_id(0) == 0)
    def _():
        run_ref[...] = jnp.zeros_like(run_ref)

    def body(r, carry):
        rows = pl.ds(pl.multiple_of(r * ROW_BLOCK, ROW_BLOCK), ROW_BLOCK)
        xf = x_ref[rows, :]
        y = xf * lax.rsqrt(jnp.mean(xf * xf, axis=-1, keepdims=True) + EPS) * g
        h_ref[rows, 0:D_MODEL] = y * (1.0 + scale_ref[pl.ds(r, 1), :]) + shift_ref[pl.ds(r, 1), :]
        return carry

    lax.fori_loop(0, n_blk, body, 0)
    logits = (_dot_split2(h_ref[:, 0:D_MODEL], wr_ref[...]) + br_ref[...]).T
    lg = [logits[k:k + 1, :] for k in range(N_EGROUPS)]
    g_max, g_idx = _row_argmax_first(lg)
    g_den = lg[0] * 0.0
    for r in lg:
        g_den = g_den + jnp.exp(r - g_max)
    g_top = 1.0 / g_den
    e_in = []
    for k in range(EXPERTS_PER_GROUP):
        sel = lg[0] * 0.0
        for gi in range(N_EGROUPS):
            row = N_EGROUPS + gi * EXPERTS_PER_GROUP + k
            sel = jnp.where(g_idx == gi, logits[row:row + 1, :], sel)
        e_in.append(sel)
    e_max, i1 = _row_argmax_first(e_in)
    e_exp = [jnp.exp(r - e_max) for r in e_in]
    e_den = e_exp[0]
    for r in e_exp[1:]:
        e_den = e_den + r
    p = [r / e_den for r in e_exp]
    p1 = p[0] * 0.0
    for k in range(EXPERTS_PER_GROUP):
        p1 = jnp.where(i1 == k, p[k], p1)
    rest = [jnp.where(i1 == k, -1.0, p[k]) for k in range(EXPERTS_PER_GROUP)]
    p2, i2 = _row_argmax_first(rest)
    norm = p1 + p2
    w1 = p1 / norm * g_top
    w2 = p2 / norm * g_top
    e_lo = jnp.minimum(i1, i2)
    e_hi = jnp.maximum(i1, i2)
    pair = jnp.where(e_lo == 0, e_hi - 1, jnp.where(e_lo == 1, e_hi + 1, PAIRS_PER_GROUP - 1))
    bucket = g_idx * PAIRS_PER_GROUP + pair
    w_lo = jnp.where(i1 < i2, w1, w2)
    w_hi = jnp.where(i1 < i2, w2, w1)
    tm = logits.shape[1]
    row = lax.broadcasted_iota(jnp.int32, (BUCKET_ROWS, tm), 0)
    onehot = (row == bucket).astype(F32)
    before = (lax.broadcasted_iota(jnp.int32, (tm, tm), 0)
              < lax.broadcasted_iota(jnp.int32, (tm, tm), 1)).astype(BF16)
    earlier = _dot(onehot.astype(BF16), before)
    rank = jnp.sum(onehot * (earlier + run_ref[:, 0:1]), axis=0, keepdims=True)
    run_ref[...] += jnp.sum(onehot, axis=1, keepdims=True)
    count_ref[...] = run_ref[...]

    route_ref[0:1, :] = bucket.astype(F32)
    route_ref[1:2, :] = rank
    route_ref[2:3, :] = w_lo
    route_ref[3:4, :] = w_hi
    route_ref[4:ROUTE_ROWS, :] = jnp.zeros((ROUTE_ROWS - 4, tm), F32)
    place = (lax.broadcasted_iota(jnp.int32, (ROUTE_ROWS, LANES), 0)
             == lax.broadcasted_iota(jnp.int32, (ROUTE_ROWS, LANES), 1)).astype(F32)
    h_ref[:, D_MODEL:D_MODEL + LANES] = lax.dot_general(
        route_ref[...], place, (((0,), (0,)), ((), ())), precision=lax.Precision.HIGHEST,
        preferred_element_type=F32)


def _router(x, norm_g, modb, w_r, b_r, layer, tm=512):
    t_rows = x.shape[0]
    nb = tm // ROW_BLOCK
    return pl.pallas_call(
        _router_kernel,
        grid=(t_rows // tm,),
        in_specs=[
            pl.BlockSpec((tm, D_MODEL), lambda i: (i, 0)),
            pl.BlockSpec((None, 1, D_MODEL), lambda i: (layer, 0, 0)),
            pl.BlockSpec((nb, D_MODEL), lambda i: (i, 3)),
            pl.BlockSpec((nb, D_MODEL), lambda i: (i, 4)),
            pl.BlockSpec((D_MODEL, LANES), lambda i: (0, 0)),
            pl.BlockSpec((1, LANES), lambda i: (0, 0)),
        ],
        out_specs=[
            pl.BlockSpec((tm, D_MODEL + LANES), lambda i: (i, 0)),
            pl.BlockSpec((ROUTE_ROWS, tm), lambda i: (0, i)),
            pl.BlockSpec((BUCKET_ROWS, LANES), lambda i: (0, 0)),
        ],
        out_shape=[
            jax.ShapeDtypeStruct((t_rows, D_MODEL + LANES), F32),
            jax.ShapeDtypeStruct((ROUTE_ROWS, t_rows), F32),
            jax.ShapeDtypeStruct((BUCKET_ROWS, LANES), F32),
        ],
        scratch_shapes=[pltpu.VMEM((BUCKET_ROWS, LANES), F32)],
        compiler_params=_cparams(1),
        name="router",
    )(x, norm_g.reshape(DEPTH, 1, D_MODEL), modb, modb, w_r, b_r)


def _dispatch_tables(route, counts, t_rows, tm, n_tiles):
    gid = route[0].astype(jnp.int32)
    rank = route[1].astype(jnp.int32)
    cnt = counts[:N_BUCKETS, 0].astype(jnp.int32)
    padded = (cnt + tm - 1) // tm * tm
    ends = jnp.cumsum(padded)
    starts = ends - padded
    pos = starts[gid] + rank
    src = jnp.zeros((n_tiles * tm,), jnp.int32).at[pos].set(jnp.arange(t_rows, dtype=jnp.int32))
    tile_start = jnp.arange(n_tiles, dtype=jnp.int32) * tm
    tile_bucket = jnp.sum((tile_start[:, None] >= ends[None, :]).astype(jnp.int32), axis=1)
    tile_bucket = jnp.minimum(tile_bucket, N_BUCKETS - 1)
    tile_rows = jnp.clip(cnt[tile_bucket] - (tile_start - starts[tile_bucket]), 0, tm)
    tile_nsub = (tile_rows + EXPERT_SUB - 1) // EXPERT_SUB
    first = (tile_bucket // PAIRS_PER_GROUP) * EXPERTS_PER_GROUP
    tile_lo = first + jnp.asarray(PAIR_LO, jnp.int32)[tile_bucket % PAIRS_PER_GROUP]
    tile_hi = first + jnp.asarray(PAIR_HI, jnp.int32)[tile_bucket % PAIRS_PER_GROUP]
    return pos, src, tile_lo, tile_hi, tile_nsub


def _start_row_gather(table_ref, idx_ref, idx_base, buf_ref, sem):
    def issue(r, carry):
        pltpu.make_async_copy(table_ref.at[pl.ds(idx_ref[idx_base + r], 1), :],
                              buf_ref.at[pl.ds(r, 1), :], sem).start()
        return carry

    lax.fori_loop(0, buf_ref.shape[0], issue, 0, unroll=8)


def _start_rows_unrolled(table_ref, idx_ref, idx_base, buf_ref, row0, n_rows, sem):
    for r in range(n_rows):
        pltpu.make_async_copy(table_ref.at[pl.ds(idx_ref[idx_base + r], 1), :],
                              buf_ref.at[pl.ds(row0 + r, 1), :], sem).start()


def _wait_row_gather(table_ref, buf_ref, sem):
    pltpu.make_async_copy(table_ref.at[pl.ds(0, buf_ref.shape[0]), :], buf_ref, sem).wait()


def _experts_kernel(f_split, lo_ref, hi_ref, ns_ref, src_ref, wg_ref, wu_ref, wd_ref, h_ref, y_ref,
                    xbuf_ref, xb_ref, rec_ref, wgb_ref, wub_ref, wdb_ref, sem):
    i = pl.program_id(0)
    s = pl.program_id(1)
    n_tiles = pl.num_programs(0)
    tm = xb_ref.shape[0]
    n_steps = pl.num_programs(1)
    n_sub = ns_ref[i]
    step_rows = tm // (2 * f_split)
    nxt = jnp.minimum(i + 1, n_tiles - 1)

    @pl.when((s == 0) & (i == 0))
    def _():
        _start_row_gather(h_ref, src_ref, 0, xbuf_ref, sem)

    @pl.when(s == 0)
    def _():
        y_ref[...] = jnp.zeros_like(y_ref)

        @pl.when((i == 0) | (ns_ref[jnp.maximum(i - 1, 0)] > 0))
        def _():
            _wait_row_gather(h_ref, xbuf_ref, sem)

        def keep(b, carry):
            rows = pl.ds(pl.multiple_of(b * EXPERT_SUB, EXPERT_SUB), EXPERT_SUB)
            xb_ref[rows, :] = xbuf_ref[rows, 0:D_MODEL].astype(BF16)
            rec_ref[rows, :] = xbuf_ref[rows, D_MODEL:D_MODEL + LANES]
            return carry

        lax.fori_loop(0, n_sub, keep, 0)

    @pl.when(n_sub > 0)
    def _():
        wgb_ref[...] = wg_ref[...].astype(BF16)
        wub_ref[...] = wu_ref[...].astype(BF16)
        wdb_ref[...] = wd_ref[...].astype(BF16)

    for k in range(1, tm // EXPERT_SUB + 1):
        @pl.when(n_sub == k)
        def _(k=k):
            _start_rows_unrolled(h_ref, src_ref, nxt * tm + s * step_rows, xbuf_ref, s * step_rows,
                                 step_rows, sem)
            e = s // f_split
            rows = slice(0, k * EXPERT_SUB)
            x = xb_ref[rows, :]
            gate_act = _dot(x, wgb_ref[...])
            up = _dot(x, wub_ref[...])
            record = rec_ref[rows, :]
            lane = lax.broadcasted_iota(jnp.int32, record.shape, 1)
            gate_col = jnp.sum(jnp.where(lane == 2 + e, record, 0.0), axis=1, keepdims=True)
            hid = (_silu(gate_act) * up * gate_col).astype(BF16)
            y_ref[rows, :] += _dot(hid, wdb_ref[...])

    @pl.when((i == n_tiles - 1) & (s == n_steps - 1) & (n_sub > 0))
    def _():
        _wait_row_gather(h_ref, xbuf_ref, sem)


def _experts(h, src, tile_lo, tile_hi, tile_nsub, w_gate, w_up, w_down, layer, tm, f_split=1):
    n_rows = src.shape[0]
    fb = D_EXPERT // f_split
    last = 2 * f_split - 1

    def expert_of(i, s, lo, hi, ns, sr):
        return jnp.where(jnp.where(ns[i] > 0, s, last) // f_split == 0, lo[i], hi[i])

    def half_of(i, s, lo, hi, ns, sr):
        return jnp.where(ns[i] > 0, s, last) % f_split

    grid_spec = pltpu.PrefetchScalarGridSpec(
        num_scalar_prefetch=4,
        grid=(n_rows // tm, 2 * f_split),
        in_specs=[
            pl.BlockSpec((None, None, D_MODEL, fb), lambda i, s, *p: (layer, expert_of(i, s, *p), 0, half_of(i, s, *p))),
            pl.BlockSpec((None, None, D_MODEL, fb), lambda i, s, *p: (layer, expert_of(i, s, *p), 0, half_of(i, s, *p))),
            pl.BlockSpec((None, None, fb, D_MODEL), lambda i, s, *p: (layer, expert_of(i, s, *p), half_of(i, s, *p), 0)),
            pl.BlockSpec(memory_space=pl.ANY),
        ],
        out_specs=pl.BlockSpec((tm, D_MODEL), lambda i, s, *_: (i, 0)),
        scratch_shapes=[
            pltpu.VMEM((tm, D_MODEL + LANES), F32),
            pltpu.VMEM((tm, D_MODEL), BF16),
            pltpu.VMEM((tm, LANES), F32),
            pltpu.VMEM((D_MODEL, fb), BF16),
            pltpu.VMEM((D_MODEL, fb), BF16),
            pltpu.VMEM((fb, D_MODEL), BF16),
            pltpu.SemaphoreType.DMA(()),
        ],
    )
    return pl.pallas_call(
        functools.partial(_experts_kernel, f_split),
        grid_spec=grid_spec,
        out_shape=jax.ShapeDtypeStruct((n_rows, D_MODEL), F32),
        compiler_params=_cparams(2),
        name="experts",
    )(tile_lo, tile_hi, tile_nsub, src, w_gate, w_up, w_down, h)


def _combine_kernel(final_norm, n_first, pos_ref, x_ref, gate_ref, fg_ref, ys_ref, *refs):
    o_refs, buf_ref, sem = refs[:-2], refs[-2], refs[-1]
    tm, d = x_ref.shape
    i = pl.program_id(0)
    slot = i % 2

    @pl.when(i == 0)
    def _():
        _start_row_gather(ys_ref, pos_ref, 0, buf_ref.at[0], sem.at[0])

    _wait_row_gather(ys_ref, buf_ref.at[slot], sem.at[slot])
    n_tiles = pl.num_programs(0)
    nxt = jnp.minimum(i + 1, n_tiles - 1)
    _start_rows_unrolled(ys_ref, pos_ref, nxt * tm, buf_ref.at[1 - slot], 0, tm, sem.at[1 - slot])

    nb = tm // ROW_BLOCK
    upd = buf_ref[slot].reshape(nb, ROW_BLOCK, d) * gate_ref[...][:, None, :]
    x = x_ref[...] + upd.reshape(tm, d)
    if final_norm:
        x = x * lax.rsqrt(jnp.mean(x * x, axis=-1, keepdims=True) + EPS) * fg_ref[...]
    if n_first is None:
        o_refs[0][...] = x
    else:
        @pl.when(pl.program_id(0) < n_first)
        def _():
            o_refs[0][...] = x

        @pl.when(pl.program_id(0) >= n_first)
        def _():
            o_refs[1][...] = x

    @pl.when(i == n_tiles - 1)
    def _():
        _wait_row_gather(ys_ref, buf_ref.at[1 - slot], sem.at[1 - slot])


def _combine(x, ys, pos, modb, final_g, final_norm, split_rows=None, tm=512):
    t_rows = x.shape[0]
    nb = tm // ROW_BLOCK
    if split_rows is None:
        n_first = None
        out_specs = pl.BlockSpec((tm, D_MODEL), lambda i, p: (i, 0))
        out_shape = jax.ShapeDtypeStruct((t_rows, D_MODEL), F32)
    else:
        assert split_rows % tm == 0 and (t_rows - split_rows) % tm == 0
        n_first = split_rows // tm
        out_specs = [pl.BlockSpec((tm, D_MODEL), lambda i, p: (jnp.minimum(i, n_first - 1), 0)),
                     pl.BlockSpec((tm, D_MODEL), lambda i, p: (jnp.maximum(i - n_first, 0), 0))]
        out_shape = [jax.ShapeDtypeStruct((split_rows, D_MODEL), F32),
                     jax.ShapeDtypeStruct((t_rows - split_rows, D_MODEL), F32)]
    grid_spec = pltpu.PrefetchScalarGridSpec(
        num_scalar_prefetch=1,
        grid=(t_rows // tm,),
        in_specs=[
            pl.BlockSpec((tm, D_MODEL), lambda i, p: (i, 0)),
            pl.BlockSpec((nb, D_MODEL), lambda i, p: (i, 5)),
            pl.BlockSpec((1, D_MODEL), lambda i, p: (0, 0)),
            pl.BlockSpec(memory_space=pl.ANY),
        ],
        out_specs=out_specs,
        scratch_shapes=[pltpu.VMEM((2, tm, D_MODEL), F32), pltpu.SemaphoreType.DMA((2,))],
    )
    return pl.pallas_call(
        functools.partial(_combine_kernel, final_norm, n_first),
        grid_spec=grid_spec,
        out_shape=out_shape,
        compiler_params=_cparams(1),
        name="combine",
    )(pos, x, modb, final_g.reshape(1, D_MODEL), ys)


def kernel(x_prompt, x_sample, state_ssm, state_conv, c_prompt, c_sample, norm1_g, norm2_g, w_ada, b_ada,
           w_in, gm_ln_g, gm_ln_b, gm_w_s, gm_b_s, conv_w, conv_b, dt_bias, a_log, d_skip, mnorm_g, w_out,
           w_router_group, b_router_group, w_router_expert, b_router_expert, w_gate, w_up, w_down,
           final_norm_g):
    n_p, len_p, _ = x_prompt.shape
    n_s, len_s, _ = x_sample.shape
    assert len_p % GM_TILE == 0 and len_s == ROW_BLOCK and (n_s * len_s) % GM_TILE == 0
    rows_p, rows_s = n_p * len_p, n_s * len_s
    t_rows = rows_p + rows_s
    n_seq = n_p + n_s
    blk_per_prompt = len_p // ROW_BLOCK
    assert t_rows % EXPERT_TILE == 0
    n_expert_tiles = t_rows // EXPERT_TILE + N_BUCKETS

    xs = (x_prompt.reshape(rows_p, D_MODEL), x_sample.reshape(rows_s, D_MODEL))

    c_all = jnp.concatenate([c_prompt, c_sample], axis=0)
    c_pad = jnp.pad(c_all, ((0, (-n_seq) % 8), (0, 0)))
    mod = _ada(c_pad, w_ada, b_ada)
    modb_all = jnp.concatenate(
        [jnp.repeat(mod[:, :n_p], blk_per_prompt, axis=1), mod[:, n_p:n_seq]], axis=1)

    half_w = gm_w_s[:, :, :ROW_BLOCK, :ROW_BLOCK]
    w_cat = jnp.stack([gm_w_s, jnp.tile(half_w, (1, 1, 2, 2))], axis=1)
    b_t = jnp.swapaxes(gm_b_s, 1, 2)
    b_cat = jnp.stack([b_t, jnp.tile(b_t[:, :ROW_BLOCK], (1, 2, 1))], axis=1)

    w_in_t = jnp.swapaxes(w_in, 1, 2)
    w_r = jnp.concatenate([w_router_group, w_router_expert], axis=2)
    w_r = jnp.pad(w_r, ((0, 0), (0, 0), (0, LANES - w_r.shape[2])))
    b_r = jnp.concatenate([b_router_group, b_router_expert], axis=1)
    b_r = jnp.pad(b_r, ((0, 0), (0, LANES - b_r.shape[1])))[:, None, :]

    ssm_p, conv_p, ssm_s, conv_s, v_rows = [], [], [], [], []
    for l in range(DEPTH):
        last = l == DEPTH - 1
        modb = modb_all[l]
        proj, dt_raw = _inproj(xs, norm1_g, modb, w_in_t, l, tn=512 if len(xs) == 2 else 1024)
        gm_out, vn = _gmlp(proj, gm_ln_g, gm_ln_b, w_cat, b_cat, l, rows_p // GM_TILE, rows_s)
        ssm_args = (conv_w, conv_b, dt_bias, a_log, d_skip, mnorm_g, l)
        ssm_out_p, hfin_p, cfin_p = _ssd(proj, dt_raw, None, None, *ssm_args, 0, n_p, len_p, SSD_PROMPT_CHUNK)
        ssm_out_s, hfin_s, cfin_s = _ssd(proj, dt_raw, state_ssm, state_conv, *ssm_args, rows_p, n_s, len_s, len_s)
        x = _outproj(gm_out, (ssm_out_p, ssm_out_s), w_out, xs, modb, l)
        h2, route, counts = _router(x, norm2_g, modb, w_r[l], b_r[l], l)
        pos, src, tile_lo, tile_hi, tile_nsub = _dispatch_tables(route, counts, t_rows, EXPERT_TILE,
                                                                 n_expert_tiles)
        ys = _experts(h2, src, tile_lo, tile_hi, tile_nsub, w_gate, w_up, w_down, l, EXPERT_TILE)
        out = _combine(x, ys, pos, modb, final_norm_g, last, rows_p if last else None)
        xs = tuple(out) if last else (out,)
        ssm_p.append(hfin_p)
        conv_p.append(cfin_p)
        ssm_s.append(hfin_s)
        conv_s.append(cfin_s)
        v_rows.append(vn.reshape(n_s, len_s, GM_WIDTH))

    y_prompt = xs[0].reshape(n_p, len_p, D_MODEL)
    y_sample = xs[1].reshape(n_s, len_s, D_MODEL)
    return (y_prompt, y_sample, jnp.stack(ssm_p), jnp.stack(conv_p), jnp.stack(ssm_s), jnp.stack(conv_s),
            jnp.stack(v_rows))
```

```python
import functools
import math

import jax
import jax.numpy as jnp
import numpy as np
from jax import lax
from jax.experimental import pallas as pl
from jax.experimental.pallas import tpu as pltpu

D_MODEL = 2048
DEPTH = 2
GM_WIDTH = 1024
GM_HEADS = 8
GM_HEAD_DIM = 128
GM_CHUNK = 128
M_WIDTH = 1024
M_HEAD_DIM = 64
M_HEADS = 16
M_GROUPS = 4
M_STATE = 128
M_CONV = 4
CONV_DIM = M_WIDTH + 2 * M_GROUPS * M_STATE
MAIN_DIM = 2 * GM_WIDTH + M_WIDTH + CONV_DIM
N_EGROUPS = 4
EXPERTS_PER_GROUP = 4
N_EXPERTS = 16
D_EXPERT = 512
N_MOD = 6
EPS = 1e-6

ROW_BLOCK = 64
LANES = 128
VMEM_LIMIT = 56 * 1024 * 1024

F32 = jnp.float32
BF16 = jnp.bfloat16


def _cparams(n_axes):
    return pltpu.CompilerParams(
        dimension_semantics=("arbitrary",) * n_axes, vmem_limit_bytes=VMEM_LIMIT)


def _silu(x):
    return x * jax.nn.sigmoid(x)


def _gelu_tanh(x):
    c = math.sqrt(2.0 / math.pi)
    return 0.5 * x * (1.0 + jnp.tanh(c * (x + 0.044715 * (x * x * x))))


def _dot(a, b):
    return jnp.dot(a, b, preferred_element_type=F32)


def _dot_nt(a, b, precision=None):
    return lax.dot_general(a, b, (((1,), (1,)), ((), ())), precision=precision,
                           preferred_element_type=F32)


def _dot_split2(a, b):
    a0 = a.astype(BF16)
    a1 = (a - a0.astype(F32)).astype(BF16)
    b0 = b.astype(BF16)
    b1 = (b - b0.astype(F32)).astype(BF16)
    return _dot(a0, b0) + _dot(a0, b1) + _dot(a1, b0)


def _dot_tn(a, b):
    return lax.dot_general(a, b, (((0,), (0,)), ((), ())), preferred_element_type=F32)


def _ada_kernel(c_ref, w_ref, b_ref, o_ref):
    c = _silu(c_ref[...]).astype(BF16)
    o_ref[...] = _dot(c, w_ref[...].astype(BF16)) + b_ref[...]


def _ada(c_pad, w_ada, b_ada):
    rows = c_pad.shape[0]
    tn = 1024
    n_out = N_MOD * D_MODEL
    return pl.pallas_call(
        _ada_kernel,
        grid=(DEPTH, n_out // tn),
        in_specs=[
            pl.BlockSpec((rows, D_MODEL), lambda l, j: (0, 0)),
            pl.BlockSpec((None, D_MODEL, tn), lambda l, j: (l, 0, j)),
            pl.BlockSpec((None, 1, tn), lambda l, j: (l, 0, j)),
        ],
        out_specs=pl.BlockSpec((None, rows, tn), lambda l, j: (l, 0, j)),
        out_shape=jax.ShapeDtypeStruct((DEPTH, rows, n_out), F32),
        compiler_params=_cparams(2),
        name="ada_ln",
    )(c_pad, w_ada, b_ada.reshape(DEPTH, 1, n_out))


def _stream_specs(xs, block, col_of, n_j):
    if len(xs) == 1:
        return [pl.BlockSpec(block, lambda i, j: (i, col_of(j)))], None
    n_first = xs[0].shape[0] // block[0]

    def first(i, j):
        return jnp.minimum(i, n_first - 1), jnp.where(i < n_first, col_of(j), col_of(n_j - 1))

    def second(i, j):
        return jnp.maximum(i - n_first, 0), jnp.where(i < n_first, col_of(0), col_of(j))

    return [pl.BlockSpec(block, first), pl.BlockSpec(block, second)], n_first


def _stream_tile(x_refs, n_first, idx):
    if n_first is None:
        return x_refs[0][idx]
    return jnp.where(pl.program_id(0) < n_first, x_refs[0][idx], x_refs[1][idx])


def _inproj_kernel(n_first, *refs):
    n_x = 1 if n_first is None else 2
    x_refs = refs[:n_x]
    g_ref, shift_ref, scale_ref, w_ref, wdt_ref, proj_ref, dt_ref, h_ref = refs[n_x:]
    j = pl.program_id(1)
    n_blk = shift_ref.shape[0]

    @pl.when(j == 0)
    def _():
        g = g_ref[...]

        def body(r, carry):
            rows = pl.ds(pl.multiple_of(r * ROW_BLOCK, ROW_BLOCK), ROW_BLOCK)
            xf = _stream_tile(x_refs, n_first, (rows, slice(None)))
            y = xf * lax.rsqrt(jnp.mean(xf * xf, axis=-1, keepdims=True) + EPS) * g
            y = y * (1.0 + scale_ref[pl.ds(r, 1), :]) + shift_ref[pl.ds(r, 1), :]
            h_ref[rows, :] = y.astype(BF16)
            return carry

        lax.fori_loop(0, n_blk, body, 0)
        dt_ref[...] = _dot_nt(h_ref[...], wdt_ref[...].astype(BF16))

    proj_ref[...] = _dot_nt(h_ref[...], w_ref[...].astype(BF16)).astype(BF16)


def _inproj(xs, norm_g, modb, w_in_t, layer, tm=1024, tn=512):
    t_rows = sum(x.shape[0] for x in xs)
    nb = tm // ROW_BLOCK
    assert all(x.shape[0] % tm == 0 for x in xs)
    x_specs, n_first = _stream_specs(xs, (tm, D_MODEL), lambda j: 0, MAIN_DIM // tn)
    return pl.pallas_call(
        functools.partial(_inproj_kernel, n_first),
        grid=(t_rows // tm, MAIN_DIM // tn),
        in_specs=x_specs + [
            pl.BlockSpec((None, 1, D_MODEL), lambda i, j: (layer, 0, 0)),
            pl.BlockSpec((nb, D_MODEL), lambda i, j: (i, 0)),
            pl.BlockSpec((nb, D_MODEL), lambda i, j: (i, 1)),
            pl.BlockSpec((None, tn, D_MODEL), lambda i, j: (layer, j, 0)),
            pl.BlockSpec((None, M_HEADS, D_MODEL), lambda i, j: (layer, MAIN_DIM // M_HEADS, 0)),
        ],
        out_specs=[
            pl.BlockSpec((tm, tn), lambda i, j: (i, j)),
            pl.BlockSpec((tm, M_HEADS), lambda i, j: (i, 0)),
        ],
        out_shape=[
            jax.ShapeDtypeStruct((t_rows, MAIN_DIM), BF16),
            jax.ShapeDtypeStruct((t_rows, M_HEADS), F32),
        ],
        scratch_shapes=[pltpu.VMEM((tm, D_MODEL), BF16)],
        compiler_params=_cparams(2),
        name="in_proj",
    )(*xs, norm_g.reshape(DEPTH, 1, D_MODEL), modb, modb, w_in_t, w_in_t)


def _gmlp_kernel(n_prompt_tiles, u_ref, v_ref, lng_ref, lnb_ref, w_ref, b_ref, out_ref, vn_ref):
    i = pl.program_id(0)
    t_idx = lax.broadcasted_iota(jnp.int32, (GM_CHUNK, GM_CHUNK), 0)
    s_idx = lax.broadcasted_iota(jnp.int32, (GM_CHUNK, GM_CHUNK), 1)
    same_seq = (t_idx // ROW_BLOCK) == (s_idx // ROW_BLOCK)
    allowed = (s_idx <= t_idx) & (same_seq | (i < n_prompt_tiles))
    w_masked = [jnp.where(allowed, w_ref[h], 0.0).astype(BF16) for h in range(GM_HEADS)]
    bias = b_ref[...]
    for r0 in range(0, u_ref.shape[0], GM_CHUNK):
        rows = slice(r0, r0 + GM_CHUNK)
        u = _gelu_tanh(u_ref[rows, :].astype(F32))
        v = _gelu_tanh(v_ref[rows, :].astype(F32))
        mu = jnp.mean(v, axis=-1, keepdims=True)
        vc = v - mu
        vn = vc * lax.rsqrt(jnp.mean(vc * vc, axis=-1, keepdims=True) + EPS)
        vn = vn * lng_ref[...] + lnb_ref[...]
        vn_ref[rows, :] = vn
        vnb = vn.astype(BF16)
        for h in range(GM_HEADS):
            cols = slice(h * GM_HEAD_DIM, (h + 1) * GM_HEAD_DIM)
            mixed = _dot(w_masked[h], vnb[:, cols]) + bias[:, h:h + 1]
            out_ref[rows, cols] = (u[:, cols] * mixed).astype(BF16)


def _gmlp(proj, ln_g, ln_b, w_cat, b_cat, layer, n_prompt_tiles, n_sample_rows):
    t_rows = proj.shape[0]
    n_tiles = t_rows // GM_TILE
    kern = functools.partial(_gmlp_kernel, n_prompt_tiles)
    return pl.pallas_call(
        kern,
        grid=(n_tiles,),
        in_specs=[
            pl.BlockSpec((GM_TILE, GM_WIDTH), lambda i: (i, 0)),
            pl.BlockSpec((GM_TILE, GM_WIDTH), lambda i: (i, 1)),
            pl.BlockSpec((None, 1, GM_WIDTH), lambda i: (layer, 0, 0)),
            pl.BlockSpec((None, 1, GM_WIDTH), lambda i: (layer, 0, 0)),
            pl.BlockSpec((None, None, GM_HEADS, GM_CHUNK, GM_CHUNK),
                         lambda i: (layer, jnp.where(i < n_prompt_tiles, 0, 1), 0, 0, 0)),
            pl.BlockSpec((None, None, GM_CHUNK, GM_HEADS),
                         lambda i: (layer, jnp.where(i < n_prompt_tiles, 0, 1), 0, 0)),
        ],
        out_specs=[
            pl.BlockSpec((GM_TILE, GM_WIDTH), lambda i: (i, 0)),
            pl.BlockSpec((GM_TILE, GM_WIDTH), lambda i: (jnp.maximum(i - n_prompt_tiles, 0), 0)),
        ],
        out_shape=[
            jax.ShapeDtypeStruct((t_rows, GM_WIDTH), BF16),
            jax.ShapeDtypeStruct((n_sample_rows, GM_WIDTH), F32),
        ],
        compiler_params=_cparams(1),
        name="gmlp",
    )(proj, proj, ln_g.reshape(DEPTH, 1, GM_WIDTH), ln_b.reshape(DEPTH, 1, GM_WIDTH), w_cat, b_cat)


GM_TILE = 4 * GM_CHUNK
CONV_PAD = 8
SSD_PROMPT_CHUNK = 256


def _widen(v, onehot):
    p0 = v.astype(BF16)
    r1 = v - p0.astype(F32)
    p1 = r1.astype(BF16)
    p2 = (r1 - p1.astype(F32)).astype(BF16)
    return _dot(p0, onehot) + _dot(p1, onehot) + _dot(p2, onehot)


def _ssd_kernel(q, zero_init, *refs):
    z_ref, xa_ref, xb_ref, dt_ref = refs[:4]
    refs = refs[4:]
    h0_ref = c0_ref = None
    if not zero_init:
        h0_ref, c0_ref = refs[:2]
        refs = refs[2:]
    cw_ref, cb_ref, dtb_ref, alog_ref, dskip_ref, ng_ref = refs[:6]
    refs = refs[6:]
    out_ref, hfin_ref, cfin_ref, state_ref, win_ref, xs_ref, bc_ref, xdt_ref, xdd_ref, y_ref = refs
    c = pl.program_id(1)
    tail = M_CONV - 1
    lo = CONV_PAD - tail
    hp = M_HEADS * M_HEAD_DIM

    @pl.when(c == 0)
    def _():
        if zero_init:
            state_ref[...] = jnp.zeros_like(state_ref)
            win_ref[lo:CONV_PAD, :] = jnp.zeros((tail, CONV_DIM), F32)
        else:
            state_ref[...] = h0_ref[...].reshape(hp, M_STATE).T
            win_ref[lo:CONV_PAD, :] = c0_ref[...]

    half = CONV_DIM // 2
    win_ref[CONV_PAD:CONV_PAD + q, 0:half] = xa_ref[...].astype(F32)
    win_ref[CONV_PAD:CONV_PAD + q, half:CONV_DIM] = xb_ref[...].astype(F32)
    for r0 in range(0, q, ROW_BLOCK):
        conv = cb_ref[...]
        for k in range(M_CONV):
            conv = conv + win_ref[lo + k + r0:lo + k + r0 + ROW_BLOCK, :] * cw_ref[k:k + 1, :]
        act = _silu(conv)
        xs_ref[r0:r0 + ROW_BLOCK, :] = act[:, 0:M_WIDTH]
        bc_ref[r0:r0 + ROW_BLOCK, :] = act[:, M_WIDTH:CONV_DIM].astype(BF16)
    new_tail = win_ref[CONV_PAD + q - tail:CONV_PAD + q, :]
    cfin_ref[...] = new_tail
    win_ref[lo:CONV_PAD, :] = new_tail

    hi = lax.Precision.HIGHEST
    dt = jax.nn.softplus(dt_ref[...] + dtb_ref[...])
    a = -jnp.exp(alog_ref[...])
    r_idx = lax.broadcasted_iota(jnp.int32, (q, q), 0)
    c_idx = lax.broadcasted_iota(jnp.int32, (q, q), 1)
    causal = c_idx <= r_idx
    a_cs = jnp.dot(causal.astype(F32), dt * a, precision=hi, preferred_element_type=F32)
    eye = (lax.broadcasted_iota(jnp.int32, (M_HEADS, M_HEADS), 0)
           == lax.broadcasted_iota(jnp.int32, (M_HEADS, M_HEADS), 1)).astype(F32)
    a_cs_t = _dot_nt(eye, a_cs, precision=hi)
    widen = (lax.broadcasted_iota(jnp.int32, (M_HEADS, hp), 1) // M_HEAD_DIM
             == lax.broadcasted_iota(jnp.int32, (M_HEADS, hp), 0)).astype(BF16)
    dt_w = _widen(dt, widen)
    acs_w = _widen(a_cs, widen)
    dskip_w = _widen(jnp.broadcast_to(dskip_ref[...], (8, M_HEADS)), widen)[0:1, :]
    a_end_w = acs_w[q - 1:q, :]
    x_dt = xs_ref[...] * dt_w
    xdt_ref[...] = x_dt.astype(BF16)
    xdd_ref[...] = (x_dt * jnp.exp(a_end_w - acs_w)).astype(BF16)
    decay_blk_w = jnp.exp(a_end_w)

    heads_per_group = M_HEADS // M_GROUPS
    gw = M_WIDTH // M_GROUPS
    pair_lane = lax.broadcasted_iota(jnp.int32, (q, 2 * M_HEAD_DIM), 1)
    for g in range(M_GROUPS):
        gcols = slice(g * gw, (g + 1) * gw)
        b_g = bc_ref[:, g * M_STATE:(g + 1) * M_STATE]
        c_g = bc_ref[:, (M_GROUPS + g) * M_STATE:(M_GROUPS + g + 1) * M_STATE]
        cb = _dot_nt(c_g, b_g)
        st = state_ref[:, gcols]
        y_off = _dot(c_g, st.astype(BF16))
        state_ref[:, gcols] = st * decay_blk_w[:, gcols] + _dot_tn(b_g, xdd_ref[:, gcols])
        for pr in range(heads_per_group // 2):
            h0 = g * heads_per_group + 2 * pr
            pcols = slice(h0 * M_HEAD_DIM, (h0 + 2) * M_HEAD_DIM)
            x_pair = xdt_ref[:, pcols]
            ys = []
            for h in (h0, h0 + 1):
                seg = a_cs[:, h:h + 1] - a_cs_t[h:h + 1, :]
                scores = cb * jnp.exp(jnp.where(causal, seg, -jnp.inf))
                ys.append(_dot(scores.astype(BF16), x_pair))
            y_ref[:, pcols] = jnp.where(pair_lane < M_HEAD_DIM, ys[0], ys[1])
        y_ref[:, gcols] = (y_ref[:, gcols] + y_off * jnp.exp(acs_w[:, gcols])
                           + dskip_w[:, gcols] * xs_ref[:, gcols])

    @pl.when(c == pl.num_programs(1) - 1)
    def _():
        hfin_ref[...] = state_ref[...].T.reshape(M_HEADS, M_HEAD_DIM, M_STATE)

    gy = y_ref[...] * _silu(z_ref[...].astype(F32))
    ng = ng_ref[...]
    for g in range(M_GROUPS):
        cols = slice(g * gw, (g + 1) * gw)
        s = gy[:, cols]
        s = s * lax.rsqrt(jnp.mean(s * s, axis=-1, keepdims=True) + EPS)
        out_ref[:, cols] = (s * ng[:, cols]).astype(BF16)


def _ssd(proj, dt_raw, h0, c0, conv_w, conv_b, dt_bias, a_log, d_skip, mnorm_g, layer,
         row_start, n_seq, seq_len, q):
    n_c = seq_len // q
    blk0 = row_start // q
    half = CONV_DIM // 2
    z_blk = (2 * GM_WIDTH) // M_WIDTH
    xa_blk = (2 * GM_WIDTH + M_WIDTH) // half
    zero_init = h0 is None

    def rows(col):
        return lambda s, c: (blk0 + s * n_c + c, col)

    def per_layer(*shape):
        return pl.BlockSpec((None,) + shape, lambda s, c: (layer,) + (0,) * len(shape))

    in_specs = [
        pl.BlockSpec((q, M_WIDTH), rows(z_blk)),
        pl.BlockSpec((q, half), rows(xa_blk)),
        pl.BlockSpec((q, half), rows(xa_blk + 1)),
        pl.BlockSpec((q, M_HEADS), rows(0)),
    ]
    args = [proj, proj, proj, dt_raw]
    if not zero_init:
        in_specs += [
            pl.BlockSpec((None, None, M_HEADS, M_HEAD_DIM, M_STATE), lambda s, c: (layer, s, 0, 0, 0)),
            pl.BlockSpec((None, None, M_CONV - 1, CONV_DIM), lambda s, c: (layer, s, 0, 0)),
        ]
        args += [h0, c0]
    in_specs += [per_layer(M_CONV, CONV_DIM), per_layer(1, CONV_DIM), per_layer(1, M_HEADS),
                 per_layer(1, M_HEADS), per_layer(1, M_HEADS), per_layer(1, M_WIDTH)]
    args += [conv_w, conv_b.reshape(DEPTH, 1, CONV_DIM), dt_bias.reshape(DEPTH, 1, M_HEADS),
             a_log.reshape(DEPTH, 1, M_HEADS), d_skip.reshape(DEPTH, 1, M_HEADS),
             mnorm_g.reshape(DEPTH, 1, M_WIDTH)]
    return pl.pallas_call(
        functools.partial(_ssd_kernel, q, zero_init),
        grid=(n_seq, n_c),
        in_specs=in_specs,
        out_specs=[
            pl.BlockSpec((q, M_WIDTH), lambda s, c: (s * n_c + c, 0)),
            pl.BlockSpec((None, M_HEADS, M_HEAD_DIM, M_STATE), lambda s, c: (s, 0, 0, 0)),
            pl.BlockSpec((None, M_CONV - 1, CONV_DIM), lambda s, c: (s, 0, 0)),
        ],
        out_shape=[
            jax.ShapeDtypeStruct((n_seq * seq_len, M_WIDTH), BF16),
            jax.ShapeDtypeStruct((n_seq, M_HEADS, M_HEAD_DIM, M_STATE), F32),
            jax.ShapeDtypeStruct((n_seq, M_CONV - 1, CONV_DIM), F32),
        ],
        scratch_shapes=[
            pltpu.VMEM((M_STATE, M_WIDTH), F32),
            pltpu.VMEM((CONV_PAD + q, CONV_DIM), F32),
            pltpu.VMEM((q, M_WIDTH), F32),
            pltpu.VMEM((q, 2 * M_GROUPS * M_STATE), BF16),
            pltpu.VMEM((q, M_WIDTH), BF16),
            pltpu.VMEM((q, M_WIDTH), BF16),
            pltpu.VMEM((q, M_WIDTH), F32),
        ],
        compiler_params=_cparams(2),
        name="conv_ssd",
    )(*args)


def _outproj_kernel(n_first_ssm, n_first, gm_ref, wa_ref, wb_ref, gate_ref, *refs):
    n_ssm = 1 if n_first_ssm is None else 2
    ssm_refs, x_refs, o_ref = refs[:n_ssm], refs[n_ssm:-1], refs[-1]
    ssm = _stream_tile(ssm_refs, n_first_ssm, Ellipsis)
    acc = _dot(gm_ref[...], wa_ref[...].astype(BF16)) + _dot(ssm, wb_ref[...].astype(BF16))
    tm, tn = acc.shape
    nb = tm // ROW_BLOCK
    upd = acc.reshape(nb, ROW_BLOCK, tn) * gate_ref[...][:, None, :]
    o_ref[...] = _stream_tile(x_refs, n_first, Ellipsis) + upd.reshape(tm, tn)


def _outproj(gm, ssms, w_out, xs, modb, layer, tm=1024, tn=512):
    t_rows = gm.shape[0]
    nb = tm // ROW_BLOCK
    gate_blk = 2 * D_MODEL // tn
    assert all(x.shape[0] % tm == 0 for x in xs)
    assert all(x.shape[0] % tm == 0 for x in ssms)
    x_specs, n_first = _stream_specs(xs, (tm, tn), lambda j: j, D_MODEL // tn)
    ssm_specs, n_first_ssm = _stream_specs(ssms, (tm, M_WIDTH), lambda j: 0, D_MODEL // tn)
    return pl.pallas_call(
        functools.partial(_outproj_kernel, n_first_ssm, n_first),
        grid=(t_rows // tm, D_MODEL // tn),
        in_specs=[
            pl.BlockSpec((tm, GM_WIDTH), lambda i, j: (i, 0)),
            pl.BlockSpec((None, GM_WIDTH, tn), lambda i, j: (layer, 0, j)),
            pl.BlockSpec((None, M_WIDTH, tn), lambda i, j: (layer, 1, j)),
            pl.BlockSpec((nb, tn), lambda i, j: (i, gate_blk + j)),
        ] + ssm_specs + x_specs,
        out_specs=pl.BlockSpec((tm, tn), lambda i, j: (i, j)),
        out_shape=jax.ShapeDtypeStruct((t_rows, D_MODEL), F32),
        compiler_params=_cparams(2),
        name="out_proj",
    )(gm, w_out, w_out, modb, *ssms, *xs)


ROUTE_ROWS = 8
EXPERT_TILE = 1024
EXPERT_SUB = 256


def _row_argmax_first(rows):
    m = rows[0]
    for r in rows[1:]:
        m = jnp.maximum(m, r)
    idx = jnp.full(m.shape, len(rows), jnp.int32)
    for k in reversed(range(len(rows))):
        idx = jnp.where(rows[k] == m, k, idx)
    return m, idx


def _router_kernel(x_ref, g_ref, shift_ref, scale_ref, wr_ref, br_ref, h_ref, route_ref, count_ref, run_ref):
    n_blk = shift_ref.shape[0]
    g = g_ref[...]

    @pl.when(pl.program_id(0) == 0)
    def _():
        run_ref[...] = jnp.zeros_like(run_ref)

    def body(r, carry):
        rows = pl.ds(pl.multiple_of(r * ROW_BLOCK, ROW_BLOCK), ROW_BLOCK)
        xf = x_ref[rows, :]
        y = xf * lax.rsqrt(jnp.mean(xf * xf, axis=-1, keepdims=True) + EPS) * g
        h_ref[rows, 0:D_MODEL] = y * (1.0 + scale_ref[pl.ds(r, 1), :]) + shift_ref[pl.ds(r, 1), :]
        return carry

    lax.fori_loop(0, n_blk, body, 0)
    logits = (_dot_split2(h_ref[:, 0:D_MODEL], wr_ref[...]) + br_ref[...]).T
    lg = [logits[k:k + 1, :] for k in range(N_EGROUPS)]
    g_max, g_idx = _row_argmax_first(lg)
    g_den = lg[0] * 0.0
    for r in lg:
        g_den = g_den + jnp.exp(r - g_max)
    g_top = 1.0 / g_den
    e_in = []
    for k in range(EXPERTS_PER_GROUP):
        sel = lg[0] * 0.0
        for gi in range(N_EGROUPS):
            row = N_EGROUPS + gi * EXPERTS_PER_GROUP + k
            sel = jnp.where(g_idx == gi, logits[row:row + 1, :], sel)
        e_in.append(sel)
    e_max, i1 = _row_argmax_first(e_in)
    e_exp = [jnp.exp(r - e_max) for r in e_in]
    e_den = e_exp[0]
    for r in e_exp[1:]:
        e_den = e_den + r
    p = [r / e_den for r in e_exp]
    p1 = p[0] * 0.0
    for k in range(EXPERTS_PER_GROUP):
        p1 = jnp.where(i1 == k, p[k], p1)
    rest = [jnp.where(i1 == k, -1.0, p[k]) for k in range(EXPERTS_PER_GROUP)]
    p2, i2 = _row_argmax_first(rest)
    norm = p1 + p2
    w1 = p1 / norm * g_top
    w2 = p2 / norm * g_top
    tm = logits.shape[1]
    row = lax.broadcasted_iota(jnp.int32, (ROUTE_ROWS, tm), 0)
    onehot = (row == g_idx).astype(F32)
    before = (lax.broadcasted_iota(jnp.int32, (tm, tm), 0)
              < lax.broadcasted_iota(jnp.int32, (tm, tm), 1)).astype(BF16)
    earlier = _dot(onehot.astype(BF16), before)
    rank = jnp.sum(onehot * (earlier + run_ref[:, 0:1]), axis=0, keepdims=True)
    run_ref[...] += jnp.sum(onehot, axis=1, keepdims=True)
    count_ref[...] = run_ref[...]

    route_ref[0:1, :] = g_idx.astype(F32)
    route_ref[1:2, :] = rank
    for k in range(EXPERTS_PER_GROUP):
        route_ref[2 + k:3 + k, :] = jnp.where(i1 == k, w1, 0.0) + jnp.where(i2 == k, w2, 0.0)
    route_ref[2 + EXPERTS_PER_GROUP:ROUTE_ROWS, :] = jnp.zeros((ROUTE_ROWS - 2 - EXPERTS_PER_GROUP, tm), F32)
    place = (lax.broadcasted_iota(jnp.int32, (ROUTE_ROWS, LANES), 0)
             == lax.broadcasted_iota(jnp.int32, (ROUTE_ROWS, LANES), 1)).astype(F32)
    h_ref[:, D_MODEL:D_MODEL + LANES] = lax.dot_general(
        route_ref[...], place, (((0,), (0,)), ((), ())), precision=lax.Precision.HIGHEST,
        preferred_element_type=F32)


def _router(x, norm_g, modb, w_r, b_r, layer, tm=512):
    t_rows = x.shape[0]
    nb = tm // ROW_BLOCK
    return pl.pallas_call(
        _router_kernel,
        grid=(t_rows // tm,),
        in_specs=[
            pl.BlockSpec((tm, D_MODEL), lambda i: (i, 0)),
            pl.BlockSpec((None, 1, D_MODEL), lambda i: (layer, 0, 0)),
            pl.BlockSpec((nb, D_MODEL), lambda i: (i, 3)),
            pl.BlockSpec((nb, D_MODEL), lambda i: (i, 4)),
            pl.BlockSpec((D_MODEL, LANES), lambda i: (0, 0)),
            pl.BlockSpec((1, LANES), lambda i: (0, 0)),
        ],
        out_specs=[
            pl.BlockSpec((tm, D_MODEL + LANES), lambda i: (i, 0)),
            pl.BlockSpec((ROUTE_ROWS, tm), lambda i: (0, i)),
            pl.BlockSpec((ROUTE_ROWS, LANES), lambda i: (0, 0)),
        ],
        out_shape=[
            jax.ShapeDtypeStruct((t_rows, D_MODEL + LANES), F32),
            jax.ShapeDtypeStruct((ROUTE_ROWS, t_rows), F32),
            jax.ShapeDtypeStruct((ROUTE_ROWS, LANES), F32),
        ],
        scratch_shapes=[pltpu.VMEM((ROUTE_ROWS, LANES), F32)],
        compiler_params=_cparams(1),
        name="router",
    )(x, norm_g.reshape(DEPTH, 1, D_MODEL), modb, modb, w_r, b_r)


def _dispatch_tables(route, counts, t_rows, tm, n_tiles):
    gid = route[0].astype(jnp.int32)
    rank = route[1].astype(jnp.int32)
    cnt = counts[:N_EGROUPS, 0].astype(jnp.int32)
    padded = (cnt + tm - 1) // tm * tm
    ends = jnp.cumsum(padded)
    starts = ends - padded
    pos = starts[gid] + rank
    src = jnp.zeros((n_tiles * tm,), jnp.int32).at[pos].set(jnp.arange(t_rows, dtype=jnp.int32))
    tile_start = jnp.arange(n_tiles, dtype=jnp.int32) * tm
    tile_group = jnp.sum((tile_start[:, None] >= ends[None, :]).astype(jnp.int32), axis=1)
    tile_group = jnp.minimum(tile_group, N_EGROUPS - 1)
    tile_rows = jnp.clip(cnt[tile_group] - (tile_start - starts[tile_group]), 0, tm)
    tile_nsub = (tile_rows + EXPERT_SUB - 1) // EXPERT_SUB
    return pos, src, tile_group, tile_nsub


def _start_row_gather(table_ref, idx_ref, idx_base, buf_ref, sem):
    def issue(r, carry):
        pltpu.make_async_copy(table_ref.at[pl.ds(idx_ref[idx_base + r], 1), :],
                              buf_ref.at[pl.ds(r, 1), :], sem).start()
        return carry

    lax.fori_loop(0, buf_ref.shape[0], issue, 0, unroll=8)


def _start_rows_unrolled(table_ref, idx_ref, idx_base, buf_ref, row0, n_rows, sem, both_queues=False):
    for r in range(n_rows):
        pltpu.make_async_copy(table_ref.at[pl.ds(idx_ref[idx_base + r], 1), :],
                              buf_ref.at[pl.ds(row0 + r, 1), :], sem).start(r % 2 if both_queues else 0)


def _wait_row_gather(table_ref, buf_ref, sem):
    pltpu.make_async_copy(table_ref.at[pl.ds(0, buf_ref.shape[0]), :], buf_ref, sem).wait()


def _experts_kernel(f_split, tg_ref, ns_ref, src_ref, wg_ref, wu_ref, wd_ref, h_ref, y_ref,
                    xbuf_ref, xb_ref, rec_ref, wgb_ref, wub_ref, wdb_ref, sem):
    i = pl.program_id(0)
    s = pl.program_id(1)
    n_tiles = pl.num_programs(0)
    tm = xb_ref.shape[0]
    n_steps = pl.num_programs(1)
    n_sub = ns_ref[i]
    step_rows = tm // (EXPERTS_PER_GROUP * f_split)
    nxt = jnp.minimum(i + 1, n_tiles - 1)

    @pl.when((s == 0) & (i == 0))
    def _():
        _start_row_gather(h_ref, src_ref, 0, xbuf_ref, sem)

    @pl.when(s == 0)
    def _():
        y_ref[...] = jnp.zeros_like(y_ref)

        @pl.when((i == 0) | (ns_ref[jnp.maximum(i - 1, 0)] > 0))
        def _():
            _wait_row_gather(h_ref, xbuf_ref, sem)

        def keep(b, carry):
            rows = pl.ds(pl.multiple_of(b * EXPERT_SUB, EXPERT_SUB), EXPERT_SUB)
            xb_ref[rows, :] = xbuf_ref[rows, 0:D_MODEL].astype(BF16)
            rec_ref[rows, :] = xbuf_ref[rows, D_MODEL:D_MODEL + LANES]
            return carry

        lax.fori_loop(0, n_sub, keep, 0)

    @pl.when(n_sub > 0)
    def _():
        wgb_ref[...] = wg_ref[...].astype(BF16)
        wub_ref[...] = wu_ref[...].astype(BF16)
        wdb_ref[...] = wd_ref[...].astype(BF16)

    for k in range(1, tm // EXPERT_SUB + 1):
        @pl.when(n_sub == k)
        def _(k=k):
            _start_rows_unrolled(h_ref, src_ref, nxt * tm + s * step_rows, xbuf_ref, s * step_rows,
                                 step_rows, sem)
            e = s // f_split
            rows = slice(0, k * EXPERT_SUB)
            x = xb_ref[rows, :]
            gate_act = _dot(x, wgb_ref[...])
            up = _dot(x, wub_ref[...])
            record = rec_ref[rows, :]
            lane = lax.broadcasted_iota(jnp.int32, record.shape, 1)
            gate_col = jnp.sum(jnp.where(lane == 2 + e, record, 0.0), axis=1, keepdims=True)
            hid = (_silu(gate_act) * up * gate_col).astype(BF16)
            y_ref[rows, :] += _dot(hid, wdb_ref[...])

    @pl.when((i == n_tiles - 1) & (s == n_steps - 1) & (n_sub > 0))
    def _():
        _wait_row_gather(h_ref, xbuf_ref, sem)


def _experts(h, src, tile_group, tile_nsub, w_gate, w_up, w_down, layer, tm, f_split=2):
    n_rows = src.shape[0]
    fb = D_EXPERT // f_split
    last = EXPERTS_PER_GROUP * f_split - 1

    def expert_of(i, s, tg, ns, sr):
        return tg[i] * EXPERTS_PER_GROUP + jnp.where(ns[i] > 0, s, last) // f_split

    def half_of(i, s, tg, ns, sr):
        return jnp.where(ns[i] > 0, s, last) % f_split

    grid_spec = pltpu.PrefetchScalarGridSpec(
        num_scalar_prefetch=3,
        grid=(n_rows // tm, EXPERTS_PER_GROUP * f_split),
        in_specs=[
            pl.BlockSpec((None, None, D_MODEL, fb), lambda i, s, *p: (layer, expert_of(i, s, *p), 0, half_of(i, s, *p))),
            pl.BlockSpec((None, None, D_MODEL, fb), lambda i, s, *p: (layer, expert_of(i, s, *p), 0, half_of(i, s, *p))),
            pl.BlockSpec((None, None, fb, D_MODEL), lambda i, s, *p: (layer, expert_of(i, s, *p), half_of(i, s, *p), 0)),
            pl.BlockSpec(memory_space=pl.ANY),
        ],
        out_specs=pl.BlockSpec((tm, D_MODEL), lambda i, s, *_: (i, 0)),
        scratch_shapes=[
            pltpu.VMEM((tm, D_MODEL + LANES), F32),
            pltpu.VMEM((tm, D_MODEL), BF16),
            pltpu.VMEM((tm, LANES), F32),
            pltpu.VMEM((D_MODEL, fb), BF16),
            pltpu.VMEM((D_MODEL, fb), BF16),
            pltpu.VMEM((fb, D_MODEL), BF16),
            pltpu.SemaphoreType.DMA(()),
        ],
    )
    return pl.pallas_call(
        functools.partial(_experts_kernel, f_split),
        grid_spec=grid_spec,
        out_shape=jax.ShapeDtypeStruct((n_rows, D_MODEL), F32),
        compiler_params=_cparams(2),
        name="experts",
    )(tile_group, tile_nsub, src, w_gate, w_up, w_down, h)


def _combine_kernel(final_norm, n_first, pos_ref, x_ref, gate_ref, fg_ref, ys_ref, *refs):
    o_refs, buf_ref, sem = refs[:-2], refs[-2], refs[-1]
    tm, d = x_ref.shape
    i = pl.program_id(0)
    slot = i % 2

    @pl.when(i == 0)
    def _():
        _start_row_gather(ys_ref, pos_ref, 0, buf_ref.at[0], sem.at[0])

    _wait_row_gather(ys_ref, buf_ref.at[slot], sem.at[slot])
    n_tiles = pl.num_programs(0)
    nxt = jnp.minimum(i + 1, n_tiles - 1)
    _start_rows_unrolled(ys_ref, pos_ref, nxt * tm, buf_ref.at[1 - slot], 0, tm, sem.at[1 - slot],
                         both_queues=True)

    nb = tm // ROW_BLOCK
    upd = buf_ref[slot].reshape(nb, ROW_BLOCK, d) * gate_ref[...][:, None, :]
    x = x_ref[...] + upd.reshape(tm, d)
    if final_norm:
        x = x * lax.rsqrt(jnp.mean(x * x, axis=-1, keepdims=True) + EPS) * fg_ref[...]
    if n_first is None:
        o_refs[0][...] = x
    else:
        @pl.when(pl.program_id(0) < n_first)
        def _():
            o_refs[0][...] = x

        @pl.when(pl.program_id(0) >= n_first)
        def _():
            o_refs[1][...] = x

    @pl.when(i == n_tiles - 1)
    def _():
        _wait_row_gather(ys_ref, buf_ref.at[1 - slot], sem.at[1 - slot])


def _combine(x, ys, pos, modb, final_g, final_norm, split_rows=None, tm=512):
    t_rows = x.shape[0]
    nb = tm // ROW_BLOCK
    if split_rows is None:
        n_first = None
        out_specs = pl.BlockSpec((tm, D_MODEL), lambda i, p: (i, 0))
        out_shape = jax.ShapeDtypeStruct((t_rows, D_MODEL), F32)
    else:
        assert split_rows % tm == 0 and (t_rows - split_rows) % tm == 0
        n_first = split_rows // tm
        out_specs = [pl.BlockSpec((tm, D_MODEL), lambda i, p: (jnp.minimum(i, n_first - 1), 0)),
                     pl.BlockSpec((tm, D_MODEL), lambda i, p: (jnp.maximum(i - n_first, 0), 0))]
        out_shape = [jax.ShapeDtypeStruct((split_rows, D_MODEL), F32),
                     jax.ShapeDtypeStruct((t_rows - split_rows, D_MODEL), F32)]
    grid_spec = pltpu.PrefetchScalarGridSpec(
        num_scalar_prefetch=1,
        grid=(t_rows // tm,),
        in_specs=[
            pl.BlockSpec((tm, D_MODEL), lambda i, p: (i, 0)),
            pl.BlockSpec((nb, D_MODEL), lambda i, p: (i, 5)),
            pl.BlockSpec((1, D_MODEL), lambda i, p: (0, 0)),
            pl.BlockSpec(memory_space=pl.ANY),
        ],
        out_specs=out_specs,
        scratch_shapes=[pltpu.VMEM((2, tm, D_MODEL), F32), pltpu.SemaphoreType.DMA((2,))],
    )
    return pl.pallas_call(
        functools.partial(_combine_kernel, final_norm, n_first),
        grid_spec=grid_spec,
        out_shape=out_shape,
        compiler_params=_cparams(1),
        name="combine",
    )(pos, x, modb, final_g.reshape(1, D_MODEL), ys)


def kernel(x_prompt, x_sample, state_ssm, state_conv, c_prompt, c_sample, norm1_g, norm2_g, w_ada, b_ada,
           w_in, gm_ln_g, gm_ln_b, gm_w_s, gm_b_s, conv_w, conv_b, dt_bias, a_log, d_skip, mnorm_g, w_out,
           w_router_group, b_router_group, w_router_expert, b_router_expert, w_gate, w_up, w_down,
           final_norm_g):
    n_p, len_p, _ = x_prompt.shape
    n_s, len_s, _ = x_sample.shape
    assert len_p % GM_TILE == 0 and len_s == ROW_BLOCK and (n_s * len_s) % GM_TILE == 0
    rows_p, rows_s = n_p * len_p, n_s * len_s
    t_rows = rows_p + rows_s
    n_seq = n_p + n_s
    blk_per_prompt = len_p // ROW_BLOCK
    assert t_rows % EXPERT_TILE == 0
    n_expert_tiles = t_rows // EXPERT_TILE + N_EGROUPS

    xs = (x_prompt.reshape(rows_p, D_MODEL), x_sample.reshape(rows_s, D_MODEL))

    c_all = jnp.concatenate([c_prompt, c_sample], axis=0)
    c_pad = jnp.pad(c_all, ((0, (-n_seq) % 8), (0, 0)))
    mod = _ada(c_pad, w_ada, b_ada)
    modb_all = jnp.concatenate(
        [jnp.repeat(mod[:, :n_p], blk_per_prompt, axis=1), mod[:, n_p:n_seq]], axis=1)

    half_w = gm_w_s[:, :, :ROW_BLOCK, :ROW_BLOCK]
    w_cat = jnp.stack([gm_w_s, jnp.tile(half_w, (1, 1, 2, 2))], axis=1)
    b_t = jnp.swapaxes(gm_b_s, 1, 2)
    b_cat = jnp.stack([b_t, jnp.tile(b_t[:, :ROW_BLOCK], (1, 2, 1))], axis=1)

    w_in_t = jnp.swapaxes(w_in, 1, 2)
    w_r = jnp.concatenate([w_router_group, w_router_expert], axis=2)
    w_r = jnp.pad(w_r, ((0, 0), (0, 0), (0, LANES - w_r.shape[2])))
    b_r = jnp.concatenate([b_router_group, b_router_expert], axis=1)
    b_r = jnp.pad(b_r, ((0, 0), (0, LANES - b_r.shape[1])))[:, None, :]

    ssm_p, conv_p, ssm_s, conv_s, v_rows = [], [], [], [], []
    for l in range(DEPTH):
        last = l == DEPTH - 1
        modb = modb_all[l]
        proj, dt_raw = _inproj(xs, norm1_g, modb, w_in_t, l, tn=512 if len(xs) == 2 else 1024)
        gm_out, vn = _gmlp(proj, gm_ln_g, gm_ln_b, w_cat, b_cat, l, rows_p // GM_TILE, rows_s)
        ssm_args = (conv_w, conv_b, dt_bias, a_log, d_skip, mnorm_g, l)
        ssm_out_p, hfin_p, cfin_p = _ssd(proj, dt_raw, None, None, *ssm_args, 0, n_p, len_p, SSD_PROMPT_CHUNK)
        ssm_out_s, hfin_s, cfin_s = _ssd(proj, dt_raw, state_ssm, state_conv, *ssm_args, rows_p, n_s, len_s, len_s)
        x = _outproj(gm_out, (ssm_out_p, ssm_out_s), w_out, xs, modb, l, tn=512 if len(xs) == 2 else 1024)
        h2, route, counts = _router(x, norm2_g, modb, w_r[l], b_r[l], l)
        pos, src, tile_group, tile_nsub = _dispatch_tables(route, counts, t_rows, EXPERT_TILE, n_expert_tiles)
        ys = _experts(h2, src, tile_group, tile_nsub, w_gate, w_up, w_down, l, EXPERT_TILE)
        out = _combine(x, ys, pos, modb, final_norm_g, last, rows_p if last else None)
        xs = tuple(out) if last else (out,)
        ssm_p.append(hfin_p)
        conv_p.append(cfin_p)
        ssm_s.append(hfin_s)
        conv_s.append(cfin_s)
        v_rows.append(vn.reshape(n_s, len_s, GM_WIDTH))

    y_prompt = xs[0].reshape(n_p, len_p, D_MODEL)
    y_sample = xs[1].reshape(n_s, len_s, D_MODEL)
    return (y_prompt, y_sample, jnp.stack(ssm_p), jnp.stack(conv_p), jnp.stack(ssm_s), jnp.stack(conv_s),
            jnp.stack(v_rows))
```

```python
import functools
import math

import jax
import jax.numpy as jnp
import numpy as np
from jax import lax
from jax.experimental import pallas as pl
from jax.experimental.pallas import tpu as pltpu

D_MODEL = 2048
DEPTH = 2
GM_WIDTH = 1024
GM_HEADS = 8
GM_HEAD_DIM = 128
GM_CHUNK = 128
M_WIDTH = 1024
M_HEAD_DIM = 64
M_HEADS = 16
M_GROUPS = 4
M_STATE = 128
M_CONV = 4
CONV_DIM = M_WIDTH + 2 * M_GROUPS * M_STATE
MAIN_DIM = 2 * GM_WIDTH + M_WIDTH + CONV_DIM
N_EGROUPS = 4
EXPERTS_PER_GROUP = 4
N_EXPERTS = 16
D_EXPERT = 512
N_MOD = 6
EPS = 1e-6

ROW_BLOCK = 64
LANES = 128
VMEM_LIMIT = 56 * 1024 * 1024

F32 = jnp.float32
BF16 = jnp.bfloat16


def _cparams(n_axes):
    return pltpu.CompilerParams(
        dimension_semantics=("arbitrary",) * n_axes, vmem_limit_bytes=VMEM_LIMIT)


def _silu(x):
    return x * jax.nn.sigmoid(x)


def _gelu_tanh(x):
    c = math.sqrt(2.0 / math.pi)
    return 0.5 * x * (1.0 + jnp.tanh(c * (x + 0.044715 * (x * x * x))))


def _dot(a, b):
    return jnp.dot(a, b, preferred_element_type=F32)


def _dot_nt(a, b, precision=None):
    return lax.dot_general(a, b, (((1,), (1,)), ((), ())), precision=precision,
                           preferred_element_type=F32)


def _dot_split2(a, b):
    a0 = a.astype(BF16)
    a1 = (a - a0.astype(F32)).astype(BF16)
    b0 = b.astype(BF16)
    b1 = (b - b0.astype(F32)).astype(BF16)
    return _dot(a0, b0) + _dot(a0, b1) + _dot(a1, b0)


def _dot_tn(a, b):
    return lax.dot_general(a, b, (((0,), (0,)), ((), ())), preferred_element_type=F32)


def _ada_kernel(c_ref, w_ref, b_ref, o_ref):
    c = _silu(c_ref[...]).astype(BF16)
    o_ref[...] = _dot(c, w_ref[...].astype(BF16)) + b_ref[...]


def _ada(c_pad, w_ada, b_ada):
    rows = c_pad.shape[0]
    tn = 1024
    n_out = N_MOD * D_MODEL
    return pl.pallas_call(
        _ada_kernel,
        grid=(DEPTH, n_out // tn),
        in_specs=[
            pl.BlockSpec((rows, D_MODEL), lambda l, j: (0, 0)),
            pl.BlockSpec((None, D_MODEL, tn), lambda l, j: (l, 0, j)),
            pl.BlockSpec((None, 1, tn), lambda l, j: (l, 0, j)),
        ],
        out_specs=pl.BlockSpec((None, rows, tn), lambda l, j: (l, 0, j)),
        out_shape=jax.ShapeDtypeStruct((DEPTH, rows, n_out), F32),
        compiler_params=_cparams(2),
        name="ada_ln",
    )(c_pad, w_ada, b_ada.reshape(DEPTH, 1, n_out))


def _stream_specs(xs, block, col_of, n_j):
    if len(xs) == 1:
        return [pl.BlockSpec(block, lambda i, j: (i, col_of(j)))], None
    n_first = xs[0].shape[0] // block[0]

    def first(i, j):
        return jnp.minimum(i, n_first - 1), jnp.where(i < n_first, col_of(j), col_of(n_j - 1))

    def second(i, j):
        return jnp.maximum(i - n_first, 0), jnp.where(i < n_first, col_of(0), col_of(j))

    return [pl.BlockSpec(block, first), pl.BlockSpec(block, second)], n_first


def _stream_tile(x_refs, n_first, idx):
    if n_first is None:
        return x_refs[0][idx]
    return jnp.where(pl.program_id(0) < n_first, x_refs[0][idx], x_refs[1][idx])


def _inproj_kernel(n_first, *refs):
    n_x = 1 if n_first is None else 2
    x_refs = refs[:n_x]
    g_ref, shift_ref, scale_ref, w_ref, wdt_ref, proj_ref, dt_ref, h_ref = refs[n_x:]
    j = pl.program_id(1)
    n_blk = shift_ref.shape[0]

    @pl.when(j == 0)
    def _():
        g = g_ref[...]

        def body(r, carry):
            rows = pl.ds(pl.multiple_of(r * ROW_BLOCK, ROW_BLOCK), ROW_BLOCK)
            xf = _stream_tile(x_refs, n_first, (rows, slice(None)))
            y = xf * lax.rsqrt(jnp.mean(xf * xf, axis=-1, keepdims=True) + EPS) * g
            y = y * (1.0 + scale_ref[pl.ds(r, 1), :]) + shift_ref[pl.ds(r, 1), :]
            h_ref[rows, :] = y.astype(BF16)
            return carry

        lax.fori_loop(0, n_blk, body, 0)
        dt_ref[...] = _dot_nt(h_ref[...], wdt_ref[...].astype(BF16))

    proj_ref[...] = _dot_nt(h_ref[...], w_ref[...].astype(BF16)).astype(BF16)


def _inproj(xs, norm_g, modb, w_in_t, layer, tm=1024, tn=512):
    t_rows = sum(x.shape[0] for x in xs)
    nb = tm // ROW_BLOCK
    assert all(x.shape[0] % tm == 0 for x in xs)
    x_specs, n_first = _stream_specs(xs, (tm, D_MODEL), lambda j: 0, MAIN_DIM // tn)
    return pl.pallas_call(
        functools.partial(_inproj_kernel, n_first),
        grid=(t_rows // tm, MAIN_DIM // tn),
        in_specs=x_specs + [
            pl.BlockSpec((None, 1, D_MODEL), lambda i, j: (layer, 0, 0)),
            pl.BlockSpec((nb, D_MODEL), lambda i, j: (i, 0)),
            pl.BlockSpec((nb, D_MODEL), lambda i, j: (i, 1)),
            pl.BlockSpec((None, tn, D_MODEL), lambda i, j: (layer, j, 0)),
            pl.BlockSpec((None, M_HEADS, D_MODEL), lambda i, j: (layer, MAIN_DIM // M_HEADS, 0)),
        ],
        out_specs=[
            pl.BlockSpec((tm, tn), lambda i, j: (i, j)),
            pl.BlockSpec((tm, M_HEADS), lambda i, j: (i, 0)),
        ],
        out_shape=[
            jax.ShapeDtypeStruct((t_rows, MAIN_DIM), BF16),
            jax.ShapeDtypeStruct((t_rows, M_HEADS), F32),
        ],
        scratch_shapes=[pltpu.VMEM((tm, D_MODEL), BF16)],
        compiler_params=_cparams(2),
        name="in_proj",
    )(*xs, norm_g.reshape(DEPTH, 1, D_MODEL), modb, modb, w_in_t, w_in_t)


def _gmlp_kernel(n_prompt_tiles, u_ref, v_ref, lng_ref, lnb_ref, w_ref, b_ref, out_ref, vn_ref):
    i = pl.program_id(0)
    t_idx = lax.broadcasted_iota(jnp.int32, (GM_CHUNK, GM_CHUNK), 0)
    s_idx = lax.broadcasted_iota(jnp.int32, (GM_CHUNK, GM_CHUNK), 1)
    same_seq = (t_idx // ROW_BLOCK) == (s_idx // ROW_BLOCK)
    allowed = (s_idx <= t_idx) & (same_seq | (i < n_prompt_tiles))
    w_masked = [jnp.where(allowed, w_ref[h], 0.0).astype(BF16) for h in range(GM_HEADS)]
    bias = b_ref[...]
    for r0 in range(0, u_ref.shape[0], GM_CHUNK):
        rows = slice(r0, r0 + GM_CHUNK)
        u = _gelu_tanh(u_ref[rows, :].astype(F32))
        v = _gelu_tanh(v_ref[rows, :].astype(F32))
        mu = jnp.mean(v, axis=-1, keepdims=True)
        vc = v - mu
        vn = vc * lax.rsqrt(jnp.mean(vc * vc, axis=-1, keepdims=True) + EPS)
        vn = vn * lng_ref[...] + lnb_ref[...]
        vn_ref[rows, :] = vn
        vnb = vn.astype(BF16)
        for h in range(GM_HEADS):
            cols = slice(h * GM_HEAD_DIM, (h + 1) * GM_HEAD_DIM)
            mixed = _dot(w_masked[h], vnb[:, cols]) + bias[:, h:h + 1]
            out_ref[rows, cols] = (u[:, cols] * mixed).astype(BF16)


def _gmlp(proj, ln_g, ln_b, w_cat, b_cat, layer, n_prompt_tiles, n_sample_rows):
    t_rows = proj.shape[0]
    n_tiles = t_rows // GM_TILE
    kern = functools.partial(_gmlp_kernel, n_prompt_tiles)
    return pl.pallas_call(
        kern,
        grid=(n_tiles,),
        in_specs=[
            pl.BlockSpec((GM_TILE, GM_WIDTH), lambda i: (i, 0)),
            pl.BlockSpec((GM_TILE, GM_WIDTH), lambda i: (i, 1)),
            pl.BlockSpec((None, 1, GM_WIDTH), lambda i: (layer, 0, 0)),
            pl.BlockSpec((None, 1, GM_WIDTH), lambda i: (layer, 0, 0)),
            pl.BlockSpec((None, None, GM_HEADS, GM_CHUNK, GM_CHUNK),
                         lambda i: (layer, jnp.where(i < n_prompt_tiles, 0, 1), 0, 0, 0)),
            pl.BlockSpec((None, None, GM_CHUNK, GM_HEADS),
                         lambda i: (layer, jnp.where(i < n_prompt_tiles, 0, 1), 0, 0)),
        ],
        out_specs=[
            pl.BlockSpec((GM_TILE, GM_WIDTH), lambda i: (i, 0)),
            pl.BlockSpec((GM_TILE, GM_WIDTH), lambda i: (jnp.maximum(i - n_prompt_tiles, 0), 0)),
        ],
        out_shape=[
            jax.ShapeDtypeStruct((t_rows, GM_WIDTH), BF16),
            jax.ShapeDtypeStruct((n_sample_rows, GM_WIDTH), F32),
        ],
        compiler_params=_cparams(1),
        name="gmlp",
    )(proj, proj, ln_g.reshape(DEPTH, 1, GM_WIDTH), ln_b.reshape(DEPTH, 1, GM_WIDTH), w_cat, b_cat)


GM_TILE = 4 * GM_CHUNK
CONV_PAD = 8
SSD_PROMPT_CHUNK = 256


def _widen(v, onehot):
    p0 = v.astype(BF16)
    r1 = v - p0.astype(F32)
    p1 = r1.astype(BF16)
    p2 = (r1 - p1.astype(F32)).astype(BF16)
    return _dot(p0, onehot) + _dot(p1, onehot) + _dot(p2, onehot)


def _ssd_kernel(q, zero_init, *refs):
    z_ref, xa_ref, xb_ref, dt_ref = refs[:4]
    refs = refs[4:]
    h0_ref = c0_ref = None
    if not zero_init:
        h0_ref, c0_ref = refs[:2]
        refs = refs[2:]
    cw_ref, cb_ref, dtb_ref, alog_ref, dskip_ref, ng_ref = refs[:6]
    refs = refs[6:]
    out_ref, hfin_ref, cfin_ref, state_ref, win_ref, xs_ref, bc_ref, xdt_ref, xdd_ref, y_ref = refs
    c = pl.program_id(1)
    tail = M_CONV - 1
    lo = CONV_PAD - tail
    hp = M_HEADS * M_HEAD_DIM

    @pl.when(c == 0)
    def _():
        if zero_init:
            state_ref[...] = jnp.zeros_like(state_ref)
            win_ref[lo:CONV_PAD, :] = jnp.zeros((tail, CONV_DIM), F32)
        else:
            state_ref[...] = h0_ref[...].reshape(hp, M_STATE).T
            win_ref[lo:CONV_PAD, :] = c0_ref[...]

    half = CONV_DIM // 2
    win_ref[CONV_PAD:CONV_PAD + q, 0:half] = xa_ref[...].astype(F32)
    win_ref[CONV_PAD:CONV_PAD + q, half:CONV_DIM] = xb_ref[...].astype(F32)
    for r0 in range(0, q, ROW_BLOCK):
        conv = cb_ref[...]
        for k in range(M_CONV):
            conv = conv + win_ref[lo + k + r0:lo + k + r0 + ROW_BLOCK, :] * cw_ref[k:k + 1, :]
        act = _silu(conv)
        xs_ref[r0:r0 + ROW_BLOCK, :] = act[:, 0:M_WIDTH]
        bc_ref[r0:r0 + ROW_BLOCK, :] = act[:, M_WIDTH:CONV_DIM].astype(BF16)
    new_tail = win_ref[CONV_PAD + q - tail:CONV_PAD + q, :]
    cfin_ref[...] = new_tail
    win_ref[lo:CONV_PAD, :] = new_tail

    hi = lax.Precision.HIGHEST
    dt = jax.nn.softplus(dt_ref[...] + dtb_ref[...])
    a = -jnp.exp(alog_ref[...])
    r_idx = lax.broadcasted_iota(jnp.int32, (q, q), 0)
    c_idx = lax.broadcasted_iota(jnp.int32, (q, q), 1)
    causal = c_idx <= r_idx
    a_cs = jnp.dot(causal.astype(F32), dt * a, precision=hi, preferred_element_type=F32)
    eye = (lax.broadcasted_iota(jnp.int32, (M_HEADS, M_HEADS), 0)
           == lax.broadcasted_iota(jnp.int32, (M_HEADS, M_HEADS), 1)).astype(F32)
    a_cs_t = _dot_nt(eye, a_cs, precision=hi)
    widen = (lax.broadcasted_iota(jnp.int32, (M_HEADS, hp), 1) // M_HEAD_DIM
             == lax.broadcasted_iota(jnp.int32, (M_HEADS, hp), 0)).astype(BF16)
    dt_w = _widen(dt, widen)
    acs_w = _widen(a_cs, widen)
    dskip_w = _widen(jnp.broadcast_to(dskip_ref[...], (8, M_HEADS)), widen)[0:1, :]
    a_end_w = acs_w[q - 1:q, :]
    x_dt = xs_ref[...] * dt_w
    xdt_ref[...] = x_dt.astype(BF16)
    xdd_ref[...] = (x_dt * jnp.exp(a_end_w - acs_w)).astype(BF16)
    decay_blk_w = jnp.exp(a_end_w)

    heads_per_group = M_HEADS // M_GROUPS
    gw = M_WIDTH // M_GROUPS
    pair_lane = lax.broadcasted_iota(jnp.int32, (q, 2 * M_HEAD_DIM), 1)
    for g in range(M_GROUPS):
        gcols = slice(g * gw, (g + 1) * gw)
        b_g = bc_ref[:, g * M_STATE:(g + 1) * M_STATE]
        c_g = bc_ref[:, (M_GROUPS + g) * M_STATE:(M_GROUPS + g + 1) * M_STATE]
        cb = _dot_nt(c_g, b_g)
        st = state_ref[:, gcols]
        y_off = _dot(c_g, st.astype(BF16))
        state_ref[:, gcols] = st * decay_blk_w[:, gcols] + _dot_tn(b_g, xdd_ref[:, gcols])
        for pr in range(heads_per_group // 2):
            h0 = g * heads_per_group + 2 * pr
            pcols = slice(h0 * M_HEAD_DIM, (h0 + 2) * M_HEAD_DIM)
            x_pair = xdt_ref[:, pcols]
            ys = []
            for h in (h0, h0 + 1):
                seg = a_cs[:, h:h + 1] - a_cs_t[h:h + 1, :]
                scores = cb * jnp.exp(jnp.where(causal, seg, -jnp.inf))
                ys.append(_dot(scores.astype(BF16), x_pair))
            y_ref[:, pcols] = jnp.where(pair_lane < M_HEAD_DIM, ys[0], ys[1])
        y_ref[:, gcols] = (y_ref[:, gcols] + y_off * jnp.exp(acs_w[:, gcols])
                           + dskip_w[:, gcols] * xs_ref[:, gcols])

    @pl.when(c == pl.num_programs(1) - 1)
    def _():
        hfin_ref[...] = state_ref[...].T.reshape(M_HEADS, M_HEAD_DIM, M_STATE)

    gy = y_ref[...] * _silu(z_ref[...].astype(F32))
    ng = ng_ref[...]
    for g in range(M_GROUPS):
        cols = slice(g * gw, (g + 1) * gw)
        s = gy[:, cols]
        s = s * lax.rsqrt(jnp.mean(s * s, axis=-1, keepdims=True) + EPS)
        out_ref[:, cols] = (s * ng[:, cols]).astype(BF16)


def _ssd(proj, dt_raw, h0, c0, conv_w, conv_b, dt_bias, a_log, d_skip, mnorm_g, layer,
         row_start, n_seq, seq_len, q):
    n_c = seq_len // q
    blk0 = row_start // q
    half = CONV_DIM // 2
    z_blk = (2 * GM_WIDTH) // M_WIDTH
    xa_blk = (2 * GM_WIDTH + M_WIDTH) // half
    zero_init = h0 is None

    def rows(col):
        return lambda s, c: (blk0 + s * n_c + c, col)

    def per_layer(*shape):
        return pl.BlockSpec((None,) + shape, lambda s, c: (layer,) + (0,) * len(shape))

    in_specs = [
        pl.BlockSpec((q, M_WIDTH), rows(z_blk)),
        pl.BlockSpec((q, half), rows(xa_blk)),
        pl.BlockSpec((q, half), rows(xa_blk + 1)),
        pl.BlockSpec((q, M_HEADS), rows(0)),
    ]
    args = [proj, proj, proj, dt_raw]
    if not zero_init:
        in_specs += [
            pl.BlockSpec((None, None, M_HEADS, M_HEAD_DIM, M_STATE), lambda s, c: (layer, s, 0, 0, 0)),
            pl.BlockSpec((None, None, M_CONV - 1, CONV_DIM), lambda s, c: (layer, s, 0, 0)),
        ]
        args += [h0, c0]
    in_specs += [per_layer(M_CONV, CONV_DIM), per_layer(1, CONV_DIM), per_layer(1, M_HEADS),
                 per_layer(1, M_HEADS), per_layer(1, M_HEADS), per_layer(1, M_WIDTH)]
    args += [conv_w, conv_b.reshape(DEPTH, 1, CONV_DIM), dt_bias.reshape(DEPTH, 1, M_HEADS),
             a_log.reshape(DEPTH, 1, M_HEADS), d_skip.reshape(DEPTH, 1, M_HEADS),
             mnorm_g.reshape(DEPTH, 1, M_WIDTH)]
    return pl.pallas_call(
        functools.partial(_ssd_kernel, q, zero_init),
        grid=(n_seq, n_c),
        in_specs=in_specs,
        out_specs=[
            pl.BlockSpec((q, M_WIDTH), lambda s, c: (s * n_c + c, 0)),
            pl.BlockSpec((None, M_HEADS, M_HEAD_DIM, M_STATE), lambda s, c: (s, 0, 0, 0)),
            pl.BlockSpec((None, M_CONV - 1, CONV_DIM), lambda s, c: (s, 0, 0)),
        ],
        out_shape=[
            jax.ShapeDtypeStruct((n_seq * seq_len, M_WIDTH), BF16),
            jax.ShapeDtypeStruct((n_seq, M_HEADS, M_HEAD_DIM, M_STATE), F32),
            jax.ShapeDtypeStruct((n_seq, M_CONV - 1, CONV_DIM), F32),
        ],
        scratch_shapes=[
            pltpu.VMEM((M_STATE, M_WIDTH), F32),
            pltpu.VMEM((CONV_PAD + q, CONV_DIM), F32),
            pltpu.VMEM((q, M_WIDTH), F32),
            pltpu.VMEM((q, 2 * M_GROUPS * M_STATE), BF16),
            pltpu.VMEM((q, M_WIDTH), BF16),
            pltpu.VMEM((q, M_WIDTH), BF16),
            pltpu.VMEM((q, M_WIDTH), F32),
        ],
        compiler_params=_cparams(2),
        name="conv_ssd",
    )(*args)


def _outproj_kernel(n_first_ssm, n_first, gm_ref, wa_ref, wb_ref, gate_ref, *refs):
    n_ssm = 1 if n_first_ssm is None else 2
    ssm_refs, x_refs, o_ref = refs[:n_ssm], refs[n_ssm:-1], refs[-1]
    ssm = _stream_tile(ssm_refs, n_first_ssm, Ellipsis)
    acc = _dot(gm_ref[...], wa_ref[...].astype(BF16)) + _dot(ssm, wb_ref[...].astype(BF16))
    tm, tn = acc.shape
    nb = tm // ROW_BLOCK
    upd = acc.reshape(nb, ROW_BLOCK, tn) * gate_ref[...][:, None, :]
    o_ref[...] = _stream_tile(x_refs, n_first, Ellipsis) + upd.reshape(tm, tn)


def _outproj(gm, ssms, w_out, xs, modb, layer, tm=1024, tn=512):
    t_rows = gm.shape[0]
    nb = tm // ROW_BLOCK
    gate_blk = 2 * D_MODEL // tn
    assert all(x.shape[0] % tm == 0 for x in xs)
    assert all(x.shape[0] % tm == 0 for x in ssms)
    x_specs, n_first = _stream_specs(xs, (tm, tn), lambda j: j, D_MODEL // tn)
    ssm_specs, n_first_ssm = _stream_specs(ssms, (tm, M_WIDTH), lambda j: 0, D_MODEL // tn)
    return pl.pallas_call(
        functools.partial(_outproj_kernel, n_first_ssm, n_first),
        grid=(t_rows // tm, D_MODEL // tn),
        in_specs=[
            pl.BlockSpec((tm, GM_WIDTH), lambda i, j: (i, 0)),
            pl.BlockSpec((None, GM_WIDTH, tn), lambda i, j: (layer, 0, j)),
            pl.BlockSpec((None, M_WIDTH, tn), lambda i, j: (layer, 1, j)),
            pl.BlockSpec((nb, tn), lambda i, j: (i, gate_blk + j)),
        ] + ssm_specs + x_specs,
        out_specs=pl.BlockSpec((tm, tn), lambda i, j: (i, j)),
        out_shape=jax.ShapeDtypeStruct((t_rows, D_MODEL), F32),
        compiler_params=_cparams(2),
        name="out_proj",
    )(gm, w_out, w_out, modb, *ssms, *xs)


ROUTE_ROWS = 8
EXPERT_TILE = 1024
EXPERT_SUB = 256
WEIGHT_RING = 3


def _row_argmax_first(rows):
    m = rows[0]
    for r in rows[1:]:
        m = jnp.maximum(m, r)
    idx = jnp.full(m.shape, len(rows), jnp.int32)
    for k in reversed(range(len(rows))):
        idx = jnp.where(rows[k] == m, k, idx)
    return m, idx


def _router_kernel(x_ref, g_ref, shift_ref, scale_ref, wr_ref, br_ref, h_ref, route_ref, count_ref, run_ref):
    n_blk = shift_ref.shape[0]
    g = g_ref[...]

    @pl.when(pl.program_id(0) == 0)
    def _():
        run_ref[...] = jnp.zeros_like(run_ref)

    def body(r, carry):
        rows = pl.ds(pl.multiple_of(r * ROW_BLOCK, ROW_BLOCK), ROW_BLOCK)
        xf = x_ref[rows, :]
        y = xf * lax.rsqrt(jnp.mean(xf * xf, axis=-1, keepdims=True) + EPS) * g
        h_ref[rows, 0:D_MODEL] = y * (1.0 + scale_ref[pl.ds(r, 1), :]) + shift_ref[pl.ds(r, 1), :]
        return carry

    lax.fori_loop(0, n_blk, body, 0)
    logits = (_dot_split2(h_ref[:, 0:D_MODEL], wr_ref[...]) + br_ref[...]).T
    lg = [logits[k:k + 1, :] for k in range(N_EGROUPS)]
    g_max, g_idx = _row_argmax_first(lg)
    g_den = lg[0] * 0.0
    for r in lg:
        g_den = g_den + jnp.exp(r - g_max)
    g_top = 1.0 / g_den
    e_in = []
    for k in range(EXPERTS_PER_GROUP):
        sel = lg[0] * 0.0
        for gi in range(N_EGROUPS):
            row = N_EGROUPS + gi * EXPERTS_PER_GROUP + k
            sel = jnp.where(g_idx == gi, logits[row:row + 1, :], sel)
        e_in.append(sel)
    e_max, i1 = _row_argmax_first(e_in)
    e_exp = [jnp.exp(r - e_max) for r in e_in]
    e_den = e_exp[0]
    for r in e_exp[1:]:
        e_den = e_den + r
    p = [r / e_den for r in e_exp]
    p1 = p[0] * 0.0
    for k in range(EXPERTS_PER_GROUP):
        p1 = jnp.where(i1 == k, p[k], p1)
    rest = [jnp.where(i1 == k, -1.0, p[k]) for k in range(EXPERTS_PER_GROUP)]
    p2, i2 = _row_argmax_first(rest)
    norm = p1 + p2
    w1 = p1 / norm * g_top
    w2 = p2 / norm * g_top
    tm = logits.shape[1]
    row = lax.broadcasted_iota(jnp.int32, (ROUTE_ROWS, tm), 0)
    onehot = (row == g_idx).astype(F32)
    before = (lax.broadcasted_iota(jnp.int32, (tm, tm), 0)
              < lax.broadcasted_iota(jnp.int32, (tm, tm), 1)).astype(BF16)
    earlier = _dot(onehot.astype(BF16), before)
    rank = jnp.sum(onehot * (earlier + run_ref[:, 0:1]), axis=0, keepdims=True)
    run_ref[...] += jnp.sum(onehot, axis=1, keepdims=True)
    count_ref[...] = run_ref[...]

    route_ref[0:1, :] = g_idx.astype(F32)
    route_ref[1:2, :] = rank
    for k in range(EXPERTS_PER_GROUP):
        route_ref[2 + k:3 + k, :] = jnp.where(i1 == k, w1, 0.0) + jnp.where(i2 == k, w2, 0.0)
    route_ref[2 + EXPERTS_PER_GROUP:ROUTE_ROWS, :] = jnp.zeros((ROUTE_ROWS - 2 - EXPERTS_PER_GROUP, tm), F32)
    place = (lax.broadcasted_iota(jnp.int32, (ROUTE_ROWS, LANES), 0)
             == lax.broadcasted_iota(jnp.int32, (ROUTE_ROWS, LANES), 1)).astype(F32)
    h_ref[:, D_MODEL:D_MODEL + LANES] = lax.dot_general(
        route_ref[...], place, (((0,), (0,)), ((), ())), precision=lax.Precision.HIGHEST,
        preferred_element_type=F32)


def _router(x, norm_g, modb, w_r, b_r, layer, tm=512):
    t_rows = x.shape[0]
    nb = tm // ROW_BLOCK
    return pl.pallas_call(
        _router_kernel,
        grid=(t_rows // tm,),
        in_specs=[
            pl.BlockSpec((tm, D_MODEL), lambda i: (i, 0)),
            pl.BlockSpec((None, 1, D_MODEL), lambda i: (layer, 0, 0)),
            pl.BlockSpec((nb, D_MODEL), lambda i: (i, 3)),
            pl.BlockSpec((nb, D_MODEL), lambda i: (i, 4)),
            pl.BlockSpec((D_MODEL, LANES), lambda i: (0, 0)),
            pl.BlockSpec((1, LANES), lambda i: (0, 0)),
        ],
        out_specs=[
            pl.BlockSpec((tm, D_MODEL + LANES), lambda i: (i, 0)),
            pl.BlockSpec((ROUTE_ROWS, tm), lambda i: (0, i)),
            pl.BlockSpec((ROUTE_ROWS, LANES), lambda i: (0, 0)),
        ],
        out_shape=[
            jax.ShapeDtypeStruct((t_rows, D_MODEL + LANES), F32),
            jax.ShapeDtypeStruct((ROUTE_ROWS, t_rows), F32),
            jax.ShapeDtypeStruct((ROUTE_ROWS, LANES), F32),
        ],
        scratch_shapes=[pltpu.VMEM((ROUTE_ROWS, LANES), F32)],
        compiler_params=_cparams(1),
        name="router",
    )(x, norm_g.reshape(DEPTH, 1, D_MODEL), modb, modb, w_r, b_r)


def _dispatch_tables(route, counts, t_rows, tm, n_tiles):
    gid = route[0].astype(jnp.int32)
    rank = route[1].astype(jnp.int32)
    cnt = counts[:N_EGROUPS, 0].astype(jnp.int32)
    padded = (cnt + tm - 1) // tm * tm
    ends = jnp.cumsum(padded)
    starts = ends - padded
    pos = starts[gid] + rank
    src = jnp.zeros((n_tiles * tm,), jnp.int32).at[pos].set(jnp.arange(t_rows, dtype=jnp.int32))
    tile_start = jnp.arange(n_tiles, dtype=jnp.int32) * tm
    tile_group = jnp.sum((tile_start[:, None] >= ends[None, :]).astype(jnp.int32), axis=1)
    tile_group = jnp.minimum(tile_group, N_EGROUPS - 1)
    tile_rows = jnp.clip(cnt[tile_group] - (tile_start - starts[tile_group]), 0, tm)
    tile_nsub = (tile_rows + EXPERT_SUB - 1) // EXPERT_SUB
    return pos, src, tile_group, tile_nsub


def _start_row_gather(table_ref, idx_ref, idx_base, buf_ref, sem):
    def issue(r, carry):
        pltpu.make_async_copy(table_ref.at[pl.ds(idx_ref[idx_base + r], 1), :],
                              buf_ref.at[pl.ds(r, 1), :], sem).start()
        return carry

    lax.fori_loop(0, buf_ref.shape[0], issue, 0, unroll=8)


def _start_rows_unrolled(table_ref, idx_ref, idx_base, buf_ref, row0, n_rows, sem):
    for r in range(n_rows):
        pltpu.make_async_copy(table_ref.at[pl.ds(idx_ref[idx_base + r], 1), :],
                              buf_ref.at[pl.ds(row0 + r, 1), :], sem).start()


def _wait_row_gather(table_ref, buf_ref, sem):
    pltpu.make_async_copy(table_ref.at[pl.ds(0, buf_ref.shape[0]), :], buf_ref, sem).wait()


def _experts_kernel(f_split, layer, tg_ref, ns_ref, src_ref, wg_hbm, wu_hbm, wd_hbm, h_ref, y_ref,
                    xbuf_ref, xb_ref, rec_ref, wgb_ref, wub_ref, wdb_ref, wg_ring, wu_ring, wd_ring, sem, wsem):
    i = pl.program_id(0)
    s = pl.program_id(1)
    n_tiles = pl.num_programs(0)
    tm = xb_ref.shape[0]
    n_steps = pl.num_programs(1)
    n_sub = ns_ref[i]
    step_rows = tm // (EXPERTS_PER_GROUP * f_split)
    nxt = jnp.minimum(i + 1, n_tiles - 1)
    fb = wg_ring.shape[2]
    t = i * n_steps + s

    def weight_copies(step):
        tile = jnp.minimum(step // n_steps, n_tiles - 1)
        e = tg_ref[tile] * EXPERTS_PER_GROUP + (step % n_steps) // f_split
        c0 = pl.multiple_of(((step % n_steps) % f_split) * fb, fb)
        slot = step % WEIGHT_RING
        return (pltpu.make_async_copy(wg_hbm.at[layer, e, :, pl.ds(c0, fb)], wg_ring.at[slot], wsem.at[slot]),
                pltpu.make_async_copy(wu_hbm.at[layer, e, :, pl.ds(c0, fb)], wu_ring.at[slot], wsem.at[slot]),
                pltpu.make_async_copy(wd_hbm.at[layer, e, pl.ds(c0, fb), :], wd_ring.at[slot], wsem.at[slot]))

    def step_occupied(step):
        return (step < n_tiles * n_steps) & (ns_ref[jnp.minimum(step // n_steps, n_tiles - 1)] > 0)

    @pl.when((t == 0) & (n_sub > 0))
    def _():
        for ahead in range(WEIGHT_RING - 1):
            for cp in weight_copies(ahead):
                cp.start()

    @pl.when((s == 0) & (i == 0))
    def _():
        _start_row_gather(h_ref, src_ref, 0, xbuf_ref, sem)

    @pl.when(s == 0)
    def _():
        y_ref[...] = jnp.zeros_like(y_ref)

        @pl.when((i == 0) | (ns_ref[jnp.maximum(i - 1, 0)] > 0))
        def _():
            _wait_row_gather(h_ref, xbuf_ref, sem)

        def keep(b, carry):
            rows = pl.ds(pl.multiple_of(b * EXPERT_SUB, EXPERT_SUB), EXPERT_SUB)
            xb_ref[rows, :] = xbuf_ref[rows, 0:D_MODEL].astype(BF16)
            rec_ref[rows, :] = xbuf_ref[rows, D_MODEL:D_MODEL + LANES]
            return carry

        lax.fori_loop(0, n_sub, keep, 0)

    @pl.when(n_sub > 0)
    def _():
        for cp in weight_copies(t):
            cp.wait()

        @pl.when(step_occupied(t + WEIGHT_RING - 1))
        def _():
            for cp in weight_copies(t + WEIGHT_RING - 1):
                cp.start()

        slot = t % WEIGHT_RING
        wgb_ref[...] = wg_ring[slot].astype(BF16)
        wub_ref[...] = wu_ring[slot].astype(BF16)
        wdb_ref[...] = wd_ring[slot].astype(BF16)

    for k in range(1, tm // EXPERT_SUB + 1):
        @pl.when(n_sub == k)
        def _(k=k):
            _start_rows_unrolled(h_ref, src_ref, nxt * tm + s * step_rows, xbuf_ref, s * step_rows,
                                 step_rows, sem)
            e = s // f_split
            rows = slice(0, k * EXPERT_SUB)
            x = xb_ref[rows, :]
            gate_act = _dot(x, wgb_ref[...])
            up = _dot(x, wub_ref[...])
            record = rec_ref[rows, :]
            lane = lax.broadcasted_iota(jnp.int32, record.shape, 1)
            gate_col = jnp.sum(jnp.where(lane == 2 + e, record, 0.0), axis=1, keepdims=True)
            hid = (_silu(gate_act) * up * gate_col).astype(BF16)
            y_ref[rows, :] += _dot(hid, wdb_ref[...])

    @pl.when((i == n_tiles - 1) & (s == n_steps - 1) & (n_sub > 0))
    def _():
        _wait_row_gather(h_ref, xbuf_ref, sem)


def _experts(h, src, tile_group, tile_nsub, w_gate, w_up, w_down, layer, tm, f_split=2):
    n_rows = src.shape[0]
    fb = D_EXPERT // f_split

    grid_spec = pltpu.PrefetchScalarGridSpec(
        num_scalar_prefetch=3,
        grid=(n_rows // tm, EXPERTS_PER_GROUP * f_split),
        in_specs=[
            pl.BlockSpec(memory_space=pl.ANY),
            pl.BlockSpec(memory_space=pl.ANY),
            pl.BlockSpec(memory_space=pl.ANY),
            pl.BlockSpec(memory_space=pl.ANY),
        ],
        out_specs=pl.BlockSpec((tm, D_MODEL), lambda i, s, *_: (i, 0)),
        scratch_shapes=[
            pltpu.VMEM((tm, D_MODEL + LANES), F32),
            pltpu.VMEM((tm, D_MODEL), BF16),
            pltpu.VMEM((tm, LANES), F32),
            pltpu.VMEM((D_MODEL, fb), BF16),
            pltpu.VMEM((D_MODEL, fb), BF16),
            pltpu.VMEM((fb, D_MODEL), BF16),
            pltpu.VMEM((WEIGHT_RING, D_MODEL, fb), F32),
            pltpu.VMEM((WEIGHT_RING, D_MODEL, fb), F32),
            pltpu.VMEM((WEIGHT_RING, fb, D_MODEL), F32),
            pltpu.SemaphoreType.DMA(()),
            pltpu.SemaphoreType.DMA((WEIGHT_RING,)),
        ],
    )
    return pl.pallas_call(
        functools.partial(_experts_kernel, f_split, layer),
        grid_spec=grid_spec,
        out_shape=jax.ShapeDtypeStruct((n_rows, D_MODEL), F32),
        compiler_params=_cparams(2),
        name="experts",
    )(tile_group, tile_nsub, src, w_gate, w_up, w_down, h)


def _combine_kernel(final_norm, n_first, pos_ref, x_ref, gate_ref, fg_ref, ys_ref, *refs):
    o_refs, buf_ref, sem = refs[:-2], refs[-2], refs[-1]
    tm, d = x_ref.shape
    i = pl.program_id(0)
    slot = i % 2

    @pl.when(i == 0)
    def _():
        _start_row_gather(ys_ref, pos_ref, 0, buf_ref.at[0], sem.at[0])

    _wait_row_gather(ys_ref, buf_ref.at[slot], sem.at[slot])
    n_tiles = pl.num_programs(0)
    nxt = jnp.minimum(i + 1, n_tiles - 1)
    _start_rows_unrolled(ys_ref, pos_ref, nxt * tm, buf_ref.at[1 - slot], 0, tm, sem.at[1 - slot])

    nb = tm // ROW_BLOCK
    upd = buf_ref[slot].reshape(nb, ROW_BLOCK, d) * gate_ref[...][:, None, :]
    x = x_ref[...] + upd.reshape(tm, d)
    if final_norm:
        x = x * lax.rsqrt(jnp.mean(x * x, axis=-1, keepdims=True) + EPS) * fg_ref[...]
    if n_first is None:
        o_refs[0][...] = x
    else:
        @pl.when(pl.program_id(0) < n_first)
        def _():
            o_refs[0][...] = x

        @pl.when(pl.program_id(0) >= n_first)
        def _():
            o_refs[1][...] = x

    @pl.when(i == n_tiles - 1)
    def _():
        _wait_row_gather(ys_ref, buf_ref.at[1 - slot], sem.at[1 - slot])


def _combine(x, ys, pos, modb, final_g, final_norm, split_rows=None, tm=512):
    t_rows = x.shape[0]
    nb = tm // ROW_BLOCK
    if split_rows is None:
        n_first = None
        out_specs = pl.BlockSpec((tm, D_MODEL), lambda i, p: (i, 0))
        out_shape = jax.ShapeDtypeStruct((t_rows, D_MODEL), F32)
    else:
        assert split_rows % tm == 0 and (t_rows - split_rows) % tm == 0
        n_first = split_rows // tm
        out_specs = [pl.BlockSpec((tm, D_MODEL), lambda i, p: (jnp.minimum(i, n_first - 1), 0)),
                     pl.BlockSpec((tm, D_MODEL), lambda i, p: (jnp.maximum(i - n_first, 0), 0))]
        out_shape = [jax.ShapeDtypeStruct((split_rows, D_MODEL), F32),
                     jax.ShapeDtypeStruct((t_rows - split_rows, D_MODEL), F32)]
    grid_spec = pltpu.PrefetchScalarGridSpec(
        num_scalar_prefetch=1,
        grid=(t_rows // tm,),
        in_specs=[
            pl.BlockSpec((tm, D_MODEL), lambda i, p: (i, 0)),
            pl.BlockSpec((nb, D_MODEL), lambda i, p: (i, 5)),
            pl.BlockSpec((1, D_MODEL), lambda i, p: (0, 0)),
            pl.BlockSpec(memory_space=pl.ANY),
        ],
        out_specs=out_specs,
        scratch_shapes=[pltpu.VMEM((2, tm, D_MODEL), F32), pltpu.SemaphoreType.DMA((2,))],
    )
    return pl.pallas_call(
        functools.partial(_combine_kernel, final_norm, n_first),
        grid_spec=grid_spec,
        out_shape=out_shape,
        compiler_params=_cparams(1),
        name="combine",
    )(pos, x, modb, final_g.reshape(1, D_MODEL), ys)


def kernel(x_prompt, x_sample, state_ssm, state_conv, c_prompt, c_sample, norm1_g, norm2_g, w_ada, b_ada,
           w_in, gm_ln_g, gm_ln_b, gm_w_s, gm_b_s, conv_w, conv_b, dt_bias, a_log, d_skip, mnorm_g, w_out,
           w_router_group, b_router_group, w_router_expert, b_router_expert, w_gate, w_up, w_down,
           final_norm_g):
    n_p, len_p, _ = x_prompt.shape
    n_s, len_s, _ = x_sample.shape
    assert len_p % GM_TILE == 0 and len_s == ROW_BLOCK and (n_s * len_s) % GM_TILE == 0
    rows_p, rows_s = n_p * len_p, n_s * len_s
    t_rows = rows_p + rows_s
    n_seq = n_p + n_s
    blk_per_prompt = len_p // ROW_BLOCK
    assert t_rows % EXPERT_TILE == 0
    n_expert_tiles = t_rows // EXPERT_TILE + N_EGROUPS

    xs = (x_prompt.reshape(rows_p, D_MODEL), x_sample.reshape(rows_s, D_MODEL))

    c_all = jnp.concatenate([c_prompt, c_sample], axis=0)
    c_pad = jnp.pad(c_all, ((0, (-n_seq) % 8), (0, 0)))
    mod = _ada(c_pad, w_ada, b_ada)
    modb_all = jnp.concatenate(
        [jnp.repeat(mod[:, :n_p], blk_per_prompt, axis=1), mod[:, n_p:n_seq]], axis=1)

    half_w = gm_w_s[:, :, :ROW_BLOCK, :ROW_BLOCK]
    w_cat = jnp.stack([gm_w_s, jnp.tile(half_w, (1, 1, 2, 2))], axis=1)
    b_t = jnp.swapaxes(gm_b_s, 1, 2)
    b_cat = jnp.stack([b_t, jnp.tile(b_t[:, :ROW_BLOCK], (1, 2, 1))], axis=1)

    w_in_t = jnp.swapaxes(w_in, 1, 2)
    w_r = jnp.concatenate([w_router_group, w_router_expert], axis=2)
    w_r = jnp.pad(w_r, ((0, 0), (0, 0), (0, LANES - w_r.shape[2])))
    b_r = jnp.concatenate([b_router_group, b_router_expert], axis=1)
    b_r = jnp.pad(b_r, ((0, 0), (0, LANES - b_r.shape[1])))[:, None, :]

    ssm_p, conv_p, ssm_s, conv_s, v_rows = [], [], [], [], []
    for l in range(DEPTH):
        last = l == DEPTH - 1
        modb = modb_all[l]
        proj, dt_raw = _inproj(xs, norm1_g, modb, w_in_t, l, tn=512 if len(xs) == 2 else 1024)
        gm_out, vn = _gmlp(proj, gm_ln_g, gm_ln_b, w_cat, b_cat, l, rows_p // GM_TILE, rows_s)
        ssm_args = (conv_w, conv_b, dt_bias, a_log, d_skip, mnorm_g, l)
        ssm_out_p, hfin_p, cfin_p = _ssd(proj, dt_raw, None, None, *ssm_args, 0, n_p, len_p, SSD_PROMPT_CHUNK)
        ssm_out_s, hfin_s, cfin_s = _ssd(proj, dt_raw, state_ssm, state_conv, *ssm_args, rows_p, n_s, len_s, len_s)
        x = _outproj(gm_out, (ssm_out_p, ssm_out_s), w_out, xs, modb, l)
        h2, route, counts = _router(x, norm2_g, modb, w_r[l], b_r[l], l)
        pos, src, tile_group, tile_nsub = _dispatch_tables(route, counts, t_rows, EXPERT_TILE, n_expert_tiles)
        ys = _experts(h2, src, tile_group, tile_nsub, w_gate, w_up, w_down, l, EXPERT_TILE)
        out = _combine(x, ys, pos, modb, final_norm_g, last, rows_p if last else None)
        xs = tuple(out) if last else (out,)
        ssm_p.append(hfin_p)
        conv_p.append(cfin_p)
        ssm_s.append(hfin_s)
        conv_s.append(cfin_s)
        v_rows.append(vn.reshape(n_s, len_s, GM_WIDTH))

    y_prompt = xs[0].reshape(n_p, len_p, D_MODEL)
    y_sample = xs[1].reshape(n_s, len_s, D_MODEL)
    return (y_prompt, y_sample, jnp.stack(ssm_p), jnp.stack(conv_p), jnp.stack(ssm_s), jnp.stack(conv_s),
            jnp.stack(v_rows))
```
